```python
import math
import jax, jax.numpy as jnp
from jax import lax
import numpy as np

D_MODEL = 4096
BATCH = 4
SEQ = 2048
DEPTH = 4

CTX_LEN = 256
GRID_W = 64
Q_BLOCK = 128
ROPE_BASE = 10000.0
NORM_EPS = 1e-6

N_BRANCH = 4
BRANCH_W = D_MODEL // N_BRANCH

MLA_HEADS = BRANCH_W // 128
MLA_NOPE = 128
MLA_ROPE = 64
MLA_V = 128
MLA_Q_LORA = 3 * D_MODEL // 16
MLA_KV_LORA = D_MODEL // 16

RWKV_HEAD = 64
RWKV_HEADS = BRANCH_W // RWKV_HEAD
RWKV_DECAY_LORA = 64
RWKV_A_LORA = 64
RWKV_GATE_LORA = 160
RWKV_GN_EPS = 64e-5

CONV_W = 3

DIFF_HEADS = BRANCH_W // 128
DIFF_QK = 64
DIFF_V = 2 * DIFF_QK

GATE_RANK = 256
MOD_RANK = 256
N_MOD = 6
D_FF = 4 * D_MODEL

MLA_COLS = MLA_Q_LORA + MLA_KV_LORA + MLA_ROPE
RWKV_COLS = 3 * BRANCH_W + 2 * RWKV_DECAY_LORA + 2 * RWKV_A_LORA + RWKV_GATE_LORA
CONV_COLS = 3 * BRANCH_W
DIFF_COLS = 2 * DIFF_HEADS * 2 * DIFF_QK + DIFF_HEADS * DIFF_V
IN_COLS = MLA_COLS + RWKV_COLS + CONV_COLS + DIFF_COLS
IN_SPLIT = (MLA_COLS, MLA_COLS + RWKV_COLS, MLA_COLS + RWKV_COLS + CONV_COLS)
MLA_SPLIT = (MLA_Q_LORA, MLA_Q_LORA + MLA_KV_LORA)
RWKV_SPLIT = (BRANCH_W, 2 * BRANCH_W, 3 * BRANCH_W, 3 * BRANCH_W + 2 * RWKV_DECAY_LORA,
              3 * BRANCH_W + 2 * RWKV_DECAY_LORA + 2 * RWKV_A_LORA)
DIFF_SPLIT = (DIFF_HEADS * 2 * DIFF_QK, 2 * DIFF_HEADS * 2 * DIFF_QK)

kernel_name = "hybrid_mla_rwkv7_conv_diffattn_dit"


def _rmsnorm(x, g):
    xf = x.astype(jnp.float32)
    y = xf * lax.rsqrt(jnp.mean(xf * xf, axis=-1, keepdims=True) + NORM_EPS)
    return (y * g.astype(jnp.float32)).astype(x.dtype)


def _axial_rope_tables(n_tokens, rot_dim):
    rows = n_tokens // GRID_W
    row = jnp.repeat(jnp.arange(rows, dtype=jnp.float32), GRID_W)
    col = jnp.tile(jnp.arange(GRID_W, dtype=jnp.float32), rows)
    n_freq = rot_dim // 4
    inv = ROPE_BASE ** (-jnp.arange(n_freq, dtype=jnp.float32) / n_freq)
    ang = jnp.concatenate([row[:, None] * inv, col[:, None] * inv], axis=-1)
    return jnp.cos(ang), jnp.sin(ang)


def _apply_rope(x, cos, sin):
    half = x.shape[-1] // 2
    shape = (1, x.shape[1]) + (1,) * (x.ndim - 3) + (half,)
    cos = cos.reshape(shape).astype(x.dtype)
    sin = sin.reshape(shape).astype(x.dtype)
    x1, x2 = x[..., :half], x[..., half:]
    return jnp.concatenate([x1 * cos - x2 * sin, x1 * sin + x2 * cos], axis=-1)


def _modulation(cond, down, up, b):
    m = (jax.nn.silu(cond) @ down) @ up + b
    return m.reshape(cond.shape[:-1] + (N_MOD, D_MODEL))


def _sweep_query_blocks(fn, *qs):
    B, T = qs[0].shape[:2]
    nb = T // Q_BLOCK
    blocks = tuple(jnp.moveaxis(q.reshape((B, nb, Q_BLOCK) + q.shape[2:]), 1, 0) for q in qs)
    out = lax.map(lambda qb: fn(*qb), blocks)
    return jnp.moveaxis(out, 0, 1).reshape((B, T) + out.shape[3:])


def _mla_attend(q_nope, q_rope, k_nope, k_rope, v):
    scale = (MLA_NOPE + MLA_ROPE) ** -0.5

    def block(qn, qr):
        s = jnp.einsum('bqhd,bshd->bhqs', qn, k_nope) + jnp.einsum('bqhr,bsr->bhqs', qr, k_rope)
        prob = jax.nn.softmax(s.astype(jnp.float32) * scale, axis=-1).astype(v.dtype)
        return jnp.einsum('bhqs,bshd->bqhd', prob, v)

    return _sweep_query_blocks(block, q_nope, q_rope)


def _diff_attend(q, k, v, lam):
    scale = DIFF_QK ** -0.5

    def block(qb):
        s = jnp.einsum('bqhnd,bshnd->bnhqs', qb, k).astype(jnp.float32) * scale
        prob = jax.nn.softmax(s, axis=-1)
        w = prob[:, 0] - lam * prob[:, 1]
        return jnp.einsum('bhqs,bshd->bqhd', w.astype(v.dtype), v)

    return _sweep_query_blocks(block, q)


def _short_conv_mixer(p_conv, conv_w):
    b_gate, c_gate, u = jnp.split(p_conv, 3, axis=-1)
    z = c_gate * u
    T = z.shape[1]
    zp = jnp.pad(z, ((0, 0), (1, 1), (0, 0)))
    y = conv_w[0] * zp[:, :T] + conv_w[1] * zp[:, 1:T + 1] + conv_w[2] * zp[:, 2:]
    return b_gate * y


def _centred_token_shift(p, mu):
    T = p.shape[1]
    pp = jnp.pad(p, ((0, 0), (1, 1), (0, 0)))
    return p + mu[0] * (pp[:, :T] - p) + mu[1] * (pp[:, 2:] - p)


def _rwkv_prepare(p, lp, need_out):
    B, T, _ = p.shape
    H, N = RWKV_HEADS, RWKV_HEAD
    p = _centred_token_shift(p, lp["rwkv_mu"])
    r, k, v, wd, ad, gd = jnp.split(p, RWKV_SPLIT, axis=-1)
    wd = jnp.tanh(wd.reshape(B, T, 2, RWKV_DECAY_LORA))
    w_log = -jax.nn.softplus(-(lp["rwkv_w0"] + jnp.einsum('btdr,drc->btdc', wd, lp["rwkv_w2"]))) - 0.5
    decay = jnp.exp(-jnp.exp(w_log.astype(jnp.float32))).reshape(B, T, 2, H, N)
    a = jax.nn.sigmoid(lp["rwkv_a0"] + jnp.einsum('btdr,drc->btdc', ad.reshape(B, T, 2, RWKV_A_LORA),
                                                  lp["rwkv_a2"])).reshape(B, T, 2, H, N)
    k = k.reshape(B, T, H, N)
    kkf = (k * lp["rwkv_k_k"].reshape(H, N)).astype(jnp.float32)
    kk = (kkf * lax.rsqrt(jnp.sum(kkf * kkf, axis=-1, keepdims=True) + 1e-12)).astype(k.dtype)
    k_dir = k[:, :, None] * (1.0 + (a - 1.0) * lp["rwkv_k_a"].reshape(H, N))
    out = dict(r=r.reshape(B, T, H, N), k=k_dir, v=v.reshape(B, T, H, N), kk=kk, a=a, w=decay)
    if need_out:
        out["g"] = jax.nn.sigmoid(gd) @ lp["rwkv_g2"]
    return out


def _rwkv7_scan(state0, r, w, k, v, kk, a, reverse, with_outputs):
    xs = tuple(jnp.moveaxis(t.astype(jnp.float32), 1, 0) for t in (r, w, k, v, kk, a))

    def step(S, inp):
        r_t, w_t, k_t, v_t, kk_t, a_t = inp
        sa = jnp.einsum('bhvk,bhk->bhv', S, kk_t)
        S = (S * w_t[:, :, None, :] - sa[..., None] * (kk_t * a_t)[:, :, None, :]
             + v_t[..., None] * k_t[:, :, None, :])
        o = jnp.einsum('bhvk,bhk->bhv', S, r_t) if with_outputs else None
        return S, o

    S, o = lax.scan(step, state0, xs, reverse=reverse)
    if with_outputs:
        return S, jnp.moveaxis(o, 0, 1).astype(r.dtype)
    return S, None


def _rwkv_bidirectional(fc, fx, need_ctx):
    B = fx["r"].shape[0]
    zeros = jnp.zeros((B, RWKV_HEADS, RWKV_HEAD, RWKV_HEAD), jnp.float32)
    o_x, o_c = [], []
    for d in range(2):
        rev = d == 1
        s_c, oc = _rwkv7_scan(zeros, fc["r"], fc["w"][:, :, d], fc["k"][:, :, d], fc["v"], fc["kk"],
                              fc["a"][:, :, d], rev, need_ctx)
        _, ox = _rwkv7_scan(s_c, fx["r"], fx["w"][:, :, d], fx["k"][:, :, d], fx["v"], fx["kk"],
                            fx["a"][:, :, d], rev, True)
        o_x.append(ox)
        o_c.append(oc)
    return o_x, o_c


def _rwkv_readout(f, o_pair, lp):
    B, T = f["r"].shape[:2]
    o = (o_pair[0] + o_pair[1]).astype(jnp.float32)
    mean = jnp.mean(o, axis=-1, keepdims=True)
    var = jnp.mean(jnp.square(o - mean), axis=-1, keepdims=True)
    o = ((o - mean) * lax.rsqrt(var + RWKV_GN_EPS)).reshape(B, T, BRANCH_W)
    o = (o * lp["rwkv_ln_g"] + lp["rwkv_ln_b"]).astype(f["r"].dtype)
    bonus_w = jnp.sum(f["r"][:, :, None] * f["k"] * lp["rwkv_r_k"], axis=(2, 4), keepdims=False)
    bonus = (bonus_w[..., None] * f["v"]).reshape(B, T, BRANCH_W)
    return (o + bonus) * f["g"]


def _stream_features(h, lp, rope, need_out):
    B, T, _ = h.shape
    p_mla, p_rwkv, p_conv, p_diff = jnp.split(h @ lp["w_in"], IN_SPLIT, axis=-1)
    cq, ckv, k_rope = jnp.split(p_mla, MLA_SPLIT, axis=-1)
    kv = (_rmsnorm(ckv, lp["mla_kv_norm_g"]) @ lp["mla_w_ukv"]).reshape(B, T, MLA_HEADS, MLA_NOPE + MLA_V)
    dq, dk, dv = jnp.split(p_diff, DIFF_SPLIT, axis=-1)
    dk = dk.reshape(B, T, DIFF_HEADS, 2, DIFF_QK)
    if rope is not None:
        k_rope = _apply_rope(k_rope, *rope[0])
        dk = _apply_rope(dk, *rope[1])
    f = dict(k_nope=kv[..., :MLA_NOPE], v_mla=kv[..., MLA_NOPE:], k_rope=k_rope, dk=dk,
             dv=dv.reshape(B, T, DIFF_HEADS, DIFF_V))
    f.update(_rwkv_prepare(p_rwkv, lp, need_out))
    if need_out:
        q = (_rmsnorm(cq, lp["mla_q_norm_g"]) @ lp["mla_w_uq"]).reshape(B, T, MLA_HEADS, MLA_NOPE + MLA_ROPE)
        q_nope, q_rope = q[..., :MLA_NOPE], q[..., MLA_NOPE:]
        dq = dq.reshape(B, T, DIFF_HEADS, 2, DIFF_QK)
        if rope is not None:
            q_rope = _apply_rope(q_rope, *rope[0])
            dq = _apply_rope(dq, *rope[1])
        f.update(q_nope=q_nope, q_rope=q_rope, dq=dq, conv=_short_conv_mixer(p_conv, lp["conv_w"]))
    return f


def _merge(h, branches, lp):
    gl = h @ lp["gate_down"]
    acc = None
    for i, y in enumerate(branches):
        gate = jax.nn.sigmoid(gl @ lp["gate_up"][:, i] + lp["gate_b"][i])
        term = gate * (y @ lp["w_branch"][i])
        acc = term if acc is None else acc + term
    return acc @ lp["w_out"]


def _mixer_output(h, f, key_feats, rwkv_pair, lp, lam, lam_init):
    B, T, _ = h.shape

    def cat(name):
        return jnp.concatenate([kf[name] for kf in key_feats], axis=1)

    y_mla = _mla_attend(f["q_nope"], f["q_rope"], cat("k_nope"), cat("k_rope"), cat("v_mla"))
    y_mla = y_mla.reshape(B, T, BRANCH_W)
    y_rwkv = _rwkv_readout(f, rwkv_pair, lp)
    y_conv = f["conv"]
    o = _diff_attend(f["dq"], cat("dk"), cat("dv"), lam)
    y_diff = (_rmsnorm(o, lp["diff_norm_g"]) * (1.0 - lam_init)).reshape(B, T, BRANCH_W)
    return _merge(h, (y_mla, y_rwkv, y_conv, y_diff), lp)


def _sq_relu_mlp(h, w1, w2):
    return jnp.square(jax.nn.relu(h @ w1)) @ w2


def setup_inputs(seed: int = 0) -> dict:
    key = jax.random.key(seed)
    ks = iter(jax.random.split(key, 48))
    L, D, BW = DEPTH, D_MODEL, BRANCH_W

    def nrm(shape, scale):
        return jax.random.normal(next(ks), shape, jnp.float32) * scale

    def gain(shape):
        return 1.0 + nrm(shape, 0.02)

    def unif(shape, lo, hi):
        return jax.random.uniform(next(ks), shape, jnp.float32, lo, hi)

    return {
        "x": nrm((BATCH, SEQ, D), 1.0),
        "c": nrm((BATCH, D), 1.0),
        "ctx": nrm((BATCH, CTX_LEN, D), 1.0),
        "c_ctx": nrm((D,), 1.0),
        "norm1_g": gain((L, D)),
        "norm2_g": gain((L, D)),
        "mod_down": nrm((L, D, MOD_RANK), D ** -0.5),
        "mod_up": nrm((L, MOD_RANK, N_MOD * D), 0.5 * MOD_RANK ** -0.5),
        "mod_b": nrm((L, N_MOD * D), 0.02),
        "w_in": nrm((L, D, IN_COLS), D ** -0.5),
        "mla_q_norm_g": gain((L, MLA_Q_LORA)),
        "mla_w_uq": nrm((L, MLA_Q_LORA, MLA_HEADS * (MLA_NOPE + MLA_ROPE)), MLA_Q_LORA ** -0.5),
        "mla_kv_norm_g": gain((L, MLA_KV_LORA)),
        "mla_w_ukv": nrm((L, MLA_KV_LORA, MLA_HEADS * (MLA_NOPE + MLA_V)), MLA_KV_LORA ** -0.5),
        "rwkv_mu": unif((L, 2, RWKV_COLS), 0.0, 0.5),
        "rwkv_w0": nrm((L, 2, BW), 0.5),
        "rwkv_w2": nrm((L, 2, RWKV_DECAY_LORA, BW), RWKV_DECAY_LORA ** -0.5),
        "rwkv_a0": nrm((L, 2, BW), 0.5),
        "rwkv_a2": nrm((L, 2, RWKV_A_LORA, BW), RWKV_A_LORA ** -0.5),
        "rwkv_g2": nrm((L, RWKV_GATE_LORA, BW), RWKV_GATE_LORA ** -0.5),
        "rwkv_k_k": 0.85 + nrm((L, BW), 0.05),
        "rwkv_k_a": 1.0 + nrm((L, BW), 0.05),
        "rwkv_r_k": nrm((L, RWKV_HEADS, RWKV_HEAD), 0.1),
        "rwkv_ln_g": gain((L, BW)),
        "rwkv_ln_b": nrm((L, BW), 0.02),
        "conv_w": nrm((L, CONV_W, BW), CONV_W ** -0.5),
        "diff_lambda": nrm((L, 4, DIFF_QK), 0.1),
        "diff_norm_g": gain((L, DIFF_V)),
        "w_branch": nrm((L, N_BRANCH, BW, D), BW ** -0.5),
        "gate_down": nrm((L, D, GATE_RANK), D ** -0.5),
        "gate_up": nrm((L, GATE_RANK, N_BRANCH, D), GATE_RANK ** -0.5),
        "gate_b": nrm((L, N_BRANCH, D), 0.1),
        "w_out": nrm((L, D, D), D ** -0.5),
        "mlp_w1": nrm((L, D, D_FF), D ** -0.5),
        "mlp_w2": nrm((L, D_FF, D), D_FF ** -0.5),
        "final_norm_g": gain((D,)),
    }


def reference(x, c, ctx, c_ctx, norm1_g, norm2_g, mod_down, mod_up, mod_b, w_in, mla_q_norm_g, mla_w_uq,
              mla_kv_norm_g, mla_w_ukv, rwkv_mu, rwkv_w0, rwkv_w2, rwkv_a0, rwkv_a2, rwkv_g2, rwkv_k_k,
              rwkv_k_a, rwkv_r_k, rwkv_ln_g, rwkv_ln_b, conv_w, diff_lambda, diff_norm_g, w_branch, gate_down,
              gate_up, gate_b, w_out, mlp_w1, mlp_w2, final_norm_g):
    T = x.shape[1]
    rope = (_axial_rope_tables(T, MLA_ROPE), _axial_rope_tables(T, DIFF_QK))
    for l in range(DEPTH):
        need_ctx = l < DEPTH - 1
        lp = dict(w_in=w_in[l], mla_q_norm_g=mla_q_norm_g[l], mla_w_uq=mla_w_uq[l],
                  mla_kv_norm_g=mla_kv_norm_g[l], mla_w_ukv=mla_w_ukv[l], rwkv_mu=rwkv_mu[l],
                  rwkv_w0=rwkv_w0[l], rwkv_w2=rwkv_w2[l], rwkv_a0=rwkv_a0[l], rwkv_a2=rwkv_a2[l],
                  rwkv_g2=rwkv_g2[l], rwkv_k_k=rwkv_k_k[l], rwkv_k_a=rwkv_k_a[l], rwkv_r_k=rwkv_r_k[l],
                  rwkv_ln_g=rwkv_ln_g[l], rwkv_ln_b=rwkv_ln_b[l], conv_w=conv_w[l],
                  diff_norm_g=diff_norm_g[l], w_branch=w_branch[l], gate_down=gate_down[l],
                  gate_up=gate_up[l], gate_b=gate_b[l], w_out=w_out[l])
        mx = _modulation(c, mod_down[l], mod_up[l], mod_b[l])[:, :, None, :]
        mc = _modulation(c_ctx, mod_down[l], mod_up[l], mod_b[l])

        hx = _rmsnorm(x, norm1_g[l]) * (1.0 + mx[:, 1]) + mx[:, 0]
        hc = _rmsnorm(ctx, norm1_g[l]) * (1.0 + mc[1]) + mc[0]
        fx = _stream_features(hx, lp, rope, True)
        fc = _stream_features(hc, lp, None, need_ctx)
        o_x, o_c = _rwkv_bidirectional(fc, fx, need_ctx)
        lq1, lk1, lq2, lk2 = diff_lambda[l].astype(jnp.float32)
        lam_init = 0.8 - 0.6 * math.exp(-0.3 * l)
        lam = jnp.exp(jnp.sum(lq1 * lk1)) - jnp.exp(jnp.sum(lq2 * lk2)) + lam_init
        x = x + mx[:, 2] * _mixer_output(hx, fx, (fc, fx), o_x, lp, lam, lam_init)
        if need_ctx:
            ctx = ctx + mc[2] * _mixer_output(hc, fc, (fc,), o_c, lp, lam, lam_init)

        h2 = _rmsnorm(x, norm2_g[l]) * (1.0 + mx[:, 4]) + mx[:, 3]
        x = x + mx[:, 5] * _sq_relu_mlp(h2, mlp_w1[l], mlp_w2[l])
        if need_ctx:
            h2c = _rmsnorm(ctx, norm2_g[l]) * (1.0 + mc[4]) + mc[3]
            ctx = ctx + mc[5] * _sq_relu_mlp(h2c, mlp_w1[l], mlp_w2[l])
    return _rmsnorm(x, final_norm_g)
```

```python
import functools
import math

import jax
import jax.numpy as jnp
from jax import lax
from jax.experimental import pallas as pl
from jax.experimental.pallas import tpu as pltpu

F32 = jnp.float32
BF16 = jnp.bfloat16

D_MODEL = 4096
BRANCH_W = 1024
GRID_W = 64
ROPE_BASE = 10000.0
NORM_EPS = 1e-6
N_MOD = 6

MLA_HEADS = 8
MLA_NOPE = 128
MLA_ROPE = 64
MLA_V = 128
MLA_Q_LORA = 768
MLA_KV_LORA = 256
MLA_QK_PAD = 256

RWKV_HEAD = 64
RWKV_HEADS = 16
RWKV_LORA = 64
RWKV_GATE_LORA = 160
RWKV_GN_EPS = 64e-5
RWKV_CHUNK = 64
RWKV_COLS = 3 * BRANCH_W + 4 * RWKV_LORA + RWKV_GATE_LORA

DIFF_HEADS = 8
DIFF_QK = 64
DIFF_V = 128
GATE_RANK = 256
MOD_RANK = 256
D_FF = 4 * D_MODEL

C_CQ = 0
C_CKV = 768
C_GL = 1024
C_KROPE = 1280
C_RWKV = 1408
C_CONV = 4992
C_DQ = 8064
C_DK = 9088
C_DV = 10112
P_COLS = 11264

VMEM_CAP = 56 * 1024 * 1024


def _cparams(sem, vmem_bytes):
    limit = int(min(VMEM_CAP, max(vmem_bytes * 1.5 + (4 << 20), 16 << 20)))
    return pltpu.CompilerParams(dimension_semantics=sem, vmem_limit_bytes=limit)


def _mod_index(i, tm, n_x_rows, t_len, n_batch):
    return jnp.where(i < n_x_rows // tm, i // (t_len // tm), n_batch)


def _dot(a, b):
    return jnp.dot(a.astype(BF16), b.astype(BF16), preferred_element_type=F32)


def _dot_nt(a, b):
    return lax.dot_general(a.astype(BF16), b.astype(BF16), (((1,), (1,)), ((), ())),
                           preferred_element_type=F32)


def _matmul_kernel(*refs, nk, has_bias, has_res, gate_row, act):
    a_ref, w_ref = refs[0], refs[1]
    pos = 2
    bias_ref = res_ref = mod_ref = None
    if has_bias:
        bias_ref = refs[pos]; pos += 1
    if has_res:
        res_ref = refs[pos]; mod_ref = refs[pos + 1]; pos += 2
    o_ref, acc_ref = refs[pos], refs[pos + 1]
    k = pl.program_id(2)

    @pl.when(k == 0)
    def _():
        acc_ref[...] = jnp.zeros_like(acc_ref)

    acc_ref[...] += _dot(a_ref[...], w_ref[...])

    @pl.when(k == nk - 1)
    def _():
        y = acc_ref[...]
        if has_bias:
            y = y + bias_ref[...]
        if act == "relu2":
            y = jnp.square(jnp.maximum(y, 0.0))
        if has_res:
            y = res_ref[...] + mod_ref[0, gate_row:gate_row + 1, :] * y
        o_ref[...] = y.astype(o_ref.dtype)


def _matmul(a, w, *, tm, tn, tk, out_dtype=F32, bias=None, res=None, mod=None, gate_row=None,
            mod_index=None, act=None, rows=None):
    m = a.shape[0] if rows is None else rows
    kdim, n = w.shape
    assert a.shape[1] == kdim and m % tm == 0 and n % tn == 0 and kdim % tk == 0
    nk = kdim // tk
    in_specs = [pl.BlockSpec((tm, tk), lambda i, j, k: (i, k)),
                pl.BlockSpec((tk, tn), lambda i, j, k: (k, j))]
    args = [a, w]
    vmem = 2 * tm * tk * a.dtype.itemsize + 2 * tk * tn * w.dtype.itemsize + tm * tn * 4
    vmem += 2 * tm * tn * jnp.dtype(out_dtype).itemsize
    if bias is not None:
        in_specs.append(pl.BlockSpec((1, tn), lambda i, j, k: (0, j)))
        args.append(bias.reshape(1, n).astype(F32))
    if res is not None:
        in_specs.append(pl.BlockSpec((tm, tn), lambda i, j, k: (i, j)))
        in_specs.append(pl.BlockSpec((1, N_MOD, tn), lambda i, j, k: (mod_index(i), 0, j)))
        args += [res, mod]
        vmem += 2 * tm * tn * 4 + 2 * 8 * tn * 4
    kern = functools.partial(_matmul_kernel, nk=nk, has_bias=bias is not None,
                             has_res=res is not None, gate_row=gate_row, act=act)
    return pl.pallas_call(
        kern,
        grid=(m // tm, n // tn, nk),
        in_specs=in_specs,
        out_specs=pl.BlockSpec((tm, tn), lambda i, j, k: (i, j)),
        out_shape=jax.ShapeDtypeStruct((m, n), out_dtype),
        scratch_shapes=[pltpu.VMEM((tm, tn), F32)],
        compiler_params=_cparams(("parallel", "parallel", "arbitrary"), vmem),
    )(*args)


def _nm_matmul_kernel(*refs, has_mod, shift_row, scale_row, act):
    if has_mod:
        x_ref, g_ref, mod_ref, w_ref, o_ref, h_ref = refs
    else:
        x_ref, g_ref, w_ref, o_ref, h_ref = refs

    @pl.when(pl.program_id(1) == 0)
    def _():
        x = x_ref[...].astype(F32)
        y = x * lax.rsqrt(jnp.mean(x * x, axis=-1, keepdims=True) + NORM_EPS) * g_ref[...]
        if has_mod:
            y = y * (1.0 + mod_ref[0, scale_row:scale_row + 1, :]) + mod_ref[0, shift_row:shift_row + 1, :]
        h_ref[...] = y.astype(BF16)

    y = jnp.dot(h_ref[...], w_ref[...].astype(BF16), preferred_element_type=F32)
    if act == "relu2":
        y = jnp.square(jnp.maximum(y, 0.0))
    o_ref[...] = y.astype(o_ref.dtype)


def _nm_matmul(x, g, w, *, tm, tn, out_dtype, x_col_block=0, kdim=None, mod=None, shift_row=None,
               scale_row=None, mod_index=None, act=None, rows=None):
    m = x.shape[0] if rows is None else rows
    kdim = x.shape[1] if kdim is None else kdim
    n = w.shape[1]
    assert w.shape[0] == kdim and m % tm == 0 and n % tn == 0
    in_specs = [pl.BlockSpec((tm, kdim), lambda i, j: (i, x_col_block)),
                pl.BlockSpec((1, kdim), lambda i, j: (0, 0))]
    args = [x, g.reshape(1, kdim).astype(F32)]
    if mod is not None:
        in_specs.append(pl.BlockSpec((1, N_MOD, kdim), lambda i, j: (mod_index(i), 0, 0)))
        args.append(mod)
    in_specs.append(pl.BlockSpec((kdim, tn), lambda i, j: (0, j)))
    args.append(w)
    vmem = (2 * tm * kdim * x.dtype.itemsize + tm * kdim * 2 + 2 * kdim * tn * w.dtype.itemsize
            + 2 * tm * tn * jnp.dtype(out_dtype).itemsize + tm * tn * 4 + 4 * 8 * kdim * 4)
    kern = functools.partial(_nm_matmul_kernel, has_mod=mod is not None, shift_row=shift_row,
                             scale_row=scale_row, act=act)
    return pl.pallas_call(
        kern,
        grid=(m // tm, n // tn),
        in_specs=in_specs,
        out_specs=pl.BlockSpec((tm, tn), lambda i, j: (i, j)),
        out_shape=jax.ShapeDtypeStruct((m, n), out_dtype),
        scratch_shapes=[pltpu.VMEM((tm, kdim), BF16)],
        compiler_params=_cparams(("parallel", "arbitrary"), vmem),
    )(*args)


def _merge_kernel(gl_ref, y0_ref, y1_ref, y2_ref, y3_ref, wb_ref, gu_ref, gb_ref, o_ref):
    gl = gl_ref[...].astype(BF16)
    acc = None
    for i, y_ref in enumerate((y0_ref, y1_ref, y2_ref, y3_ref)):
        gate = jax.nn.sigmoid(jnp.dot(gl, gu_ref[i], preferred_element_type=F32) + gb_ref[i:i + 1, :])
        term = gate * jnp.dot(y_ref[...], wb_ref[i], preferred_element_type=F32)
        acc = term if acc is None else acc + term
    o_ref[...] = acc.astype(o_ref.dtype)


def _merge(p, ys, wb, gu, gb, *, tm, tn, rows):
    n = wb.shape[2]
    bw = wb.shape[1]
    gr = gu.shape[1]
    y_spec = pl.BlockSpec((tm, bw), lambda i, j: (i, 0))
    vmem = (2 * tm * gr * 4 + 4 * 2 * tm * bw * 2 + 2 * 4 * bw * tn * 2 + 2 * 4 * gr * tn * 2
            + 2 * tm * tn * 2 + 3 * tm * tn * 4)
    return pl.pallas_call(
        _merge_kernel,
        grid=(rows // tm, n // tn),
        in_specs=[pl.BlockSpec((tm, gr), lambda i, j: (i, C_GL // GATE_RANK)),
                  y_spec, y_spec, y_spec, y_spec,
                  pl.BlockSpec((4, bw, tn), lambda i, j: (0, 0, j)),
                  pl.BlockSpec((4, gr, tn), lambda i, j: (0, 0, j)),
                  pl.BlockSpec((4, tn), lambda i, j: (0, j))],
        out_specs=pl.BlockSpec((tm, tn), lambda i, j: (i, j)),
        out_shape=jax.ShapeDtypeStruct((rows, n), BF16),
        compiler_params=_cparams(("parallel", "arbitrary"), vmem),
    )(p, *ys, wb, gu, gb)


def _softmax_parts(s_list, scale):
    m = None
    for s in s_list:
        sm = jnp.max(s, axis=-1, keepdims=True)
        m = sm if m is None else jnp.maximum(m, sm)
    e_list = [jnp.exp((s - m) * scale) for s in s_list]
    l = None
    for e in e_list:
        es = jnp.sum(e, axis=-1, keepdims=True)
        l = es if l is None else l + es
    return e_list, l


def _mla_kernel(*refs, n_seg, scale):
    q_ref = refs[0]
    k_refs = refs[1:1 + n_seg]
    v_refs = refs[1 + n_seg:1 + 2 * n_seg]
    o_ref = refs[1 + 2 * n_seg]
    q = q_ref[...]
    e_list, l = _softmax_parts([_dot_nt(q, k_ref[...]) for k_ref in k_refs], scale)
    o = None
    for e, v_ref in zip(e_list, v_refs):
        t = _dot(e, v_ref[...])
        o = t if o is None else o + t
    o_ref[...] = (o / l).astype(o_ref.dtype)


def _diff_kernel(*refs, n_seg, scale, out_scale):
    lam_ref, g_ref, q_ref = refs[0], refs[1], refs[2]
    k_refs = refs[3:3 + n_seg]
    v_refs = refs[3 + n_seg:3 + 2 * n_seg]
    o_ref = refs[3 + 2 * n_seg]
    q = q_ref[...]
    lane = lax.broadcasted_iota(jnp.int32, q.shape, 1)
    q1 = jnp.where(lane < DIFF_QK, q, jnp.zeros_like(q))
    q2 = jnp.where(lane < DIFF_QK, jnp.zeros_like(q), q)
    ks = [k_ref[...] for k_ref in k_refs]
    outs = []
    for qh in (q1, q2):
        e_list, l = _softmax_parts([_dot_nt(qh, k) for k in ks], scale)
        o = None
        for e, v_ref in zip(e_list, v_refs):
            t = _dot(e, v_ref[...])
            o = t if o is None else o + t
        outs.append(o / l)
    o = outs[0] - lam_ref[...] * outs[1]
    y = o * lax.rsqrt(jnp.mean(o * o, axis=-1, keepdims=True) + NORM_EPS) * g_ref[...]
    o_ref[...] = (y * out_scale).astype(o_ref.dtype)


def _attention(kind, q, k, v, *, dk, dv, heads, tq, q_block0, n_q_blocks_per_batch, n_batch, segs,
               out_rows, out_block0, scale, extra=(), out_scale=None):
    nq = n_q_blocks_per_batch
    in_specs = []
    args = []
    for e in extra:
        in_specs.append(pl.BlockSpec(e.shape, lambda b, h, i: (0, 0)))
        args.append(e)
    in_specs.append(pl.BlockSpec((tq, dk), lambda b, h, i: (q_block0 + b * nq + i, h)))
    args.append(q)
    vmem = 2 * tq * dk * 2 + 2 * tq * dv * 2
    for (seg_len, row0) in segs:
        blk0 = row0 // seg_len
        in_specs.append(pl.BlockSpec((seg_len, dk), lambda b, h, i, blk0=blk0: (blk0 + b, h)))
        args.append(k)
        vmem += 2 * seg_len * dk * 2
    for (seg_len, row0) in segs:
        blk0 = row0 // seg_len
        in_specs.append(pl.BlockSpec((seg_len, dv), lambda b, h, i, blk0=blk0: (blk0 + b, h)))
        args.append(v)
        vmem += 2 * seg_len * dv * 2 + 6 * tq * seg_len * 4
    if kind == "mla":
        kern = functools.partial(_mla_kernel, n_seg=len(segs), scale=scale)
    else:
        kern = functools.partial(_diff_kernel, n_seg=len(segs), scale=scale, out_scale=out_scale)
    return pl.pallas_call(
        kern,
        grid=(n_batch, heads, nq),
        in_specs=in_specs,
        out_specs=pl.BlockSpec((tq, dv), lambda b, h, i: (out_block0 + b * nq + i, h)),
        out_shape=jax.ShapeDtypeStruct((out_rows, heads * dv), BF16),
        compiler_params=_cparams(("parallel", "parallel", "arbitrary"), vmem),
    )(*args)


def _split3(x):
    hi = x.astype(BF16)
    r1 = x - hi.astype(F32)
    mid = r1.astype(BF16)
    lo = (r1 - mid.astype(F32)).astype(BF16)
    return hi, mid, lo


def _rwkv_kernel(r_ref, kk_ref, v_ref, lw_ref, kd_ref, ka_ref, o_ref, s_ref, *, precise):
    C = RWKV_CHUNK
    W = 2 * RWKV_HEAD
    n_pairs = RWKV_HEADS // 2
    d = pl.program_id(1)
    sign = 1 - 2 * d

    @pl.when(pl.program_id(2) == 0)
    def _():
        s_ref[...] = jnp.zeros_like(s_ref)

    if precise:
        def mm(a, b):
            return jnp.dot(a, b, preferred_element_type=F32, precision=lax.Precision.HIGHEST)

        def mm_nt(a, b):
            return lax.dot_general(a, b, (((1,), (1,)), ((), ())), preferred_element_type=F32,
                                   precision=lax.Precision.HIGHEST)
    else:
        mm, mm_nt = _dot, _dot_nt

    rc = lax.broadcasted_iota(jnp.int32, (C, C), 0)
    cc = lax.broadcasted_iota(jnp.int32, (C, C), 1)
    m_incl = jnp.where((rc - cc) * sign >= 0, 1.0, 0.0).astype(BF16)
    lw = lw_ref[0]
    cum = None
    for part in _split3(lw):
        t = jnp.dot(m_incl, part, preferred_element_type=F32)
        cum = t if cum is None else cum + t
    tot = jnp.sum(lw, axis=0, keepdims=True)
    e_incl = jnp.exp(cum)
    e_excl = jnp.exp(cum - lw)
    e_ninc = jnp.exp(-cum)
    e_rem = jnp.exp(tot - cum)
    e_tot = jnp.exp(tot)

    kk = kk_ref[...]
    ka = ka_ref[0]
    kd = kd_ref[0]
    a_t = kk * e_excl
    b_t = -ka * e_ninc
    k_t = kd * e_ninc
    r_t = r_ref[...] * e_incl
    b_h = -ka * e_rem
    k_h = kd * e_rem
    v = v_ref[...]

    lane = lax.broadcasted_iota(jnp.int32, (C, W), 1)
    first = lane < RWKV_HEAD
    r2 = lax.broadcasted_iota(jnp.int32, (W, W), 0)
    c2 = lax.broadcasted_iota(jnp.int32, (W, W), 1)
    same = (r2 // C) == (c2 // C)
    d2 = ((r2 % C) - (c2 % C)) * sign
    strict_bd = same & (d2 > 0)
    incl_bd = same & (d2 >= 0)
    eye = jnp.where(r2 == c2, 1.0, 0.0).astype(F32)

    def stack2(x):
        return jnp.concatenate([jnp.where(first, x, 0.0), jnp.where(first, 0.0, x)], axis=0)

    for p in range(n_pairs):
        sl = slice(W * p, W * (p + 1))
        a2, b2, k2, rr2 = stack2(a_t[:, sl]), stack2(b_t[:, sl]), stack2(k_t[:, sl]), stack2(r_t[:, sl])
        v2, bh2, kh2 = stack2(v[:, sl]), stack2(b_h[:, sl]), stack2(k_h[:, sl])
        g = mm_nt(jnp.concatenate([a2, rr2], axis=0), jnp.concatenate([b2, k2], axis=0))
        l_m = jnp.where(strict_bd, g[:W, :W], 0.0)
        m_ak = jnp.where(strict_bd, g[:W, W:], 0.0)
        a_rb = jnp.where(incl_bd, g[W:, :W], 0.0)
        a_rk = jnp.where(incl_bd, g[W:, W:], 0.0)
        t_inv = eye + l_m
        l_pow = l_m
        for _ in range(int(math.log2(C)) - 1):
            l_pow = mm(l_pow, l_pow)
            t_inv = t_inv + mm(t_inv, l_pow)
        wu = mm(t_inv, jnp.concatenate([a2, mm(m_ak, v2)], axis=1))
        s_bd = s_ref[p]
        wr = mm(jnp.concatenate([wu[:, :W], rr2], axis=0), s_bd)
        z2 = wr[:W] + wu[:, W:]
        zv = jnp.concatenate([z2, v2], axis=0)
        o2 = wr[W:] + mm(jnp.concatenate([a_rb, a_rk], axis=1), zv)
        o_ref[0, :, sl] = o2[:C] + o2[C:]
        bk_t = jnp.concatenate([bh2, kh2], axis=0).T
        tot_col = jnp.broadcast_to(e_tot[:, sl], (W, W)).T
        s_ref[p] = tot_col * s_bd + mm(bk_t, zv)


def _rwkv_scan(r, kk, v, lw, kd, ka, *, n_batch, t_len, c_len, precise):
    C = RWKV_CHUNK
    rows, width = r.shape
    nct, nxt = c_len // C, t_len // C
    ctx_blk0 = n_batch * t_len // C

    def blk(b, d, s):
        j_c = jnp.where(d == 0, s, nct - 1 - s)
        j_x = jnp.where(d == 0, s - nct, nxt - 1 - (s - nct))
        return jnp.where(s < nct, ctx_blk0 + b * nct + j_c, b * nxt + j_x)

    shared = pl.BlockSpec((C, width), lambda b, d, s: (blk(b, d, s), 0))
    per_dir = pl.BlockSpec((1, C, width), lambda b, d, s: (d, blk(b, d, s), 0))
    vmem = 2 * 7 * C * width * 4 + (RWKV_HEADS // 2) * 128 * 128 * 4 + 64 * C * width * 4
    return pl.pallas_call(
        functools.partial(_rwkv_kernel, precise=precise),
        grid=(n_batch, 2, nct + nxt),
        in_specs=[shared, shared, shared, per_dir, per_dir, per_dir],
        out_specs=per_dir,
        out_shape=jax.ShapeDtypeStruct((2, rows, width), F32),
        scratch_shapes=[pltpu.VMEM((RWKV_HEADS // 2, 128, 128), F32)],
        compiler_params=_cparams(("parallel", "parallel", "arbitrary"), vmem),
    )(r, kk, v, lw, kd, ka)


def _rmsnorm_kernel(x_ref, g_ref, o_ref):
    x = x_ref[...]
    o_ref[...] = x * lax.rsqrt(jnp.mean(x * x, axis=-1, keepdims=True) + NORM_EPS) * g_ref[...]


def _final_norm(xs, g, *, rows, tm):
    dm = xs.shape[1]
    return pl.pallas_call(
        _rmsnorm_kernel,
        grid=(rows // tm,),
        in_specs=[pl.BlockSpec((tm, dm), lambda i: (i, 0)), pl.BlockSpec((1, dm), lambda i: (0, 0))],
        out_specs=pl.BlockSpec((tm, dm), lambda i: (i, 0)),
        out_shape=jax.ShapeDtypeStruct((rows, dm), F32),
        compiler_params=_cparams(("parallel",), 4 * tm * dm * 4),
    )(xs, g.reshape(1, dm))


def _rope_tables(n_tokens):
    rows = n_tokens // GRID_W
    row = jnp.repeat(jnp.arange(rows, dtype=F32), GRID_W)
    col = jnp.tile(jnp.arange(GRID_W, dtype=F32), rows)
    n_freq = 64 // 4
    inv = ROPE_BASE ** (-jnp.arange(n_freq, dtype=F32) / n_freq)
    ang = jnp.concatenate([row[:, None] * inv, col[:, None] * inv], axis=-1)
    return jnp.cos(ang), jnp.sin(ang)


def _rope_x_rows(a, n_x, t_len, cos, sin, group_stride=1, group_offset=0):
    ax = a[:n_x].reshape(n_x // t_len, t_len, -1, group_stride, 64)
    sel = ax[:, :, :, group_offset, :]
    x1, x2 = sel[..., :32], sel[..., 32:]
    c = cos[None, :, None, :]
    s = sin[None, :, None, :]
    roped = jnp.concatenate([x1 * c - x2 * s, x1 * s + x2 * c], axis=-1)
    out = jnp.concatenate([ax[:, :, :, :group_offset], roped[:, :, :, None], ax[:, :, :, group_offset + 1:]],
                          axis=3).reshape(n_x, -1)
    return jnp.concatenate([out, a[n_x:]], axis=0)


def _per_stream(a, n_x, t_len, c_len):
    return a[:n_x].reshape(n_x // t_len, t_len, -1), a[n_x:].reshape(-1, c_len, a.shape[-1])


def _shift_prev_next(a, n_x, t_len, c_len):
    outs_p, outs_n = [], []
    for part in _per_stream(a, n_x, t_len, c_len):
        z = jnp.zeros_like(part[:, :1])
        outs_p.append(jnp.concatenate([z, part[:, :-1]], axis=1).reshape(-1, a.shape[-1]))
        outs_n.append(jnp.concatenate([part[:, 1:], z], axis=1).reshape(-1, a.shape[-1]))
    return jnp.concatenate(outs_p, axis=0), jnp.concatenate(outs_n, axis=0)


def _block_diag2(w2):
    z = jnp.zeros_like(w2[0])
    return jnp.concatenate([jnp.concatenate([w2[0], z], axis=1), jnp.concatenate([z, w2[1]], axis=1)], axis=0)


def kernel(x, c, ctx, c_ctx, norm1_g, norm2_g, mod_down, mod_up, mod_b, w_in, mla_q_norm_g, mla_w_uq,
           mla_kv_norm_g, mla_w_ukv, rwkv_mu, rwkv_w0, rwkv_w2, rwkv_a0, rwkv_a2, rwkv_g2, rwkv_k_k,
           rwkv_k_a, rwkv_r_k, rwkv_ln_g, rwkv_ln_b, conv_w, diff_lambda, diff_norm_g, w_branch, gate_down,
           gate_up, gate_b, w_out, mlp_w1, mlp_w2, final_norm_g):
    n_batch, t_len, dm = x.shape
    c_len = ctx.shape[1]
    depth = w_in.shape[0]
    assert dm == D_MODEL and t_len % 256 == 0 and c_len % RWKV_CHUNK == 0
    n_x = n_batch * t_len
    n_c = n_batch * c_len
    n_rows = n_x + n_c
    tm = 512 if (t_len % 512 == 0 and n_c % 512 == 0) else 256
    assert t_len % tm == 0 and n_c % tm == 0
    tq = 256
    assert c_len % tq == 0 or c_len < tq
    tq_c = min(tq, c_len)
    mod_index = functools.partial(_mod_index, tm=tm, n_x_rows=n_x, t_len=t_len, n_batch=n_batch)

    zeros = lambda *s: jnp.zeros(s, F32)
    w_in_p = jnp.concatenate(
        [w_in[:, :, :1024], gate_down, w_in[:, :, 1024:1088], zeros(depth, dm, 64),
         w_in[:, :, 1088:1088 + RWKV_COLS], zeros(depth, dm, C_CONV - C_RWKV - RWKV_COLS),
         w_in[:, :, 1088 + RWKV_COLS:], zeros(depth, dm, P_COLS - C_DV - BRANCH_W)], axis=-1).astype(BF16)
    w_uq_p = jnp.pad(mla_w_uq.reshape(depth, MLA_Q_LORA, MLA_HEADS, MLA_NOPE + MLA_ROPE),
                     ((0, 0), (0, 0), (0, 0), (0, MLA_QK_PAD - MLA_NOPE - MLA_ROPE))
                     ).reshape(depth, MLA_Q_LORA, MLA_HEADS * MLA_QK_PAD).astype(BF16)
    w_ukv_r = mla_w_ukv.reshape(depth, MLA_KV_LORA, MLA_HEADS, MLA_NOPE + MLA_V)
    w_ukv_p = jnp.concatenate([w_ukv_r[..., :MLA_NOPE].reshape(depth, MLA_KV_LORA, -1),
                               w_ukv_r[..., MLA_NOPE:].reshape(depth, MLA_KV_LORA, -1)], axis=-1).astype(BF16)
    w_branch_b = w_branch.astype(BF16)
    gate_up_b = jnp.moveaxis(gate_up, 2, 1).astype(BF16)
    w_out_b = w_out.astype(BF16)
    w1_b = mlp_w1.astype(BF16)
    w2_b = mlp_w2.astype(BF16)
    g2_p = jnp.pad(rwkv_g2, ((0, 0), (0, 256 - RWKV_GATE_LORA), (0, 0)))

    cond = jnp.concatenate([c, c_ctx[None, :], zeros(16 - n_batch - 1, dm)], axis=0)
    cond = jax.nn.silu(cond)
    mods = []
    for l in range(depth):
        low = _matmul(cond, mod_down[l], tm=16, tn=MOD_RANK, tk=dm)
        up = _matmul(low, mod_up[l], tm=16, tn=2048, tk=MOD_RANK, bias=mod_b[l])
        mods.append(up.reshape(16, N_MOD, dm))

    cos, sin = _rope_tables(t_len)
    xs = jnp.concatenate([x.reshape(n_x, dm), ctx.reshape(n_c, dm)], axis=0)

    for l in range(depth):
        need_ctx = l < depth - 1
        mod = mods[l]
        lam_init = 0.8 - 0.6 * math.exp(-0.3 * l)
        lq1, lk1, lq2, lk2 = diff_lambda[l]
        lam = jnp.exp(jnp.sum(lq1 * lk1)) - jnp.exp(jnp.sum(lq2 * lk2)) + lam_init
        lam_row = jnp.full((1, DIFF_V), 1.0, F32) * lam

        p = _nm_matmul(xs, norm1_g[l], w_in_p[l], tm=tm, tn=512, out_dtype=F32, mod=mod, shift_row=0,
                       scale_row=1, mod_index=mod_index)

        q = _nm_matmul(p, mla_q_norm_g[l], w_uq_p[l], tm=tm, tn=512, out_dtype=F32,
                       x_col_block=C_CQ // MLA_Q_LORA, kdim=MLA_Q_LORA)
        kv = _nm_matmul(p, mla_kv_norm_g[l], w_ukv_p[l], tm=tm, tn=512, out_dtype=F32,
                        x_col_block=C_CKV // MLA_KV_LORA, kdim=MLA_KV_LORA)
        q = _rope_x_rows(q, n_x, t_len, cos, sin, group_stride=4, group_offset=2).astype(BF16)
        k_rope = _rope_x_rows(p[:, C_KROPE:C_KROPE + MLA_ROPE], n_x, t_len, cos, sin)
        k_mla = jnp.concatenate(
            [kv[:, :MLA_HEADS * MLA_NOPE].reshape(n_rows, MLA_HEADS, MLA_NOPE),
             jnp.broadcast_to(k_rope[:, None, :], (n_rows, MLA_HEADS, MLA_ROPE)),
             jnp.zeros((n_rows, MLA_HEADS, MLA_QK_PAD - MLA_NOPE - MLA_ROPE), F32)], axis=-1
        ).reshape(n_rows, MLA_HEADS * MLA_QK_PAD).astype(BF16)
        v_mla = kv[:, MLA_HEADS * MLA_NOPE:].astype(BF16)
        mla_scale = (MLA_NOPE + MLA_ROPE) ** -0.5
        y_mla = _attention("mla", q, k_mla, v_mla, dk=MLA_QK_PAD, dv=MLA_V, heads=MLA_HEADS, tq=tq,
                           q_block0=0, n_q_blocks_per_batch=t_len // tq, n_batch=n_batch,
                           segs=[(c_len, n_x), (t_len, 0)], out_rows=n_x, out_block0=0, scale=mla_scale)

        dq = _rope_x_rows(p[:, C_DQ:C_DQ + BRANCH_W], n_x, t_len, cos, sin).astype(BF16)
        dk = _rope_x_rows(p[:, C_DK:C_DK + BRANCH_W], n_x, t_len, cos, sin).astype(BF16)
        dv = p[:, C_DV:C_DV + BRANCH_W].astype(BF16)
        diff_extra = (lam_row, diff_norm_g[l].reshape(1, DIFF_V))
        y_diff = _attention("diff", dq, dk, dv, dk=2 * DIFF_QK, dv=DIFF_V, heads=DIFF_HEADS, tq=tq,
                            q_block0=0, n_q_blocks_per_batch=t_len // tq, n_batch=n_batch,
                            segs=[(c_len, n_x), (t_len, 0)], out_rows=n_x, out_block0=0,
                            scale=DIFF_QK ** -0.5, extra=diff_extra, out_scale=1.0 - lam_init)
        if need_ctx:
            y_mla_c = _attention("mla", q, k_mla, v_mla, dk=MLA_QK_PAD, dv=MLA_V, heads=MLA_HEADS, tq=tq_c,
                                 q_block0=n_x // tq_c, n_q_blocks_per_batch=c_len // tq_c, n_batch=n_batch,
                                 segs=[(c_len, n_x)], out_rows=n_c, out_block0=0, scale=mla_scale)
            y_diff_c = _attention("diff", dq, dk, dv, dk=2 * DIFF_QK, dv=DIFF_V, heads=DIFF_HEADS, tq=tq_c,
                                  q_block0=n_x // tq_c, n_q_blocks_per_batch=c_len // tq_c, n_batch=n_batch,
                                  segs=[(c_len, n_x)], out_rows=n_c, out_block0=0,
                                  scale=DIFF_QK ** -0.5, extra=diff_extra, out_scale=1.0 - lam_init)
            y_mla = jnp.concatenate([y_mla, y_mla_c], axis=0)
            y_diff = jnp.concatenate([y_diff, y_diff_c], axis=0)

        pr = p[:, C_RWKV:C_RWKV + RWKV_COLS]
        pr_prev, pr_next = _shift_prev_next(pr, n_x, t_len, c_len)
        pr = pr + rwkv_mu[l, 0] * (pr_prev - pr) + rwkv_mu[l, 1] * (pr_next - pr)
        r_ = pr[:, :BRANCH_W]
        k_ = pr[:, BRANCH_W:2 * BRANCH_W]
        v_ = pr[:, 2 * BRANCH_W:3 * BRANCH_W]
        o3 = 3 * BRANCH_W
        wd = jnp.tanh(pr[:, o3:o3 + 2 * RWKV_LORA])
        ad = pr[:, o3 + 2 * RWKV_LORA:o3 + 4 * RWKV_LORA]
        gd = jnp.pad(jax.nn.sigmoid(pr[:, o3 + 4 * RWKV_LORA:]), ((0, 0), (0, 256 - RWKV_GATE_LORA)))
        w_pre = _matmul(wd, _block_diag2(rwkv_w2[l]), tm=tm, tn=2048, tk=2 * RWKV_LORA,
                        bias=rwkv_w0[l].reshape(-1))
        a_pre = _matmul(ad, _block_diag2(rwkv_a2[l]), tm=tm, tn=2048, tk=2 * RWKV_LORA,
                        bias=rwkv_a0[l].reshape(-1))
        g_out = _matmul(gd, g2_p[l], tm=tm, tn=BRANCH_W, tk=256)
        w_log = -jax.nn.softplus(-w_pre) - 0.5
        lw = jnp.moveaxis((-jnp.exp(w_log)).reshape(n_rows, 2, BRANCH_W), 1, 0)
        a_sig = jnp.moveaxis(jax.nn.sigmoid(a_pre).reshape(n_rows, 2, BRANCH_W), 1, 0)
        kkf = (k_ * rwkv_k_k[l]).reshape(n_rows, RWKV_HEADS, RWKV_HEAD)
        kk = (kkf * lax.rsqrt(jnp.sum(kkf * kkf, axis=-1, keepdims=True) + 1e-12)).reshape(n_rows, BRANCH_W)
        k_dir = k_[None] * (1.0 + (a_sig - 1.0) * rwkv_k_a[l])
        kka = kk[None] * a_sig
        o_dirs = _rwkv_scan(r_, kk, v_, lw, k_dir, kka, n_batch=n_batch, t_len=t_len, c_len=c_len,
                            precise=True)
        o_sum = (o_dirs[0] + o_dirs[1]).reshape(n_rows, RWKV_HEADS, RWKV_HEAD)
        mean = jnp.mean(o_sum, axis=-1, keepdims=True)
        var = jnp.mean(jnp.square(o_sum - mean), axis=-1, keepdims=True)
        o_n = ((o_sum - mean) * lax.rsqrt(var + RWKV_GN_EPS)).reshape(n_rows, BRANCH_W)
        o_n = o_n * rwkv_ln_g[l] + rwkv_ln_b[l]
        bonus_w = jnp.sum((r_[None] * k_dir * rwkv_r_k[l].reshape(-1)).reshape(2, n_rows, RWKV_HEADS, RWKV_HEAD),
                          axis=(0, 3))
        bonus = (bonus_w[..., None] * v_.reshape(n_rows, RWKV_HEADS, RWKV_HEAD)).reshape(n_rows, BRANCH_W)
        y_rwkv = ((o_n + bonus) * g_out).astype(BF16)

        pc = p[:, C_CONV:C_CONV + 3 * BRANCH_W]
        z = pc[:, BRANCH_W:2 * BRANCH_W] * pc[:, 2 * BRANCH_W:]
        z_prev, z_next = _shift_prev_next(z, n_x, t_len, c_len)
        y_conv = (pc[:, :BRANCH_W] * (conv_w[l, 0] * z_prev + conv_w[l, 1] * z + conv_w[l, 2] * z_next)).astype(BF16)

        rows = n_rows if need_ctx else n_x
        acc = _merge(p, (y_mla, y_rwkv, y_conv, y_diff), w_branch_b[l], gate_up_b[l], gate_b[l],
                     tm=tm, tn=512, rows=rows)
        xs_new = _matmul(acc, w_out_b[l], tm=tm, tn=1024, tk=1024, res=xs, mod=mod, gate_row=2,
                         mod_index=mod_index, rows=rows)

        hid = _nm_matmul(xs_new, norm2_g[l], w1_b[l], tm=tm, tn=512, out_dtype=BF16, mod=mod, shift_row=3,
                         scale_row=4, mod_index=mod_index, act="relu2", rows=rows)
        xs = _matmul(hid, w2_b[l], tm=tm, tn=1024, tk=1024, res=xs_new, mod=mod, gate_row=5,
                     mod_index=mod_index, rows=rows)

    out = _final_norm(xs, final_norm_g, rows=n_x, tm=tm)
    return out.reshape(n_batch, t_len, dm)
```

```python
import functools
import math

import jax
import jax.numpy as jnp
from jax import lax
from jax.experimental import pallas as pl
from jax.experimental.pallas import tpu as pltpu

F32 = jnp.float32
BF16 = jnp.bfloat16

D_MODEL = 4096
BRANCH_W = 1024
GRID_W = 64
ROPE_BASE = 10000.0
NORM_EPS = 1e-6
N_MOD = 6
LANES = 128
HALO = 8

MLA_HEADS = 8
MLA_NOPE = 128
MLA_ROPE = 64
MLA_V = 128
MLA_Q_LORA = 768
MLA_KV_LORA = 256
MLA_QK_PAD = 256

RWKV_HEAD = 64
RWKV_HEADS = 16
RWKV_LORA = 64
RWKV_GATE_LORA = 160
RWKV_GN_EPS = 64e-5
RWKV_CHUNK = 64
RWKV_LORA_PAD = 512

DIFF_HEADS = 8
DIFF_QK = 64
DIFF_V = 128
GATE_RANK = 256
MOD_RANK = 256

C_CQ = 0
C_CKV = 768
C_R = 1024
C_K = 2048
C_V = 3072
C_CB = 4096
C_CC = 5120
C_CU = 6144
C_DQ = 7168
C_DK = 8192
C_DV = 9216
C_LORA = 10240
C_GL = 10752
C_KROPE = 11008
P_COLS = 11264

VMEM_CAP = 56 * 1024 * 1024


def _cparams(sem, vmem_bytes):
    limit = int(min(VMEM_CAP, max(vmem_bytes * 1.5 + (4 << 20), 16 << 20)))
    return pltpu.CompilerParams(dimension_semantics=sem, vmem_limit_bytes=limit)


def _mod_index(i, tm, n_x_rows, t_len, n_batch):
    return jnp.where(i < n_x_rows // tm, i // (t_len // tm), n_batch)


def _dot(a, b):
    return jnp.dot(a.astype(BF16), b.astype(BF16), preferred_element_type=F32)


def _dot_nt(a, b):
    return lax.dot_general(a.astype(BF16), b.astype(BF16), (((1,), (1,)), ((), ())),
                           preferred_element_type=F32)


def _split3(x):
    hi = x.astype(BF16)
    r1 = x - hi.astype(F32)
    mid = r1.astype(BF16)
    lo = (r1 - mid.astype(F32)).astype(BF16)
    return hi, mid, lo


def _dot_exact_lhs(m_bf16, x):
    out = None
    for part in _split3(x):
        t = jnp.dot(m_bf16, part, preferred_element_type=F32)
        out = t if out is None else out + t
    return out


def _head_sums(x):
    r = lax.broadcasted_iota(jnp.int32, (LANES, LANES), 0)
    c = lax.broadcasted_iota(jnp.int32, (LANES, LANES), 1)
    ones_bd = jnp.where((r // RWKV_HEAD) == (c // RWKV_HEAD), 1.0, 0.0).astype(BF16)
    parts = _split3(x)
    cols = []
    for j in range(x.shape[1] // LANES):
        acc = None
        for part in parts:
            t = jnp.dot(part[:, j * LANES:(j + 1) * LANES], ones_bd, preferred_element_type=F32)
            acc = t if acc is None else acc + t
        cols.append(acc)
    return jnp.concatenate(cols, axis=1)


def _rope128(x, cos_t, sin_t):
    lane = lax.broadcasted_iota(jnp.int32, x.shape, 1)
    swapped = jnp.where((lane % 64) < 32, pltpu.roll(x, 96, 1), pltpu.roll(x, 32, 1))
    return x * cos_t + swapped * sin_t


def _matmul_kernel(*refs, nk, has_bias, has_res, gate_row, act):
    a_ref, w_ref = refs[0], refs[1]
    pos = 2
    bias_ref = res_ref = mod_ref = None
    if has_bias:
        bias_ref = refs[pos]; pos += 1
    if has_res:
        res_ref = refs[pos]; mod_ref = refs[pos + 1]; pos += 2
    o_ref, acc_ref = refs[pos], refs[pos + 1]
    k = pl.program_id(2)

    @pl.when(k == 0)
    def _():
        acc_ref[...] = jnp.zeros_like(acc_ref)

    acc_ref[...] += _dot(a_ref[...], w_ref[...])

    @pl.when(k == nk - 1)
    def _():
        y = acc_ref[...]
        if has_bias:
            y = y + bias_ref[...]
        if act == "relu2":
            y = jnp.square(jnp.maximum(y, 0.0))
        if has_res:
            y = res_ref[...] + mod_ref[0, gate_row:gate_row + 1, :] * y
        o_ref[...] = y.astype(o_ref.dtype)


def _matmul(a, w, *, tm, tn, tk, out_dtype=F32, bias=None, res=None, mod=None, gate_row=None,
            mod_index=None, act=None, rows=None, name=None):
    m = a.shape[0] if rows is None else rows
    kdim, n = w.shape
    assert a.shape[1] == kdim and m % tm == 0 and n % tn == 0 and kdim % tk == 0
    nk = kdim // tk
    in_specs = [pl.BlockSpec((tm, tk), lambda i, j, k: (i, k)),
                pl.BlockSpec((tk, tn), lambda i, j, k: (k, j))]
    args = [a, w]
    vmem = 2 * tm * tk * a.dtype.itemsize + 2 * tk * tn * w.dtype.itemsize + tm * tn * 4
    vmem += 2 * tm * tn * jnp.dtype(out_dtype).itemsize
    if bias is not None:
        in_specs.append(pl.BlockSpec((1, tn), lambda i, j, k: (0, j)))
        args.append(bias.reshape(1, n).astype(F32))
    if res is not None:
        in_specs.append(pl.BlockSpec((tm, tn), lambda i, j, k: (i, j)))
        in_specs.append(pl.BlockSpec((1, N_MOD, tn), lambda i, j, k: (mod_index(i), 0, j)))
        args += [res, mod]
        vmem += 2 * tm * tn * 4 + 2 * 8 * tn * 4
    kern = functools.partial(_matmul_kernel, nk=nk, has_bias=bias is not None,
                             has_res=res is not None, gate_row=gate_row, act=act)
    return pl.pallas_call(
        kern,
        grid=(m // tm, n // tn, nk),
        in_specs=in_specs,
        out_specs=pl.BlockSpec((tm, tn), lambda i, j, k: (i, j)),
        out_shape=jax.ShapeDtypeStruct((m, n), out_dtype),
        scratch_shapes=[pltpu.VMEM((tm, tn), F32)],
        compiler_params=_cparams(("parallel", "parallel", "arbitrary"), vmem),
        name=name,
    )(*args)


def _nm_matmul_kernel(*refs, has_mod, shift_row, scale_row, act):
    if has_mod:
        x_ref, g_ref, mod_ref, w_ref, o_ref, h_ref = refs
    else:
        x_ref, g_ref, w_ref, o_ref, h_ref = refs

    @pl.when(pl.program_id(1) == 0)
    def _():
        x = x_ref[...].astype(F32)
        y = x * lax.rsqrt(jnp.mean(x * x, axis=-1, keepdims=True) + NORM_EPS) * g_ref[...]
        if has_mod:
            y = y * (1.0 + mod_ref[0, scale_row:scale_row + 1, :]) + mod_ref[0, shift_row:shift_row + 1, :]
        h_ref[...] = y.astype(BF16)

    y = jnp.dot(h_ref[...], w_ref[...].astype(BF16), preferred_element_type=F32)
    if act == "relu2":
        y = jnp.square(jnp.maximum(y, 0.0))
    o_ref[...] = y.astype(o_ref.dtype)


def _nm_matmul(x, g, w, *, tm, tn, out_dtype, x_col_block=0, kdim=None, mod=None, shift_row=None,
               scale_row=None, mod_index=None, act=None, rows=None, name=None):
    m = x.shape[0] if rows is None else rows
    kdim = x.shape[1] if kdim is None else kdim
    n = w.shape[1]
    assert w.shape[0] == kdim and m % tm == 0 and n % tn == 0
    in_specs = [pl.BlockSpec((tm, kdim), lambda i, j: (i, x_col_block)),
                pl.BlockSpec((1, kdim), lambda i, j: (0, 0))]
    args = [x, g.reshape(1, kdim).astype(F32)]
    if mod is not None:
        in_specs.append(pl.BlockSpec((1, N_MOD, kdim), lambda i, j: (mod_index(i), 0, 0)))
        args.append(mod)
    in_specs.append(pl.BlockSpec((kdim, tn), lambda i, j: (0, j)))
    args.append(w)
    vmem = (2 * tm * kdim * x.dtype.itemsize + tm * kdim * 2 + 2 * kdim * tn * w.dtype.itemsize
            + 2 * tm * tn * jnp.dtype(out_dtype).itemsize + tm * tn * 4 + 4 * 8 * kdim * 4)
    kern = functools.partial(_nm_matmul_kernel, has_mod=mod is not None, shift_row=shift_row,
                             scale_row=scale_row, act=act)
    return pl.pallas_call(
        kern,
        grid=(m // tm, n // tn),
        in_specs=in_specs,
        out_specs=pl.BlockSpec((tm, tn), lambda i, j: (i, j)),
        out_shape=jax.ShapeDtypeStruct((m, n), out_dtype),
        scratch_shapes=[pltpu.VMEM((tm, kdim), BF16)],
        compiler_params=_cparams(("parallel", "arbitrary"), vmem),
        name=name,
    )(*args)


def _merge_kernel(gl_ref, y0_ref, y1_ref, y2_ref, y3_ref, wb_ref, gu_ref, gb_ref, o_ref):
    gl = gl_ref[...].astype(BF16)
    acc = None
    for i, y_ref in enumerate((y0_ref, y1_ref, y2_ref, y3_ref)):
        gate = jax.nn.sigmoid(jnp.dot(gl, gu_ref[i], preferred_element_type=F32) + gb_ref[i:i + 1, :])
        term = gate * jnp.dot(y_ref[...], wb_ref[i], preferred_element_type=F32)
        acc = term if acc is None else acc + term
    o_ref[...] = acc.astype(o_ref.dtype)


def _merge(p, ys, wb, gu, gb, *, tm, tn, rows):
    n = wb.shape[2]
    bw = wb.shape[1]
    gr = gu.shape[1]
    y_spec = pl.BlockSpec((tm, bw), lambda i, j: (i, 0))
    vmem = (2 * tm * gr * 4 + 4 * 2 * tm * bw * 2 + 2 * 4 * bw * tn * 2 + 2 * 4 * gr * tn * 2
            + 2 * tm * tn * 2 + 3 * tm * tn * 4)
    return pl.pallas_call(
        _merge_kernel,
        grid=(rows // tm, n // tn),
        in_specs=[pl.BlockSpec((tm, gr), lambda i, j: (i, C_GL // GATE_RANK)),
                  y_spec, y_spec, y_spec, y_spec,
                  pl.BlockSpec((4, bw, tn), lambda i, j: (0, 0, j)),
                  pl.BlockSpec((4, gr, tn), lambda i, j: (0, 0, j)),
                  pl.BlockSpec((4, tn), lambda i, j: (0, j))],
        out_specs=pl.BlockSpec((tm, tn), lambda i, j: (i, j)),
        out_shape=jax.ShapeDtypeStruct((rows, n), BF16),
        compiler_params=_cparams(("parallel", "arbitrary"), vmem),
        name="merge",
    )(p, *ys, wb, gu, gb)


def _softmax_parts(s_list, scale):
    m = None
    for s in s_list:
        sm = jnp.max(s, axis=-1, keepdims=True)
        m = sm if m is None else jnp.maximum(m, sm)
    e_list = [jnp.exp((s - m) * scale) for s in s_list]
    l = None
    for e in e_list:
        es = jnp.sum(e, axis=-1, keepdims=True)
        l = es if l is None else l + es
    return e_list, l


def _attend(q, ks, vs, scale):
    e_list, l = _softmax_parts([_dot_nt(q, k) for k in ks], scale)
    o = None
    for e, v in zip(e_list, vs):
        t = jnp.dot(e.astype(BF16), v, preferred_element_type=F32)
        o = t if o is None else o + t
    return o / l


def _mla_kernel(cos_ref, sin_ref, q_ref, knx_ref, knc_ref, krx_ref, krc_ref, vx_ref, vc_ref, o_ref,
                kx_s, kc_s, *, nq, tq, scale):
    i = pl.program_id(2)

    @pl.when(i == 0)
    def _():
        kx_s[:, :MLA_NOPE] = knx_ref[...]
        kx_s[:, MLA_NOPE:] = _rope128(krx_ref[...], cos_ref[...], sin_ref[...]).astype(BF16)
        kc_s[:, :MLA_NOPE] = knc_ref[...]
        kc_s[:, MLA_NOPE:] = krc_ref[...].astype(BF16)

    @pl.when(i < nq)
    def _():
        row0 = pl.multiple_of(jnp.minimum(i, nq - 1) * tq, tq)
        q = q_ref[...]
        q_rope = _rope128(q[:, MLA_NOPE:].astype(F32), cos_ref[pl.ds(row0, tq), :], sin_ref[pl.ds(row0, tq), :])
        q = jnp.concatenate([q[:, :MLA_NOPE], q_rope.astype(BF16)], axis=1)
        o = _attend(q, (kc_s[...], kx_s[...]), (vc_ref[...], vx_ref[...]), scale)
        o_ref[...] = o.astype(o_ref.dtype)

    @pl.when(i >= nq)
    def _():
        o = _attend(q_ref[...], (kc_s[...],), (vc_ref[...],), scale)
        o_ref[...] = o.astype(o_ref.dtype)


def _diff_finish(o1, o2, lam_ref, g_ref, out_scale):
    o = o1 - lam_ref[...] * o2
    y = o * lax.rsqrt(jnp.mean(o * o, axis=-1, keepdims=True) + NORM_EPS) * g_ref[...]
    return y * out_scale


def _diff_kernel(cos_ref, sin_ref, lam_ref, g_ref, q_ref, kx_ref, kc_ref, vx_ref, vc_ref, o_ref,
                 kx_s, kc_s, vx_s, vc_s, *, nq, tq, scale, out_scale):
    i = pl.program_id(2)

    @pl.when(i == 0)
    def _():
        kx_s[...] = _rope128(kx_ref[...], cos_ref[...], sin_ref[...]).astype(BF16)
        kc_s[...] = kc_ref[...].astype(BF16)
        vx_s[...] = vx_ref[...].astype(BF16)
        vc_s[...] = vc_ref[...].astype(BF16)

    def halves(q):
        lane = lax.broadcasted_iota(jnp.int32, q.shape, 1)
        return jnp.where(lane < DIFF_QK, q, 0.0).astype(BF16), jnp.where(lane < DIFF_QK, 0.0, q).astype(BF16)

    @pl.when(i < nq)
    def _():
        row0 = pl.multiple_of(jnp.minimum(i, nq - 1) * tq, tq)
        q1, q2 = halves(_rope128(q_ref[...], cos_ref[pl.ds(row0, tq), :], sin_ref[pl.ds(row0, tq), :]))
        ks, vs = (kc_s[...], kx_s[...]), (vc_s[...], vx_s[...])
        y = _diff_finish(_attend(q1, ks, vs, scale), _attend(q2, ks, vs, scale), lam_ref, g_ref, out_scale)
        o_ref[...] = y.astype(o_ref.dtype)

    @pl.when(i >= nq)
    def _():
        q1, q2 = halves(q_ref[...])
        ks, vs = (kc_s[...],), (vc_s[...],)
        y = _diff_finish(_attend(q1, ks, vs, scale), _attend(q2, ks, vs, scale), lam_ref, g_ref, out_scale)
        o_ref[...] = y.astype(o_ref.dtype)


def _query_row_block(b, i, nq, nqc, n_x, tq):
    return jnp.where(i < nq, b * nq + i, n_x // tq + b * nqc + (i - nq))


def _mla_attention(q, kv, p, cos_t, sin_t, *, n_batch, t_len, c_len, tq, with_ctx):
    n_x = n_batch * t_len
    nq = t_len // tq
    nqc = c_len // tq if with_ctx else 0
    out_rows = n_x + (n_batch * c_len if with_ctx else 0)
    cblk0 = n_x // c_len
    qmap = lambda b, h, i: (_query_row_block(b, i, nq, nqc, n_x, tq), h)
    full = lambda b, h, i: (0, 0)
    vmem = (4 * t_len * LANES * 4 + 2 * tq * 256 * 2 + 2 * (t_len + c_len) * LANES * (2 + 4 + 2)
            + (t_len + c_len) * 256 * 2 + 2 * tq * LANES * 2 + 6 * tq * (t_len + c_len) * 4)
    return pl.pallas_call(
        functools.partial(_mla_kernel, nq=nq, tq=tq, scale=(MLA_NOPE + MLA_ROPE) ** -0.5),
        grid=(n_batch, MLA_HEADS, nq + nqc),
        in_specs=[pl.BlockSpec((t_len, LANES), full), pl.BlockSpec((t_len, LANES), full),
                  pl.BlockSpec((tq, MLA_QK_PAD), qmap),
                  pl.BlockSpec((t_len, MLA_NOPE), lambda b, h, i: (b, h)),
                  pl.BlockSpec((c_len, MLA_NOPE), lambda b, h, i: (cblk0 + b, h)),
                  pl.BlockSpec((t_len, LANES), lambda b, h, i: (b, C_KROPE // LANES)),
                  pl.BlockSpec((c_len, LANES), lambda b, h, i: (cblk0 + b, C_KROPE // LANES)),
                  pl.BlockSpec((t_len, MLA_V), lambda b, h, i: (b, MLA_HEADS + h)),
                  pl.BlockSpec((c_len, MLA_V), lambda b, h, i: (cblk0 + b, MLA_HEADS + h))],
        out_specs=pl.BlockSpec((tq, MLA_V), qmap),
        out_shape=jax.ShapeDtypeStruct((out_rows, MLA_HEADS * MLA_V), BF16),
        scratch_shapes=[pltpu.VMEM((t_len, MLA_QK_PAD), BF16), pltpu.VMEM((c_len, MLA_QK_PAD), BF16)],
        compiler_params=_cparams(("parallel", "parallel", "arbitrary"), vmem),
        name="mla_attention",
    )(cos_t, sin_t, q, kv, kv, p, p, kv, kv)


def _diff_attention(p, cos_t, sin_t, lam_row, g_row, *, n_batch, t_len, c_len, tq, with_ctx, out_scale):
    n_x = n_batch * t_len
    nq = t_len // tq
    nqc = c_len // tq if with_ctx else 0
    out_rows = n_x + (n_batch * c_len if with_ctx else 0)
    cblk0 = n_x // c_len
    full = lambda b, h, i: (0, 0)
    qblk, kblk, vblk = C_DQ // LANES, C_DK // LANES, C_DV // LANES
    vmem = (4 * t_len * LANES * 4 + 2 * tq * LANES * 4 + 4 * (t_len + c_len) * LANES * 4
            + 2 * (t_len + c_len) * LANES * 2 + 2 * tq * LANES * 2 + 8 * tq * (t_len + c_len) * 4)
    return pl.pallas_call(
        functools.partial(_diff_kernel, nq=nq, tq=tq, scale=DIFF_QK ** -0.5, out_scale=out_scale),
        grid=(n_batch, DIFF_HEADS, nq + nqc),
        in_specs=[pl.BlockSpec((t_len, LANES), full), pl.BlockSpec((t_len, LANES), full),
                  pl.BlockSpec((1, DIFF_V), full), pl.BlockSpec((1, DIFF_V), full),
                  pl.BlockSpec((tq, LANES), lambda b, h, i: (_query_row_block(b, i, nq, nqc, n_x, tq), qblk + h)),
                  pl.BlockSpec((t_len, LANES), lambda b, h, i: (b, kblk + h)),
                  pl.BlockSpec((c_len, LANES), lambda b, h, i: (cblk0 + b, kblk + h)),
                  pl.BlockSpec((t_len, LANES), lambda b, h, i: (b, vblk + h)),
                  pl.BlockSpec((c_len, LANES), lambda b, h, i: (cblk0 + b, vblk + h))],
        out_specs=pl.BlockSpec((tq, DIFF_V), lambda b, h, i: (_query_row_block(b, i, nq, nqc, n_x, tq), h)),
        out_shape=jax.ShapeDtypeStruct((out_rows, DIFF_HEADS * DIFF_V), BF16),
        scratch_shapes=[pltpu.VMEM((t_len, LANES), BF16), pltpu.VMEM((c_len, LANES), BF16),
                        pltpu.VMEM((t_len, LANES), BF16), pltpu.VMEM((c_len, LANES), BF16)],
        compiler_params=_cparams(("parallel", "parallel", "arbitrary"), vmem),
        name="diff_attention",
    )(cos_t, sin_t, lam_row, g_row, p, p, p, p, p)


def _prep_kernel(r_ref, k_ref, v_ref, cb_ref, cc_ref, cu_ref, lo_ref, hp_ref, hn_ref,
                 mu_ref, mul_ref, kk_w_ref, ka_w_ref, rk_w_ref, w0_ref, a0_ref, w2_ref, a2_ref, g2_ref,
                 cw_ref,
                 r_o, kk_o, v_o, lw_o, kd_o, ka_o, bonus_o, g_o, conv_o, *, tr, n_x, t_len, c_len):
    i = pl.program_id(0)
    g0 = i * tr
    seq = jnp.where(g0 < n_x, t_len, c_len)
    has_prev = ((g0 % seq) != 0).astype(F32)
    has_next = (((g0 + tr) % seq) != 0).astype(F32)
    row = lax.broadcasted_iota(jnp.int32, (tr, 1), 0)

    def neighbours(x, col0):
        width = x.shape[1]
        before = hp_ref[HALO - 1:HALO, col0:col0 + width] * has_prev
        after = hn_ref[0:1, col0:col0 + width] * has_next
        prev = jnp.where(row == 0, before, pltpu.roll(x, 1, 0))
        nxt = jnp.where(row == tr - 1, after, pltpu.roll(x, tr - 1, 0))
        return prev, nxt

    def shifted(x, col0, mu0, mu1):
        prev, nxt = neighbours(x, col0)
        return x + mu0 * (prev - x) + mu1 * (nxt - x)

    r = shifted(r_ref[...], C_R, mu_ref[0:1, :BRANCH_W], mu_ref[1:2, :BRANCH_W])
    k = shifted(k_ref[...], C_K, mu_ref[0:1, BRANCH_W:2 * BRANCH_W], mu_ref[1:2, BRANCH_W:2 * BRANCH_W])
    v = shifted(v_ref[...], C_V, mu_ref[0:1, 2 * BRANCH_W:], mu_ref[1:2, 2 * BRANCH_W:])
    lo = shifted(lo_ref[...], C_LORA, mul_ref[0:1, :], mul_ref[1:2, :])
    wd = jnp.tanh(lo[:, :2 * RWKV_LORA])
    ad = lo[:, 2 * RWKV_LORA:4 * RWKV_LORA]
    gd = jax.nn.sigmoid(lo[:, 4 * RWKV_LORA:])
    w_pre = _dot(wd, w2_ref[...]) + w0_ref[...]
    a_sig = jax.nn.sigmoid(_dot(ad, a2_ref[...]) + a0_ref[...])
    g_o[...] = _dot(gd, g2_ref[...])
    w_log = -(jnp.maximum(-w_pre, 0.0) + jnp.log(1.0 + jnp.exp(-jnp.abs(w_pre)))) - 0.5
    lw = -jnp.exp(w_log)
    kkf = k * kk_w_ref[...]
    kk = kkf * lax.rsqrt(_head_sums(kkf * kkf) + 1e-12)
    r_o[...] = r
    kk_o[...] = kk
    v_o[...] = v
    k_sum = None
    for d in range(2):
        a_d = a_sig[:, d * BRANCH_W:(d + 1) * BRANCH_W]
        k_d = k * (1.0 + (a_d - 1.0) * ka_w_ref[...])
        lw_o[d] = lw[:, d * BRANCH_W:(d + 1) * BRANCH_W]
        kd_o[d] = k_d
        ka_o[d] = kk * a_d
        k_sum = k_d if k_sum is None else k_sum + k_d
    bonus_o[...] = _head_sums(r * k_sum * rk_w_ref[...]) * v

    z = cc_ref[...] * cu_ref[...]
    z_before = hp_ref[HALO - 1:HALO, C_CC:C_CC + BRANCH_W] * hp_ref[HALO - 1:HALO, C_CU:C_CU + BRANCH_W] * has_prev
    z_after = hn_ref[0:1, C_CC:C_CC + BRANCH_W] * hn_ref[0:1, C_CU:C_CU + BRANCH_W] * has_next
    z_prev = jnp.where(row == 0, z_before, pltpu.roll(z, 1, 0))
    z_next = jnp.where(row == tr - 1, z_after, pltpu.roll(z, tr - 1, 0))
    y = cb_ref[...] * (cw_ref[0:1, :] * z_prev + cw_ref[1:2, :] * z + cw_ref[2:3, :] * z_next)
    conv_o[...] = y.astype(conv_o.dtype)


def _mixer_prep(p, mu_rkv, mu_lora, kk_w, ka_w, rk_w, w0, a0, w2bd, a2bd, g2p, conv_w, *, tr, n_x, t_len,
                c_len):
    n_rows = p.shape[0]
    bw = BRANCH_W
    last_halo = n_rows // HALO - 1
    col = lambda c: (lambda i: (i, c))
    const = lambda i: (0, 0)
    main = [pl.BlockSpec((tr, bw), col(C_R // bw)), pl.BlockSpec((tr, bw), col(C_K // bw)),
            pl.BlockSpec((tr, bw), col(C_V // bw)), pl.BlockSpec((tr, bw), col(C_CB // bw)),
            pl.BlockSpec((tr, bw), col(C_CC // bw)), pl.BlockSpec((tr, bw), col(C_CU // bw)),
            pl.BlockSpec((tr, RWKV_LORA_PAD), col(C_LORA // RWKV_LORA_PAD)),
            pl.BlockSpec((HALO, P_COLS), lambda i: (jnp.maximum(i * (tr // HALO) - 1, 0), 0)),
            pl.BlockSpec((HALO, P_COLS), lambda i: (jnp.minimum((i + 1) * (tr // HALO), last_halo), 0))]
    params = [mu_rkv, mu_lora, kk_w, ka_w, rk_w, w0, a0, w2bd, a2bd, g2p, conv_w]
    param_specs = [pl.BlockSpec(a.shape, const) for a in params]
    row_spec = pl.BlockSpec((tr, bw), lambda i: (i, 0))
    dir_spec = pl.BlockSpec((2, tr, bw), lambda i: (0, i, 0))
    f32_rows = jax.ShapeDtypeStruct((n_rows, bw), F32)
    f32_dirs = jax.ShapeDtypeStruct((2, n_rows, bw), F32)
    vmem = 2 * (7 * tr * bw * 4 + 2 * HALO * P_COLS * 4 + 12 * tr * bw * 4) + 30 * tr * bw * 4
    return pl.pallas_call(
        functools.partial(_prep_kernel, tr=tr, n_x=n_x, t_len=t_len, c_len=c_len),
        grid=(n_rows // tr,),
        in_specs=main + param_specs,
        out_specs=[row_spec, row_spec, row_spec, dir_spec, dir_spec, dir_spec, row_spec, row_spec, row_spec],
        out_shape=[f32_rows, f32_rows, f32_rows, f32_dirs, f32_dirs, f32_dirs, f32_rows, f32_rows,
                   jax.ShapeDtypeStruct((n_rows, bw), BF16)],
        compiler_params=_cparams(("parallel",), vmem),
        name="mixer_prep",
    )(p, p, p, p, p, p, p, p, p, *params)


def _rwkv_kernel(r_ref, kk_ref, v_ref, lw_ref, kd_ref, ka_ref, o_ref, s_ref):
    C = RWKV_CHUNK
    W = 2 * RWKV_HEAD
    n_pairs = RWKV_HEADS // 2
    d = pl.program_id(1)
    sign = 1 - 2 * d

    @pl.when(pl.program_id(2) == 0)
    def _():
        s_ref[...] = jnp.zeros_like(s_ref)

    mm, mm_nt = _dot, _dot_nt

    rc = lax.broadcasted_iota(jnp.int32, (C, C), 0)
    cc = lax.broadcasted_iota(jnp.int32, (C, C), 1)
    m_incl = jnp.where((rc - cc) * sign >= 0, 1.0, 0.0).astype(BF16)
    lw = lw_ref[0]
    cum = _dot_exact_lhs(m_incl, lw)
    tot = jnp.sum(lw, axis=0, keepdims=True)
    e_incl = jnp.exp(cum)
    e_excl = jnp.exp(cum - lw)
    e_ninc = jnp.exp(-cum)
    e_rem = jnp.exp(tot - cum)
    e_tot = jnp.exp(tot)

    kk = kk_ref[...]
    ka = ka_ref[0]
    kd = kd_ref[0]
    a_t = kk * e_excl
    b_t = -ka * e_ninc
    k_t = kd * e_ninc
    r_t = r_ref[...] * e_incl
    b_h = -ka * e_rem
    k_h = kd * e_rem
    v = v_ref[...]

    lane = lax.broadcasted_iota(jnp.int32, (C, W), 1)
    first = lane < RWKV_HEAD
    r2 = lax.broadcasted_iota(jnp.int32, (W, W), 0)
    c2 = lax.broadcasted_iota(jnp.int32, (W, W), 1)
    same = (r2 // C) == (c2 // C)
    d2 = ((r2 % C) - (c2 % C)) * sign
    strict_bd = same & (d2 > 0)
    incl_bd = same & (d2 >= 0)
    eye = jnp.where(r2 == c2, 1.0, 0.0).astype(F32)

    def stack2(x):
        return jnp.concatenate([jnp.where(first, x, 0.0), jnp.where(first, 0.0, x)], axis=0)

    pairs = range(n_pairs)
    sls = [slice(W * p, W * (p + 1)) for p in pairs]
    a2 = [stack2(a_t[:, sl]).astype(BF16) for sl in sls]
    rr2 = [stack2(r_t[:, sl]).astype(BF16) for sl in sls]
    bk2 = [jnp.concatenate([stack2(b_t[:, sl]), stack2(k_t[:, sl])], axis=0).astype(BF16) for sl in sls]
    v2 = [stack2(v[:, sl]).astype(BF16) for sl in sls]
    g = [mm_nt(jnp.concatenate([a2[p], rr2[p]], axis=0), bk2[p]) for p in pairs]
    l_pow = [jnp.where(strict_bd, g[p][:W, :W], 0.0) for p in pairs]
    m_ak = [jnp.where(strict_bd, g[p][:W, W:], 0.0).astype(BF16) for p in pairs]
    a_r = [jnp.where(jnp.concatenate([incl_bd, incl_bd], axis=1), g[p][W:], 0.0).astype(BF16) for p in pairs]
    mv = [mm(m_ak[p], v2[p]) for p in pairs]
    t_inv = [eye + l_pow[p] for p in pairs]
    for _ in range(int(math.log2(C)) - 1):
        lb = [l_pow[p].astype(BF16) for p in pairs]
        l_pow = [mm(lb[p], lb[p]) for p in pairs]
        t_inv = [t_inv[p] + mm(t_inv[p], l_pow[p]) for p in pairs]
    wu = [mm(t_inv[p], jnp.concatenate([a2[p], mv[p].astype(BF16)], axis=1)) for p in pairs]
    s_bd = [s_ref[p] for p in pairs]
    wr = [mm(jnp.concatenate([wu[p][:, :W].astype(BF16), rr2[p]], axis=0), s_bd[p]) for p in pairs]
    zv = [jnp.concatenate([(wr[p][:W] + wu[p][:, W:]).astype(BF16), v2[p]], axis=0) for p in pairs]
    o2 = [wr[p][W:] + mm(a_r[p], zv[p]) for p in pairs]
    for p in pairs:
        o_ref[0, :, sls[p]] = o2[p][:C] + o2[p][C:]
    for p in pairs:
        bk_h = jnp.concatenate([stack2(b_h[:, sls[p]]), stack2(k_h[:, sls[p]])], axis=0)
        tot_col = jnp.broadcast_to(e_tot[:, sls[p]], (W, W)).T
        s_ref[p] = tot_col * s_bd[p] + mm(bk_h.T, zv[p])


def _rwkv_scan(r, kk, v, lw, kd, ka, *, n_batch, t_len, c_len):
    C = RWKV_CHUNK
    rows, width = r.shape
    nct, nxt = c_len // C, t_len // C
    ctx_blk0 = n_batch * t_len // C

    def blk(b, d, s):
        j_c = jnp.where(d == 0, s, nct - 1 - s)
        j_x = jnp.where(d == 0, s - nct, nxt - 1 - (s - nct))
        return jnp.where(s < nct, ctx_blk0 + b * nct + j_c, b * nxt + j_x)

    shared = pl.BlockSpec((C, width), lambda b, d, s: (blk(b, d, s), 0))
    per_dir = pl.BlockSpec((1, C, width), lambda b, d, s: (d, blk(b, d, s), 0))
    vmem = 2 * 7 * C * width * 4 + (RWKV_HEADS // 2) * 128 * 128 * 4 + 64 * C * width * 4
    return pl.pallas_call(
        _rwkv_kernel,
        grid=(n_batch, 2, nct + nxt),
        in_specs=[shared, shared, shared, per_dir, per_dir, per_dir],
        out_specs=per_dir,
        out_shape=jax.ShapeDtypeStruct((2, rows, width), F32),
        scratch_shapes=[pltpu.VMEM((RWKV_HEADS // 2, 128, 128), F32)],
        compiler_params=_cparams(("parallel", "parallel", "arbitrary"), vmem),
        name="rwkv_scan",
    )(r, kk, v, lw, kd, ka)


def _rwkv_readout_kernel(o_ref, bonus_ref, g_ref, lng_ref, lnb_ref, y_ref):
    o = o_ref[0] + o_ref[1]
    mean = _head_sums(o) * (1.0 / RWKV_HEAD)
    cen = o - mean
    var = _head_sums(cen * cen) * (1.0 / RWKV_HEAD)
    o_n = cen * lax.rsqrt(var + RWKV_GN_EPS) * lng_ref[...] + lnb_ref[...]
    y_ref[...] = ((o_n + bonus_ref[...]) * g_ref[...]).astype(y_ref.dtype)


def _rwkv_readout(o_dirs, bonus, g_out, ln_g, ln_b, *, tr):
    n_rows, bw = bonus.shape
    row_spec = pl.BlockSpec((tr, bw), lambda i: (i, 0))
    const = pl.BlockSpec((1, bw), lambda i: (0, 0))
    return pl.pallas_call(
        _rwkv_readout_kernel,
        grid=(n_rows // tr,),
        in_specs=[pl.BlockSpec((2, tr, bw), lambda i: (0, i, 0)), row_spec, row_spec, const, const],
        out_specs=row_spec,
        out_shape=jax.ShapeDtypeStruct((n_rows, bw), BF16),
        compiler_params=_cparams(("parallel",), 2 * 5 * tr * bw * 4 + 16 * tr * bw * 4),
        name="rwkv_readout",
    )(o_dirs, bonus, g_out, ln_g.reshape(1, bw), ln_b.reshape(1, bw))


def _rmsnorm_kernel(x_ref, g_ref, o_ref):
    x = x_ref[...]
    o_ref[...] = x * lax.rsqrt(jnp.mean(x * x, axis=-1, keepdims=True) + NORM_EPS) * g_ref[...]


def _final_norm(xs, g, *, rows, tm):
    dm = xs.shape[1]
    return pl.pallas_call(
        _rmsnorm_kernel,
        grid=(rows // tm,),
        in_specs=[pl.BlockSpec((tm, dm), lambda i: (i, 0)), pl.BlockSpec((1, dm), lambda i: (0, 0))],
        out_specs=pl.BlockSpec((tm, dm), lambda i: (i, 0)),
        out_shape=jax.ShapeDtypeStruct((rows, dm), F32),
        compiler_params=_cparams(("parallel",), 4 * tm * dm * 4),
        name="final_norm",
    )(xs, g.reshape(1, dm))


def _rope_tables128(n_tokens):
    rows = n_tokens // GRID_W
    row = jnp.repeat(jnp.arange(rows, dtype=F32), GRID_W)
    col = jnp.tile(jnp.arange(GRID_W, dtype=F32), rows)
    n_freq = 64 // 4
    inv = ROPE_BASE ** (-jnp.arange(n_freq, dtype=F32) / n_freq)
    ang = jnp.concatenate([row[:, None] * inv, col[:, None] * inv], axis=-1)
    cos, sin = jnp.cos(ang), jnp.sin(ang)
    return jnp.concatenate([cos, cos, cos, cos], axis=-1), jnp.concatenate([-sin, sin, -sin, sin], axis=-1)


def _block_diag2(w2):
    z = jnp.zeros_like(w2[0])
    return jnp.concatenate([jnp.concatenate([w2[0], z], axis=1), jnp.concatenate([z, w2[1]], axis=1)], axis=0)


def kernel(x, c, ctx, c_ctx, norm1_g, norm2_g, mod_down, mod_up, mod_b, w_in, mla_q_norm_g, mla_w_uq,
           mla_kv_norm_g, mla_w_ukv, rwkv_mu, rwkv_w0, rwkv_w2, rwkv_a0, rwkv_a2, rwkv_g2, rwkv_k_k,
           rwkv_k_a, rwkv_r_k, rwkv_ln_g, rwkv_ln_b, conv_w, diff_lambda, diff_norm_g, w_branch, gate_down,
           gate_up, gate_b, w_out, mlp_w1, mlp_w2, final_norm_g):
    n_batch, t_len, dm = x.shape
    c_len = ctx.shape[1]
    depth = w_in.shape[0]
    bw = BRANCH_W
    n_x = n_batch * t_len
    n_c = n_batch * c_len
    n_rows = n_x + n_c
    tm = 512 if (t_len % 512 == 0 and n_c % 512 == 0) else 256
    tq = min(256, c_len)
    tr = min(256, c_len)
    assert dm == D_MODEL and t_len % tm == 0 and n_c % tm == 0 and n_x % c_len == 0
    assert t_len % tq == 0 and c_len % tq == 0 and c_len % RWKV_CHUNK == 0 and t_len % c_len == 0
    mod_index = functools.partial(_mod_index, tm=tm, n_x_rows=n_x, t_len=t_len, n_batch=n_batch)

    zeros = lambda *s: jnp.zeros(s, F32)
    rw0 = 1088
    cv0 = rw0 + 3 * bw + 4 * RWKV_LORA + RWKV_GATE_LORA
    lora_w = 4 * RWKV_LORA + RWKV_GATE_LORA
    w_in_p = jnp.concatenate(
        [w_in[:, :, :1024], w_in[:, :, rw0:rw0 + 3 * bw], w_in[:, :, cv0:],
         w_in[:, :, rw0 + 3 * bw:cv0], zeros(depth, dm, RWKV_LORA_PAD - lora_w), gate_down,
         w_in[:, :, 1024:rw0], zeros(depth, dm, P_COLS - C_KROPE - MLA_ROPE)], axis=-1).astype(BF16)
    w_uq_p = jnp.pad(mla_w_uq.reshape(depth, MLA_Q_LORA, MLA_HEADS, MLA_NOPE + MLA_ROPE),
                     ((0, 0), (0, 0), (0, 0), (0, MLA_QK_PAD - MLA_NOPE - MLA_ROPE))
                     ).reshape(depth, MLA_Q_LORA, MLA_HEADS * MLA_QK_PAD).astype(BF16)
    w_ukv_r = mla_w_ukv.reshape(depth, MLA_KV_LORA, MLA_HEADS, MLA_NOPE + MLA_V)
    w_ukv_p = jnp.concatenate([w_ukv_r[..., :MLA_NOPE].reshape(depth, MLA_KV_LORA, -1),
                               w_ukv_r[..., MLA_NOPE:].reshape(depth, MLA_KV_LORA, -1)], axis=-1).astype(BF16)
    w_branch_b = w_branch.astype(BF16)
    gate_up_b = jnp.moveaxis(gate_up, 2, 1).astype(BF16)
    w_out_b = w_out.astype(BF16)
    w1_b = mlp_w1.astype(BF16)
    w2_b = mlp_w2.astype(BF16)
    g2_p = jnp.pad(rwkv_g2, ((0, 0), (0, RWKV_LORA_PAD - 4 * RWKV_LORA - RWKV_GATE_LORA), (0, 0))).astype(BF16)
    mu_lora = jnp.pad(rwkv_mu[:, :, 3 * bw:], ((0, 0), (0, 0), (0, RWKV_LORA_PAD - lora_w)))

    cond = jnp.concatenate([c, c_ctx[None, :], zeros(16 - n_batch - 1, dm)], axis=0)
    cond = jax.nn.silu(cond)
    mods = []
    for l in range(depth):
        low = _matmul(cond, mod_down[l], tm=16, tn=MOD_RANK, tk=dm, name="mod_down")
        up = _matmul(low, mod_up[l], tm=16, tn=2048, tk=MOD_RANK, bias=mod_b[l], name="mod_up")
        mods.append(up.reshape(16, N_MOD, dm))

    cos_t, sin_t = _rope_tables128(t_len)
    xs = jnp.concatenate([x.reshape(n_x, dm), ctx.reshape(n_c, dm)], axis=0)

    for l in range(depth):
        need_ctx = l < depth - 1
        mod = mods[l]
        lam_init = 0.8 - 0.6 * math.exp(-0.3 * l)
        lq1, lk1, lq2, lk2 = diff_lambda[l]
        lam = jnp.exp(jnp.sum(lq1 * lk1)) - jnp.exp(jnp.sum(lq2 * lk2)) + lam_init
        lam_row = jnp.full((1, DIFF_V), 1.0, F32) * lam

        p = _nm_matmul(xs, norm1_g[l], w_in_p[l], tm=tm, tn=512, out_dtype=F32, mod=mod, shift_row=0,
                       scale_row=1, mod_index=mod_index, name="in_proj")

        q = _nm_matmul(p, mla_q_norm_g[l], w_uq_p[l], tm=tm, tn=512, out_dtype=BF16,
                       x_col_block=C_CQ // MLA_Q_LORA, kdim=MLA_Q_LORA, name="mla_q")
        kv = _nm_matmul(p, mla_kv_norm_g[l], w_ukv_p[l], tm=tm, tn=512, out_dtype=BF16,
                        x_col_block=C_CKV // MLA_KV_LORA, kdim=MLA_KV_LORA, name="mla_kv")
        y_mla = _mla_attention(q, kv, p, cos_t, sin_t, n_batch=n_batch, t_len=t_len, c_len=c_len, tq=tq,
                               with_ctx=need_ctx)

        y_diff = _diff_attention(p, cos_t, sin_t, lam_row, diff_norm_g[l].reshape(1, DIFF_V),
                                 n_batch=n_batch, t_len=t_len, c_len=c_len, tq=tq, with_ctx=need_ctx,
                                 out_scale=1.0 - lam_init)

        r_, kk, v_, lw, k_dir, kka, bonus, g_out, y_conv = _mixer_prep(
            p, rwkv_mu[l, :, :3 * bw], mu_lora[l], rwkv_k_k[l].reshape(1, bw), rwkv_k_a[l].reshape(1, bw),
            rwkv_r_k[l].reshape(1, bw), rwkv_w0[l].reshape(1, 2 * bw), rwkv_a0[l].reshape(1, 2 * bw),
            _block_diag2(rwkv_w2[l]).astype(BF16), _block_diag2(rwkv_a2[l]).astype(BF16), g2_p[l],
            conv_w[l], tr=min(128, tr), n_x=n_x, t_len=t_len, c_len=c_len)
        o_dirs = _rwkv_scan(r_, kk, v_, lw, k_dir, kka, n_batch=n_batch, t_len=t_len, c_len=c_len)
        y_rwkv = _rwkv_readout(o_dirs, bonus, g_out, rwkv_ln_g[l], rwkv_ln_b[l], tr=tr)

        rows = n_rows if need_ctx else n_x
        acc = _merge(p, (y_mla, y_rwkv, y_conv, y_diff), w_branch_b[l], gate_up_b[l], gate_b[l],
                     tm=tm, tn=512, rows=rows)
        xs_new = _matmul(acc, w_out_b[l], tm=tm, tn=1024, tk=1024, res=xs, mod=mod, gate_row=2,
                         mod_index=mod_index, rows=rows, name="out_proj")

        hid = _nm_matmul(xs_new, norm2_g[l], w1_b[l], tm=tm, tn=512, out_dtype=BF16, mod=mod, shift_row=3,
                         scale_row=4, mod_index=mod_index, act="relu2", rows=rows, name="mlp_up")
        xs = _matmul(hid, w2_b[l], tm=tm, tn=1024, tk=1024, res=xs_new, mod=mod, gate_row=5,
                     mod_index=mod_index, rows=rows, name="mlp_down")

    out = _final_norm(xs, final_norm_g, rows=n_x, tm=tm)
    return out.reshape(n_batch, t_len, dm)
```

```python
import functools
import math

import jax
import jax.numpy as jnp
from jax import lax
from jax.experimental import pallas as pl
from jax.experimental.pallas import tpu as pltpu

F32 = jnp.float32
BF16 = jnp.bfloat16

D_MODEL = 4096
BRANCH_W = 1024
GRID_W = 64
ROPE_BASE = 10000.0
NORM_EPS = 1e-6
N_MOD = 6
LANES = 128
HALO = 8
LOG2_E = 1.4426950408889634

MLA_HEADS = 8
MLA_NOPE = 128
MLA_ROPE = 64
MLA_V = 128
MLA_Q_LORA = 768
MLA_KV_LORA = 256
MLA_QK_PAD = 256

RWKV_HEAD = 64
RWKV_HEADS = 16
RWKV_LORA = 64
RWKV_GATE_LORA = 160
RWKV_GN_EPS = 64e-5
RWKV_CHUNK = 64
RWKV_LORA_PAD = 512

DIFF_HEADS = 8
DIFF_QK = 64
DIFF_V = 128
GATE_RANK = 256
MOD_RANK = 256

C_CQ = 0
C_CKV = 768
C_R = 1024
C_K = 2048
C_V = 3072
C_CB = 4096
C_CC = 5120
C_CU = 6144
C_DQ = 7168
C_DK = 8192
C_DV = 9216
C_LORA = 10240
C_GL = 10752
C_KROPE = 11008
P_COLS = 11264

VMEM_CAP = 56 * 1024 * 1024


def _cparams(sem, vmem_bytes):
    limit = int(min(VMEM_CAP, max(vmem_bytes * 1.5 + (4 << 20), 16 << 20)))
    return pltpu.CompilerParams(dimension_semantics=sem, vmem_limit_bytes=limit)


def _mod_index(i, tm, n_x_rows, t_len, n_batch):
    return jnp.where(i < n_x_rows // tm, i // (t_len // tm), n_batch)


def _dot(a, b):
    return jnp.dot(a.astype(BF16), b.astype(BF16), preferred_element_type=F32)


def _dot_nt(a, b):
    return lax.dot_general(a.astype(BF16), b.astype(BF16), (((1,), (1,)), ((), ())),
                           preferred_element_type=F32)


def _split3(x):
    hi = x.astype(BF16)
    r1 = x - hi.astype(F32)
    mid = r1.astype(BF16)
    lo = (r1 - mid.astype(F32)).astype(BF16)
    return hi, mid, lo


def _dot_exact_lhs(m_bf16, x):
    out = None
    for part in _split3(x):
        t = jnp.dot(m_bf16, part, preferred_element_type=F32)
        out = t if out is None else out + t
    return out


def _head_sums(x):
    r = lax.broadcasted_iota(jnp.int32, (LANES, LANES), 0)
    c = lax.broadcasted_iota(jnp.int32, (LANES, LANES), 1)
    ones_bd = jnp.where((r // RWKV_HEAD) == (c // RWKV_HEAD), 1.0, 0.0).astype(BF16)
    parts = _split3(x)
    cols = []
    for j in range(x.shape[1] // LANES):
        acc = None
        for part in parts:
            t = jnp.dot(part[:, j * LANES:(j + 1) * LANES], ones_bd, preferred_element_type=F32)
            acc = t if acc is None else acc + t
        cols.append(acc)
    return jnp.concatenate(cols, axis=1)


def _rope128(x, cos_t, sin_t):
    lane = lax.broadcasted_iota(jnp.int32, x.shape, 1)
    swapped = jnp.where((lane % 64) < 32, pltpu.roll(x, 96, 1), pltpu.roll(x, 32, 1))
    return x * cos_t + swapped * sin_t


def _matmul_kernel(*refs, nk, has_bias, has_res, gate_row, act):
    a_ref, w_ref = refs[0], refs[1]
    pos = 2
    bias_ref = res_ref = mod_ref = None
    if has_bias:
        bias_ref = refs[pos]; pos += 1
    if has_res:
        res_ref = refs[pos]; mod_ref = refs[pos + 1]; pos += 2
    o_ref, acc_ref = refs[pos], refs[pos + 1]
    k = pl.program_id(2)

    @pl.when(k == 0)
    def _():
        acc_ref[...] = jnp.zeros_like(acc_ref)

    acc_ref[...] += _dot(a_ref[...], w_ref[...])

    @pl.when(k == nk - 1)
    def _():
        y = acc_ref[...]
        if has_bias:
            y = y + bias_ref[...]
        if act == "relu2":
            y = jnp.square(jnp.maximum(y, 0.0))
        if has_res:
            y = res_ref[...] + mod_ref[0, gate_row:gate_row + 1, :] * y
        o_ref[...] = y.astype(o_ref.dtype)


def _matmul(a, w, *, tm, tn, tk, out_dtype=F32, bias=None, res=None, mod=None, gate_row=None,
            mod_index=None, act=None, rows=None, name=None):
    m = a.shape[0] if rows is None else rows
    kdim, n = w.shape
    assert a.shape[1] == kdim and m % tm == 0 and n % tn == 0 and kdim % tk == 0
    nk = kdim // tk
    in_specs = [pl.BlockSpec((tm, tk), lambda i, j, k: (i, k)),
                pl.BlockSpec((tk, tn), lambda i, j, k: (k, j))]
    args = [a, w]
    vmem = 2 * tm * tk * a.dtype.itemsize + 2 * tk * tn * w.dtype.itemsize + tm * tn * 4
    vmem += 2 * tm * tn * jnp.dtype(out_dtype).itemsize
    if bias is not None:
        in_specs.append(pl.BlockSpec((1, tn), lambda i, j, k: (0, j)))
        args.append(bias.reshape(1, n).astype(F32))
    if res is not None:
        in_specs.append(pl.BlockSpec((tm, tn), lambda i, j, k: (i, j)))
        in_specs.append(pl.BlockSpec((1, N_MOD, tn), lambda i, j, k: (mod_index(i), 0, j)))
        args += [res, mod]
        vmem += 2 * tm * tn * 4 + 2 * 8 * tn * 4
    kern = functools.partial(_matmul_kernel, nk=nk, has_bias=bias is not None,
                             has_res=res is not None, gate_row=gate_row, act=act)
    return pl.pallas_call(
        kern,
        grid=(m // tm, n // tn, nk),
        in_specs=in_specs,
        out_specs=pl.BlockSpec((tm, tn), lambda i, j, k: (i, j)),
        out_shape=jax.ShapeDtypeStruct((m, n), out_dtype),
        scratch_shapes=[pltpu.VMEM((tm, tn), F32)],
        compiler_params=_cparams(("parallel", "parallel", "arbitrary"), vmem),
        name=name,
    )(*args)


def _nm_matmul_kernel(*refs, has_mod, shift_row, scale_row, act):
    if has_mod:
        x_ref, g_ref, mod_ref, w_ref, o_ref, h_ref = refs
    else:
        x_ref, g_ref, w_ref, o_ref, h_ref = refs

    @pl.when(pl.program_id(1) == 0)
    def _():
        x = x_ref[...].astype(F32)
        y = x * lax.rsqrt(jnp.mean(x * x, axis=-1, keepdims=True) + NORM_EPS) * g_ref[...]
        if has_mod:
            y = y * (1.0 + mod_ref[0, scale_row:scale_row + 1, :]) + mod_ref[0, shift_row:shift_row + 1, :]
        h_ref[...] = y.astype(BF16)

    y = jnp.dot(h_ref[...], w_ref[...].astype(BF16), preferred_element_type=F32)
    if act == "relu2":
        y = jnp.square(jnp.maximum(y, 0.0))
    o_ref[...] = y.astype(o_ref.dtype)


def _nm_matmul(x, g, w, *, tm, tn, out_dtype, x_col_block=0, kdim=None, mod=None, shift_row=None,
               scale_row=None, mod_index=None, act=None, rows=None, name=None):
    m = x.shape[0] if rows is None else rows
    kdim = x.shape[1] if kdim is None else kdim
    n = w.shape[1]
    assert w.shape[0] == kdim and m % tm == 0 and n % tn == 0
    in_specs = [pl.BlockSpec((tm, kdim), lambda i, j: (i, x_col_block)),
                pl.BlockSpec((1, kdim), lambda i, j: (0, 0))]
    args = [x, g.reshape(1, kdim).astype(F32)]
    if mod is not None:
        in_specs.append(pl.BlockSpec((1, N_MOD, kdim), lambda i, j: (mod_index(i), 0, 0)))
        args.append(mod)
    in_specs.append(pl.BlockSpec((kdim, tn), lambda i, j: (0, j)))
    args.append(w)
    vmem = (2 * tm * kdim * x.dtype.itemsize + tm * kdim * 2 + 2 * kdim * tn * w.dtype.itemsize
            + 2 * tm * tn * jnp.dtype(out_dtype).itemsize + tm * tn * 4 + 4 * 8 * kdim * 4)
    kern = functools.partial(_nm_matmul_kernel, has_mod=mod is not None, shift_row=shift_row,
                             scale_row=scale_row, act=act)
    return pl.pallas_call(
        kern,
        grid=(m // tm, n // tn),
        in_specs=in_specs,
        out_specs=pl.BlockSpec((tm, tn), lambda i, j: (i, j)),
        out_shape=jax.ShapeDtypeStruct((m, n), out_dtype),
        scratch_shapes=[pltpu.VMEM((tm, kdim), BF16)],
        compiler_params=_cparams(("parallel", "arbitrary"), vmem),
        name=name,
    )(*args)


def _norm_mod_kernel(x_ref, g_ref, mod_ref, h_ref, *, shift_row, scale_row):
    x = x_ref[...]
    y = x * lax.rsqrt(jnp.mean(x * x, axis=-1, keepdims=True) + NORM_EPS) * g_ref[...]
    y = y * (1.0 + mod_ref[0, scale_row:scale_row + 1, :]) + mod_ref[0, shift_row:shift_row + 1, :]
    h_ref[...] = y.astype(h_ref.dtype)


def _norm_mod(xs, g, mod, *, shift_row, scale_row, mod_index, tm, rows):
    dm = xs.shape[1]
    return pl.pallas_call(
        functools.partial(_norm_mod_kernel, shift_row=shift_row, scale_row=scale_row),
        grid=(rows // tm,),
        in_specs=[pl.BlockSpec((tm, dm), lambda i: (i, 0)), pl.BlockSpec((1, dm), lambda i: (0, 0)),
                  pl.BlockSpec((1, N_MOD, dm), lambda i: (mod_index(i), 0, 0))],
        out_specs=pl.BlockSpec((tm, dm), lambda i: (i, 0)),
        out_shape=jax.ShapeDtypeStruct((rows, dm), BF16),
        compiler_params=_cparams(("parallel",), 2 * tm * dm * 6 + 4 * tm * dm * 4),
        name="norm_mod",
    )(xs, g.reshape(1, dm), mod)


def _fullk_kernel(*refs, has_res, gate_row, act):
    if has_res:
        a_ref, w_ref, res_ref, mod_ref, o_ref = refs
    else:
        a_ref, w_ref, o_ref = refs
    y = jnp.dot(a_ref[...], w_ref[...], preferred_element_type=F32)
    if act == "relu2":
        y = jnp.square(jnp.maximum(y, 0.0))
    if has_res:
        y = res_ref[...] + mod_ref[0, gate_row:gate_row + 1, :] * y
    o_ref[...] = y.astype(o_ref.dtype)


def _fullk_matmul(a, w, *, tm, tn, out_dtype, act=None, res=None, mod=None, gate_row=None, mod_index=None,
                  rows=None, name=None):
    m = a.shape[0] if rows is None else rows
    kdim, n = w.shape
    assert a.shape[1] == kdim and m % tm == 0 and n % tn == 0 and a.dtype == BF16 and w.dtype == BF16
    in_specs = [pl.BlockSpec((tm, kdim), lambda i, j: (i, 0)), pl.BlockSpec((kdim, tn), lambda i, j: (0, j))]
    args = [a, w]
    vmem = 2 * tm * kdim * 2 + 2 * kdim * tn * 2 + 2 * tm * tn * jnp.dtype(out_dtype).itemsize + 2 * tm * tn * 4
    if res is not None:
        in_specs += [pl.BlockSpec((tm, tn), lambda i, j: (i, j)),
                     pl.BlockSpec((1, N_MOD, tn), lambda i, j: (mod_index(i), 0, j))]
        args += [res, mod]
        vmem += 2 * tm * tn * 4
    return pl.pallas_call(
        functools.partial(_fullk_kernel, has_res=res is not None, gate_row=gate_row, act=act),
        grid=(m // tm, n // tn),
        in_specs=in_specs,
        out_specs=pl.BlockSpec((tm, tn), lambda i, j: (i, j)),
        out_shape=jax.ShapeDtypeStruct((m, n), out_dtype),
        compiler_params=_cparams(("parallel", "arbitrary"), vmem),
        name=name,
    )(*args)


def _merge_kernel(gl_ref, y0_ref, y1_ref, y2_ref, y3_ref, wb_ref, gu_ref, gb_ref, o_ref):
    gl = gl_ref[...].astype(BF16)
    acc = None
    for i, y_ref in enumerate((y0_ref, y1_ref, y2_ref, y3_ref)):
        gate = jax.nn.sigmoid(jnp.dot(gl, gu_ref[i], preferred_element_type=F32) + gb_ref[i:i + 1, :])
        term = gate * jnp.dot(y_ref[...], wb_ref[i], preferred_element_type=F32)
        acc = term if acc is None else acc + term
    o_ref[...] = acc.astype(o_ref.dtype)


def _merge(p, ys, wb, gu, gb, *, tm, tn, rows):
    n = wb.shape[2]
    bw = wb.shape[1]
    gr = gu.shape[1]
    y_spec = pl.BlockSpec((tm, bw), lambda i, j: (i, 0))
    vmem = (2 * tm * gr * 4 + 4 * 2 * tm * bw * 2 + 2 * 4 * bw * tn * 2 + 2 * 4 * gr * tn * 2
            + 2 * tm * tn * 2 + 3 * tm * tn * 4)
    return pl.pallas_call(
        _merge_kernel,
        grid=(rows // tm, n // tn),
        in_specs=[pl.BlockSpec((tm, gr), lambda i, j: (i, C_GL // GATE_RANK)),
                  y_spec, y_spec, y_spec, y_spec,
                  pl.BlockSpec((4, bw, tn), lambda i, j: (0, 0, j)),
                  pl.BlockSpec((4, gr, tn), lambda i, j: (0, 0, j)),
                  pl.BlockSpec((4, tn), lambda i, j: (0, j))],
        out_specs=pl.BlockSpec((tm, tn), lambda i, j: (i, j)),
        out_shape=jax.ShapeDtypeStruct((rows, n), BF16),
        compiler_params=_cparams(("parallel", "arbitrary"), vmem),
        name="merge",
    )(p, *ys, wb, gu, gb)


def _softmax_parts(s_list, scale):
    m = None
    for s in s_list:
        sm = jnp.max(s, axis=-1, keepdims=True)
        m = sm if m is None else jnp.maximum(m, sm)
    e_list = [jnp.exp2((s - m) * (scale * LOG2_E)) for s in s_list]
    l = None
    for e in e_list:
        es = jnp.sum(e, axis=-1, keepdims=True)
        l = es if l is None else l + es
    return e_list, l


def _attend(q, ks, vs, scale):
    e_list, l = _softmax_parts([_dot_nt(q, k) for k in ks], scale)
    o = None
    for e, v in zip(e_list, vs):
        t = jnp.dot(e.astype(BF16), v, preferred_element_type=F32)
        o = t if o is None else o + t
    return o / l


def _mla_kernel(cos_ref, sin_ref, q_ref, knx_ref, knc_ref, krx_ref, krc_ref, vx_ref, vc_ref, o_ref,
                kx_s, kc_s, *, nq, tq, scale):
    i = pl.program_id(2)

    @pl.when(i == 0)
    def _():
        kx_s[:, :MLA_NOPE] = knx_ref[...]
        kx_s[:, MLA_NOPE:] = _rope128(krx_ref[...], cos_ref[...], sin_ref[...]).astype(BF16)
        kc_s[:, :MLA_NOPE] = knc_ref[...]
        kc_s[:, MLA_NOPE:] = krc_ref[...].astype(BF16)

    @pl.when(i < nq)
    def _():
        row0 = pl.multiple_of(jnp.minimum(i, nq - 1) * tq, tq)
        q = q_ref[...]
        q_rope = _rope128(q[:, MLA_NOPE:].astype(F32), cos_ref[pl.ds(row0, tq), :], sin_ref[pl.ds(row0, tq), :])
        q = jnp.concatenate([q[:, :MLA_NOPE], q_rope.astype(BF16)], axis=1)
        o = _attend(q, (kc_s[...], kx_s[...]), (vc_ref[...], vx_ref[...]), scale)
        o_ref[...] = o.astype(o_ref.dtype)

    @pl.when(i >= nq)
    def _():
        o = _attend(q_ref[...], (kc_s[...],), (vc_ref[...],), scale)
        o_ref[...] = o.astype(o_ref.dtype)


def _diff_finish(o1, o2, lam_ref, g_ref, out_scale):
    o = o1 - lam_ref[...] * o2
    y = o * lax.rsqrt(jnp.mean(o * o, axis=-1, keepdims=True) + NORM_EPS) * g_ref[...]
    return y * out_scale


def _diff_kernel(cos_ref, sin_ref, lam_ref, g_ref, q_ref, kx_ref, kc_ref, vx_ref, vc_ref, o_ref,
                 kx_s, kc_s, vx_s, vc_s, *, nq, tq, scale, out_scale):
    i = pl.program_id(2)

    @pl.when(i == 0)
    def _():
        kx_s[...] = _rope128(kx_ref[...], cos_ref[...], sin_ref[...]).astype(BF16)
        kc_s[...] = kc_ref[...].astype(BF16)
        vx_s[...] = vx_ref[...].astype(BF16)
        vc_s[...] = vc_ref[...].astype(BF16)

    def halves(q):
        lane = lax.broadcasted_iota(jnp.int32, q.shape, 1)
        return jnp.where(lane < DIFF_QK, q, 0.0).astype(BF16), jnp.where(lane < DIFF_QK, 0.0, q).astype(BF16)

    @pl.when(i < nq)
    def _():
        row0 = pl.multiple_of(jnp.minimum(i, nq - 1) * tq, tq)
        q1, q2 = halves(_rope128(q_ref[...], cos_ref[pl.ds(row0, tq), :], sin_ref[pl.ds(row0, tq), :]))
        ks, vs = (kc_s[...], kx_s[...]), (vc_s[...], vx_s[...])
        y = _diff_finish(_attend(q1, ks, vs, scale), _attend(q2, ks, vs, scale), lam_ref, g_ref, out_scale)
        o_ref[...] = y.astype(o_ref.dtype)

    @pl.when(i >= nq)
    def _():
        q1, q2 = halves(q_ref[...])
        ks, vs = (kc_s[...],), (vc_s[...],)
        y = _diff_finish(_attend(q1, ks, vs, scale), _attend(q2, ks, vs, scale), lam_ref, g_ref, out_scale)
        o_ref[...] = y.astype(o_ref.dtype)


def _query_row_block(b, i, nq, nqc, n_x, tq):
    return jnp.where(i < nq, b * nq + i, n_x // tq + b * nqc + (i - nq))


def _mla_attention(q, kv, p, cos_t, sin_t, *, n_batch, t_len, c_len, tq, with_ctx):
    n_x = n_batch * t_len
    nq = t_len // tq
    nqc = c_len // tq if with_ctx else 0
    out_rows = n_x + (n_batch * c_len if with_ctx else 0)
    cblk0 = n_x // c_len
    qmap = lambda b, h, i: (_query_row_block(b, i, nq, nqc, n_x, tq), h)
    full = lambda b, h, i: (0, 0)
    vmem = (4 * t_len * LANES * 4 + 2 * tq * 256 * 2 + 2 * (t_len + c_len) * LANES * (2 + 4 + 2)
            + (t_len + c_len) * 256 * 2 + 2 * tq * LANES * 2 + 6 * tq * (t_len + c_len) * 4)
    return pl.pallas_call(
        functools.partial(_mla_kernel, nq=nq, tq=tq, scale=(MLA_NOPE + MLA_ROPE) ** -0.5),
        grid=(n_batch, MLA_HEADS, nq + nqc),
        in_specs=[pl.BlockSpec((t_len, LANES), full), pl.BlockSpec((t_len, LANES), full),
                  pl.BlockSpec((tq, MLA_QK_PAD), qmap),
                  pl.BlockSpec((t_len, MLA_NOPE), lambda b, h, i: (b, h)),
                  pl.BlockSpec((c_len, MLA_NOPE), lambda b, h, i: (cblk0 + b, h)),
                  pl.BlockSpec((t_len, LANES), lambda b, h, i: (b, C_KROPE // LANES)),
                  pl.BlockSpec((c_len, LANES), lambda b, h, i: (cblk0 + b, C_KROPE // LANES)),
                  pl.BlockSpec((t_len, MLA_V), lambda b, h, i: (b, MLA_HEADS + h)),
                  pl.BlockSpec((c_len, MLA_V), lambda b, h, i: (cblk0 + b, MLA_HEADS + h))],
        out_specs=pl.BlockSpec((tq, MLA_V), qmap),
        out_shape=jax.ShapeDtypeStruct((out_rows, MLA_HEADS * MLA_V), BF16),
        scratch_shapes=[pltpu.VMEM((t_len, MLA_QK_PAD), BF16), pltpu.VMEM((c_len, MLA_QK_PAD), BF16)],
        compiler_params=_cparams(("parallel", "parallel", "arbitrary"), vmem),
        name="mla_attention",
    )(cos_t, sin_t, q, kv, kv, p, p, kv, kv)


def _diff_attention(p, cos_t, sin_t, lam_row, g_row, *, n_batch, t_len, c_len, tq, with_ctx, out_scale):
    n_x = n_batch * t_len
    nq = t_len // tq
    nqc = c_len // tq if with_ctx else 0
    out_rows = n_x + (n_batch * c_len if with_ctx else 0)
    cblk0 = n_x // c_len
    full = lambda b, h, i: (0, 0)
    qblk, kblk, vblk = C_DQ // LANES, C_DK // LANES, C_DV // LANES
    vmem = (4 * t_len * LANES * 4 + 2 * tq * LANES * 4 + 4 * (t_len + c_len) * LANES * 4
            + 2 * (t_len + c_len) * LANES * 2 + 2 * tq * LANES * 2 + 8 * tq * (t_len + c_len) * 4)
    return pl.pallas_call(
        functools.partial(_diff_kernel, nq=nq, tq=tq, scale=DIFF_QK ** -0.5, out_scale=out_scale),
        grid=(n_batch, DIFF_HEADS, nq + nqc),
        in_specs=[pl.BlockSpec((t_len, LANES), full), pl.BlockSpec((t_len, LANES), full),
                  pl.BlockSpec((1, DIFF_V), full), pl.BlockSpec((1, DIFF_V), full),
                  pl.BlockSpec((tq, LANES), lambda b, h, i: (_query_row_block(b, i, nq, nqc, n_x, tq), qblk + h)),
                  pl.BlockSpec((t_len, LANES), lambda b, h, i: (b, kblk + h)),
                  pl.BlockSpec((c_len, LANES), lambda b, h, i: (cblk0 + b, kblk + h)),
                  pl.BlockSpec((t_len, LANES), lambda b, h, i: (b, vblk + h)),
                  pl.BlockSpec((c_len, LANES), lambda b, h, i: (cblk0 + b, vblk + h))],
        out_specs=pl.BlockSpec((tq, DIFF_V), lambda b, h, i: (_query_row_block(b, i, nq, nqc, n_x, tq), h)),
        out_shape=jax.ShapeDtypeStruct((out_rows, DIFF_HEADS * DIFF_V), BF16),
        scratch_shapes=[pltpu.VMEM((t_len, LANES), BF16), pltpu.VMEM((c_len, LANES), BF16),
                        pltpu.VMEM((t_len, LANES), BF16), pltpu.VMEM((c_len, LANES), BF16)],
        compiler_params=_cparams(("parallel", "parallel", "arbitrary"), vmem),
        name="diff_attention",
    )(cos_t, sin_t, lam_row, g_row, p, p, p, p, p)


def _prep_kernel(r_ref, k_ref, v_ref, cb_ref, cc_ref, cu_ref, lo_ref, hp_ref, hn_ref,
                 mu_ref, mul_ref, kk_w_ref, ka_w_ref, rk_w_ref, w0_ref, a0_ref, w2_ref, a2_ref, g2_ref,
                 cw_ref,
                 r_o, kk_o, v_o, lw_o, kd_o, ka_o, bonus_o, g_o, conv_o, *, tr, n_x, t_len, c_len):
    i = pl.program_id(0)
    g0 = i * tr
    seq = jnp.where(g0 < n_x, t_len, c_len)
    has_prev = ((g0 % seq) != 0).astype(F32)
    has_next = (((g0 + tr) % seq) != 0).astype(F32)
    row = lax.broadcasted_iota(jnp.int32, (tr, 1), 0)

    def neighbours(x, col0):
        width = x.shape[1]
        before = hp_ref[HALO - 1:HALO, col0:col0 + width] * has_prev
        after = hn_ref[0:1, col0:col0 + width] * has_next
        prev = jnp.where(row == 0, before, pltpu.roll(x, 1, 0))
        nxt = jnp.where(row == tr - 1, after, pltpu.roll(x, tr - 1, 0))
        return prev, nxt

    def shifted(x, col0, mu0, mu1):
        prev, nxt = neighbours(x, col0)
        return x + mu0 * (prev - x) + mu1 * (nxt - x)

    r = shifted(r_ref[...], C_R, mu_ref[0:1, :BRANCH_W], mu_ref[1:2, :BRANCH_W])
    k = shifted(k_ref[...], C_K, mu_ref[0:1, BRANCH_W:2 * BRANCH_W], mu_ref[1:2, BRANCH_W:2 * BRANCH_W])
    v = shifted(v_ref[...], C_V, mu_ref[0:1, 2 * BRANCH_W:], mu_ref[1:2, 2 * BRANCH_W:])
    lo = shifted(lo_ref[...], C_LORA, mul_ref[0:1, :], mul_ref[1:2, :])
    wd = jnp.tanh(lo[:, :2 * RWKV_LORA])
    ad = lo[:, 2 * RWKV_LORA:4 * RWKV_LORA]
    gd = jax.nn.sigmoid(lo[:, 4 * RWKV_LORA:])
    w_pre = _dot(wd, w2_ref[...]) + w0_ref[...]
    a_sig = jax.nn.sigmoid(_dot(ad, a2_ref[...]) + a0_ref[...])
    g_o[...] = _dot(gd, g2_ref[...])
    w_log = -(jnp.maximum(-w_pre, 0.0) + jnp.log(1.0 + jnp.exp(-jnp.abs(w_pre)))) - 0.5
    lw = -jnp.exp(w_log)
    kkf = k * kk_w_ref[...]
    kk = kkf * lax.rsqrt(_head_sums(kkf * kkf) + 1e-12)
    r_o[...] = r
    kk_o[...] = kk
    v_o[...] = v
    k_sum = None
    for d in range(2):
        a_d = a_sig[:, d * BRANCH_W:(d + 1) * BRANCH_W]
        k_d = k * (1.0 + (a_d - 1.0) * ka_w_ref[...])
        lw_o[d] = lw[:, d * BRANCH_W:(d + 1) * BRANCH_W]
        kd_o[d] = k_d
        ka_o[d] = kk * a_d
        k_sum = k_d if k_sum is None else k_sum + k_d
    bonus_o[...] = _head_sums(r * k_sum * rk_w_ref[...]) * v

    z = cc_ref[...] * cu_ref[...]
    z_before = hp_ref[HALO - 1:HALO, C_CC:C_CC + BRANCH_W] * hp_ref[HALO - 1:HALO, C_CU:C_CU + BRANCH_W] * has_prev
    z_after = hn_ref[0:1, C_CC:C_CC + BRANCH_W] * hn_ref[0:1, C_CU:C_CU + BRANCH_W] * has_next
    z_prev = jnp.where(row == 0, z_before, pltpu.roll(z, 1, 0))
    z_next = jnp.where(row == tr - 1, z_after, pltpu.roll(z, tr - 1, 0))
    y = cb_ref[...] * (cw_ref[0:1, :] * z_prev + cw_ref[1:2, :] * z + cw_ref[2:3, :] * z_next)
    conv_o[...] = y.astype(conv_o.dtype)


def _mixer_prep(p, mu_rkv, mu_lora, kk_w, ka_w, rk_w, w0, a0, w2bd, a2bd, g2p, conv_w, *, tr, n_x, t_len,
                c_len):
    n_rows = p.shape[0]
    bw = BRANCH_W
    last_halo = n_rows // HALO - 1
    col = lambda c: (lambda i: (i, c))
    const = lambda i: (0, 0)
    main = [pl.BlockSpec((tr, bw), col(C_R // bw)), pl.BlockSpec((tr, bw), col(C_K // bw)),
            pl.BlockSpec((tr, bw), col(C_V // bw)), pl.BlockSpec((tr, bw), col(C_CB // bw)),
            pl.BlockSpec((tr, bw), col(C_CC // bw)), pl.BlockSpec((tr, bw), col(C_CU // bw)),
            pl.BlockSpec((tr, RWKV_LORA_PAD), col(C_LORA // RWKV_LORA_PAD)),
            pl.BlockSpec((HALO, P_COLS), lambda i: (jnp.maximum(i * (tr // HALO) - 1, 0), 0)),
            pl.BlockSpec((HALO, P_COLS), lambda i: (jnp.minimum((i + 1) * (tr // HALO), last_halo), 0))]
    params = [mu_rkv, mu_lora, kk_w, ka_w, rk_w, w0, a0, w2bd, a2bd, g2p, conv_w]
    param_specs = [pl.BlockSpec(a.shape, const) for a in params]
    row_spec = pl.BlockSpec((tr, bw), lambda i: (i, 0))
    dir_spec = pl.BlockSpec((2, tr, bw), lambda i: (0, i, 0))
    f32_rows = jax.ShapeDtypeStruct((n_rows, bw), F32)
    f32_dirs = jax.ShapeDtypeStruct((2, n_rows, bw), F32)
    vmem = 2 * (7 * tr * bw * 4 + 2 * HALO * P_COLS * 4 + 12 * tr * bw * 4) + 30 * tr * bw * 4
    return pl.pallas_call(
        functools.partial(_prep_kernel, tr=tr, n_x=n_x, t_len=t_len, c_len=c_len),
        grid=(n_rows // tr,),
        in_specs=main + param_specs,
        out_specs=[row_spec, row_spec, row_spec, dir_spec, dir_spec, dir_spec, row_spec, row_spec, row_spec],
        out_shape=[f32_rows, f32_rows, f32_rows, f32_dirs, f32_dirs, f32_dirs, f32_rows, f32_rows,
                   jax.ShapeDtypeStruct((n_rows, bw), BF16)],
        compiler_params=_cparams(("parallel",), vmem),
        name="mixer_prep",
    )(p, p, p, p, p, p, p, p, p, *params)


def _rwkv_kernel(r_ref, kk_ref, v_ref, lw_ref, kd_ref, ka_ref, o_ref, s_ref):
    C = RWKV_CHUNK
    W = 2 * RWKV_HEAD
    n_pairs = RWKV_HEADS // 2
    d = pl.program_id(1)
    sign = 1 - 2 * d

    @pl.when(pl.program_id(2) == 0)
    def _():
        s_ref[...] = jnp.zeros_like(s_ref)

    mm, mm_nt = _dot, _dot_nt

    rc = lax.broadcasted_iota(jnp.int32, (C, C), 0)
    cc = lax.broadcasted_iota(jnp.int32, (C, C), 1)
    m_incl = jnp.where((rc - cc) * sign >= 0, 1.0, 0.0).astype(BF16)
    lw = lw_ref[0]
    cum = _dot_exact_lhs(m_incl, lw)
    tot = jnp.sum(lw, axis=0, keepdims=True)
    e_incl = jnp.exp(cum)
    e_excl = jnp.exp(cum - lw)
    e_ninc = jnp.exp(-cum)
    e_rem = jnp.exp(tot - cum)
    e_tot = jnp.exp(tot)

    kk = kk_ref[...]
    ka = ka_ref[0]
    kd = kd_ref[0]
    a_t = kk * e_excl
    b_t = -ka * e_ninc
    k_t = kd * e_ninc
    r_t = r_ref[...] * e_incl
    b_h = -ka * e_rem
    k_h = kd * e_rem
    v = v_ref[...]

    lane = lax.broadcasted_iota(jnp.int32, (C, W), 1)
    first = lane < RWKV_HEAD
    r2 = lax.broadcasted_iota(jnp.int32, (W, W), 0)
    c2 = lax.broadcasted_iota(jnp.int32, (W, W), 1)
    same = (r2 // C) == (c2 // C)
    d2 = ((r2 % C) - (c2 % C)) * sign
    strict_bd = same & (d2 > 0)
    incl_bd = same & (d2 >= 0)
    eye = jnp.where(r2 == c2, 1.0, 0.0).astype(F32)

    def stack2(x):
        return jnp.concatenate([jnp.where(first, x, 0.0), jnp.where(first, 0.0, x)], axis=0)

    pairs = range(n_pairs)
    sls = [slice(W * p, W * (p + 1)) for p in pairs]
    a2 = [stack2(a_t[:, sl]).astype(BF16) for sl in sls]
    rr2 = [stack2(r_t[:, sl]).astype(BF16) for sl in sls]
    bk2 = [jnp.concatenate([stack2(b_t[:, sl]), stack2(k_t[:, sl])], axis=0).astype(BF16) for sl in sls]
    v2 = [stack2(v[:, sl]).astype(BF16) for sl in sls]
    g = [mm_nt(jnp.concatenate([a2[p], rr2[p]], axis=0), bk2[p]) for p in pairs]
    l_pow = [jnp.where(strict_bd, g[p][:W, :W], 0.0) for p in pairs]
    m_ak = [jnp.where(strict_bd, g[p][:W, W:], 0.0).astype(BF16) for p in pairs]
    a_r = [jnp.where(jnp.concatenate([incl_bd, incl_bd], axis=1), g[p][W:], 0.0).astype(BF16) for p in pairs]
    mv = [mm(m_ak[p], v2[p]) for p in pairs]
    t_inv = [eye + l_pow[p] for p in pairs]
    for _ in range(int(math.log2(C)) - 1):
        lb = [l_pow[p].astype(BF16) for p in pairs]
        l_pow = [mm(lb[p], lb[p]) for p in pairs]
        t_inv = [t_inv[p] + mm(t_inv[p], l_pow[p]) for p in pairs]
    wu = [mm(t_inv[p], jnp.concatenate([a2[p], mv[p].astype(BF16)], axis=1)) for p in pairs]
    s_bd = [s_ref[p] for p in pairs]
    wr = [mm(jnp.concatenate([wu[p][:, :W].astype(BF16), rr2[p]], axis=0), s_bd[p]) for p in pairs]
    zv = [jnp.concatenate([(wr[p][:W] + wu[p][:, W:]).astype(BF16), v2[p]], axis=0) for p in pairs]
    o2 = [wr[p][W:] + mm(a_r[p], zv[p]) for p in pairs]
    for p in pairs:
        o_ref[0, :, sls[p]] = o2[p][:C] + o2[p][C:]
    for p in pairs:
        bk_h = jnp.concatenate([stack2(b_h[:, sls[p]]), stack2(k_h[:, sls[p]])], axis=0)
        tot_col = jnp.broadcast_to(e_tot[:, sls[p]], (W, W)).T
        s_ref[p] = tot_col * s_bd[p] + mm(bk_h.T, zv[p])


def _rwkv_scan(r, kk, v, lw, kd, ka, *, n_batch, t_len, c_len):
    C = RWKV_CHUNK
    rows, width = r.shape
    nct, nxt = c_len // C, t_len // C
    ctx_blk0 = n_batch * t_len // C

    def blk(b, d, s):
        j_c = jnp.where(d == 0, s, nct - 1 - s)
        j_x = jnp.where(d == 0, s - nct, nxt - 1 - (s - nct))
        return jnp.where(s < nct, ctx_blk0 + b * nct + j_c, b * nxt + j_x)

    shared = pl.BlockSpec((C, width), lambda b, d, s: (blk(b, d, s), 0))
    per_dir = pl.BlockSpec((1, C, width), lambda b, d, s: (d, blk(b, d, s), 0))
    vmem = 2 * 7 * C * width * 4 + (RWKV_HEADS // 2) * 128 * 128 * 4 + 64 * C * width * 4
    return pl.pallas_call(
        _rwkv_kernel,
        grid=(n_batch, 2, nct + nxt),
        in_specs=[shared, shared, shared, per_dir, per_dir, per_dir],
        out_specs=per_dir,
        out_shape=jax.ShapeDtypeStruct((2, rows, width), F32),
        scratch_shapes=[pltpu.VMEM((RWKV_HEADS // 2, 128, 128), F32)],
        compiler_params=_cparams(("parallel", "parallel", "arbitrary"), vmem),
        name="rwkv_scan",
    )(r, kk, v, lw, kd, ka)


def _rwkv_readout_kernel(o_ref, bonus_ref, g_ref, lng_ref, lnb_ref, y_ref):
    o = o_ref[0] + o_ref[1]
    mean = _head_sums(o) * (1.0 / RWKV_HEAD)
    cen = o - mean
    var = _head_sums(cen * cen) * (1.0 / RWKV_HEAD)
    o_n = cen * lax.rsqrt(var + RWKV_GN_EPS) * lng_ref[...] + lnb_ref[...]
    y_ref[...] = ((o_n + bonus_ref[...]) * g_ref[...]).astype(y_ref.dtype)


def _rwkv_readout(o_dirs, bonus, g_out, ln_g, ln_b, *, tr):
    n_rows, bw = bonus.shape
    row_spec = pl.BlockSpec((tr, bw), lambda i: (i, 0))
    const = pl.BlockSpec((1, bw), lambda i: (0, 0))
    return pl.pallas_call(
        _rwkv_readout_kernel,
        grid=(n_rows // tr,),
        in_specs=[pl.BlockSpec((2, tr, bw), lambda i: (0, i, 0)), row_spec, row_spec, const, const],
        out_specs=row_spec,
        out_shape=jax.ShapeDtypeStruct((n_rows, bw), BF16),
        compiler_params=_cparams(("parallel",), 2 * 5 * tr * bw * 4 + 16 * tr * bw * 4),
        name="rwkv_readout",
    )(o_dirs, bonus, g_out, ln_g.reshape(1, bw), ln_b.reshape(1, bw))


def _rmsnorm_kernel(x_ref, g_ref, o_ref):
    x = x_ref[...]
    o_ref[...] = x * lax.rsqrt(jnp.mean(x * x, axis=-1, keepdims=True) + NORM_EPS) * g_ref[...]


def _final_norm(xs, g, *, rows, tm):
    dm = xs.shape[1]
    return pl.pallas_call(
        _rmsnorm_kernel,
        grid=(rows // tm,),
        in_specs=[pl.BlockSpec((tm, dm), lambda i: (i, 0)), pl.BlockSpec((1, dm), lambda i: (0, 0))],
        out_specs=pl.BlockSpec((tm, dm), lambda i: (i, 0)),
        out_shape=jax.ShapeDtypeStruct((rows, dm), F32),
        compiler_params=_cparams(("parallel",), 4 * tm * dm * 4),
        name="final_norm",
    )(xs, g.reshape(1, dm))


def _rope_tables128(n_tokens):
    rows = n_tokens // GRID_W
    row = jnp.repeat(jnp.arange(rows, dtype=F32), GRID_W)
    col = jnp.tile(jnp.arange(GRID_W, dtype=F32), rows)
    n_freq = 64 // 4
    inv = ROPE_BASE ** (-jnp.arange(n_freq, dtype=F32) / n_freq)
    ang = jnp.concatenate([row[:, None] * inv, col[:, None] * inv], axis=-1)
    cos, sin = jnp.cos(ang), jnp.sin(ang)
    return jnp.concatenate([cos, cos, cos, cos], axis=-1), jnp.concatenate([-sin, sin, -sin, sin], axis=-1)


def _block_diag2(w2):
    z = jnp.zeros_like(w2[0])
    return jnp.concatenate([jnp.concatenate([w2[0], z], axis=1), jnp.concatenate([z, w2[1]], axis=1)], axis=0)


def kernel(x, c, ctx, c_ctx, norm1_g, norm2_g, mod_down, mod_up, mod_b, w_in, mla_q_norm_g, mla_w_uq,
           mla_kv_norm_g, mla_w_ukv, rwkv_mu, rwkv_w0, rwkv_w2, rwkv_a0, rwkv_a2, rwkv_g2, rwkv_k_k,
           rwkv_k_a, rwkv_r_k, rwkv_ln_g, rwkv_ln_b, conv_w, diff_lambda, diff_norm_g, w_branch, gate_down,
           gate_up, gate_b, w_out, mlp_w1, mlp_w2, final_norm_g):
    n_batch, t_len, dm = x.shape
    c_len = ctx.shape[1]
    depth = w_in.shape[0]
    bw = BRANCH_W
    n_x = n_batch * t_len
    n_c = n_batch * c_len
    n_rows = n_x + n_c
    tm = 512 if (t_len % 512 == 0 and n_c % 512 == 0) else 256
    tq = min(256, c_len)
    tr = min(256, c_len)
    assert dm == D_MODEL and t_len % tm == 0 and n_c % tm == 0 and n_x % c_len == 0
    assert t_len % tq == 0 and c_len % tq == 0 and c_len % RWKV_CHUNK == 0 and t_len % c_len == 0
    tm_big = 1024 if (t_len % 1024 == 0 and n_c % 1024 == 0) else tm
    mod_index = functools.partial(_mod_index, tm=tm, n_x_rows=n_x, t_len=t_len, n_batch=n_batch)
    mod_index_big = functools.partial(_mod_index, tm=tm_big, n_x_rows=n_x, t_len=t_len, n_batch=n_batch)
    mod_index_nm = functools.partial(_mod_index, tm=256, n_x_rows=n_x, t_len=t_len, n_batch=n_batch)

    zeros = lambda *s: jnp.zeros(s, F32)
    rw0 = 1088
    cv0 = rw0 + 3 * bw + 4 * RWKV_LORA + RWKV_GATE_LORA
    lora_w = 4 * RWKV_LORA + RWKV_GATE_LORA
    w_in_p = jnp.concatenate(
        [w_in[:, :, :1024], w_in[:, :, rw0:rw0 + 3 * bw], w_in[:, :, cv0:],
         w_in[:, :, rw0 + 3 * bw:cv0], zeros(depth, dm, RWKV_LORA_PAD - lora_w), gate_down,
         w_in[:, :, 1024:rw0], zeros(depth, dm, P_COLS - C_KROPE - MLA_ROPE)], axis=-1).astype(BF16)
    w_uq_p = jnp.pad(mla_w_uq.reshape(depth, MLA_Q_LORA, MLA_HEADS, MLA_NOPE + MLA_ROPE),
                     ((0, 0), (0, 0), (0, 0), (0, MLA_QK_PAD - MLA_NOPE - MLA_ROPE))
                     ).reshape(depth, MLA_Q_LORA, MLA_HEADS * MLA_QK_PAD).astype(BF16)
    w_ukv_r = mla_w_ukv.reshape(depth, MLA_KV_LORA, MLA_HEADS, MLA_NOPE + MLA_V)
    w_ukv_p = jnp.concatenate([w_ukv_r[..., :MLA_NOPE].reshape(depth, MLA_KV_LORA, -1),
                               w_ukv_r[..., MLA_NOPE:].reshape(depth, MLA_KV_LORA, -1)], axis=-1).astype(BF16)
    w_branch_b = w_branch.astype(BF16)
    gate_up_b = jnp.moveaxis(gate_up, 2, 1).astype(BF16)
    w_out_b = w_out.astype(BF16)
    w1_b = mlp_w1.astype(BF16)
    w2_b = mlp_w2.astype(BF16)
    g2_p = jnp.pad(rwkv_g2, ((0, 0), (0, RWKV_LORA_PAD - 4 * RWKV_LORA - RWKV_GATE_LORA), (0, 0))).astype(BF16)
    mu_lora = jnp.pad(rwkv_mu[:, :, 3 * bw:], ((0, 0), (0, 0), (0, RWKV_LORA_PAD - lora_w)))

    cond = jnp.concatenate([c, c_ctx[None, :], zeros(16 - n_batch - 1, dm)], axis=0)
    cond = jax.nn.silu(cond)
    mods = []
    for l in range(depth):
        low = _matmul(cond, mod_down[l], tm=16, tn=MOD_RANK, tk=dm, name="mod_down")
        up = _matmul(low, mod_up[l], tm=16, tn=2048, tk=MOD_RANK, bias=mod_b[l], name="mod_up")
        mods.append(up.reshape(16, N_MOD, dm))

    cos_t, sin_t = _rope_tables128(t_len)
    xs = jnp.concatenate([x.reshape(n_x, dm), ctx.reshape(n_c, dm)], axis=0)

    for l in range(depth):
        need_ctx = l < depth - 1
        mod = mods[l]
        lam_init = 0.8 - 0.6 * math.exp(-0.3 * l)
        lq1, lk1, lq2, lk2 = diff_lambda[l]
        lam = jnp.exp(jnp.sum(lq1 * lk1)) - jnp.exp(jnp.sum(lq2 * lk2)) + lam_init
        lam_row = jnp.full((1, DIFF_V), 1.0, F32) * lam

        h1 = _norm_mod(xs, norm1_g[l], mod, shift_row=0, scale_row=1, mod_index=mod_index_nm, tm=256,
                       rows=n_rows)
        p = _fullk_matmul(h1, w_in_p[l], tm=tm_big, tn=512, out_dtype=F32, name="in_proj")

        q = _nm_matmul(p, mla_q_norm_g[l], w_uq_p[l], tm=tm, tn=512, out_dtype=BF16,
                       x_col_block=C_CQ // MLA_Q_LORA, kdim=MLA_Q_LORA, name="mla_q")
        kv = _nm_matmul(p, mla_kv_norm_g[l], w_ukv_p[l], tm=tm, tn=512, out_dtype=BF16,
                        x_col_block=C_CKV // MLA_KV_LORA, kdim=MLA_KV_LORA, name="mla_kv")
        y_mla = _mla_attention(q, kv, p, cos_t, sin_t, n_batch=n_batch, t_len=t_len, c_len=c_len, tq=tq,
                               with_ctx=need_ctx)

        y_diff = _diff_attention(p, cos_t, sin_t, lam_row, diff_norm_g[l].reshape(1, DIFF_V),
                                 n_batch=n_batch, t_len=t_len, c_len=c_len, tq=tq, with_ctx=need_ctx,
                                 out_scale=1.0 - lam_init)

        r_, kk, v_, lw, k_dir, kka, bonus, g_out, y_conv = _mixer_prep(
            p, rwkv_mu[l, :, :3 * bw], mu_lora[l], rwkv_k_k[l].reshape(1, bw), rwkv_k_a[l].reshape(1, bw),
            rwkv_r_k[l].reshape(1, bw), rwkv_w0[l].reshape(1, 2 * bw), rwkv_a0[l].reshape(1, 2 * bw),
            _block_diag2(rwkv_w2[l]).astype(BF16), _block_diag2(rwkv_a2[l]).astype(BF16), g2_p[l],
            conv_w[l], tr=min(128, tr), n_x=n_x, t_len=t_len, c_len=c_len)
        o_dirs = _rwkv_scan(r_, kk, v_, lw, k_dir, kka, n_batch=n_batch, t_len=t_len, c_len=c_len)
        y_rwkv = _rwkv_readout(o_dirs, bonus, g_out, rwkv_ln_g[l], rwkv_ln_b[l], tr=tr)

        rows = n_rows if need_ctx else n_x
        acc = _merge(p, (y_mla, y_rwkv, y_conv, y_diff), w_branch_b[l], gate_up_b[l], gate_b[l],
                     tm=tm_big, tn=512, rows=rows)
        xs_new = _fullk_matmul(acc, w_out_b[l], tm=tm_big, tn=512, out_dtype=F32, res=xs, mod=mod, gate_row=2,
                               mod_index=mod_index_big, rows=rows, name="out_proj")

        h2 = _norm_mod(xs_new, norm2_g[l], mod, shift_row=3, scale_row=4, mod_index=mod_index_nm, tm=256,
                       rows=rows)
        hid = _fullk_matmul(h2, w1_b[l], tm=tm_big, tn=512, out_dtype=BF16, act="relu2", name="mlp_up")
        xs = _matmul(hid, w2_b[l], tm=tm_big, tn=1024, tk=1024, res=xs_new, mod=mod, gate_row=5,
                     mod_index=mod_index_big, rows=rows, name="mlp_down")

    out = _final_norm(xs, final_norm_g, rows=n_x, tm=tm)
    return out.reshape(n_batch, t_len, dm)
```

```python
import functools
import math

import jax
import jax.numpy as jnp
from jax import lax
from jax.experimental import pallas as pl
from jax.experimental.pallas import tpu as pltpu

F32 = jnp.float32
BF16 = jnp.bfloat16

D_MODEL = 4096
BRANCH_W = 1024
GRID_W = 64
ROPE_BASE = 10000.0
NORM_EPS = 1e-6
N_MOD = 6
LANES = 128
HALO = 8
LOG2_E = 1.4426950408889634

MLA_HEADS = 8
MLA_NOPE = 128
MLA_ROPE = 64
MLA_V = 128
MLA_Q_LORA = 768
MLA_KV_LORA = 256
MLA_QK_PAD = 256

RWKV_HEAD = 64
RWKV_HEADS = 16
RWKV_LORA = 64
RWKV_GATE_LORA = 160
RWKV_GN_EPS = 64e-5
RWKV_CHUNK = 64
RWKV_LORA_PAD = 512

DIFF_HEADS = 8
DIFF_QK = 64
DIFF_V = 128
GATE_RANK = 256
MOD_RANK = 256

C_CQ = 0
C_CKV = 768
C_R = 1024
C_K = 2048
C_V = 3072
C_CB = 4096
C_CC = 5120
C_CU = 6144
C_DQ = 7168
C_DK = 8192
C_DV = 9216
C_LORA = 10240
C_GL = 10752
C_KROPE = 11008
P_COLS = 11264

VMEM_CAP = 56 * 1024 * 1024


def _cparams(sem, vmem_bytes):
    limit = int(min(VMEM_CAP, max(vmem_bytes * 1.5 + (4 << 20), 16 << 20)))
    return pltpu.CompilerParams(dimension_semantics=sem, vmem_limit_bytes=limit)


def _mod_index(i, tm, n_x_rows, t_len, n_batch):
    return jnp.where(i < n_x_rows // tm, i // (t_len // tm), n_batch)


def _dot(a, b):
    return jnp.dot(a.astype(BF16), b.astype(BF16), preferred_element_type=F32)


def _dot_nt(a, b):
    return lax.dot_general(a.astype(BF16), b.astype(BF16), (((1,), (1,)), ((), ())),
                           preferred_element_type=F32)


def _split3(x):
    hi = x.astype(BF16)
    r1 = x - hi.astype(F32)
    mid = r1.astype(BF16)
    lo = (r1 - mid.astype(F32)).astype(BF16)
    return hi, mid, lo


def _dot_exact_lhs(m_bf16, x):
    out = None
    for part in _split3(x):
        t = jnp.dot(m_bf16, part, preferred_element_type=F32)
        out = t if out is None else out + t
    return out


def _head_sums(x):
    r = lax.broadcasted_iota(jnp.int32, (LANES, LANES), 0)
    c = lax.broadcasted_iota(jnp.int32, (LANES, LANES), 1)
    ones_bd = jnp.where((r // RWKV_HEAD) == (c // RWKV_HEAD), 1.0, 0.0).astype(BF16)
    parts = _split3(x)
    cols = []
    for j in range(x.shape[1] // LANES):
        acc = None
        for part in parts:
            t = jnp.dot(part[:, j * LANES:(j + 1) * LANES], ones_bd, preferred_element_type=F32)
            acc = t if acc is None else acc + t
        cols.append(acc)
    return jnp.concatenate(cols, axis=1)


def _rope128(x, cos_t, sin_t):
    lane = lax.broadcasted_iota(jnp.int32, x.shape, 1)
    swapped = jnp.where((lane % 64) < 32, pltpu.roll(x, 96, 1), pltpu.roll(x, 32, 1))
    return x * cos_t + swapped * sin_t


def _matmul_kernel(*refs, nk, has_bias, has_res, gate_row, act):
    a_ref, w_ref = refs[0], refs[1]
    pos = 2
    bias_ref = res_ref = mod_ref = None
    if has_bias:
        bias_ref = refs[pos]; pos += 1
    if has_res:
        res_ref = refs[pos]; mod_ref = refs[pos + 1]; pos += 2
    o_ref, acc_ref = refs[pos], refs[pos + 1]
    k = pl.program_id(2)

    @pl.when(k == 0)
    def _():
        acc_ref[...] = jnp.zeros_like(acc_ref)

    acc_ref[...] += _dot(a_ref[...], w_ref[...])

    @pl.when(k == nk - 1)
    def _():
        y = acc_ref[...]
        if has_bias:
            y = y + bias_ref[...]
        if act == "relu2":
            y = jnp.square(jnp.maximum(y, 0.0))
        if has_res:
            y = res_ref[...] + mod_ref[0, gate_row:gate_row + 1, :] * y
        o_ref[...] = y.astype(o_ref.dtype)


def _matmul(a, w, *, tm, tn, tk, out_dtype=F32, bias=None, res=None, mod=None, gate_row=None,
            mod_index=None, act=None, rows=None, name=None):
    m = a.shape[0] if rows is None else rows
    kdim, n = w.shape
    assert a.shape[1] == kdim and m % tm == 0 and n % tn == 0 and kdim % tk == 0
    nk = kdim // tk
    in_specs = [pl.BlockSpec((tm, tk), lambda i, j, k: (i, k)),
                pl.BlockSpec((tk, tn), lambda i, j, k: (k, j))]
    args = [a, w]
    vmem = 2 * tm * tk * a.dtype.itemsize + 2 * tk * tn * w.dtype.itemsize + tm * tn * 4
    vmem += 2 * tm * tn * jnp.dtype(out_dtype).itemsize
    if bias is not None:
        in_specs.append(pl.BlockSpec((1, tn), lambda i, j, k: (0, j)))
        args.append(bias.reshape(1, n).astype(F32))
    if res is not None:
        in_specs.append(pl.BlockSpec((tm, tn), lambda i, j, k: (i, j)))
        in_specs.append(pl.BlockSpec((1, N_MOD, tn), lambda i, j, k: (mod_index(i), 0, j)))
        args += [res, mod]
        vmem += 2 * tm * tn * 4 + 2 * 8 * tn * 4
    kern = functools.partial(_matmul_kernel, nk=nk, has_bias=bias is not None,
                             has_res=res is not None, gate_row=gate_row, act=act)
    return pl.pallas_call(
        kern,
        grid=(m // tm, n // tn, nk),
        in_specs=in_specs,
        out_specs=pl.BlockSpec((tm, tn), lambda i, j, k: (i, j)),
        out_shape=jax.ShapeDtypeStruct((m, n), out_dtype),
        scratch_shapes=[pltpu.VMEM((tm, tn), F32)],
        compiler_params=_cparams(("parallel", "parallel", "arbitrary"), vmem),
        name=name,
    )(*args)


def _nm_matmul_kernel(*refs, has_mod, shift_row, scale_row, act):
    if has_mod:
        x_ref, g_ref, mod_ref, w_ref, o_ref, h_ref = refs
    else:
        x_ref, g_ref, w_ref, o_ref, h_ref = refs

    @pl.when(pl.program_id(1) == 0)
    def _():
        x = x_ref[...].astype(F32)
        y = x * lax.rsqrt(jnp.mean(x * x, axis=-1, keepdims=True) + NORM_EPS) * g_ref[...]
        if has_mod:
            y = y * (1.0 + mod_ref[0, scale_row:scale_row + 1, :]) + mod_ref[0, shift_row:shift_row + 1, :]
        h_ref[...] = y.astype(BF16)

    y = jnp.dot(h_ref[...], w_ref[...].astype(BF16), preferred_element_type=F32)
    if act == "relu2":
        y = jnp.square(jnp.maximum(y, 0.0))
    o_ref[...] = y.astype(o_ref.dtype)


def _nm_matmul(x, g, w, *, tm, tn, out_dtype, x_col_block=0, kdim=None, mod=None, shift_row=None,
               scale_row=None, mod_index=None, act=None, rows=None, name=None):
    m = x.shape[0] if rows is None else rows
    kdim = x.shape[1] if kdim is None else kdim
    n = w.shape[1]
    assert w.shape[0] == kdim and m % tm == 0 and n % tn == 0
    in_specs = [pl.BlockSpec((tm, kdim), lambda i, j: (i, x_col_block)),
                pl.BlockSpec((1, kdim), lambda i, j: (0, 0))]
    args = [x, g.reshape(1, kdim).astype(F32)]
    if mod is not None:
        in_specs.append(pl.BlockSpec((1, N_MOD, kdim), lambda i, j: (mod_index(i), 0, 0)))
        args.append(mod)
    in_specs.append(pl.BlockSpec((kdim, tn), lambda i, j: (0, j)))
    args.append(w)
    vmem = (2 * tm * kdim * x.dtype.itemsize + tm * kdim * 2 + 2 * kdim * tn * w.dtype.itemsize
            + 2 * tm * tn * jnp.dtype(out_dtype).itemsize + tm * tn * 4 + 4 * 8 * kdim * 4)
    kern = functools.partial(_nm_matmul_kernel, has_mod=mod is not None, shift_row=shift_row,
                             scale_row=scale_row, act=act)
    return pl.pallas_call(
        kern,
        grid=(m // tm, n // tn),
        in_specs=in_specs,
        out_specs=pl.BlockSpec((tm, tn), lambda i, j: (i, j)),
        out_shape=jax.ShapeDtypeStruct((m, n), out_dtype),
        scratch_shapes=[pltpu.VMEM((tm, kdim), BF16)],
        compiler_params=_cparams(("parallel", "arbitrary"), vmem),
        name=name,
    )(*args)


def _norm_mod_kernel(x_ref, g_ref, mod_ref, h_ref, *, shift_row, scale_row):
    x = x_ref[...]
    y = x * lax.rsqrt(jnp.mean(x * x, axis=-1, keepdims=True) + NORM_EPS) * g_ref[...]
    y = y * (1.0 + mod_ref[0, scale_row:scale_row + 1, :]) + mod_ref[0, shift_row:shift_row + 1, :]
    h_ref[...] = y.astype(h_ref.dtype)


def _norm_mod(xs, g, mod, *, shift_row, scale_row, mod_index, tm, rows):
    dm = xs.shape[1]
    return pl.pallas_call(
        functools.partial(_norm_mod_kernel, shift_row=shift_row, scale_row=scale_row),
        grid=(rows // tm,),
        in_specs=[pl.BlockSpec((tm, dm), lambda i: (i, 0)), pl.BlockSpec((1, dm), lambda i: (0, 0)),
                  pl.BlockSpec((1, N_MOD, dm), lambda i: (mod_index(i), 0, 0))],
        out_specs=pl.BlockSpec((tm, dm), lambda i: (i, 0)),
        out_shape=jax.ShapeDtypeStruct((rows, dm), BF16),
        compiler_params=_cparams(("parallel",), 2 * tm * dm * 6 + 4 * tm * dm * 4),
        name="norm_mod",
    )(xs, g.reshape(1, dm), mod)


def _fullk_kernel(*refs, has_res, has_cast, gate_row, act):
    refs = list(refs)
    if has_cast:
        cast_dst = refs.pop()
        cast_src = refs.pop(-2)
        cast_dst[...] = cast_src[...].astype(cast_dst.dtype)
    if has_res:
        a_ref, w_ref, res_ref, mod_ref, o_ref = refs
    else:
        a_ref, w_ref, o_ref = refs
    y = jnp.dot(a_ref[...], w_ref[...], preferred_element_type=F32)
    if act == "relu2":
        y = jnp.square(jnp.maximum(y, 0.0))
    if has_res:
        y = res_ref[...] + mod_ref[0, gate_row:gate_row + 1, :] * y
    o_ref[...] = y.astype(o_ref.dtype)


def _cast_rows_per_step(n_rows, n_steps):
    rows = 16
    while n_rows % rows or n_rows // rows > n_steps:
        rows *= 2
    return rows


def _fullk_matmul(a, w, *, tm, tn, out_dtype, act=None, res=None, mod=None, gate_row=None, mod_index=None,
                  rows=None, name=None, cast_src=None):
    m = a.shape[0] if rows is None else rows
    kdim, n = w.shape
    assert a.shape[1] == kdim and m % tm == 0 and n % tn == 0 and a.dtype == BF16 and w.dtype == BF16
    in_specs = [pl.BlockSpec((tm, kdim), lambda i, j: (i, 0)), pl.BlockSpec((kdim, tn), lambda i, j: (0, j))]
    args = [a, w]
    vmem = 2 * tm * kdim * 2 + 2 * kdim * tn * 2 + 2 * tm * tn * jnp.dtype(out_dtype).itemsize + 2 * tm * tn * 4
    if res is not None:
        in_specs += [pl.BlockSpec((tm, tn), lambda i, j: (i, j)),
                     pl.BlockSpec((1, N_MOD, tn), lambda i, j: (mod_index(i), 0, j))]
        args += [res, mod]
        vmem += 2 * tm * tn * 4
    out_specs = pl.BlockSpec((tm, tn), lambda i, j: (i, j))
    out_shape = jax.ShapeDtypeStruct((m, n), out_dtype)
    if cast_src is not None:
        nj = n // tn
        c_rows, c_cols = cast_src.shape
        cr = _cast_rows_per_step(c_rows, (m // tm) * nj)
        cast_spec = pl.BlockSpec((cr, c_cols), lambda i, j: (jnp.minimum(i * nj + j, c_rows // cr - 1), 0))
        in_specs.append(cast_spec)
        args.append(cast_src)
        out_specs = [out_specs, cast_spec]
        out_shape = [out_shape, jax.ShapeDtypeStruct(cast_src.shape, BF16)]
        vmem += 2 * cr * c_cols * 6
    return pl.pallas_call(
        functools.partial(_fullk_kernel, has_res=res is not None, has_cast=cast_src is not None,
                          gate_row=gate_row, act=act),
        grid=(m // tm, n // tn),
        in_specs=in_specs,
        out_specs=out_specs,
        out_shape=out_shape,
        compiler_params=_cparams(("arbitrary", "arbitrary"), vmem),
        name=name,
    )(*args)


def _merge_kernel(gl_ref, y0_ref, y1_ref, y2_ref, y3_ref, wb_ref, gu_ref, gb_ref, o_ref):
    gl = gl_ref[...].astype(BF16)
    acc = None
    for i, y_ref in enumerate((y0_ref, y1_ref, y2_ref, y3_ref)):
        gate = jax.nn.sigmoid(jnp.dot(gl, gu_ref[i], preferred_element_type=F32) + gb_ref[i:i + 1, :])
        term = gate * jnp.dot(y_ref[...], wb_ref[i], preferred_element_type=F32)
        acc = term if acc is None else acc + term
    o_ref[...] = acc.astype(o_ref.dtype)


def _merge(p, ys, wb, gu, gb, *, tm, tn, rows):
    n = wb.shape[2]
    bw = wb.shape[1]
    gr = gu.shape[1]
    y_spec = pl.BlockSpec((tm, bw), lambda i, j: (i, 0))
    vmem = (2 * tm * gr * 4 + 4 * 2 * tm * bw * 2 + 2 * 4 * bw * tn * 2 + 2 * 4 * gr * tn * 2
            + 2 * tm * tn * 2 + 3 * tm * tn * 4)
    return pl.pallas_call(
        _merge_kernel,
        grid=(rows // tm, n // tn),
        in_specs=[pl.BlockSpec((tm, gr), lambda i, j: (i, C_GL // GATE_RANK)),
                  y_spec, y_spec, y_spec, y_spec,
                  pl.BlockSpec((4, bw, tn), lambda i, j: (0, 0, j)),
                  pl.BlockSpec((4, gr, tn), lambda i, j: (0, 0, j)),
                  pl.BlockSpec((4, tn), lambda i, j: (0, j))],
        out_specs=pl.BlockSpec((tm, tn), lambda i, j: (i, j)),
        out_shape=jax.ShapeDtypeStruct((rows, n), BF16),
        compiler_params=_cparams(("parallel", "arbitrary"), vmem),
        name="merge",
    )(p, *ys, wb, gu, gb)


def _softmax_parts(s_list, scale):
    m = None
    for s in s_list:
        sm = jnp.max(s, axis=-1, keepdims=True)
        m = sm if m is None else jnp.maximum(m, sm)
    e_list = [jnp.exp2((s - m) * (scale * LOG2_E)) for s in s_list]
    l = None
    for e in e_list:
        es = jnp.sum(e, axis=-1, keepdims=True)
        l = es if l is None else l + es
    return e_list, l


def _attend(q, ks, vs, scale):
    e_list, l = _softmax_parts([_dot_nt(q, k) for k in ks], scale)
    o = None
    for e, v in zip(e_list, vs):
        t = jnp.dot(e.astype(BF16), v, preferred_element_type=F32)
        o = t if o is None else o + t
    return o / l


def _mla_kernel(cos_ref, sin_ref, q_ref, knx_ref, knc_ref, krx_ref, krc_ref, vx_ref, vc_ref, o_ref,
                kx_s, kc_s, *, nq, tq, scale):
    i = pl.program_id(2)

    @pl.when(i == 0)
    def _():
        kx_s[:, :MLA_NOPE] = knx_ref[...]
        kx_s[:, MLA_NOPE:] = _rope128(krx_ref[...], cos_ref[...], sin_ref[...]).astype(BF16)
        kc_s[:, :MLA_NOPE] = knc_ref[...]
        kc_s[:, MLA_NOPE:] = krc_ref[...].astype(BF16)

    @pl.when(i < nq)
    def _():
        row0 = pl.multiple_of(jnp.minimum(i, nq - 1) * tq, tq)
        q = q_ref[...]
        q_rope = _rope128(q[:, MLA_NOPE:].astype(F32), cos_ref[pl.ds(row0, tq), :], sin_ref[pl.ds(row0, tq), :])
        q = jnp.concatenate([q[:, :MLA_NOPE], q_rope.astype(BF16)], axis=1)
        o = _attend(q, (kc_s[...], kx_s[...]), (vc_ref[...], vx_ref[...]), scale)
        o_ref[...] = o.astype(o_ref.dtype)

    @pl.when(i >= nq)
    def _():
        o = _attend(q_ref[...], (kc_s[...],), (vc_ref[...],), scale)
        o_ref[...] = o.astype(o_ref.dtype)


def _diff_finish(o1, o2, lam_ref, g_ref, out_scale):
    o = o1 - lam_ref[...] * o2
    y = o * lax.rsqrt(jnp.mean(o * o, axis=-1, keepdims=True) + NORM_EPS) * g_ref[...]
    return y * out_scale


def _diff_kernel(cos_ref, sin_ref, lam_ref, g_ref, q_ref, kx_ref, kc_ref, vx_ref, vc_ref, o_ref,
                 kx_s, kc_s, vx_s, vc_s, *, nq, tq, scale, out_scale):
    i = pl.program_id(2)

    @pl.when(i == 0)
    def _():
        kx_s[...] = _rope128(kx_ref[...], cos_ref[...], sin_ref[...]).astype(BF16)
        kc_s[...] = kc_ref[...].astype(BF16)
        vx_s[...] = vx_ref[...].astype(BF16)
        vc_s[...] = vc_ref[...].astype(BF16)

    def halves(q):
        lane = lax.broadcasted_iota(jnp.int32, q.shape, 1)
        return jnp.where(lane < DIFF_QK, q, 0.0).astype(BF16), jnp.where(lane < DIFF_QK, 0.0, q).astype(BF16)

    @pl.when(i < nq)
    def _():
        row0 = pl.multiple_of(jnp.minimum(i, nq - 1) * tq, tq)
        q1, q2 = halves(_rope128(q_ref[...], cos_ref[pl.ds(row0, tq), :], sin_ref[pl.ds(row0, tq), :]))
        ks, vs = (kc_s[...], kx_s[...]), (vc_s[...], vx_s[...])
        y = _diff_finish(_attend(q1, ks, vs, scale), _attend(q2, ks, vs, scale), lam_ref, g_ref, out_scale)
        o_ref[...] = y.astype(o_ref.dtype)

    @pl.when(i >= nq)
    def _():
        q1, q2 = halves(q_ref[...])
        ks, vs = (kc_s[...],), (vc_s[...],)
        y = _diff_finish(_attend(q1, ks, vs, scale), _attend(q2, ks, vs, scale), lam_ref, g_ref, out_scale)
        o_ref[...] = y.astype(o_ref.dtype)


def _query_row_block(b, i, nq, nqc, n_x, tq):
    return jnp.where(i < nq, b * nq + i, n_x // tq + b * nqc + (i - nq))


def _mla_attention(q, kv, p, cos_t, sin_t, *, n_batch, t_len, c_len, tq, with_ctx):
    n_x = n_batch * t_len
    nq = t_len // tq
    nqc = c_len // tq if with_ctx else 0
    out_rows = n_x + (n_batch * c_len if with_ctx else 0)
    cblk0 = n_x // c_len
    qmap = lambda b, h, i: (_query_row_block(b, i, nq, nqc, n_x, tq), h)
    full = lambda b, h, i: (0, 0)
    vmem = (4 * t_len * LANES * 4 + 2 * tq * 256 * 2 + 2 * (t_len + c_len) * LANES * (2 + 4 + 2)
            + (t_len + c_len) * 256 * 2 + 2 * tq * LANES * 2 + 6 * tq * (t_len + c_len) * 4)
    return pl.pallas_call(
        functools.partial(_mla_kernel, nq=nq, tq=tq, scale=(MLA_NOPE + MLA_ROPE) ** -0.5),
        grid=(n_batch, MLA_HEADS, nq + nqc),
        in_specs=[pl.BlockSpec((t_len, LANES), full), pl.BlockSpec((t_len, LANES), full),
                  pl.BlockSpec((tq, MLA_QK_PAD), qmap),
                  pl.BlockSpec((t_len, MLA_NOPE), lambda b, h, i: (b, h)),
                  pl.BlockSpec((c_len, MLA_NOPE), lambda b, h, i: (cblk0 + b, h)),
                  pl.BlockSpec((t_len, LANES), lambda b, h, i: (b, C_KROPE // LANES)),
                  pl.BlockSpec((c_len, LANES), lambda b, h, i: (cblk0 + b, C_KROPE // LANES)),
                  pl.BlockSpec((t_len, MLA_V), lambda b, h, i: (b, MLA_HEADS + h)),
                  pl.BlockSpec((c_len, MLA_V), lambda b, h, i: (cblk0 + b, MLA_HEADS + h))],
        out_specs=pl.BlockSpec((tq, MLA_V), qmap),
        out_shape=jax.ShapeDtypeStruct((out_rows, MLA_HEADS * MLA_V), BF16),
        scratch_shapes=[pltpu.VMEM((t_len, MLA_QK_PAD), BF16), pltpu.VMEM((c_len, MLA_QK_PAD), BF16)],
        compiler_params=_cparams(("parallel", "parallel", "arbitrary"), vmem),
        name="mla_attention",
    )(cos_t, sin_t, q, kv, kv, p, p, kv, kv)


def _diff_attention(p, cos_t, sin_t, lam_row, g_row, *, n_batch, t_len, c_len, tq, with_ctx, out_scale):
    n_x = n_batch * t_len
    nq = t_len // tq
    nqc = c_len // tq if with_ctx else 0
    out_rows = n_x + (n_batch * c_len if with_ctx else 0)
    cblk0 = n_x // c_len
    full = lambda b, h, i: (0, 0)
    qblk, kblk, vblk = C_DQ // LANES, C_DK // LANES, C_DV // LANES
    vmem = (4 * t_len * LANES * 4 + 2 * tq * LANES * 4 + 4 * (t_len + c_len) * LANES * 4
            + 2 * (t_len + c_len) * LANES * 2 + 2 * tq * LANES * 2 + 8 * tq * (t_len + c_len) * 4)
    return pl.pallas_call(
        functools.partial(_diff_kernel, nq=nq, tq=tq, scale=DIFF_QK ** -0.5, out_scale=out_scale),
        grid=(n_batch, DIFF_HEADS, nq + nqc),
        in_specs=[pl.BlockSpec((t_len, LANES), full), pl.BlockSpec((t_len, LANES), full),
                  pl.BlockSpec((1, DIFF_V), full), pl.BlockSpec((1, DIFF_V), full),
                  pl.BlockSpec((tq, LANES), lambda b, h, i: (_query_row_block(b, i, nq, nqc, n_x, tq), qblk + h)),
                  pl.BlockSpec((t_len, LANES), lambda b, h, i: (b, kblk + h)),
                  pl.BlockSpec((c_len, LANES), lambda b, h, i: (cblk0 + b, kblk + h)),
                  pl.BlockSpec((t_len, LANES), lambda b, h, i: (b, vblk + h)),
                  pl.BlockSpec((c_len, LANES), lambda b, h, i: (cblk0 + b, vblk + h))],
        out_specs=pl.BlockSpec((tq, DIFF_V), lambda b, h, i: (_query_row_block(b, i, nq, nqc, n_x, tq), h)),
        out_shape=jax.ShapeDtypeStruct((out_rows, DIFF_HEADS * DIFF_V), BF16),
        scratch_shapes=[pltpu.VMEM((t_len, LANES), BF16), pltpu.VMEM((c_len, LANES), BF16),
                        pltpu.VMEM((t_len, LANES), BF16), pltpu.VMEM((c_len, LANES), BF16)],
        compiler_params=_cparams(("parallel", "parallel", "arbitrary"), vmem),
        name="diff_attention",
    )(cos_t, sin_t, lam_row, g_row, p, p, p, p, p)


def _prep_kernel(r_ref, k_ref, v_ref, cb_ref, cc_ref, cu_ref, lo_ref, hp_ref, hn_ref,
                 mu_ref, mul_ref, kk_w_ref, ka_w_ref, rk_w_ref, w0_ref, a0_ref, w2_ref, a2_ref, g2_ref,
                 cw_ref,
                 r_o, kk_o, v_o, lw_o, kd_o, ka_o, bonus_o, g_o, conv_o, *, tr, n_x, t_len, c_len):
    i = pl.program_id(0)
    g0 = i * tr
    seq = jnp.where(g0 < n_x, t_len, c_len)
    has_prev = ((g0 % seq) != 0).astype(F32)
    has_next = (((g0 + tr) % seq) != 0).astype(F32)
    row = lax.broadcasted_iota(jnp.int32, (tr, 1), 0)

    def neighbours(x, col0):
        width = x.shape[1]
        before = hp_ref[HALO - 1:HALO, col0:col0 + width] * has_prev
        after = hn_ref[0:1, col0:col0 + width] * has_next
        prev = jnp.where(row == 0, before, pltpu.roll(x, 1, 0))
        nxt = jnp.where(row == tr - 1, after, pltpu.roll(x, tr - 1, 0))
        return prev, nxt

    def shifted(x, col0, mu0, mu1):
        prev, nxt = neighbours(x, col0)
        return x + mu0 * (prev - x) + mu1 * (nxt - x)

    r = shifted(r_ref[...], C_R, mu_ref[0:1, :BRANCH_W], mu_ref[1:2, :BRANCH_W])
    k = shifted(k_ref[...], C_K, mu_ref[0:1, BRANCH_W:2 * BRANCH_W], mu_ref[1:2, BRANCH_W:2 * BRANCH_W])
    v = shifted(v_ref[...], C_V, mu_ref[0:1, 2 * BRANCH_W:], mu_ref[1:2, 2 * BRANCH_W:])
    lo = shifted(lo_ref[...], C_LORA, mul_ref[0:1, :], mul_ref[1:2, :])
    wd = jnp.tanh(lo[:, :2 * RWKV_LORA])
    ad = lo[:, 2 * RWKV_LORA:4 * RWKV_LORA]
    gd = jax.nn.sigmoid(lo[:, 4 * RWKV_LORA:])
    w_pre = _dot(wd, w2_ref[...]) + w0_ref[...]
    a_sig = jax.nn.sigmoid(_dot(ad, a2_ref[...]) + a0_ref[...])
    g_o[...] = _dot(gd, g2_ref[...])
    w_log = -(jnp.maximum(-w_pre, 0.0) + jnp.log(1.0 + jnp.exp(-jnp.abs(w_pre)))) - 0.5
    lw = -jnp.exp(w_log)
    kkf = k * kk_w_ref[...]
    kk = kkf * lax.rsqrt(_head_sums(kkf * kkf) + 1e-12)
    r_o[...] = r
    kk_o[...] = kk
    v_o[...] = v
    k_sum = None
    for d in range(2):
        a_d = a_sig[:, d * BRANCH_W:(d + 1) * BRANCH_W]
        k_d = k * (1.0 + (a_d - 1.0) * ka_w_ref[...])
        lw_o[d] = lw[:, d * BRANCH_W:(d + 1) * BRANCH_W]
        kd_o[d] = k_d
        ka_o[d] = kk * a_d
        k_sum = k_d if k_sum is None else k_sum + k_d
    bonus_o[...] = _head_sums(r * k_sum * rk_w_ref[...]) * v

    z = cc_ref[...] * cu_ref[...]
    z_before = hp_ref[HALO - 1:HALO, C_CC:C_CC + BRANCH_W] * hp_ref[HALO - 1:HALO, C_CU:C_CU + BRANCH_W] * has_prev
    z_after = hn_ref[0:1, C_CC:C_CC + BRANCH_W] * hn_ref[0:1, C_CU:C_CU + BRANCH_W] * has_next
    z_prev = jnp.where(row == 0, z_before, pltpu.roll(z, 1, 0))
    z_next = jnp.where(row == tr - 1, z_after, pltpu.roll(z, tr - 1, 0))
    y = cb_ref[...] * (cw_ref[0:1, :] * z_prev + cw_ref[1:2, :] * z + cw_ref[2:3, :] * z_next)
    conv_o[...] = y.astype(conv_o.dtype)


def _mixer_prep(p, mu_rkv, mu_lora, kk_w, ka_w, rk_w, w0, a0, w2bd, a2bd, g2p, conv_w, *, tr, n_x, t_len,
                c_len):
    n_rows = p.shape[0]
    bw = BRANCH_W
    last_halo = n_rows // HALO - 1
    col = lambda c: (lambda i: (i, c))
    const = lambda i: (0, 0)
    main = [pl.BlockSpec((tr, bw), col(C_R // bw)), pl.BlockSpec((tr, bw), col(C_K // bw)),
            pl.BlockSpec((tr, bw), col(C_V // bw)), pl.BlockSpec((tr, bw), col(C_CB // bw)),
            pl.BlockSpec((tr, bw), col(C_CC // bw)), pl.BlockSpec((tr, bw), col(C_CU // bw)),
            pl.BlockSpec((tr, RWKV_LORA_PAD), col(C_LORA // RWKV_LORA_PAD)),
            pl.BlockSpec((HALO, P_COLS), lambda i: (jnp.maximum(i * (tr // HALO) - 1, 0), 0)),
            pl.BlockSpec((HALO, P_COLS), lambda i: (jnp.minimum((i + 1) * (tr // HALO), last_halo), 0))]
    params = [mu_rkv, mu_lora, kk_w, ka_w, rk_w, w0, a0, w2bd, a2bd, g2p, conv_w]
    param_specs = [pl.BlockSpec(a.shape, const) for a in params]
    row_spec = pl.BlockSpec((tr, bw), lambda i: (i, 0))
    dir_spec = pl.BlockSpec((2, tr, bw), lambda i: (0, i, 0))
    f32_rows = jax.ShapeDtypeStruct((n_rows, bw), F32)
    f32_dirs = jax.ShapeDtypeStruct((2, n_rows, bw), F32)
    vmem = 2 * (7 * tr * bw * 4 + 2 * HALO * P_COLS * 4 + 12 * tr * bw * 4) + 30 * tr * bw * 4
    return pl.pallas_call(
        functools.partial(_prep_kernel, tr=tr, n_x=n_x, t_len=t_len, c_len=c_len),
        grid=(n_rows // tr,),
        in_specs=main + param_specs,
        out_specs=[row_spec, row_spec, row_spec, dir_spec, dir_spec, dir_spec, row_spec, row_spec, row_spec],
        out_shape=[f32_rows, f32_rows, f32_rows, f32_dirs, f32_dirs, f32_dirs, f32_rows, f32_rows,
                   jax.ShapeDtypeStruct((n_rows, bw), BF16)],
        compiler_params=_cparams(("parallel",), vmem),
        name="mixer_prep",
    )(p, p, p, p, p, p, p, p, p, *params)


def _rwkv_kernel(r0_ref, kk0_ref, v0_ref, r1_ref, kk1_ref, v1_ref, lw0_ref, kd0_ref, ka0_ref,
                 lw1_ref, kd1_ref, ka1_ref, o0_ref, o1_ref, s_ref):
    C = RWKV_CHUNK
    W = 2 * RWKV_HEAD
    n_pairs = RWKV_HEADS // 2

    @pl.when(pl.program_id(1) == 0)
    def _():
        s_ref[...] = jnp.zeros_like(s_ref)

    mm, mm_nt = _dot, _dot_nt
    lane = lax.broadcasted_iota(jnp.int32, (C, W), 1)
    row = lax.broadcasted_iota(jnp.int32, (C, W), 0)
    first = lane < RWKV_HEAD
    rc = lax.broadcasted_iota(jnp.int32, (C, C), 0)
    cc = lax.broadcasted_iota(jnp.int32, (C, C), 1)

    def bdiag(x):
        return jnp.concatenate([jnp.where(first, x, jnp.zeros_like(x)), jnp.where(first, jnp.zeros_like(x), x)],
                               axis=0)

    def direction(sign, r_ref, kk_ref, v_ref, lw_ref, kd_ref, ka_ref):
        m_incl = jnp.where((rc - cc) * sign >= 0, 1.0, 0.0).astype(BF16)
        lw = lw_ref[0]
        cum = _dot_exact_lhs(m_incl, lw)
        tot = jnp.sum(lw, axis=0, keepdims=True)
        e_ninc = jnp.exp(-cum)
        e_rem = jnp.exp(tot - cum)
        kk, ka, kd = kk_ref[...], ka_ref[0], kd_ref[0]
        order = (row - (lane % RWKV_HEAD)) * sign
        return dict(a_t=kk * jnp.exp(cum - lw), b_t=-ka * e_ninc, k_t=kd * e_ninc, r_t=r_ref[...] * jnp.exp(cum),
                    b_h=-ka * e_rem, k_h=kd * e_rem, v=v_ref[...], e_tot=jnp.exp(tot), strict=order > 0,
                    incl=order >= 0, eye=jnp.where(order == 0, 1.0, 0.0).astype(F32))

    dirs = (direction(1, r0_ref, kk0_ref, v0_ref, lw0_ref, kd0_ref, ka0_ref),
            direction(-1, r1_ref, kk1_ref, v1_ref, lw1_ref, kd1_ref, ka1_ref))
    o_refs = (o0_ref, o1_ref)

    chains = [(d, p) for d in range(2) for p in range(n_pairs)]
    n = range(len(chains))
    sl = [slice(W * p, W * (p + 1)) for (_, p) in chains]
    dd = [dirs[d] for (d, _) in chains]
    ar = [jnp.concatenate([dd[c]["a_t"][:, sl[c]], dd[c]["r_t"][:, sl[c]]], axis=0).astype(BF16) for c in n]
    a_bd = [bdiag(dd[c]["a_t"][:, sl[c]].astype(BF16)) for c in n]
    bk_bd = [jnp.concatenate([bdiag(dd[c]["b_t"][:, sl[c]].astype(BF16)), bdiag(dd[c]["k_t"][:, sl[c]].astype(BF16))],
                             axis=0) for c in n]
    v_bd = [bdiag(dd[c]["v"][:, sl[c]].astype(BF16)) for c in n]
    g = [mm_nt(ar[c], bk_bd[c]) for c in n]
    l_pow = [jnp.where(dd[c]["strict"], g[c][:C, :W], 0.0) for c in n]
    m_ak = [jnp.where(dd[c]["strict"], g[c][:C, W:], 0.0).astype(BF16) for c in n]
    a_r = [jnp.where(jnp.concatenate([dd[c]["incl"], dd[c]["incl"]], axis=1), g[c][C:], 0.0).astype(BF16) for c in n]
    mv = [mm(m_ak[c], v_bd[c]) for c in n]
    l_bd = [bdiag(l_pow[c].astype(BF16)) for c in n]
    t_inv = [dd[c]["eye"] + l_pow[c] for c in n]
    l_pow = [mm(l_pow[c], l_bd[c]) for c in n]
    for _ in range(int(math.log2(C)) - 2):
        l_bd = [bdiag(l_pow[c].astype(BF16)) for c in n]
        lt = [mm(jnp.concatenate([l_pow[c], t_inv[c]], axis=0), l_bd[c]) for c in n]
        l_pow = [lt[c][:C] for c in n]
        t_inv = [t_inv[c] + lt[c][C:] for c in n]
    t_inv = [t_inv[c] + mm(t_inv[c], bdiag(l_pow[c].astype(BF16))) for c in n]
    wu = [mm(t_inv[c], jnp.concatenate([a_bd[c], bdiag(mv[c].astype(BF16))], axis=1)) for c in n]
    s_bd = [s_ref[d, p] for (d, p) in chains]
    wr = [mm(jnp.concatenate([wu[c][:, :W], dd[c]["r_t"][:, sl[c]]], axis=0), s_bd[c]) for c in n]
    zv = [jnp.concatenate([bdiag((wr[c][:C] + wu[c][:, W:]).astype(BF16)), v_bd[c]], axis=0) for c in n]
    for c, (d, p) in enumerate(chains):
        o_refs[d][:, sl[c]] = wr[c][C:] + mm(a_r[c], zv[c])
    for c, (d, p) in enumerate(chains):
        bk_h = jnp.concatenate([bdiag(dd[c]["b_h"][:, sl[c]]), bdiag(dd[c]["k_h"][:, sl[c]])], axis=0)
        tot_col = jnp.broadcast_to(dd[c]["e_tot"][:, sl[c]], (W, W)).T
        s_ref[d, p] = tot_col * s_bd[c] + mm(bk_h.T, zv[c])


def _rwkv_scan(r, kk, v, lw, kd, ka, *, n_batch, t_len, c_len):
    C = RWKV_CHUNK
    rows, width = r.shape
    nct, nxt = c_len // C, t_len // C
    ctx_blk0 = n_batch * t_len // C

    def blk(b, d, s):
        j_c = s if d == 0 else nct - 1 - s
        j_x = s - nct if d == 0 else nxt - 1 - (s - nct)
        return jnp.where(s < nct, ctx_blk0 + b * nct + j_c, b * nxt + j_x)

    def shared(d):
        return pl.BlockSpec((C, width), lambda b, s: (blk(b, d, s), 0))

    def per_dir(d):
        return pl.BlockSpec((1, C, width), lambda b, s: (d, blk(b, d, s), 0))

    vmem = 2 * 14 * C * width * 4 + RWKV_HEADS * 128 * 128 * 4 + 128 * C * width * 4
    out = jax.ShapeDtypeStruct((rows, width), F32)
    return pl.pallas_call(
        _rwkv_kernel,
        grid=(n_batch, nct + nxt),
        in_specs=[shared(0), shared(0), shared(0), shared(1), shared(1), shared(1),
                  per_dir(0), per_dir(0), per_dir(0), per_dir(1), per_dir(1), per_dir(1)],
        out_specs=[shared(0), shared(1)],
        out_shape=[out, out],
        scratch_shapes=[pltpu.VMEM((2, RWKV_HEADS // 2, 128, 128), F32)],
        compiler_params=_cparams(("parallel", "arbitrary"), vmem),
        name="rwkv_scan",
    )(r, kk, v, r, kk, v, lw, kd, ka, lw, kd, ka)


def _rwkv_readout_kernel(o0_ref, o1_ref, bonus_ref, g_ref, lng_ref, lnb_ref, y_ref):
    o = o0_ref[...] + o1_ref[...]
    mean = _head_sums(o) * (1.0 / RWKV_HEAD)
    cen = o - mean
    var = _head_sums(cen * cen) * (1.0 / RWKV_HEAD)
    o_n = cen * lax.rsqrt(var + RWKV_GN_EPS) * lng_ref[...] + lnb_ref[...]
    y_ref[...] = ((o_n + bonus_ref[...]) * g_ref[...]).astype(y_ref.dtype)


def _rwkv_readout(o_dirs, bonus, g_out, ln_g, ln_b, *, tr):
    n_rows, bw = bonus.shape
    row_spec = pl.BlockSpec((tr, bw), lambda i: (i, 0))
    const = pl.BlockSpec((1, bw), lambda i: (0, 0))
    return pl.pallas_call(
        _rwkv_readout_kernel,
        grid=(n_rows // tr,),
        in_specs=[row_spec, row_spec, row_spec, row_spec, const, const],
        out_specs=row_spec,
        out_shape=jax.ShapeDtypeStruct((n_rows, bw), BF16),
        compiler_params=_cparams(("parallel",), 2 * 5 * tr * bw * 4 + 16 * tr * bw * 4),
        name="rwkv_readout",
    )(o_dirs[0], o_dirs[1], bonus, g_out, ln_g.reshape(1, bw), ln_b.reshape(1, bw))


def _rmsnorm_kernel(x_ref, g_ref, o_ref):
    x = x_ref[...]
    o_ref[...] = x * lax.rsqrt(jnp.mean(x * x, axis=-1, keepdims=True) + NORM_EPS) * g_ref[...]


def _final_norm(xs, g, *, rows, tm):
    dm = xs.shape[1]
    return pl.pallas_call(
        _rmsnorm_kernel,
        grid=(rows // tm,),
        in_specs=[pl.BlockSpec((tm, dm), lambda i: (i, 0)), pl.BlockSpec((1, dm), lambda i: (0, 0))],
        out_specs=pl.BlockSpec((tm, dm), lambda i: (i, 0)),
        out_shape=jax.ShapeDtypeStruct((rows, dm), F32),
        compiler_params=_cparams(("parallel",), 4 * tm * dm * 4),
        name="final_norm",
    )(xs, g.reshape(1, dm))


def _rope_tables128(n_tokens):
    rows = n_tokens // GRID_W
    row = jnp.repeat(jnp.arange(rows, dtype=F32), GRID_W)
    col = jnp.tile(jnp.arange(GRID_W, dtype=F32), rows)
    n_freq = 64 // 4
    inv = ROPE_BASE ** (-jnp.arange(n_freq, dtype=F32) / n_freq)
    ang = jnp.concatenate([row[:, None] * inv, col[:, None] * inv], axis=-1)
    cos, sin = jnp.cos(ang), jnp.sin(ang)
    return jnp.concatenate([cos, cos, cos, cos], axis=-1), jnp.concatenate([-sin, sin, -sin, sin], axis=-1)


def _block_diag2(w2):
    z = jnp.zeros_like(w2[0])
    return jnp.concatenate([jnp.concatenate([w2[0], z], axis=1), jnp.concatenate([z, w2[1]], axis=1)], axis=0)


def kernel(x, c, ctx, c_ctx, norm1_g, norm2_g, mod_down, mod_up, mod_b, w_in, mla_q_norm_g, mla_w_uq,
           mla_kv_norm_g, mla_w_ukv, rwkv_mu, rwkv_w0, rwkv_w2, rwkv_a0, rwkv_a2, rwkv_g2, rwkv_k_k,
           rwkv_k_a, rwkv_r_k, rwkv_ln_g, rwkv_ln_b, conv_w, diff_lambda, diff_norm_g, w_branch, gate_down,
           gate_up, gate_b, w_out, mlp_w1, mlp_w2, final_norm_g):
    n_batch, t_len, dm = x.shape
    c_len = ctx.shape[1]
    depth = w_in.shape[0]
    bw = BRANCH_W
    n_x = n_batch * t_len
    n_c = n_batch * c_len
    n_rows = n_x + n_c
    tm = 512 if (t_len % 512 == 0 and n_c % 512 == 0) else 256
    tq = min(256, c_len)
    tr = min(256, c_len)
    assert dm == D_MODEL and t_len % tm == 0 and n_c % tm == 0 and n_x % c_len == 0
    assert t_len % tq == 0 and c_len % tq == 0 and c_len % RWKV_CHUNK == 0 and t_len % c_len == 0
    tm_big = 1024 if (t_len % 1024 == 0 and n_c % 1024 == 0) else tm
    mod_index = functools.partial(_mod_index, tm=tm, n_x_rows=n_x, t_len=t_len, n_batch=n_batch)
    mod_index_big = functools.partial(_mod_index, tm=tm_big, n_x_rows=n_x, t_len=t_len, n_batch=n_batch)
    mod_index_nm = functools.partial(_mod_index, tm=256, n_x_rows=n_x, t_len=t_len, n_batch=n_batch)

    zeros = lambda *s: jnp.zeros(s, F32)
    rw0 = 1088
    cv0 = rw0 + 3 * bw + 4 * RWKV_LORA + RWKV_GATE_LORA
    lora_w = 4 * RWKV_LORA + RWKV_GATE_LORA
    w_in_p = jnp.concatenate(
        [w_in[:, :, :1024], w_in[:, :, rw0:rw0 + 3 * bw], w_in[:, :, cv0:],
         w_in[:, :, rw0 + 3 * bw:cv0], zeros(depth, dm, RWKV_LORA_PAD - lora_w), gate_down,
         w_in[:, :, 1024:rw0], zeros(depth, dm, P_COLS - C_KROPE - MLA_ROPE)], axis=-1).astype(BF16)
    w_uq_p = jnp.pad(mla_w_uq.reshape(depth, MLA_Q_LORA, MLA_HEADS, MLA_NOPE + MLA_ROPE),
                     ((0, 0), (0, 0), (0, 0), (0, MLA_QK_PAD - MLA_NOPE - MLA_ROPE))
                     ).reshape(depth, MLA_Q_LORA, MLA_HEADS * MLA_QK_PAD).astype(BF16)
    w_ukv_r = mla_w_ukv.reshape(depth, MLA_KV_LORA, MLA_HEADS, MLA_NOPE + MLA_V)
    w_ukv_p = jnp.concatenate([w_ukv_r[..., :MLA_NOPE].reshape(depth, MLA_KV_LORA, -1),
                               w_ukv_r[..., MLA_NOPE:].reshape(depth, MLA_KV_LORA, -1)], axis=-1).astype(BF16)
    w_branch_b = w_branch.astype(BF16)
    gate_up_b = jnp.moveaxis(gate_up, 2, 1).astype(BF16)
    w_out_b = w_out.astype(BF16)
    g2_p = jnp.pad(rwkv_g2, ((0, 0), (0, RWKV_LORA_PAD - 4 * RWKV_LORA - RWKV_GATE_LORA), (0, 0))).astype(BF16)
    mu_lora = jnp.pad(rwkv_mu[:, :, 3 * bw:], ((0, 0), (0, 0), (0, RWKV_LORA_PAD - lora_w)))

    cond = jnp.concatenate([c, c_ctx[None, :], zeros(16 - n_batch - 1, dm)], axis=0)
    cond = jax.nn.silu(cond)
    mods = []
    for l in range(depth):
        low = _matmul(cond, mod_down[l], tm=16, tn=MOD_RANK, tk=dm, name="mod_down")
        up = _matmul(low, mod_up[l], tm=16, tn=2048, tk=MOD_RANK, bias=mod_b[l], name="mod_up")
        mods.append(up.reshape(16, N_MOD, dm))

    cos_t, sin_t = _rope_tables128(t_len)
    xs = jnp.concatenate([x.reshape(n_x, dm), ctx.reshape(n_c, dm)], axis=0)

    for l in range(depth):
        need_ctx = l < depth - 1
        mod = mods[l]
        lam_init = 0.8 - 0.6 * math.exp(-0.3 * l)
        lq1, lk1, lq2, lk2 = diff_lambda[l]
        lam = jnp.exp(jnp.sum(lq1 * lk1)) - jnp.exp(jnp.sum(lq2 * lk2)) + lam_init
        lam_row = jnp.full((1, DIFF_V), 1.0, F32) * lam

        h1 = _norm_mod(xs, norm1_g[l], mod, shift_row=0, scale_row=1, mod_index=mod_index_nm, tm=256,
                       rows=n_rows)
        p, w1_b = _fullk_matmul(h1, w_in_p[l], tm=tm_big, tn=512, out_dtype=F32, name="in_proj",
                                cast_src=mlp_w1[l])

        q = _nm_matmul(p, mla_q_norm_g[l], w_uq_p[l], tm=tm, tn=512, out_dtype=BF16,
                       x_col_block=C_CQ // MLA_Q_LORA, kdim=MLA_Q_LORA, name="mla_q")
        kv = _nm_matmul(p, mla_kv_norm_g[l], w_ukv_p[l], tm=tm, tn=512, out_dtype=BF16,
                        x_col_block=C_CKV // MLA_KV_LORA, kdim=MLA_KV_LORA, name="mla_kv")
        y_mla = _mla_attention(q, kv, p, cos_t, sin_t, n_batch=n_batch, t_len=t_len, c_len=c_len, tq=tq,
                               with_ctx=need_ctx)

        y_diff = _diff_attention(p, cos_t, sin_t, lam_row, diff_norm_g[l].reshape(1, DIFF_V),
                                 n_batch=n_batch, t_len=t_len, c_len=c_len, tq=tq, with_ctx=need_ctx,
                                 out_scale=1.0 - lam_init)

        r_, kk, v_, lw, k_dir, kka, bonus, g_out, y_conv = _mixer_prep(
            p, rwkv_mu[l, :, :3 * bw], mu_lora[l], rwkv_k_k[l].reshape(1, bw), rwkv_k_a[l].reshape(1, bw),
            rwkv_r_k[l].reshape(1, bw), rwkv_w0[l].reshape(1, 2 * bw), rwkv_a0[l].reshape(1, 2 * bw),
            _block_diag2(rwkv_w2[l]).astype(BF16), _block_diag2(rwkv_a2[l]).astype(BF16), g2_p[l],
            conv_w[l], tr=min(128, tr), n_x=n_x, t_len=t_len, c_len=c_len)
        o_dirs = _rwkv_scan(r_, kk, v_, lw, k_dir, kka, n_batch=n_batch, t_len=t_len, c_len=c_len)
        y_rwkv = _rwkv_readout(o_dirs, bonus, g_out, rwkv_ln_g[l], rwkv_ln_b[l], tr=tr)

        rows = n_rows if need_ctx else n_x
        acc = _merge(p, (y_mla, y_rwkv, y_conv, y_diff), w_branch_b[l], gate_up_b[l], gate_b[l],
                     tm=tm_big, tn=512, rows=rows)
        xs_new = _fullk_matmul(acc, w_out_b[l], tm=tm_big, tn=512, out_dtype=F32, res=xs, mod=mod, gate_row=2,
                               mod_index=mod_index_big, rows=rows, name="out_proj")

        h2 = _norm_mod(xs_new, norm2_g[l], mod, shift_row=3, scale_row=4, mod_index=mod_index_nm, tm=256,
                       rows=rows)
        hid, w2_b = _fullk_matmul(h2, w1_b, tm=tm_big, tn=512, out_dtype=BF16, act="relu2", name="mlp_up",
                                  cast_src=mlp_w2[l])
        xs = _matmul(hid, w2_b, tm=tm_big, tn=1024, tk=2048, res=xs_new, mod=mod, gate_row=5,
                     mod_index=mod_index_big, rows=rows, name="mlp_down")

    out = _final_norm(xs, final_norm_g, rows=n_x, tm=tm)
    return out.reshape(n_batch, t_len, dm)
```

```python
import functools
import math

import jax
import jax.numpy as jnp
from jax import lax
from jax.experimental import pallas as pl
from jax.experimental.pallas import tpu as pltpu

F32 = jnp.float32
BF16 = jnp.bfloat16

D_MODEL = 4096
BRANCH_W = 1024
GRID_W = 64
ROPE_BASE = 10000.0
NORM_EPS = 1e-6
N_MOD = 6
LANES = 128
HALO = 8
LOG2_E = 1.4426950408889634

MLA_HEADS = 8
MLA_NOPE = 128
MLA_ROPE = 64
MLA_V = 128
MLA_Q_LORA = 768
MLA_KV_LORA = 256
MLA_QK_PAD = 256

RWKV_HEAD = 64
RWKV_HEADS = 16
RWKV_LORA = 64
RWKV_GATE_LORA = 160
RWKV_GN_EPS = 64e-5
RWKV_CHUNK = 64
RWKV_LORA_PAD = 512

DIFF_HEADS = 8
DIFF_QK = 64
DIFF_V = 128
GATE_RANK = 256
MOD_RANK = 256

C_CQ = 0
C_CKV = 768
C_R = 1024
C_K = 2048
C_V = 3072
C_CB = 4096
C_CC = 5120
C_CU = 6144
C_DQ = 7168
C_DK = 8192
C_DV = 9216
C_LORA = 10240
C_GL = 10752
C_KROPE = 11008
P_COLS = 11264

VMEM_CAP = 56 * 1024 * 1024


def _cparams(sem, vmem_bytes):
    limit = int(min(VMEM_CAP, max(vmem_bytes * 1.5 + (4 << 20), 16 << 20)))
    return pltpu.CompilerParams(dimension_semantics=sem, vmem_limit_bytes=limit)


def _mod_index(i, tm, n_x_rows, t_len, n_batch):
    return jnp.where(i < n_x_rows // tm, i // (t_len // tm), n_batch)


def _dot(a, b):
    return jnp.dot(a.astype(BF16), b.astype(BF16), preferred_element_type=F32)


def _dot_nt(a, b):
    return lax.dot_general(a.astype(BF16), b.astype(BF16), (((1,), (1,)), ((), ())),
                           preferred_element_type=F32)


def _split3(x):
    hi = x.astype(BF16)
    r1 = x - hi.astype(F32)
    mid = r1.astype(BF16)
    lo = (r1 - mid.astype(F32)).astype(BF16)
    return hi, mid, lo


def _dot_exact_lhs(m_bf16, x):
    out = None
    for part in _split3(x):
        t = jnp.dot(m_bf16, part, preferred_element_type=F32)
        out = t if out is None else out + t
    return out


def _head_sums(x):
    r = lax.broadcasted_iota(jnp.int32, (LANES, LANES), 0)
    c = lax.broadcasted_iota(jnp.int32, (LANES, LANES), 1)
    ones_bd = jnp.where((r // RWKV_HEAD) == (c // RWKV_HEAD), 1.0, 0.0).astype(BF16)
    parts = _split3(x)
    cols = []
    for j in range(x.shape[1] // LANES):
        acc = None
        for part in parts:
            t = jnp.dot(part[:, j * LANES:(j + 1) * LANES], ones_bd, preferred_element_type=F32)
            acc = t if acc is None else acc + t
        cols.append(acc)
    return jnp.concatenate(cols, axis=1)


def _rope128(x, cos_t, sin_t):
    lane = lax.broadcasted_iota(jnp.int32, x.shape, 1)
    swapped = jnp.where((lane % 64) < 32, pltpu.roll(x, 96, 1), pltpu.roll(x, 32, 1))
    return x * cos_t + swapped * sin_t


def _matmul_kernel(*refs, nk, has_bias, has_res, gate_row, act):
    a_ref, w_ref = refs[0], refs[1]
    pos = 2
    bias_ref = res_ref = mod_ref = None
    if has_bias:
        bias_ref = refs[pos]; pos += 1
    if has_res:
        res_ref = refs[pos]; mod_ref = refs[pos + 1]; pos += 2
    o_ref, acc_ref = refs[pos], refs[pos + 1]
    k = pl.program_id(2)

    @pl.when(k == 0)
    def _():
        acc_ref[...] = jnp.zeros_like(acc_ref)

    acc_ref[...] += _dot(a_ref[...], w_ref[...])

    @pl.when(k == nk - 1)
    def _():
        y = acc_ref[...]
        if has_bias:
            y = y + bias_ref[...]
        if act == "relu2":
            y = jnp.square(jnp.maximum(y, 0.0))
        if has_res:
            y = res_ref[...] + mod_ref[0, gate_row:gate_row + 1, :] * y
        o_ref[...] = y.astype(o_ref.dtype)


def _matmul(a, w, *, tm, tn, tk, out_dtype=F32, bias=None, res=None, mod=None, gate_row=None,
            mod_index=None, act=None, rows=None, name=None):
    m = a.shape[0] if rows is None else rows
    kdim, n = w.shape
    assert a.shape[1] == kdim and m % tm == 0 and n % tn == 0 and kdim % tk == 0
    nk = kdim // tk
    in_specs = [pl.BlockSpec((tm, tk), lambda i, j, k: (i, k)),
                pl.BlockSpec((tk, tn), lambda i, j, k: (k, j))]
    args = [a, w]
    vmem = 2 * tm * tk * a.dtype.itemsize + 2 * tk * tn * w.dtype.itemsize + tm * tn * 4
    vmem += 2 * tm * tn * jnp.dtype(out_dtype).itemsize
    if bias is not None:
        in_specs.append(pl.BlockSpec((1, tn), lambda i, j, k: (0, j)))
        args.append(bias.reshape(1, n).astype(F32))
    if res is not None:
        in_specs.append(pl.BlockSpec((tm, tn), lambda i, j, k: (i, j)))
        in_specs.append(pl.BlockSpec((1, N_MOD, tn), lambda i, j, k: (mod_index(i), 0, j)))
        args += [res, mod]
        vmem += 2 * tm * tn * 4 + 2 * 8 * tn * 4
    kern = functools.partial(_matmul_kernel, nk=nk, has_bias=bias is not None,
                             has_res=res is not None, gate_row=gate_row, act=act)
    return pl.pallas_call(
        kern,
        grid=(m // tm, n // tn, nk),
        in_specs=in_specs,
        out_specs=pl.BlockSpec((tm, tn), lambda i, j, k: (i, j)),
        out_shape=jax.ShapeDtypeStruct((m, n), out_dtype),
        scratch_shapes=[pltpu.VMEM((tm, tn), F32)],
        compiler_params=_cparams(("parallel", "parallel", "arbitrary"), vmem),
        name=name,
    )(*args)


def _nm_matmul_kernel(*refs, has_mod, shift_row, scale_row, act):
    if has_mod:
        x_ref, g_ref, mod_ref, w_ref, o_ref, h_ref = refs
    else:
        x_ref, g_ref, w_ref, o_ref, h_ref = refs

    @pl.when(pl.program_id(1) == 0)
    def _():
        x = x_ref[...].astype(F32)
        y = x * lax.rsqrt(jnp.mean(x * x, axis=-1, keepdims=True) + NORM_EPS) * g_ref[...]
        if has_mod:
            y = y * (1.0 + mod_ref[0, scale_row:scale_row + 1, :]) + mod_ref[0, shift_row:shift_row + 1, :]
        h_ref[...] = y.astype(BF16)

    y = jnp.dot(h_ref[...], w_ref[...].astype(BF16), preferred_element_type=F32)
    if act == "relu2":
        y = jnp.square(jnp.maximum(y, 0.0))
    o_ref[...] = y.astype(o_ref.dtype)


def _nm_matmul(x, g, w, *, tm, tn, out_dtype, x_col_block=0, kdim=None, mod=None, shift_row=None,
               scale_row=None, mod_index=None, act=None, rows=None, name=None):
    m = x.shape[0] if rows is None else rows
    kdim = x.shape[1] if kdim is None else kdim
    n = w.shape[1]
    assert w.shape[0] == kdim and m % tm == 0 and n % tn == 0
    in_specs = [pl.BlockSpec((tm, kdim), lambda i, j: (i, x_col_block)),
                pl.BlockSpec((1, kdim), lambda i, j: (0, 0))]
    args = [x, g.reshape(1, kdim).astype(F32)]
    if mod is not None:
        in_specs.append(pl.BlockSpec((1, N_MOD, kdim), lambda i, j: (mod_index(i), 0, 0)))
        args.append(mod)
    in_specs.append(pl.BlockSpec((kdim, tn), lambda i, j: (0, j)))
    args.append(w)
    vmem = (2 * tm * kdim * x.dtype.itemsize + tm * kdim * 2 + 2 * kdim * tn * w.dtype.itemsize
            + 2 * tm * tn * jnp.dtype(out_dtype).itemsize + tm * tn * 4 + 4 * 8 * kdim * 4)
    kern = functools.partial(_nm_matmul_kernel, has_mod=mod is not None, shift_row=shift_row,
                             scale_row=scale_row, act=act)
    return pl.pallas_call(
        kern,
        grid=(m // tm, n // tn),
        in_specs=in_specs,
        out_specs=pl.BlockSpec((tm, tn), lambda i, j: (i, j)),
        out_shape=jax.ShapeDtypeStruct((m, n), out_dtype),
        scratch_shapes=[pltpu.VMEM((tm, kdim), BF16)],
        compiler_params=_cparams(("parallel", "arbitrary"), vmem),
        name=name,
    )(*args)


def _norm_mod_kernel(x_ref, g_ref, mod_ref, h_ref, *, shift_row, scale_row):
    x = x_ref[...]
    y = x * lax.rsqrt(jnp.mean(x * x, axis=-1, keepdims=True) + NORM_EPS) * g_ref[...]
    y = y * (1.0 + mod_ref[0, scale_row:scale_row + 1, :]) + mod_ref[0, shift_row:shift_row + 1, :]
    h_ref[...] = y.astype(h_ref.dtype)


def _norm_mod(xs, g, mod, *, shift_row, scale_row, mod_index, tm, rows):
    dm = xs.shape[1]
    return pl.pallas_call(
        functools.partial(_norm_mod_kernel, shift_row=shift_row, scale_row=scale_row),
        grid=(rows // tm,),
        in_specs=[pl.BlockSpec((tm, dm), lambda i: (i, 0)), pl.BlockSpec((1, dm), lambda i: (0, 0)),
                  pl.BlockSpec((1, N_MOD, dm), lambda i: (mod_index(i), 0, 0))],
        out_specs=pl.BlockSpec((tm, dm), lambda i: (i, 0)),
        out_shape=jax.ShapeDtypeStruct((rows, dm), BF16),
        compiler_params=_cparams(("parallel",), 2 * tm * dm * 6 + 4 * tm * dm * 4),
        name="norm_mod",
    )(xs, g.reshape(1, dm), mod)


def _fullk_kernel(*refs, has_res, has_cast, gate_row, act):
    refs = list(refs)
    if has_cast:
        cast_dst = refs.pop()
        cast_src = refs.pop(-2)
        cast_dst[...] = cast_src[...].astype(cast_dst.dtype)
    if has_res:
        a_ref, w_ref, res_ref, mod_ref, o_ref = refs
    else:
        a_ref, w_ref, o_ref = refs
    y = jnp.dot(a_ref[...], w_ref[...], preferred_element_type=F32)
    if act == "relu2":
        y = jnp.square(jnp.maximum(y, 0.0))
    if has_res:
        y = res_ref[...] + mod_ref[0, gate_row:gate_row + 1, :] * y
    o_ref[...] = y.astype(o_ref.dtype)


def _cast_rows_per_step(n_rows, n_steps):
    rows = 16
    while n_rows % rows or n_rows // rows > n_steps:
        rows *= 2
    return rows


def _fullk_matmul(a, w, *, tm, tn, out_dtype, act=None, res=None, mod=None, gate_row=None, mod_index=None,
                  rows=None, name=None, layer=None, cast_src=None, cast_layer=None):
    m = a.shape[0] if rows is None else rows
    _, kdim, n = w.shape
    assert a.shape[1] == kdim and m % tm == 0 and n % tn == 0 and a.dtype == BF16 and w.dtype == BF16
    in_specs = [pl.BlockSpec((tm, kdim), lambda i, j: (i, 0)),
                pl.BlockSpec((None, kdim, tn), lambda i, j: (layer, 0, j))]
    args = [a, w]
    vmem = 2 * tm * kdim * 2 + 2 * kdim * tn * 2 + 2 * tm * tn * jnp.dtype(out_dtype).itemsize + 2 * tm * tn * 4
    if res is not None:
        in_specs += [pl.BlockSpec((tm, tn), lambda i, j: (i, j)),
                     pl.BlockSpec((1, N_MOD, tn), lambda i, j: (mod_index(i), 0, j))]
        args += [res, mod]
        vmem += 2 * tm * tn * 4
    out_specs = pl.BlockSpec((tm, tn), lambda i, j: (i, j))
    out_shape = jax.ShapeDtypeStruct((m, n), out_dtype)
    if cast_src is not None:
        nj = n // tn
        _, c_rows, c_cols = cast_src.shape
        cr = _cast_rows_per_step(c_rows, (m // tm) * nj)
        cast_blk = lambda i, j: jnp.minimum(i * nj + j, c_rows // cr - 1)
        cast_layer = layer if cast_layer is None else cast_layer
        in_specs.append(pl.BlockSpec((None, cr, c_cols), lambda i, j: (cast_layer, cast_blk(i, j), 0)))
        args.append(cast_src)
        out_specs = [out_specs, pl.BlockSpec((cr, c_cols), lambda i, j: (cast_blk(i, j), 0))]
        out_shape = [out_shape, jax.ShapeDtypeStruct((c_rows, c_cols), BF16)]
        vmem += 2 * cr * c_cols * 6
    return pl.pallas_call(
        functools.partial(_fullk_kernel, has_res=res is not None, has_cast=cast_src is not None,
                          gate_row=gate_row, act=act),
        grid=(m // tm, n // tn),
        in_specs=in_specs,
        out_specs=out_specs,
        out_shape=out_shape,
        compiler_params=_cparams(("arbitrary", "arbitrary"), vmem),
        name=name,
    )(*args)


def _merge_kernel(gl_ref, y0_ref, y1_ref, y2_ref, y3_ref, wb_ref, gu_ref, gb_ref, o_ref):
    gl = gl_ref[...].astype(BF16)
    acc = None
    for i, y_ref in enumerate((y0_ref, y1_ref, y2_ref, y3_ref)):
        gate = jax.nn.sigmoid(jnp.dot(gl, gu_ref[i], preferred_element_type=F32) + gb_ref[i:i + 1, :])
        term = gate * jnp.dot(y_ref[...], wb_ref[i], preferred_element_type=F32)
        acc = term if acc is None else acc + term
    o_ref[...] = acc.astype(o_ref.dtype)


def _merge(p, ys, wb, gu, gb, *, layer, tm, tn, rows):
    n = wb.shape[3]
    bw = wb.shape[2]
    gr = gu.shape[2]
    y_spec = pl.BlockSpec((tm, bw), lambda i, j: (i, 0))
    vmem = (2 * tm * gr * 4 + 4 * 2 * tm * bw * 2 + 2 * 4 * bw * tn * 2 + 2 * 4 * gr * tn * 2
            + 2 * tm * tn * 2 + 3 * tm * tn * 4)
    return pl.pallas_call(
        _merge_kernel,
        grid=(rows // tm, n // tn),
        in_specs=[pl.BlockSpec((tm, gr), lambda i, j: (i, C_GL // GATE_RANK)),
                  y_spec, y_spec, y_spec, y_spec,
                  pl.BlockSpec((None, 4, bw, tn), lambda i, j: (layer, 0, 0, j)),
                  pl.BlockSpec((None, 4, gr, tn), lambda i, j: (layer, 0, 0, j)),
                  pl.BlockSpec((None, 4, tn), lambda i, j: (layer, 0, j))],
        out_specs=pl.BlockSpec((tm, tn), lambda i, j: (i, j)),
        out_shape=jax.ShapeDtypeStruct((rows, n), BF16),
        compiler_params=_cparams(("parallel", "arbitrary"), vmem),
        name="merge",
    )(p, *ys, wb, gu, gb)


def _softmax_parts(s_list, scale):
    m = None
    for s in s_list:
        sm = jnp.max(s, axis=-1, keepdims=True)
        m = sm if m is None else jnp.maximum(m, sm)
    e_list = [jnp.exp2((s - m) * (scale * LOG2_E)) for s in s_list]
    l = None
    for e in e_list:
        es = jnp.sum(e, axis=-1, keepdims=True)
        l = es if l is None else l + es
    return e_list, l


def _attend(q, ks, vs, scale):
    e_list, l = _softmax_parts([_dot_nt(q, k) for k in ks], scale)
    o = None
    for e, v in zip(e_list, vs):
        t = jnp.dot(e.astype(BF16), v, preferred_element_type=F32)
        o = t if o is None else o + t
    return o / l


def _mla_kernel(cos_ref, sin_ref, q_ref, knx_ref, knc_ref, krx_ref, krc_ref, vx_ref, vc_ref, o_ref,
                kx_s, kc_s, *, nq, tq, scale):
    i = pl.program_id(2)

    @pl.when(i == 0)
    def _():
        kx_s[:, :MLA_NOPE] = knx_ref[...]
        kx_s[:, MLA_NOPE:] = _rope128(krx_ref[...], cos_ref[...], sin_ref[...]).astype(BF16)
        kc_s[:, :MLA_NOPE] = knc_ref[...]
        kc_s[:, MLA_NOPE:] = krc_ref[...].astype(BF16)

    @pl.when(i < nq)
    def _():
        row0 = pl.multiple_of(jnp.minimum(i, nq - 1) * tq, tq)
        q = q_ref[...]
        q_rope = _rope128(q[:, MLA_NOPE:].astype(F32), cos_ref[pl.ds(row0, tq), :], sin_ref[pl.ds(row0, tq), :])
        q = jnp.concatenate([q[:, :MLA_NOPE], q_rope.astype(BF16)], axis=1)
        o = _attend(q, (kc_s[...], kx_s[...]), (vc_ref[...], vx_ref[...]), scale)
        o_ref[...] = o.astype(o_ref.dtype)

    @pl.when(i >= nq)
    def _():
        o = _attend(q_ref[...], (kc_s[...],), (vc_ref[...],), scale)
        o_ref[...] = o.astype(o_ref.dtype)


def _diff_finish(o1, o2, lam_ref, g_ref, out_scale):
    o = o1 - lam_ref[...] * o2
    y = o * lax.rsqrt(jnp.mean(o * o, axis=-1, keepdims=True) + NORM_EPS) * g_ref[...]
    return y * out_scale


def _diff_kernel(cos_ref, sin_ref, lam_ref, g_ref, q_ref, kx_ref, kc_ref, vx_ref, vc_ref, o_ref,
                 kx_s, kc_s, vx_s, vc_s, *, nq, tq, scale, out_scale):
    i = pl.program_id(2)

    @pl.when(i == 0)
    def _():
        kx_s[...] = _rope128(kx_ref[...], cos_ref[...], sin_ref[...]).astype(BF16)
        kc_s[...] = kc_ref[...].astype(BF16)
        vx_s[...] = vx_ref[...].astype(BF16)
        vc_s[...] = vc_ref[...].astype(BF16)

    def halves(q):
        lane = lax.broadcasted_iota(jnp.int32, q.shape, 1)
        return jnp.where(lane < DIFF_QK, q, 0.0).astype(BF16), jnp.where(lane < DIFF_QK, 0.0, q).astype(BF16)

    @pl.when(i < nq)
    def _():
        row0 = pl.multiple_of(jnp.minimum(i, nq - 1) * tq, tq)
        q1, q2 = halves(_rope128(q_ref[...], cos_ref[pl.ds(row0, tq), :], sin_ref[pl.ds(row0, tq), :]))
        ks, vs = (kc_s[...], kx_s[...]), (vc_s[...], vx_s[...])
        y = _diff_finish(_attend(q1, ks, vs, scale), _attend(q2, ks, vs, scale), lam_ref, g_ref, out_scale)
        o_ref[...] = y.astype(o_ref.dtype)

    @pl.when(i >= nq)
    def _():
        q1, q2 = halves(q_ref[...])
        ks, vs = (kc_s[...],), (vc_s[...],)
        y = _diff_finish(_attend(q1, ks, vs, scale), _attend(q2, ks, vs, scale), lam_ref, g_ref, out_scale)
        o_ref[...] = y.astype(o_ref.dtype)


def _query_row_block(b, i, nq, nqc, n_x, tq):
    return jnp.where(i < nq, b * nq + i, n_x // tq + b * nqc + (i - nq))


def _mla_attention(q, kv, p, cos_t, sin_t, *, n_batch, t_len, c_len, tq, with_ctx):
    n_x = n_batch * t_len
    nq = t_len // tq
    nqc = c_len // tq if with_ctx else 0
    out_rows = n_x + (n_batch * c_len if with_ctx else 0)
    cblk0 = n_x // c_len
    qmap = lambda b, h, i: (_query_row_block(b, i, nq, nqc, n_x, tq), h)
    full = lambda b, h, i: (0, 0)
    vmem = (4 * t_len * LANES * 4 + 2 * tq * 256 * 2 + 2 * (t_len + c_len) * LANES * (2 + 4 + 2)
            + (t_len + c_len) * 256 * 2 + 2 * tq * LANES * 2 + 6 * tq * (t_len + c_len) * 4)
    return pl.pallas_call(
        functools.partial(_mla_kernel, nq=nq, tq=tq, scale=(MLA_NOPE + MLA_ROPE) ** -0.5),
        grid=(n_batch, MLA_HEADS, nq + nqc),
        in_specs=[pl.BlockSpec((t_len, LANES), full), pl.BlockSpec((t_len, LANES), full),
                  pl.BlockSpec((tq, MLA_QK_PAD), qmap),
                  pl.BlockSpec((t_len, MLA_NOPE), lambda b, h, i: (b, h)),
                  pl.BlockSpec((c_len, MLA_NOPE), lambda b, h, i: (cblk0 + b, h)),
                  pl.BlockSpec((t_len, LANES), lambda b, h, i: (b, C_KROPE // LANES)),
                  pl.BlockSpec((c_len, LANES), lambda b, h, i: (cblk0 + b, C_KROPE // LANES)),
                  pl.BlockSpec((t_len, MLA_V), lambda b, h, i: (b, MLA_HEADS + h)),
                  pl.BlockSpec((c_len, MLA_V), lambda b, h, i: (cblk0 + b, MLA_HEADS + h))],
        out_specs=pl.BlockSpec((tq, MLA_V), qmap),
        out_shape=jax.ShapeDtypeStruct((out_rows, MLA_HEADS * MLA_V), BF16),
        scratch_shapes=[pltpu.VMEM((t_len, MLA_QK_PAD), BF16), pltpu.VMEM((c_len, MLA_QK_PAD), BF16)],
        compiler_params=_cparams(("parallel", "parallel", "arbitrary"), vmem),
        name="mla_attention",
    )(cos_t, sin_t, q, kv, kv, p, p, kv, kv)


def _diff_attention(p, cos_t, sin_t, lam_row, g_row, *, n_batch, t_len, c_len, tq, with_ctx, out_scale):
    n_x = n_batch * t_len
    nq = t_len // tq
    nqc = c_len // tq if with_ctx else 0
    out_rows = n_x + (n_batch * c_len if with_ctx else 0)
    cblk0 = n_x // c_len
    full = lambda b, h, i: (0, 0)
    qblk, kblk, vblk = C_DQ // LANES, C_DK // LANES, C_DV // LANES
    vmem = (4 * t_len * LANES * 4 + 2 * tq * LANES * 4 + 4 * (t_len + c_len) * LANES * 4
            + 2 * (t_len + c_len) * LANES * 2 + 2 * tq * LANES * 2 + 8 * tq * (t_len + c_len) * 4)
    return pl.pallas_call(
        functools.partial(_diff_kernel, nq=nq, tq=tq, scale=DIFF_QK ** -0.5, out_scale=out_scale),
        grid=(n_batch, DIFF_HEADS, nq + nqc),
        in_specs=[pl.BlockSpec((t_len, LANES), full), pl.BlockSpec((t_len, LANES), full),
                  pl.BlockSpec((1, DIFF_V), full), pl.BlockSpec((1, DIFF_V), full),
                  pl.BlockSpec((tq, LANES), lambda b, h, i: (_query_row_block(b, i, nq, nqc, n_x, tq), qblk + h)),
                  pl.BlockSpec((t_len, LANES), lambda b, h, i: (b, kblk + h)),
                  pl.BlockSpec((c_len, LANES), lambda b, h, i: (cblk0 + b, kblk + h)),
                  pl.BlockSpec((t_len, LANES), lambda b, h, i: (b, vblk + h)),
                  pl.BlockSpec((c_len, LANES), lambda b, h, i: (cblk0 + b, vblk + h))],
        out_specs=pl.BlockSpec((tq, DIFF_V), lambda b, h, i: (_query_row_block(b, i, nq, nqc, n_x, tq), h)),
        out_shape=jax.ShapeDtypeStruct((out_rows, DIFF_HEADS * DIFF_V), BF16),
        scratch_shapes=[pltpu.VMEM((t_len, LANES), BF16), pltpu.VMEM((c_len, LANES), BF16),
                        pltpu.VMEM((t_len, LANES), BF16), pltpu.VMEM((c_len, LANES), BF16)],
        compiler_params=_cparams(("parallel", "parallel", "arbitrary"), vmem),
        name="diff_attention",
    )(cos_t, sin_t, lam_row, g_row, p, p, p, p, p)


def _prep_kernel(r_ref, k_ref, v_ref, cb_ref, cc_ref, cu_ref, lo_ref, hp_ref, hn_ref,
                 mu_ref, mul_ref, kk_w_ref, ka_w_ref, rk_w_ref, w0_ref, a0_ref, w2_ref, a2_ref, g2_ref,
                 cw_ref,
                 r_o, kk_o, v_o, lw_o, kd_o, ka_o, bonus_o, g_o, conv_o, *, tr, n_x, t_len, c_len):
    i = pl.program_id(0)
    g0 = i * tr
    seq = jnp.where(g0 < n_x, t_len, c_len)
    has_prev = ((g0 % seq) != 0).astype(F32)
    has_next = (((g0 + tr) % seq) != 0).astype(F32)
    row = lax.broadcasted_iota(jnp.int32, (tr, 1), 0)

    def neighbours(x, col0):
        width = x.shape[1]
        before = hp_ref[HALO - 1:HALO, col0:col0 + width] * has_prev
        after = hn_ref[0:1, col0:col0 + width] * has_next
        prev = jnp.where(row == 0, before, pltpu.roll(x, 1, 0))
        nxt = jnp.where(row == tr - 1, after, pltpu.roll(x, tr - 1, 0))
        return prev, nxt

    def shifted(x, col0, mu0, mu1):
        prev, nxt = neighbours(x, col0)
        return x + mu0 * (prev - x) + mu1 * (nxt - x)

    r = shifted(r_ref[...], C_R, mu_ref[0:1, :BRANCH_W], mu_ref[1:2, :BRANCH_W])
    k = shifted(k_ref[...], C_K, mu_ref[0:1, BRANCH_W:2 * BRANCH_W], mu_ref[1:2, BRANCH_W:2 * BRANCH_W])
    v = shifted(v_ref[...], C_V, mu_ref[0:1, 2 * BRANCH_W:], mu_ref[1:2, 2 * BRANCH_W:])
    lo = shifted(lo_ref[...], C_LORA, mul_ref[0:1, :], mul_ref[1:2, :])
    wd = jnp.tanh(lo[:, :2 * RWKV_LORA])
    ad = lo[:, 2 * RWKV_LORA:4 * RWKV_LORA]
    gd = jax.nn.sigmoid(lo[:, 4 * RWKV_LORA:])
    w_pre = _dot(wd, w2_ref[...]) + w0_ref[...]
    a_sig = jax.nn.sigmoid(_dot(ad, a2_ref[...]) + a0_ref[...])
    g_o[...] = _dot(gd, g2_ref[...])
    w_log = -(jnp.maximum(-w_pre, 0.0) + jnp.log(1.0 + jnp.exp(-jnp.abs(w_pre)))) - 0.5
    lw = -jnp.exp(w_log)
    kkf = k * kk_w_ref[...]
    kk = kkf * lax.rsqrt(_head_sums(kkf * kkf) + 1e-12)
    r_o[...] = r
    kk_o[...] = kk
    v_o[...] = v
    k_sum = None
    for d in range(2):
        a_d = a_sig[:, d * BRANCH_W:(d + 1) * BRANCH_W]
        k_d = k * (1.0 + (a_d - 1.0) * ka_w_ref[...])
        lw_o[d] = lw[:, d * BRANCH_W:(d + 1) * BRANCH_W]
        kd_o[d] = k_d
        ka_o[d] = kk * a_d
        k_sum = k_d if k_sum is None else k_sum + k_d
    bonus_o[...] = _head_sums(r * k_sum * rk_w_ref[...]) * v

    z = cc_ref[...] * cu_ref[...]
    z_before = hp_ref[HALO - 1:HALO, C_CC:C_CC + BRANCH_W] * hp_ref[HALO - 1:HALO, C_CU:C_CU + BRANCH_W] * has_prev
    z_after = hn_ref[0:1, C_CC:C_CC + BRANCH_W] * hn_ref[0:1, C_CU:C_CU + BRANCH_W] * has_next
    z_prev = jnp.where(row == 0, z_before, pltpu.roll(z, 1, 0))
    z_next = jnp.where(row == tr - 1, z_after, pltpu.roll(z, tr - 1, 0))
    y = cb_ref[...] * (cw_ref[0:1, :] * z_prev + cw_ref[1:2, :] * z + cw_ref[2:3, :] * z_next)
    conv_o[...] = y.astype(conv_o.dtype)


def _mixer_prep(p, mu_rkv, mu_lora, kk_w, ka_w, rk_w, w0, a0, w2bd, a2bd, g2p, conv_w, *, tr, n_x, t_len,
                c_len):
    n_rows = p.shape[0]
    bw = BRANCH_W
    last_halo = n_rows // HALO - 1
    col = lambda c: (lambda i: (i, c))
    const = lambda i: (0, 0)
    main = [pl.BlockSpec((tr, bw), col(C_R // bw)), pl.BlockSpec((tr, bw), col(C_K // bw)),
            pl.BlockSpec((tr, bw), col(C_V // bw)), pl.BlockSpec((tr, bw), col(C_CB // bw)),
            pl.BlockSpec((tr, bw), col(C_CC // bw)), pl.BlockSpec((tr, bw), col(C_CU // bw)),
            pl.BlockSpec((tr, RWKV_LORA_PAD), col(C_LORA // RWKV_LORA_PAD)),
            pl.BlockSpec((HALO, P_COLS), lambda i: (jnp.maximum(i * (tr // HALO) - 1, 0), 0)),
            pl.BlockSpec((HALO, P_COLS), lambda i: (jnp.minimum((i + 1) * (tr // HALO), last_halo), 0))]
    params = [mu_rkv, mu_lora, kk_w, ka_w, rk_w, w0, a0, w2bd, a2bd, g2p, conv_w]
    param_specs = [pl.BlockSpec(a.shape, const) for a in params]
    row_spec = pl.BlockSpec((tr, bw), lambda i: (i, 0))
    dir_spec = pl.BlockSpec((2, tr, bw), lambda i: (0, i, 0))
    f32_rows = jax.ShapeDtypeStruct((n_rows, bw), F32)
    f32_dirs = jax.ShapeDtypeStruct((2, n_rows, bw), F32)
    vmem = 2 * (7 * tr * bw * 4 + 2 * HALO * P_COLS * 4 + 12 * tr * bw * 4) + 30 * tr * bw * 4
    return pl.pallas_call(
        functools.partial(_prep_kernel, tr=tr, n_x=n_x, t_len=t_len, c_len=c_len),
        grid=(n_rows // tr,),
        in_specs=main + param_specs,
        out_specs=[row_spec, row_spec, row_spec, dir_spec, dir_spec, dir_spec, row_spec, row_spec, row_spec],
        out_shape=[f32_rows, f32_rows, f32_rows, f32_dirs, f32_dirs, f32_dirs, f32_rows, f32_rows,
                   jax.ShapeDtypeStruct((n_rows, bw), BF16)],
        compiler_params=_cparams(("parallel",), vmem),
        name="mixer_prep",
    )(p, p, p, p, p, p, p, p, p, *params)


def _rwkv_kernel(r0_ref, kk0_ref, v0_ref, r1_ref, kk1_ref, v1_ref, lw0_ref, kd0_ref, ka0_ref,
                 lw1_ref, kd1_ref, ka1_ref, o0_ref, o1_ref, s_ref):
    C = RWKV_CHUNK
    W = 2 * RWKV_HEAD
    n_pairs = RWKV_HEADS // 2

    @pl.when(pl.program_id(1) == 0)
    def _():
        s_ref[...] = jnp.zeros_like(s_ref)

    mm, mm_nt = _dot, _dot_nt
    lane = lax.broadcasted_iota(jnp.int32, (C, W), 1)
    row = lax.broadcasted_iota(jnp.int32, (C, W), 0)
    first = lane < RWKV_HEAD
    rc = lax.broadcasted_iota(jnp.int32, (C, C), 0)
    cc = lax.broadcasted_iota(jnp.int32, (C, C), 1)

    def bdiag(x):
        return jnp.concatenate([jnp.where(first, x, jnp.zeros_like(x)), jnp.where(first, jnp.zeros_like(x), x)],
                               axis=0)

    def direction(sign, r_ref, kk_ref, v_ref, lw_ref, kd_ref, ka_ref):
        m_incl = jnp.where((rc - cc) * sign >= 0, 1.0, 0.0).astype(BF16)
        lw = lw_ref[0]
        cum = _dot_exact_lhs(m_incl, lw)
        tot = jnp.sum(lw, axis=0, keepdims=True)
        e_ninc = jnp.exp(-cum)
        e_rem = jnp.exp(tot - cum)
        kk, ka, kd = kk_ref[...], ka_ref[0], kd_ref[0]
        order = (row - (lane % RWKV_HEAD)) * sign
        return dict(a_t=kk * jnp.exp(cum - lw), b_t=-ka * e_ninc, k_t=kd * e_ninc, r_t=r_ref[...] * jnp.exp(cum),
                    b_h=-ka * e_rem, k_h=kd * e_rem, v=v_ref[...], e_tot=jnp.exp(tot), strict=order > 0,
                    incl=order >= 0, eye=jnp.where(order == 0, 1.0, 0.0).astype(F32))

    dirs = (direction(1, r0_ref, kk0_ref, v0_ref, lw0_ref, kd0_ref, ka0_ref),
            direction(-1, r1_ref, kk1_ref, v1_ref, lw1_ref, kd1_ref, ka1_ref))
    o_refs = (o0_ref, o1_ref)

    chains = [(d, p) for d in range(2) for p in range(n_pairs)]
    n = range(len(chains))
    sl = [slice(W * p, W * (p + 1)) for (_, p) in chains]
    dd = [dirs[d] for (d, _) in chains]
    ar = [jnp.concatenate([dd[c]["a_t"][:, sl[c]], dd[c]["r_t"][:, sl[c]]], axis=0).astype(BF16) for c in n]
    a_bd = [bdiag(dd[c]["a_t"][:, sl[c]].astype(BF16)) for c in n]
    bk_bd = [jnp.concatenate([bdiag(dd[c]["b_t"][:, sl[c]].astype(BF16)), bdiag(dd[c]["k_t"][:, sl[c]].astype(BF16))],
                             axis=0) for c in n]
    v_bd = [bdiag(dd[c]["v"][:, sl[c]].astype(BF16)) for c in n]
    g = [mm_nt(ar[c], bk_bd[c]) for c in n]
    l_pow = [jnp.where(dd[c]["strict"], g[c][:C, :W], 0.0) for c in n]
    m_ak = [jnp.where(dd[c]["strict"], g[c][:C, W:], 0.0).astype(BF16) for c in n]
    a_r = [jnp.where(jnp.concatenate([dd[c]["incl"], dd[c]["incl"]], axis=1), g[c][C:], 0.0).astype(BF16) for c in n]
    mv = [mm(m_ak[c], v_bd[c]) for c in n]
    l_bd = [bdiag(l_pow[c].astype(BF16)) for c in n]
    t_inv = [dd[c]["eye"] + l_pow[c] for c in n]
    l_pow = [mm(l_pow[c], l_bd[c]) for c in n]
    for _ in range(int(math.log2(C)) - 2):
        l_bd = [bdiag(l_pow[c].astype(BF16)) for c in n]
        lt = [mm(jnp.concatenate([l_pow[c], t_inv[c]], axis=0), l_bd[c]) for c in n]
        l_pow = [lt[c][:C] for c in n]
        t_inv = [t_inv[c] + lt[c][C:] for c in n]
    t_inv = [t_inv[c] + mm(t_inv[c], bdiag(l_pow[c].astype(BF16))) for c in n]
    wu = [mm(t_inv[c], jnp.concatenate([a_bd[c], bdiag(mv[c].astype(BF16))], axis=1)) for c in n]
    s_bd = [s_ref[d, p] for (d, p) in chains]
    wr = [mm(jnp.concatenate([wu[c][:, :W], dd[c]["r_t"][:, sl[c]]], axis=0), s_bd[c]) for c in n]
    zv = [jnp.concatenate([bdiag((wr[c][:C] + wu[c][:, W:]).astype(BF16)), v_bd[c]], axis=0) for c in n]
    for c, (d, p) in enumerate(chains):
        o_refs[d][:, sl[c]] = wr[c][C:] + mm(a_r[c], zv[c])
    for c, (d, p) in enumerate(chains):
        bk_h = jnp.concatenate([bdiag(dd[c]["b_h"][:, sl[c]]), bdiag(dd[c]["k_h"][:, sl[c]])], axis=0)
        tot_col = jnp.broadcast_to(dd[c]["e_tot"][:, sl[c]], (W, W)).T
        s_ref[d, p] = tot_col * s_bd[c] + mm(bk_h.T, zv[c])


def _rwkv_scan(r, kk, v, lw, kd, ka, *, n_batch, t_len, c_len):
    C = RWKV_CHUNK
    rows, width = r.shape
    nct, nxt = c_len // C, t_len // C
    ctx_blk0 = n_batch * t_len // C

    def blk(b, d, s):
        j_c = s if d == 0 else nct - 1 - s
        j_x = s - nct if d == 0 else nxt - 1 - (s - nct)
        return jnp.where(s < nct, ctx_blk0 + b * nct + j_c, b * nxt + j_x)

    def shared(d):
        return pl.BlockSpec((C, width), lambda b, s: (blk(b, d, s), 0))

    def per_dir(d):
        return pl.BlockSpec((1, C, width), lambda b, s: (d, blk(b, d, s), 0))

    vmem = 2 * 14 * C * width * 4 + RWKV_HEADS * 128 * 128 * 4 + 128 * C * width * 4
    out = jax.ShapeDtypeStruct((rows, width), F32)
    return pl.pallas_call(
        _rwkv_kernel,
        grid=(n_batch, nct + nxt),
        in_specs=[shared(0), shared(0), shared(0), shared(1), shared(1), shared(1),
                  per_dir(0), per_dir(0), per_dir(0), per_dir(1), per_dir(1), per_dir(1)],
        out_specs=[shared(0), shared(1)],
        out_shape=[out, out],
        scratch_shapes=[pltpu.VMEM((2, RWKV_HEADS // 2, 128, 128), F32)],
        compiler_params=_cparams(("parallel", "arbitrary"), vmem),
        name="rwkv_scan",
    )(r, kk, v, r, kk, v, lw, kd, ka, lw, kd, ka)


def _rwkv_readout_kernel(o0_ref, o1_ref, bonus_ref, g_ref, lng_ref, lnb_ref, y_ref):
    o = o0_ref[...] + o1_ref[...]
    mean = _head_sums(o) * (1.0 / RWKV_HEAD)
    cen = o - mean
    var = _head_sums(cen * cen) * (1.0 / RWKV_HEAD)
    o_n = cen * lax.rsqrt(var + RWKV_GN_EPS) * lng_ref[...] + lnb_ref[...]
    y_ref[...] = ((o_n + bonus_ref[...]) * g_ref[...]).astype(y_ref.dtype)


def _rwkv_readout(o_dirs, bonus, g_out, ln_g, ln_b, *, tr):
    n_rows, bw = bonus.shape
    row_spec = pl.BlockSpec((tr, bw), lambda i: (i, 0))
    const = pl.BlockSpec((1, bw), lambda i: (0, 0))
    return pl.pallas_call(
        _rwkv_readout_kernel,
        grid=(n_rows // tr,),
        in_specs=[row_spec, row_spec, row_spec, row_spec, const, const],
        out_specs=row_spec,
        out_shape=jax.ShapeDtypeStruct((n_rows, bw), BF16),
        compiler_params=_cparams(("parallel",), 2 * 5 * tr * bw * 4 + 16 * tr * bw * 4),
        name="rwkv_readout",
    )(o_dirs[0], o_dirs[1], bonus, g_out, ln_g.reshape(1, bw), ln_b.reshape(1, bw))


def _rmsnorm_kernel(x_ref, g_ref, o_ref):
    x = x_ref[...]
    o_ref[...] = x * lax.rsqrt(jnp.mean(x * x, axis=-1, keepdims=True) + NORM_EPS) * g_ref[...]


def _final_norm(xs, g, *, rows, tm):
    dm = xs.shape[1]
    return pl.pallas_call(
        _rmsnorm_kernel,
        grid=(rows // tm,),
        in_specs=[pl.BlockSpec((tm, dm), lambda i: (i, 0)), pl.BlockSpec((1, dm), lambda i: (0, 0))],
        out_specs=pl.BlockSpec((tm, dm), lambda i: (i, 0)),
        out_shape=jax.ShapeDtypeStruct((rows, dm), F32),
        compiler_params=_cparams(("parallel",), 4 * tm * dm * 4),
        name="final_norm",
    )(xs, g.reshape(1, dm))


W_IN_RWKV = 1088
W_IN_LORA = W_IN_RWKV + 3 * BRANCH_W
W_IN_CONV = W_IN_LORA + 4 * RWKV_LORA + RWKV_GATE_LORA
W_IN_COLS = W_IN_CONV + 6 * BRANCH_W


def _w_in_layout_kernel(w_ref, gd_ref, o_ref):
    def put(dst, src0, width):
        o_ref[:, dst:dst + width] = w_ref[:, src0:src0 + width].astype(o_ref.dtype)

    put(C_CQ, 0, MLA_Q_LORA + MLA_KV_LORA)
    put(C_R, W_IN_RWKV, 3 * BRANCH_W)
    put(C_CB, W_IN_CONV, 6 * BRANCH_W)
    lora_w = W_IN_CONV - W_IN_LORA
    put(C_LORA, W_IN_LORA, lora_w)
    o_ref[:, C_LORA + lora_w:C_GL] = jnp.zeros((o_ref.shape[0], C_GL - C_LORA - lora_w), o_ref.dtype)
    o_ref[:, C_GL:C_GL + GATE_RANK] = gd_ref[...].astype(o_ref.dtype)
    put(C_KROPE, MLA_Q_LORA + MLA_KV_LORA, MLA_ROPE)
    o_ref[:, C_KROPE + MLA_ROPE:] = jnp.zeros((o_ref.shape[0], P_COLS - C_KROPE - MLA_ROPE), o_ref.dtype)


def _w_in_layout(w_in, gate_down, *, rb):
    depth, dm, cols = w_in.shape
    assert cols == W_IN_COLS and dm % rb == 0
    return pl.pallas_call(
        _w_in_layout_kernel,
        grid=(depth, dm // rb),
        in_specs=[pl.BlockSpec((None, rb, cols), lambda l, i: (l, i, 0)),
                  pl.BlockSpec((None, rb, GATE_RANK), lambda l, i: (l, i, 0))],
        out_specs=pl.BlockSpec((None, rb, P_COLS), lambda l, i: (l, i, 0)),
        out_shape=jax.ShapeDtypeStruct((depth, dm, P_COLS), BF16),
        compiler_params=_cparams(("parallel", "parallel"), 2 * rb * cols * 4 + 2 * rb * P_COLS * 2 + 4 * rb * cols * 4),
        name="w_in_layout",
    )(w_in, gate_down)


def _rope_tables128(n_tokens):
    rows = n_tokens // GRID_W
    row = jnp.repeat(jnp.arange(rows, dtype=F32), GRID_W)
    col = jnp.tile(jnp.arange(GRID_W, dtype=F32), rows)
    n_freq = 64 // 4
    inv = ROPE_BASE ** (-jnp.arange(n_freq, dtype=F32) / n_freq)
    ang = jnp.concatenate([row[:, None] * inv, col[:, None] * inv], axis=-1)
    cos, sin = jnp.cos(ang), jnp.sin(ang)
    return jnp.concatenate([cos, cos, cos, cos], axis=-1), jnp.concatenate([-sin, sin, -sin, sin], axis=-1)


def _block_diag2(w2):
    z = jnp.zeros_like(w2[0])
    return jnp.concatenate([jnp.concatenate([w2[0], z], axis=1), jnp.concatenate([z, w2[1]], axis=1)], axis=0)


def kernel(x, c, ctx, c_ctx, norm1_g, norm2_g, mod_down, mod_up, mod_b, w_in, mla_q_norm_g, mla_w_uq,
           mla_kv_norm_g, mla_w_ukv, rwkv_mu, rwkv_w0, rwkv_w2, rwkv_a0, rwkv_a2, rwkv_g2, rwkv_k_k,
           rwkv_k_a, rwkv_r_k, rwkv_ln_g, rwkv_ln_b, conv_w, diff_lambda, diff_norm_g, w_branch, gate_down,
           gate_up, gate_b, w_out, mlp_w1, mlp_w2, final_norm_g):
    n_batch, t_len, dm = x.shape
    c_len = ctx.shape[1]
    depth = w_in.shape[0]
    bw = BRANCH_W
    n_x = n_batch * t_len
    n_c = n_batch * c_len
    n_rows = n_x + n_c
    tm = 512 if (t_len % 512 == 0 and n_c % 512 == 0) else 256
    tq = min(256, c_len)
    tr = min(256, c_len)
    assert dm == D_MODEL and t_len % tm == 0 and n_c % tm == 0 and n_x % c_len == 0
    assert t_len % tq == 0 and c_len % tq == 0 and c_len % RWKV_CHUNK == 0 and t_len % c_len == 0
    tm_big = 1024 if (t_len % 1024 == 0 and n_c % 1024 == 0) else tm
    mod_index = functools.partial(_mod_index, tm=tm, n_x_rows=n_x, t_len=t_len, n_batch=n_batch)
    mod_index_big = functools.partial(_mod_index, tm=tm_big, n_x_rows=n_x, t_len=t_len, n_batch=n_batch)
    mod_index_nm = functools.partial(_mod_index, tm=256, n_x_rows=n_x, t_len=t_len, n_batch=n_batch)

    zeros = lambda *s: jnp.zeros(s, F32)
    lora_w = 4 * RWKV_LORA + RWKV_GATE_LORA
    w_in_p = _w_in_layout(w_in, gate_down, rb=128)
    w_uq_p = jnp.pad(mla_w_uq.reshape(depth, MLA_Q_LORA, MLA_HEADS, MLA_NOPE + MLA_ROPE),
                     ((0, 0), (0, 0), (0, 0), (0, MLA_QK_PAD - MLA_NOPE - MLA_ROPE))
                     ).reshape(depth, MLA_Q_LORA, MLA_HEADS * MLA_QK_PAD).astype(BF16)
    w_ukv_r = mla_w_ukv.reshape(depth, MLA_KV_LORA, MLA_HEADS, MLA_NOPE + MLA_V)
    w_ukv_p = jnp.concatenate([w_ukv_r[..., :MLA_NOPE].reshape(depth, MLA_KV_LORA, -1),
                               w_ukv_r[..., MLA_NOPE:].reshape(depth, MLA_KV_LORA, -1)], axis=-1).astype(BF16)
    w_branch_b = w_branch.astype(BF16)
    gate_up_b = jnp.moveaxis(gate_up, 2, 1).astype(BF16)
    w_out_b = w_out.astype(BF16)
    g2_p = jnp.pad(rwkv_g2, ((0, 0), (0, RWKV_LORA_PAD - 4 * RWKV_LORA - RWKV_GATE_LORA), (0, 0))).astype(BF16)
    mu_lora = jnp.pad(rwkv_mu[:, :, 3 * bw:], ((0, 0), (0, 0), (0, RWKV_LORA_PAD - lora_w)))

    cond = jnp.concatenate([c, c_ctx[None, :], zeros(16 - n_batch - 1, dm)], axis=0)
    cond = jax.nn.silu(cond)
    mods = []
    for l in range(depth):
        low = _matmul(cond, mod_down[l], tm=16, tn=MOD_RANK, tk=dm, name="mod_down")
        up = _matmul(low, mod_up[l], tm=16, tn=2048, tk=MOD_RANK, bias=mod_b[l], name="mod_up")
        mods.append(up.reshape(16, N_MOD, dm))

    cos_t, sin_t = _rope_tables128(t_len)
    xs = jnp.concatenate([x.reshape(n_x, dm), ctx.reshape(n_c, dm)], axis=0)

    for l in range(depth):
        need_ctx = l < depth - 1
        mod = mods[l]
        lam_init = 0.8 - 0.6 * math.exp(-0.3 * l)
        lq1, lk1, lq2, lk2 = diff_lambda[l]
        lam = jnp.exp(jnp.sum(lq1 * lk1)) - jnp.exp(jnp.sum(lq2 * lk2)) + lam_init
        lam_row = jnp.full((1, DIFF_V), 1.0, F32) * lam

        h1 = _norm_mod(xs, norm1_g[l], mod, shift_row=0, scale_row=1, mod_index=mod_index_nm, tm=256,
                       rows=n_rows)
        p, w1_b = _fullk_matmul(h1, w_in_p, layer=l, tm=tm_big, tn=512, out_dtype=F32, name="in_proj",
                                cast_src=mlp_w1)

        q = _nm_matmul(p, mla_q_norm_g[l], w_uq_p[l], tm=tm, tn=512, out_dtype=BF16,
                       x_col_block=C_CQ // MLA_Q_LORA, kdim=MLA_Q_LORA, name="mla_q")
        kv = _nm_matmul(p, mla_kv_norm_g[l], w_ukv_p[l], tm=tm, tn=512, out_dtype=BF16,
                        x_col_block=C_CKV // MLA_KV_LORA, kdim=MLA_KV_LORA, name="mla_kv")
        y_mla = _mla_attention(q, kv, p, cos_t, sin_t, n_batch=n_batch, t_len=t_len, c_len=c_len, tq=tq,
                               with_ctx=need_ctx)

        y_diff = _diff_attention(p, cos_t, sin_t, lam_row, diff_norm_g[l].reshape(1, DIFF_V),
                                 n_batch=n_batch, t_len=t_len, c_len=c_len, tq=tq, with_ctx=need_ctx,
                                 out_scale=1.0 - lam_init)

        r_, kk, v_, lw, k_dir, kka, bonus, g_out, y_conv = _mixer_prep(
            p, rwkv_mu[l, :, :3 * bw], mu_lora[l], rwkv_k_k[l].reshape(1, bw), rwkv_k_a[l].reshape(1, bw),
            rwkv_r_k[l].reshape(1, bw), rwkv_w0[l].reshape(1, 2 * bw), rwkv_a0[l].reshape(1, 2 * bw),
            _block_diag2(rwkv_w2[l]).astype(BF16), _block_diag2(rwkv_a2[l]).astype(BF16), g2_p[l],
            conv_w[l], tr=min(128, tr), n_x=n_x, t_len=t_len, c_len=c_len)
        o_dirs = _rwkv_scan(r_, kk, v_, lw, k_dir, kka, n_batch=n_batch, t_len=t_len, c_len=c_len)
        y_rwkv = _rwkv_readout(o_dirs, bonus, g_out, rwkv_ln_g[l], rwkv_ln_b[l], tr=tr)

        rows = n_rows if need_ctx else n_x
        acc = _merge(p, (y_mla, y_rwkv, y_conv, y_diff), w_branch_b, gate_up_b, gate_b, layer=l,
                     tm=tm_big, tn=512, rows=rows)
        xs_new = _fullk_matmul(acc, w_out_b, layer=l, tm=tm_big, tn=512, out_dtype=F32, res=xs, mod=mod,
                               gate_row=2, mod_index=mod_index_big, rows=rows, name="out_proj")

        h2 = _norm_mod(xs_new, norm2_g[l], mod, shift_row=3, scale_row=4, mod_index=mod_index_nm, tm=256,
                       rows=rows)
        hid, w2_b = _fullk_matmul(h2, w1_b[None], layer=0, tm=tm_big, tn=512, out_dtype=BF16, act="relu2",
                                  name="mlp_up", cast_src=mlp_w2, cast_layer=l)
        xs = _matmul(hid, w2_b, tm=tm_big, tn=1024, tk=2048, res=xs_new, mod=mod, gate_row=5,
                     mod_index=mod_index_big, rows=rows, name="mlp_down")

    out = _final_norm(xs, final_norm_g, rows=n_x, tm=tm)
    return out.reshape(n_batch, t_len, dm)
```

```python
import functools
import math

import jax
import jax.numpy as jnp
from jax import lax
from jax.experimental import pallas as pl
from jax.experimental.pallas import tpu as pltpu

F32 = jnp.float32
BF16 = jnp.bfloat16

D_MODEL = 4096
BRANCH_W = 1024
GRID_W = 64
ROPE_BASE = 10000.0
NORM_EPS = 1e-6
N_MOD = 6
LANES = 128
HALO = 8
LOG2_E = 1.4426950408889634

MLA_HEADS = 8
MLA_NOPE = 128
MLA_ROPE = 64
MLA_V = 128
MLA_Q_LORA = 768
MLA_KV_LORA = 256
MLA_QK_PAD = 256

RWKV_HEAD = 64
RWKV_HEADS = 16
RWKV_LORA = 64
RWKV_GATE_LORA = 160
RWKV_GN_EPS = 64e-5
RWKV_CHUNK = 64
RWKV_LORA_PAD = 512

DIFF_HEADS = 8
DIFF_QK = 64
DIFF_V = 128
GATE_RANK = 256
MOD_RANK = 256

C_CQ = 0
C_CKV = 768
C_R = 1024
C_K = 2048
C_V = 3072
C_CB = 4096
C_CC = 5120
C_CU = 6144
C_DQ = 7168
C_DK = 8192
C_DV = 9216
C_LORA = 10240
C_GL = 10752
C_KROPE = 11008
P_COLS = 11264

VMEM_CAP = 56 * 1024 * 1024


def _cparams(sem, vmem_bytes):
    limit = int(min(VMEM_CAP, max(vmem_bytes * 1.5 + (4 << 20), 16 << 20)))
    return pltpu.CompilerParams(dimension_semantics=sem, vmem_limit_bytes=limit)


def _mod_index(i, tm, n_x_rows, t_len, n_batch):
    return jnp.where(i < n_x_rows // tm, i // (t_len // tm), n_batch)


def _dot(a, b):
    return jnp.dot(a.astype(BF16), b.astype(BF16), preferred_element_type=F32)


def _dot_nt(a, b):
    return lax.dot_general(a.astype(BF16), b.astype(BF16), (((1,), (1,)), ((), ())),
                           preferred_element_type=F32)


def _split3(x):
    hi = x.astype(BF16)
    r1 = x - hi.astype(F32)
    mid = r1.astype(BF16)
    lo = (r1 - mid.astype(F32)).astype(BF16)
    return hi, mid, lo


def _dot_exact_lhs(m_bf16, x):
    out = None
    for part in _split3(x):
        t = jnp.dot(m_bf16, part, preferred_element_type=F32)
        out = t if out is None else out + t
    return out


def _head_sums(x):
    r = lax.broadcasted_iota(jnp.int32, (LANES, LANES), 0)
    c = lax.broadcasted_iota(jnp.int32, (LANES, LANES), 1)
    ones_bd = jnp.where((r // RWKV_HEAD) == (c // RWKV_HEAD), 1.0, 0.0).astype(BF16)
    parts = _split3(x)
    cols = []
    for j in range(x.shape[1] // LANES):
        acc = None
        for part in parts:
            t = jnp.dot(part[:, j * LANES:(j + 1) * LANES], ones_bd, preferred_element_type=F32)
            acc = t if acc is None else acc + t
        cols.append(acc)
    return jnp.concatenate(cols, axis=1)


def _rope128(x, cos_t, sin_t):
    lane = lax.broadcasted_iota(jnp.int32, x.shape, 1)
    swapped = jnp.where((lane % 64) < 32, pltpu.roll(x, 96, 1), pltpu.roll(x, 32, 1))
    return x * cos_t + swapped * sin_t


def _matmul_kernel(*refs, nk, has_bias, has_res, gate_row, act):
    a_ref, w_ref = refs[0], refs[1]
    pos = 2
    bias_ref = res_ref = mod_ref = None
    if has_bias:
        bias_ref = refs[pos]; pos += 1
    if has_res:
        res_ref = refs[pos]; mod_ref = refs[pos + 1]; pos += 2
    o_ref, acc_ref = refs[pos], refs[pos + 1]
    k = pl.program_id(2)

    @pl.when(k == 0)
    def _():
        acc_ref[...] = jnp.zeros_like(acc_ref)

    acc_ref[...] += _dot(a_ref[...], w_ref[...])

    @pl.when(k == nk - 1)
    def _():
        y = acc_ref[...]
        if has_bias:
            y = y + bias_ref[...]
        if act == "relu2":
            y = jnp.square(jnp.maximum(y, 0.0))
        if has_res:
            y = res_ref[...] + mod_ref[0, gate_row:gate_row + 1, :] * y
        o_ref[...] = y.astype(o_ref.dtype)


def _matmul(a, w, *, tm, tn, tk, out_dtype=F32, bias=None, res=None, mod=None, gate_row=None,
            mod_index=None, act=None, rows=None, name=None):
    m = a.shape[0] if rows is None else rows
    kdim, n = w.shape
    assert a.shape[1] == kdim and m % tm == 0 and n % tn == 0 and kdim % tk == 0
    nk = kdim // tk
    in_specs = [pl.BlockSpec((tm, tk), lambda i, j, k: (i, k)),
                pl.BlockSpec((tk, tn), lambda i, j, k: (k, j))]
    args = [a, w]
    vmem = 2 * tm * tk * a.dtype.itemsize + 2 * tk * tn * w.dtype.itemsize + tm * tn * 4
    vmem += 2 * tm * tn * jnp.dtype(out_dtype).itemsize
    if bias is not None:
        in_specs.append(pl.BlockSpec((1, tn), lambda i, j, k: (0, j)))
        args.append(bias.reshape(1, n).astype(F32))
    if res is not None:
        in_specs.append(pl.BlockSpec((tm, tn), lambda i, j, k: (i, j)))
        in_specs.append(pl.BlockSpec((1, N_MOD, tn), lambda i, j, k: (mod_index(i), 0, j)))
        args += [res, mod]
        vmem += 2 * tm * tn * 4 + 2 * 8 * tn * 4
    kern = functools.partial(_matmul_kernel, nk=nk, has_bias=bias is not None,
                             has_res=res is not None, gate_row=gate_row, act=act)
    return pl.pallas_call(
        kern,
        grid=(m // tm, n // tn, nk),
        in_specs=in_specs,
        out_specs=pl.BlockSpec((tm, tn), lambda i, j, k: (i, j)),
        out_shape=jax.ShapeDtypeStruct((m, n), out_dtype),
        scratch_shapes=[pltpu.VMEM((tm, tn), F32)],
        compiler_params=_cparams(("parallel", "parallel", "arbitrary"), vmem),
        name=name,
    )(*args)


def _nm_matmul_kernel(*refs, has_mod, shift_row, scale_row, act):
    if has_mod:
        x_ref, g_ref, mod_ref, w_ref, o_ref, h_ref = refs
    else:
        x_ref, g_ref, w_ref, o_ref, h_ref = refs

    @pl.when(pl.program_id(1) == 0)
    def _():
        x = x_ref[...].astype(F32)
        y = x * lax.rsqrt(jnp.mean(x * x, axis=-1, keepdims=True) + NORM_EPS) * g_ref[...]
        if has_mod:
            y = y * (1.0 + mod_ref[0, scale_row:scale_row + 1, :]) + mod_ref[0, shift_row:shift_row + 1, :]
        h_ref[...] = y.astype(BF16)

    y = jnp.dot(h_ref[...], w_ref[...].astype(BF16), preferred_element_type=F32)
    if act == "relu2":
        y = jnp.square(jnp.maximum(y, 0.0))
    o_ref[...] = y.astype(o_ref.dtype)


def _nm_matmul(x, g, w, *, tm, tn, out_dtype, x_col_block=0, kdim=None, mod=None, shift_row=None,
               scale_row=None, mod_index=None, act=None, rows=None, name=None):
    m = x.shape[0] if rows is None else rows
    kdim = x.shape[1] if kdim is None else kdim
    n = w.shape[1]
    assert w.shape[0] == kdim and m % tm == 0 and n % tn == 0
    in_specs = [pl.BlockSpec((tm, kdim), lambda i, j: (i, x_col_block)),
                pl.BlockSpec((1, kdim), lambda i, j: (0, 0))]
    args = [x, g.reshape(1, kdim).astype(F32)]
    if mod is not None:
        in_specs.append(pl.BlockSpec((1, N_MOD, kdim), lambda i, j: (mod_index(i), 0, 0)))
        args.append(mod)
    in_specs.append(pl.BlockSpec((kdim, tn), lambda i, j: (0, j)))
    args.append(w)
    vmem = (2 * tm * kdim * x.dtype.itemsize + tm * kdim * 2 + 2 * kdim * tn * w.dtype.itemsize
            + 2 * tm * tn * jnp.dtype(out_dtype).itemsize + tm * tn * 4 + 4 * 8 * kdim * 4)
    kern = functools.partial(_nm_matmul_kernel, has_mod=mod is not None, shift_row=shift_row,
                             scale_row=scale_row, act=act)
    return pl.pallas_call(
        kern,
        grid=(m // tm, n // tn),
        in_specs=in_specs,
        out_specs=pl.BlockSpec((tm, tn), lambda i, j: (i, j)),
        out_shape=jax.ShapeDtypeStruct((m, n), out_dtype),
        scratch_shapes=[pltpu.VMEM((tm, kdim), BF16)],
        compiler_params=_cparams(("parallel", "arbitrary"), vmem),
        name=name,
    )(*args)


def _norm_mod_kernel(x_ref, g_ref, mod_ref, h_ref, *, shift_row, scale_row):
    x = x_ref[...]
    y = x * lax.rsqrt(jnp.mean(x * x, axis=-1, keepdims=True) + NORM_EPS) * g_ref[...]
    y = y * (1.0 + mod_ref[0, scale_row:scale_row + 1, :]) + mod_ref[0, shift_row:shift_row + 1, :]
    h_ref[...] = y.astype(h_ref.dtype)


def _norm_mod(xs, g, mod, *, shift_row, scale_row, mod_index, tm, rows):
    dm = xs.shape[1]
    return pl.pallas_call(
        functools.partial(_norm_mod_kernel, shift_row=shift_row, scale_row=scale_row),
        grid=(rows // tm,),
        in_specs=[pl.BlockSpec((tm, dm), lambda i: (i, 0)), pl.BlockSpec((1, dm), lambda i: (0, 0)),
                  pl.BlockSpec((1, N_MOD, dm), lambda i: (mod_index(i), 0, 0))],
        out_specs=pl.BlockSpec((tm, dm), lambda i: (i, 0)),
        out_shape=jax.ShapeDtypeStruct((rows, dm), BF16),
        compiler_params=_cparams(("parallel",), 2 * tm * dm * 6 + 4 * tm * dm * 4),
        name="norm_mod",
    )(xs, g.reshape(1, dm), mod)


def _fullk_kernel(*refs, has_res, has_cast, gate_row, act):
    refs = list(refs)
    if has_cast:
        cast_dst = refs.pop()
        cast_src = refs.pop(-2)
        cast_dst[...] = cast_src[...].astype(cast_dst.dtype)
    if has_res:
        a_ref, w_ref, res_ref, mod_ref, o_ref = refs
    else:
        a_ref, w_ref, o_ref = refs
    y = jnp.dot(a_ref[...], w_ref[...], preferred_element_type=F32)
    if act == "relu2":
        y = jnp.square(jnp.maximum(y, 0.0))
    if has_res:
        y = res_ref[...] + mod_ref[0, gate_row:gate_row + 1, :] * y
    o_ref[...] = y.astype(o_ref.dtype)


def _cast_rows_per_step(n_rows, n_steps):
    rows = 16
    while n_rows % rows or n_rows // rows > n_steps:
        rows *= 2
    return rows


def _fullk_matmul(a, w, *, tm, tn, out_dtype, act=None, res=None, mod=None, gate_row=None, mod_index=None,
                  rows=None, name=None, layer=None, cast_src=None, cast_layer=None):
    m = a.shape[0] if rows is None else rows
    _, kdim, n = w.shape
    assert a.shape[1] == kdim and m % tm == 0 and n % tn == 0 and a.dtype == BF16 and w.dtype == BF16
    in_specs = [pl.BlockSpec((tm, kdim), lambda i, j: (i, 0)),
                pl.BlockSpec((None, kdim, tn), lambda i, j: (layer, 0, j))]
    args = [a, w]
    vmem = 2 * tm * kdim * 2 + 2 * kdim * tn * 2 + 2 * tm * tn * jnp.dtype(out_dtype).itemsize + 2 * tm * tn * 4
    if res is not None:
        in_specs += [pl.BlockSpec((tm, tn), lambda i, j: (i, j)),
                     pl.BlockSpec((1, N_MOD, tn), lambda i, j: (mod_index(i), 0, j))]
        args += [res, mod]
        vmem += 2 * tm * tn * 4
    out_specs = pl.BlockSpec((tm, tn), lambda i, j: (i, j))
    out_shape = jax.ShapeDtypeStruct((m, n), out_dtype)
    if cast_src is not None:
        nj = n // tn
        _, c_rows, c_cols = cast_src.shape
        cr = _cast_rows_per_step(c_rows, (m // tm) * nj)
        cast_blk = lambda i, j: jnp.minimum(i * nj + j, c_rows // cr - 1)
        cast_layer = layer if cast_layer is None else cast_layer
        in_specs.append(pl.BlockSpec((None, cr, c_cols), lambda i, j: (cast_layer, cast_blk(i, j), 0)))
        args.append(cast_src)
        out_specs = [out_specs, pl.BlockSpec((cr, c_cols), lambda i, j: (cast_blk(i, j), 0))]
        out_shape = [out_shape, jax.ShapeDtypeStruct((c_rows, c_cols), BF16)]
        vmem += 2 * cr * c_cols * 6
    return pl.pallas_call(
        functools.partial(_fullk_kernel, has_res=res is not None, has_cast=cast_src is not None,
                          gate_row=gate_row, act=act),
        grid=(m // tm, n // tn),
        in_specs=in_specs,
        out_specs=out_specs,
        out_shape=out_shape,
        compiler_params=_cparams(("arbitrary", "arbitrary"), vmem),
        name=name,
    )(*args)


def _merge_kernel(gl_ref, y0_ref, y1_ref, y2_ref, y3_ref, wb_ref, gu_ref, gb_ref, o_ref):
    gl = gl_ref[...].astype(BF16)
    acc = None
    for i, y_ref in enumerate((y0_ref, y1_ref, y2_ref, y3_ref)):
        gate = jax.nn.sigmoid(jnp.dot(gl, gu_ref[i], preferred_element_type=F32) + gb_ref[i:i + 1, :])
        term = gate * jnp.dot(y_ref[...], wb_ref[i], preferred_element_type=F32)
        acc = term if acc is None else acc + term
    o_ref[...] = acc.astype(o_ref.dtype)


def _merge(p, ys, wb, gu, gb, *, layer, tm, tn, rows):
    n = wb.shape[3]
    bw = wb.shape[2]
    gr = gu.shape[2]
    y_spec = pl.BlockSpec((tm, bw), lambda i, j: (i, 0))
    vmem = (2 * tm * gr * 4 + 4 * 2 * tm * bw * 2 + 2 * 4 * bw * tn * 2 + 2 * 4 * gr * tn * 2
            + 2 * tm * tn * 2 + 3 * tm * tn * 4)
    return pl.pallas_call(
        _merge_kernel,
        grid=(rows // tm, n // tn),
        in_specs=[pl.BlockSpec((tm, gr), lambda i, j: (i, C_GL // GATE_RANK)),
                  y_spec, y_spec, y_spec, y_spec,
                  pl.BlockSpec((None, 4, bw, tn), lambda i, j: (layer, 0, 0, j)),
                  pl.BlockSpec((None, 4, gr, tn), lambda i, j: (layer, 0, 0, j)),
                  pl.BlockSpec((None, 4, tn), lambda i, j: (layer, 0, j))],
        out_specs=pl.BlockSpec((tm, tn), lambda i, j: (i, j)),
        out_shape=jax.ShapeDtypeStruct((rows, n), BF16),
        compiler_params=_cparams(("parallel", "arbitrary"), vmem),
        name="merge",
    )(p, *ys, wb, gu, gb)


def _softmax_parts(s_list, scale):
    m = None
    for s in s_list:
        sm = jnp.max(s, axis=-1, keepdims=True)
        m = sm if m is None else jnp.maximum(m, sm)
    e_list = [jnp.exp2((s - m) * (scale * LOG2_E)) for s in s_list]
    l = None
    for e in e_list:
        es = jnp.sum(e, axis=-1, keepdims=True)
        l = es if l is None else l + es
    return e_list, l


def _attend(q, ks, vs, scale):
    e_list, l = _softmax_parts([_dot_nt(q, k) for k in ks], scale)
    o = None
    for e, v in zip(e_list, vs):
        t = jnp.dot(e.astype(BF16), v, preferred_element_type=F32)
        o = t if o is None else o + t
    return o / l


def _mla_kernel(cos_ref, sin_ref, q_ref, knx_ref, knc_ref, krx_ref, krc_ref, vx_ref, vc_ref, o_ref,
                kx_s, kc_s, *, tq, scale):
    i = pl.program_id(2)

    @pl.when(i == 0)
    def _():
        kx_s[:, :MLA_NOPE] = knx_ref[...]
        kx_s[:, MLA_NOPE:] = _rope128(krx_ref[...], cos_ref[...], sin_ref[...]).astype(BF16)
        kc_s[:, :MLA_NOPE] = knc_ref[...]
        kc_s[:, MLA_NOPE:] = krc_ref[...].astype(BF16)

    row0 = pl.multiple_of(i * tq, tq)
    q = q_ref[...]
    q_rope = _rope128(q[:, MLA_NOPE:].astype(F32), cos_ref[pl.ds(row0, tq), :], sin_ref[pl.ds(row0, tq), :])
    q = jnp.concatenate([q[:, :MLA_NOPE], q_rope.astype(BF16)], axis=1)
    o = _attend(q, (kc_s[...], kx_s[...]), (vc_ref[...], vx_ref[...]), scale)
    o_ref[...] = o.astype(o_ref.dtype)


def _mla_ctx_kernel(q_ref, knc_ref, krc_ref, vc_ref, y_hbm, o_ref, *, scale):
    del y_hbm
    kc = jnp.concatenate([knc_ref[...], krc_ref[...].astype(BF16)], axis=1)
    o = _attend(q_ref[...], (kc,), (vc_ref[...],), scale)
    o_ref[...] = o.astype(o_ref.dtype)


def _diff_finish(o1, o2, lam_ref, g_ref, out_scale):
    o = o1 - lam_ref[...] * o2
    y = o * lax.rsqrt(jnp.mean(o * o, axis=-1, keepdims=True) + NORM_EPS) * g_ref[...]
    return y * out_scale


def _diff_halves(q):
    lane = lax.broadcasted_iota(jnp.int32, q.shape, 1)
    return jnp.where(lane < DIFF_QK, q, 0.0).astype(BF16), jnp.where(lane < DIFF_QK, 0.0, q).astype(BF16)


def _diff_kernel(cos_ref, sin_ref, lam_ref, g_ref, q_ref, kx_ref, kc_ref, vx_ref, vc_ref, o_ref,
                 kx_s, kc_s, vx_s, vc_s, *, tq, scale, out_scale):
    i = pl.program_id(2)

    @pl.when(i == 0)
    def _():
        kx_s[...] = _rope128(kx_ref[...], cos_ref[...], sin_ref[...]).astype(BF16)
        kc_s[...] = kc_ref[...].astype(BF16)
        vx_s[...] = vx_ref[...].astype(BF16)
        vc_s[...] = vc_ref[...].astype(BF16)

    row0 = pl.multiple_of(i * tq, tq)
    q1, q2 = _diff_halves(_rope128(q_ref[...], cos_ref[pl.ds(row0, tq), :], sin_ref[pl.ds(row0, tq), :]))
    ks, vs = (kc_s[...], kx_s[...]), (vc_s[...], vx_s[...])
    y = _diff_finish(_attend(q1, ks, vs, scale), _attend(q2, ks, vs, scale), lam_ref, g_ref, out_scale)
    o_ref[...] = y.astype(o_ref.dtype)


def _diff_ctx_kernel(lam_ref, g_ref, q_ref, kc_ref, vc_ref, y_hbm, o_ref, *, scale, out_scale):
    del y_hbm
    q1, q2 = _diff_halves(q_ref[...])
    ks, vs = (kc_ref[...].astype(BF16),), (vc_ref[...].astype(BF16),)
    y = _diff_finish(_attend(q1, ks, vs, scale), _attend(q2, ks, vs, scale), lam_ref, g_ref, out_scale)
    o_ref[...] = y.astype(o_ref.dtype)


def _mla_attention(q, kv, p, cos_t, sin_t, *, n_batch, t_len, c_len, tq, with_ctx):
    n_x = n_batch * t_len
    nq = t_len // tq
    out_rows = n_x + (n_batch * c_len if with_ctx else 0)
    cblk0 = n_x // c_len
    scale = (MLA_NOPE + MLA_ROPE) ** -0.5
    qmap = lambda b, h, i: (b * nq + i, h)
    full = lambda b, h, i: (0, 0)
    vmem = (4 * t_len * LANES * 4 + 2 * tq * 256 * 2 + 2 * (t_len + c_len) * LANES * (2 + 4 + 2)
            + (t_len + c_len) * 256 * 2 + 2 * tq * LANES * 2 + 6 * tq * (t_len + c_len) * 4)
    y = pl.pallas_call(
        functools.partial(_mla_kernel, tq=tq, scale=scale),
        grid=(n_batch, MLA_HEADS, nq),
        in_specs=[pl.BlockSpec((t_len, LANES), full), pl.BlockSpec((t_len, LANES), full),
                  pl.BlockSpec((tq, MLA_QK_PAD), qmap),
                  pl.BlockSpec((t_len, MLA_NOPE), lambda b, h, i: (b, h)),
                  pl.BlockSpec((c_len, MLA_NOPE), lambda b, h, i: (cblk0 + b, h)),
                  pl.BlockSpec((t_len, LANES), lambda b, h, i: (b, C_KROPE // LANES)),
                  pl.BlockSpec((c_len, LANES), lambda b, h, i: (cblk0 + b, C_KROPE // LANES)),
                  pl.BlockSpec((t_len, MLA_V), lambda b, h, i: (b, MLA_HEADS + h)),
                  pl.BlockSpec((c_len, MLA_V), lambda b, h, i: (cblk0 + b, MLA_HEADS + h))],
        out_specs=pl.BlockSpec((tq, MLA_V), qmap),
        out_shape=jax.ShapeDtypeStruct((out_rows, MLA_HEADS * MLA_V), BF16),
        scratch_shapes=[pltpu.VMEM((t_len, MLA_QK_PAD), BF16), pltpu.VMEM((c_len, MLA_QK_PAD), BF16)],
        compiler_params=_cparams(("parallel", "parallel", "arbitrary"), vmem),
        name="mla_attention",
    )(cos_t, sin_t, q, kv, kv, p, p, kv, kv)
    if not with_ctx:
        return y
    cmap = lambda b, h: (cblk0 + b, h)
    return pl.pallas_call(
        functools.partial(_mla_ctx_kernel, scale=scale),
        grid=(n_batch, MLA_HEADS),
        in_specs=[pl.BlockSpec((c_len, MLA_QK_PAD), cmap),
                  pl.BlockSpec((c_len, MLA_NOPE), cmap),
                  pl.BlockSpec((c_len, LANES), lambda b, h: (cblk0 + b, C_KROPE // LANES)),
                  pl.BlockSpec((c_len, MLA_V), lambda b, h: (cblk0 + b, MLA_HEADS + h)),
                  pl.BlockSpec(memory_space=pl.ANY)],
        out_specs=pl.BlockSpec((c_len, MLA_V), cmap),
        out_shape=jax.ShapeDtypeStruct(y.shape, y.dtype),
        input_output_aliases={4: 0},
        compiler_params=_cparams(("parallel", "parallel"), 16 * c_len * c_len * 4 + 8 * c_len * 256 * 4),
        name="mla_attention_ctx",
    )(q, kv, p, kv, y)


def _diff_attention(p, cos_t, sin_t, lam_row, g_row, *, n_batch, t_len, c_len, tq, with_ctx, out_scale):
    n_x = n_batch * t_len
    nq = t_len // tq
    out_rows = n_x + (n_batch * c_len if with_ctx else 0)
    cblk0 = n_x // c_len
    scale = DIFF_QK ** -0.5
    full = lambda b, h, i: (0, 0)
    qblk, kblk, vblk = C_DQ // LANES, C_DK // LANES, C_DV // LANES
    vmem = (4 * t_len * LANES * 4 + 2 * tq * LANES * 4 + 4 * (t_len + c_len) * LANES * 4
            + 2 * (t_len + c_len) * LANES * 2 + 2 * tq * LANES * 2 + 8 * tq * (t_len + c_len) * 4)
    y = pl.pallas_call(
        functools.partial(_diff_kernel, tq=tq, scale=scale, out_scale=out_scale),
        grid=(n_batch, DIFF_HEADS, nq),
        in_specs=[pl.BlockSpec((t_len, LANES), full), pl.BlockSpec((t_len, LANES), full),
                  pl.BlockSpec((1, DIFF_V), full), pl.BlockSpec((1, DIFF_V), full),
                  pl.BlockSpec((tq, LANES), lambda b, h, i: (b * nq + i, qblk + h)),
                  pl.BlockSpec((t_len, LANES), lambda b, h, i: (b, kblk + h)),
                  pl.BlockSpec((c_len, LANES), lambda b, h, i: (cblk0 + b, kblk + h)),
                  pl.BlockSpec((t_len, LANES), lambda b, h, i: (b, vblk + h)),
                  pl.BlockSpec((c_len, LANES), lambda b, h, i: (cblk0 + b, vblk + h))],
        out_specs=pl.BlockSpec((tq, DIFF_V), lambda b, h, i: (b * nq + i, h)),
        out_shape=jax.ShapeDtypeStruct((out_rows, DIFF_HEADS * DIFF_V), BF16),
        scratch_shapes=[pltpu.VMEM((t_len, LANES), BF16), pltpu.VMEM((c_len, LANES), BF16),
                        pltpu.VMEM((t_len, LANES), BF16), pltpu.VMEM((c_len, LANES), BF16)],
        compiler_params=_cparams(("parallel", "parallel", "arbitrary"), vmem),
        name="diff_attention",
    )(cos_t, sin_t, lam_row, g_row, p, p, p, p, p)
    if not with_ctx:
        return y
    one = lambda b, h: (0, 0)
    return pl.pallas_call(
        functools.partial(_diff_ctx_kernel, scale=scale, out_scale=out_scale),
        grid=(n_batch, DIFF_HEADS),
        in_specs=[pl.BlockSpec((1, DIFF_V), one), pl.BlockSpec((1, DIFF_V), one),
                  pl.BlockSpec((c_len, LANES), lambda b, h: (cblk0 + b, qblk + h)),
                  pl.BlockSpec((c_len, LANES), lambda b, h: (cblk0 + b, kblk + h)),
                  pl.BlockSpec((c_len, LANES), lambda b, h: (cblk0 + b, vblk + h)),
                  pl.BlockSpec(memory_space=pl.ANY)],
        out_specs=pl.BlockSpec((c_len, DIFF_V), lambda b, h: (cblk0 + b, h)),
        out_shape=jax.ShapeDtypeStruct(y.shape, y.dtype),
        input_output_aliases={5: 0},
        compiler_params=_cparams(("parallel", "parallel"), 24 * c_len * c_len * 4 + 8 * c_len * LANES * 4),
        name="diff_attention_ctx",
    )(lam_row, g_row, p, p, p, y)


def _prep_kernel(r_ref, k_ref, v_ref, cb_ref, cc_ref, cu_ref, lo_ref, hp_ref, hn_ref,
                 mu_ref, mul_ref, kk_w_ref, ka_w_ref, rk_w_ref, w0_ref, a0_ref, w2_ref, a2_ref, g2_ref,
                 cw_ref,
                 r_o, kk_o, v_o, lw_o, kd_o, ka_o, bonus_o, g_o, conv_o, *, tr, n_x, t_len, c_len):
    i = pl.program_id(0)
    g0 = i * tr
    seq = jnp.where(g0 < n_x, t_len, c_len)
    has_prev = ((g0 % seq) != 0).astype(F32)
    has_next = (((g0 + tr) % seq) != 0).astype(F32)
    row = lax.broadcasted_iota(jnp.int32, (tr, 1), 0)

    def neighbours(x, col0):
        width = x.shape[1]
        before = hp_ref[HALO - 1:HALO, col0:col0 + width] * has_prev
        after = hn_ref[0:1, col0:col0 + width] * has_next
        prev = jnp.where(row == 0, before, pltpu.roll(x, 1, 0))
        nxt = jnp.where(row == tr - 1, after, pltpu.roll(x, tr - 1, 0))
        return prev, nxt

    def shifted(x, col0, mu0, mu1):
        prev, nxt = neighbours(x, col0)
        return x + mu0 * (prev - x) + mu1 * (nxt - x)

    r = shifted(r_ref[...], C_R, mu_ref[0:1, :BRANCH_W], mu_ref[1:2, :BRANCH_W])
    k = shifted(k_ref[...], C_K, mu_ref[0:1, BRANCH_W:2 * BRANCH_W], mu_ref[1:2, BRANCH_W:2 * BRANCH_W])
    v = shifted(v_ref[...], C_V, mu_ref[0:1, 2 * BRANCH_W:], mu_ref[1:2, 2 * BRANCH_W:])
    lo = shifted(lo_ref[...], C_LORA, mul_ref[0:1, :], mul_ref[1:2, :])
    wd = jnp.tanh(lo[:, :2 * RWKV_LORA])
    ad = lo[:, 2 * RWKV_LORA:4 * RWKV_LORA]
    gd = jax.nn.sigmoid(lo[:, 4 * RWKV_LORA:])
    w_pre = _dot(wd, w2_ref[...]) + w0_ref[...]
    a_sig = jax.nn.sigmoid(_dot(ad, a2_ref[...]) + a0_ref[...])
    g_o[...] = _dot(gd, g2_ref[...])
    w_log = -(jnp.maximum(-w_pre, 0.0) + jnp.log(1.0 + jnp.exp(-jnp.abs(w_pre)))) - 0.5
    lw = -jnp.exp(w_log)
    kkf = k * kk_w_ref[...]
    kk = kkf * lax.rsqrt(_head_sums(kkf * kkf) + 1e-12)
    r_o[...] = r
    kk_o[...] = kk
    v_o[...] = v
    k_sum = None
    for d in range(2):
        a_d = a_sig[:, d * BRANCH_W:(d + 1) * BRANCH_W]
        k_d = k * (1.0 + (a_d - 1.0) * ka_w_ref[...])
        lw_o[d] = lw[:, d * BRANCH_W:(d + 1) * BRANCH_W]
        kd_o[d] = k_d
        ka_o[d] = kk * a_d
        k_sum = k_d if k_sum is None else k_sum + k_d
    bonus_o[...] = _head_sums(r * k_sum * rk_w_ref[...]) * v

    z = cc_ref[...] * cu_ref[...]
    z_before = hp_ref[HALO - 1:HALO, C_CC:C_CC + BRANCH_W] * hp_ref[HALO - 1:HALO, C_CU:C_CU + BRANCH_W] * has_prev
    z_after = hn_ref[0:1, C_CC:C_CC + BRANCH_W] * hn_ref[0:1, C_CU:C_CU + BRANCH_W] * has_next
    z_prev = jnp.where(row == 0, z_before, pltpu.roll(z, 1, 0))
    z_next = jnp.where(row == tr - 1, z_after, pltpu.roll(z, tr - 1, 0))
    y = cb_ref[...] * (cw_ref[0:1, :] * z_prev + cw_ref[1:2, :] * z + cw_ref[2:3, :] * z_next)
    conv_o[...] = y.astype(conv_o.dtype)


def _mixer_prep(p, mu_rkv, mu_lora, kk_w, ka_w, rk_w, w0, a0, w2bd, a2bd, g2p, conv_w, *, tr, n_x, t_len,
                c_len):
    n_rows = p.shape[0]
    bw = BRANCH_W
    last_halo = n_rows // HALO - 1
    col = lambda c: (lambda i: (i, c))
    const = lambda i: (0, 0)
    main = [pl.BlockSpec((tr, bw), col(C_R // bw)), pl.BlockSpec((tr, bw), col(C_K // bw)),
            pl.BlockSpec((tr, bw), col(C_V // bw)), pl.BlockSpec((tr, bw), col(C_CB // bw)),
            pl.BlockSpec((tr, bw), col(C_CC // bw)), pl.BlockSpec((tr, bw), col(C_CU // bw)),
            pl.BlockSpec((tr, RWKV_LORA_PAD), col(C_LORA // RWKV_LORA_PAD)),
            pl.BlockSpec((HALO, P_COLS), lambda i: (jnp.maximum(i * (tr // HALO) - 1, 0), 0)),
            pl.BlockSpec((HALO, P_COLS), lambda i: (jnp.minimum((i + 1) * (tr // HALO), last_halo), 0))]
    params = [mu_rkv, mu_lora, kk_w, ka_w, rk_w, w0, a0, w2bd, a2bd, g2p, conv_w]
    param_specs = [pl.BlockSpec(a.shape, const) for a in params]
    row_spec = pl.BlockSpec((tr, bw), lambda i: (i, 0))
    dir_spec = pl.BlockSpec((2, tr, bw), lambda i: (0, i, 0))
    f32_rows = jax.ShapeDtypeStruct((n_rows, bw), F32)
    f32_dirs = jax.ShapeDtypeStruct((2, n_rows, bw), F32)
    vmem = 2 * (7 * tr * bw * 4 + 2 * HALO * P_COLS * 4 + 12 * tr * bw * 4) + 30 * tr * bw * 4
    return pl.pallas_call(
        functools.partial(_prep_kernel, tr=tr, n_x=n_x, t_len=t_len, c_len=c_len),
        grid=(n_rows // tr,),
        in_specs=main + param_specs,
        out_specs=[row_spec, row_spec, row_spec, dir_spec, dir_spec, dir_spec, row_spec, row_spec, row_spec],
        out_shape=[f32_rows, f32_rows, f32_rows, f32_dirs, f32_dirs, f32_dirs, f32_rows, f32_rows,
                   jax.ShapeDtypeStruct((n_rows, bw), BF16)],
        compiler_params=_cparams(("parallel",), vmem),
        name="mixer_prep",
    )(p, p, p, p, p, p, p, p, p, *params)


def _rwkv_kernel(r0_ref, kk0_ref, v0_ref, r1_ref, kk1_ref, v1_ref, lw0_ref, kd0_ref, ka0_ref,
                 lw1_ref, kd1_ref, ka1_ref, o0_ref, o1_ref, s_ref):
    C = RWKV_CHUNK
    W = 2 * RWKV_HEAD
    n_pairs = RWKV_HEADS // 2

    @pl.when(pl.program_id(1) == 0)
    def _():
        s_ref[...] = jnp.zeros_like(s_ref)

    mm, mm_nt = _dot, _dot_nt
    lane = lax.broadcasted_iota(jnp.int32, (C, W), 1)
    row = lax.broadcasted_iota(jnp.int32, (C, W), 0)
    first = lane < RWKV_HEAD
    rc = lax.broadcasted_iota(jnp.int32, (C, C), 0)
    cc = lax.broadcasted_iota(jnp.int32, (C, C), 1)

    def bdiag(x):
        return jnp.concatenate([jnp.where(first, x, jnp.zeros_like(x)), jnp.where(first, jnp.zeros_like(x), x)],
                               axis=0)

    def direction(sign, r_ref, kk_ref, v_ref, lw_ref, kd_ref, ka_ref):
        m_incl = jnp.where((rc - cc) * sign >= 0, 1.0, 0.0).astype(BF16)
        lw = lw_ref[0]
        cum = _dot_exact_lhs(m_incl, lw)
        tot = jnp.sum(lw, axis=0, keepdims=True)
        e_ninc = jnp.exp(-cum)
        e_rem = jnp.exp(tot - cum)
        kk, ka, kd = kk_ref[...], ka_ref[0], kd_ref[0]
        order = (row - (lane % RWKV_HEAD)) * sign
        return dict(a_t=kk * jnp.exp(cum - lw), b_t=-ka * e_ninc, k_t=kd * e_ninc, r_t=r_ref[...] * jnp.exp(cum),
                    b_h=-ka * e_rem, k_h=kd * e_rem, v=v_ref[...], e_tot=jnp.exp(tot), strict=order > 0,
                    incl=order >= 0, eye=jnp.where(order == 0, 1.0, 0.0).astype(F32))

    dirs = (direction(1, r0_ref, kk0_ref, v0_ref, lw0_ref, kd0_ref, ka0_ref),
            direction(-1, r1_ref, kk1_ref, v1_ref, lw1_ref, kd1_ref, ka1_ref))
    o_refs = (o0_ref, o1_ref)

    chains = [(d, p) for d in range(2) for p in range(n_pairs)]
    n = range(len(chains))
    sl = [slice(W * p, W * (p + 1)) for (_, p) in chains]
    dd = [dirs[d] for (d, _) in chains]
    ar = [jnp.concatenate([dd[c]["a_t"][:, sl[c]], dd[c]["r_t"][:, sl[c]]], axis=0).astype(BF16) for c in n]
    a_bd = [bdiag(dd[c]["a_t"][:, sl[c]].astype(BF16)) for c in n]
    bk_bd = [jnp.concatenate([bdiag(dd[c]["b_t"][:, sl[c]].astype(BF16)), bdiag(dd[c]["k_t"][:, sl[c]].astype(BF16))],
                             axis=0) for c in n]
    v_bd = [bdiag(dd[c]["v"][:, sl[c]].astype(BF16)) for c in n]
    g = [mm_nt(ar[c], bk_bd[c]) for c in n]
    l_pow = [jnp.where(dd[c]["strict"], g[c][:C, :W], 0.0) for c in n]
    m_ak = [jnp.where(dd[c]["strict"], g[c][:C, W:], 0.0).astype(BF16) for c in n]
    a_r = [jnp.where(jnp.concatenate([dd[c]["incl"], dd[c]["incl"]], axis=1), g[c][C:], 0.0).astype(BF16) for c in n]
    mv = [mm(m_ak[c], v_bd[c]) for c in n]
    l_bd = [bdiag(l_pow[c].astype(BF16)) for c in n]
    t_inv = [dd[c]["eye"] + l_pow[c] for c in n]
    l_pow = [mm(l_pow[c], l_bd[c]) for c in n]
    for _ in range(int(math.log2(C)) - 2):
        l_bd = [bdiag(l_pow[c].astype(BF16)) for c in n]
        lt = [mm(jnp.concatenate([l_pow[c], t_inv[c]], axis=0), l_bd[c]) for c in n]
        l_pow = [lt[c][:C] for c in n]
        t_inv = [t_inv[c] + lt[c][C:] for c in n]
    t_inv = [t_inv[c] + mm(t_inv[c], bdiag(l_pow[c].astype(BF16))) for c in n]
    wu = [mm(t_inv[c], jnp.concatenate([a_bd[c], bdiag(mv[c].astype(BF16))], axis=1)) for c in n]
    s_bd = [s_ref[d, p] for (d, p) in chains]
    wr = [mm(jnp.concatenate([wu[c][:, :W], dd[c]["r_t"][:, sl[c]]], axis=0), s_bd[c]) for c in n]
    zv = [jnp.concatenate([bdiag((wr[c][:C] + wu[c][:, W:]).astype(BF16)), v_bd[c]], axis=0) for c in n]
    for c, (d, p) in enumerate(chains):
        o_refs[d][:, sl[c]] = wr[c][C:] + mm(a_r[c], zv[c])
    for c, (d, p) in enumerate(chains):
        bk_h = jnp.concatenate([bdiag(dd[c]["b_h"][:, sl[c]]), bdiag(dd[c]["k_h"][:, sl[c]])], axis=0)
        tot_col = jnp.broadcast_to(dd[c]["e_tot"][:, sl[c]], (W, W)).T
        s_ref[d, p] = tot_col * s_bd[c] + mm(bk_h.T, zv[c])


def _rwkv_scan(r, kk, v, lw, kd, ka, *, n_batch, t_len, c_len):
    C = RWKV_CHUNK
    rows, width = r.shape
    nct, nxt = c_len // C, t_len // C
    ctx_blk0 = n_batch * t_len // C

    def blk(b, d, s):
        j_c = s if d == 0 else nct - 1 - s
        j_x = s - nct if d == 0 else nxt - 1 - (s - nct)
        return jnp.where(s < nct, ctx_blk0 + b * nct + j_c, b * nxt + j_x)

    def shared(d):
        return pl.BlockSpec((C, width), lambda b, s: (blk(b, d, s), 0))

    def per_dir(d):
        return pl.BlockSpec((1, C, width), lambda b, s: (d, blk(b, d, s), 0))

    vmem = 2 * 14 * C * width * 4 + RWKV_HEADS * 128 * 128 * 4 + 128 * C * width * 4
    out = jax.ShapeDtypeStruct((rows, width), F32)
    return pl.pallas_call(
        _rwkv_kernel,
        grid=(n_batch, nct + nxt),
        in_specs=[shared(0), shared(0), shared(0), shared(1), shared(1), shared(1),
                  per_dir(0), per_dir(0), per_dir(0), per_dir(1), per_dir(1), per_dir(1)],
        out_specs=[shared(0), shared(1)],
        out_shape=[out, out],
        scratch_shapes=[pltpu.VMEM((2, RWKV_HEADS // 2, 128, 128), F32)],
        compiler_params=_cparams(("parallel", "arbitrary"), vmem),
        name="rwkv_scan",
    )(r, kk, v, r, kk, v, lw, kd, ka, lw, kd, ka)


def _rwkv_readout_kernel(o0_ref, o1_ref, bonus_ref, g_ref, lng_ref, lnb_ref, y_ref):
    o = o0_ref[...] + o1_ref[...]
    mean = _head_sums(o) * (1.0 / RWKV_HEAD)
    cen = o - mean
    var = _head_sums(cen * cen) * (1.0 / RWKV_HEAD)
    o_n = cen * lax.rsqrt(var + RWKV_GN_EPS) * lng_ref[...] + lnb_ref[...]
    y_ref[...] = ((o_n + bonus_ref[...]) * g_ref[...]).astype(y_ref.dtype)


def _rwkv_readout(o_dirs, bonus, g_out, ln_g, ln_b, *, tr):
    n_rows, bw = bonus.shape
    row_spec = pl.BlockSpec((tr, bw), lambda i: (i, 0))
    const = pl.BlockSpec((1, bw), lambda i: (0, 0))
    return pl.pallas_call(
        _rwkv_readout_kernel,
        grid=(n_rows // tr,),
        in_specs=[row_spec, row_spec, row_spec, row_spec, const, const],
        out_specs=row_spec,
        out_shape=jax.ShapeDtypeStruct((n_rows, bw), BF16),
        compiler_params=_cparams(("parallel",), 2 * 5 * tr * bw * 4 + 16 * tr * bw * 4),
        name="rwkv_readout",
    )(o_dirs[0], o_dirs[1], bonus, g_out, ln_g.reshape(1, bw), ln_b.reshape(1, bw))


def _rmsnorm_kernel(x_ref, g_ref, o_ref):
    x = x_ref[...]
    o_ref[...] = x * lax.rsqrt(jnp.mean(x * x, axis=-1, keepdims=True) + NORM_EPS) * g_ref[...]


def _final_norm(xs, g, *, rows, tm):
    dm = xs.shape[1]
    return pl.pallas_call(
        _rmsnorm_kernel,
        grid=(rows // tm,),
        in_specs=[pl.BlockSpec((tm, dm), lambda i: (i, 0)), pl.BlockSpec((1, dm), lambda i: (0, 0))],
        out_specs=pl.BlockSpec((tm, dm), lambda i: (i, 0)),
        out_shape=jax.ShapeDtypeStruct((rows, dm), F32),
        compiler_params=_cparams(("parallel",), 4 * tm * dm * 4),
        name="final_norm",
    )(xs, g.reshape(1, dm))


def _rope_tables128(n_tokens):
    rows = n_tokens // GRID_W
    row = jnp.repeat(jnp.arange(rows, dtype=F32), GRID_W)
    col = jnp.tile(jnp.arange(GRID_W, dtype=F32), rows)
    n_freq = 64 // 4
    inv = ROPE_BASE ** (-jnp.arange(n_freq, dtype=F32) / n_freq)
    ang = jnp.concatenate([row[:, None] * inv, col[:, None] * inv], axis=-1)
    cos, sin = jnp.cos(ang), jnp.sin(ang)
    return jnp.concatenate([cos, cos, cos, cos], axis=-1), jnp.concatenate([-sin, sin, -sin, sin], axis=-1)


def _block_diag2(w2):
    z = jnp.zeros_like(w2[0])
    return jnp.concatenate([jnp.concatenate([w2[0], z], axis=1), jnp.concatenate([z, w2[1]], axis=1)], axis=0)


def kernel(x, c, ctx, c_ctx, norm1_g, norm2_g, mod_down, mod_up, mod_b, w_in, mla_q_norm_g, mla_w_uq,
           mla_kv_norm_g, mla_w_ukv, rwkv_mu, rwkv_w0, rwkv_w2, rwkv_a0, rwkv_a2, rwkv_g2, rwkv_k_k,
           rwkv_k_a, rwkv_r_k, rwkv_ln_g, rwkv_ln_b, conv_w, diff_lambda, diff_norm_g, w_branch, gate_down,
           gate_up, gate_b, w_out, mlp_w1, mlp_w2, final_norm_g):
    n_batch, t_len, dm = x.shape
    c_len = ctx.shape[1]
    depth = w_in.shape[0]
    bw = BRANCH_W
    n_x = n_batch * t_len
    n_c = n_batch * c_len
    n_rows = n_x + n_c
    tm = 512 if (t_len % 512 == 0 and n_c % 512 == 0) else 256
    tq = 512 if t_len % 512 == 0 else min(256, c_len)
    tr = min(256, c_len)
    assert dm == D_MODEL and t_len % tm == 0 and n_c % tm == 0 and n_x % c_len == 0
    assert t_len % tq == 0 and c_len % RWKV_CHUNK == 0 and t_len % c_len == 0
    tm_big = 1024 if (t_len % 1024 == 0 and n_c % 1024 == 0) else tm
    mod_index = functools.partial(_mod_index, tm=tm, n_x_rows=n_x, t_len=t_len, n_batch=n_batch)
    mod_index_big = functools.partial(_mod_index, tm=tm_big, n_x_rows=n_x, t_len=t_len, n_batch=n_batch)
    mod_index_nm = functools.partial(_mod_index, tm=256, n_x_rows=n_x, t_len=t_len, n_batch=n_batch)

    zeros = lambda *s: jnp.zeros(s, F32)
    lora_w = 4 * RWKV_LORA + RWKV_GATE_LORA
    rw0 = MLA_Q_LORA + MLA_KV_LORA + MLA_ROPE
    cv0 = rw0 + 3 * bw + lora_w
    w_in_p = jnp.concatenate(
        [w_in[:, :, :C_R], w_in[:, :, rw0:rw0 + 3 * bw], w_in[:, :, cv0:],
         w_in[:, :, rw0 + 3 * bw:cv0], zeros(depth, dm, RWKV_LORA_PAD - lora_w), gate_down,
         w_in[:, :, C_R:rw0], zeros(depth, dm, P_COLS - C_KROPE - MLA_ROPE)], axis=-1).astype(BF16)
    w_uq_p = jnp.pad(mla_w_uq.reshape(depth, MLA_Q_LORA, MLA_HEADS, MLA_NOPE + MLA_ROPE),
                     ((0, 0), (0, 0), (0, 0), (0, MLA_QK_PAD - MLA_NOPE - MLA_ROPE))
                     ).reshape(depth, MLA_Q_LORA, MLA_HEADS * MLA_QK_PAD).astype(BF16)
    w_ukv_r = mla_w_ukv.reshape(depth, MLA_KV_LORA, MLA_HEADS, MLA_NOPE + MLA_V)
    w_ukv_p = jnp.concatenate([w_ukv_r[..., :MLA_NOPE].reshape(depth, MLA_KV_LORA, -1),
                               w_ukv_r[..., MLA_NOPE:].reshape(depth, MLA_KV_LORA, -1)], axis=-1).astype(BF16)
    w_branch_b = w_branch.astype(BF16)
    gate_up_b = jnp.moveaxis(gate_up, 2, 1).astype(BF16)
    w_out_b = w_out.astype(BF16)
    g2_p = jnp.pad(rwkv_g2, ((0, 0), (0, RWKV_LORA_PAD - 4 * RWKV_LORA - RWKV_GATE_LORA), (0, 0))).astype(BF16)
    mu_lora = jnp.pad(rwkv_mu[:, :, 3 * bw:], ((0, 0), (0, 0), (0, RWKV_LORA_PAD - lora_w)))

    cond = jnp.concatenate([c, c_ctx[None, :], zeros(16 - n_batch - 1, dm)], axis=0)
    cond = jax.nn.silu(cond)
    mods = []
    for l in range(depth):
        low = _matmul(cond, mod_down[l], tm=16, tn=MOD_RANK, tk=dm, name="mod_down")
        up = _matmul(low, mod_up[l], tm=16, tn=2048, tk=MOD_RANK, bias=mod_b[l], name="mod_up")
        mods.append(up.reshape(16, N_MOD, dm))

    cos_t, sin_t = _rope_tables128(t_len)
    xs = jnp.concatenate([x.reshape(n_x, dm), ctx.reshape(n_c, dm)], axis=0)

    for l in range(depth):
        need_ctx = l < depth - 1
        mod = mods[l]
        lam_init = 0.8 - 0.6 * math.exp(-0.3 * l)
        lq1, lk1, lq2, lk2 = diff_lambda[l]
        lam = jnp.exp(jnp.sum(lq1 * lk1)) - jnp.exp(jnp.sum(lq2 * lk2)) + lam_init
        lam_row = jnp.full((1, DIFF_V), 1.0, F32) * lam

        h1 = _norm_mod(xs, norm1_g[l], mod, shift_row=0, scale_row=1, mod_index=mod_index_nm, tm=256,
                       rows=n_rows)
        p, w1_b = _fullk_matmul(h1, w_in_p, layer=l, tm=tm_big, tn=512, out_dtype=F32, name="in_proj",
                                cast_src=mlp_w1)

        q = _nm_matmul(p, mla_q_norm_g[l], w_uq_p[l], tm=tm, tn=512, out_dtype=BF16,
                       x_col_block=C_CQ // MLA_Q_LORA, kdim=MLA_Q_LORA, name="mla_q")
        kv = _nm_matmul(p, mla_kv_norm_g[l], w_ukv_p[l], tm=tm, tn=512, out_dtype=BF16,
                        x_col_block=C_CKV // MLA_KV_LORA, kdim=MLA_KV_LORA, name="mla_kv")
        y_mla = _mla_attention(q, kv, p, cos_t, sin_t, n_batch=n_batch, t_len=t_len, c_len=c_len, tq=tq,
                               with_ctx=need_ctx)

        y_diff = _diff_attention(p, cos_t, sin_t, lam_row, diff_norm_g[l].reshape(1, DIFF_V),
                                 n_batch=n_batch, t_len=t_len, c_len=c_len, tq=tq, with_ctx=need_ctx,
                                 out_scale=1.0 - lam_init)

        r_, kk, v_, lw, k_dir, kka, bonus, g_out, y_conv = _mixer_prep(
            p, rwkv_mu[l, :, :3 * bw], mu_lora[l], rwkv_k_k[l].reshape(1, bw), rwkv_k_a[l].reshape(1, bw),
            rwkv_r_k[l].reshape(1, bw), rwkv_w0[l].reshape(1, 2 * bw), rwkv_a0[l].reshape(1, 2 * bw),
            _block_diag2(rwkv_w2[l]).astype(BF16), _block_diag2(rwkv_a2[l]).astype(BF16), g2_p[l],
            conv_w[l], tr=min(128, tr), n_x=n_x, t_len=t_len, c_len=c_len)
        o_dirs = _rwkv_scan(r_, kk, v_, lw, k_dir, kka, n_batch=n_batch, t_len=t_len, c_len=c_len)
        y_rwkv = _rwkv_readout(o_dirs, bonus, g_out, rwkv_ln_g[l], rwkv_ln_b[l], tr=tr)

        rows = n_rows if need_ctx else n_x
        acc = _merge(p, (y_mla, y_rwkv, y_conv, y_diff), w_branch_b, gate_up_b, gate_b, layer=l,
                     tm=tm_big, tn=512, rows=rows)
        xs_new = _fullk_matmul(acc, w_out_b, layer=l, tm=tm_big, tn=512, out_dtype=F32, res=xs, mod=mod,
                               gate_row=2, mod_index=mod_index_big, rows=rows, name="out_proj")

        h2 = _norm_mod(xs_new, norm2_g[l], mod, shift_row=3, scale_row=4, mod_index=mod_index_nm, tm=256,
                       rows=rows)
        hid, w2_b = _fullk_matmul(h2, w1_b[None], layer=0, tm=tm_big, tn=512, out_dtype=BF16, act="relu2",
                                  name="mlp_up", cast_src=mlp_w2, cast_layer=l)
        xs = _matmul(hid, w2_b, tm=tm_big, tn=1024, tk=2048, res=xs_new, mod=mod, gate_row=5,
                     mod_index=mod_index_big, rows=rows, name="mlp_down")

    out = _final_norm(xs, final_norm_g, rows=n_x, tm=tm)
    return out.reshape(n_batch, t_len, dm)
```

```python
import functools
import math

import jax
import jax.numpy as jnp
from jax import lax
from jax.experimental import pallas as pl
from jax.experimental.pallas import tpu as pltpu

F32 = jnp.float32
BF16 = jnp.bfloat16

D_MODEL = 4096
BRANCH_W = 1024
GRID_W = 64
ROPE_BASE = 10000.0
NORM_EPS = 1e-6
N_MOD = 6
LANES = 128
HALO = 8
LOG2_E = 1.4426950408889634

MLA_HEADS = 8
MLA_NOPE = 128
MLA_ROPE = 64
MLA_V = 128
MLA_Q_LORA = 768
MLA_KV_LORA = 256
MLA_QK_PAD = 256

RWKV_HEAD = 64
RWKV_HEADS = 16
RWKV_LORA = 64
RWKV_GATE_LORA = 160
RWKV_GN_EPS = 64e-5
RWKV_CHUNK = 64
RWKV_LORA_PAD = 512

DIFF_HEADS = 8
DIFF_QK = 64
DIFF_V = 128
GATE_RANK = 256
MOD_RANK = 256

C_CQ = 0
C_CKV = 768
C_R = 1024
C_K = 2048
C_V = 3072
C_CB = 4096
C_CC = 5120
C_CU = 6144
C_DQ = 7168
C_DK = 8192
C_DV = 9216
C_LORA = 10240
C_GL = 10752
C_KROPE = 11008
P_COLS = 11264

TN_UP = 1024
VMEM_CAP = 56 * 1024 * 1024


def _cparams(sem, vmem_bytes):
    limit = int(min(VMEM_CAP, max(vmem_bytes * 1.5 + (4 << 20), 16 << 20)))
    return pltpu.CompilerParams(dimension_semantics=sem, vmem_limit_bytes=limit)


def _mod_index(i, tm, n_x_rows, t_len, n_batch):
    return jnp.where(i < n_x_rows // tm, i // (t_len // tm), n_batch)


def _dot(a, b):
    return jnp.dot(a.astype(BF16), b.astype(BF16), preferred_element_type=F32)


def _dot_nt(a, b):
    return lax.dot_general(a.astype(BF16), b.astype(BF16), (((1,), (1,)), ((), ())),
                           preferred_element_type=F32)


def _split3(x):
    hi = x.astype(BF16)
    r1 = x - hi.astype(F32)
    mid = r1.astype(BF16)
    lo = (r1 - mid.astype(F32)).astype(BF16)
    return hi, mid, lo


def _dot_exact_lhs(m_bf16, x):
    out = None
    for part in _split3(x):
        t = jnp.dot(m_bf16, part, preferred_element_type=F32)
        out = t if out is None else out + t
    return out


def _head_sums(x):
    r = lax.broadcasted_iota(jnp.int32, (LANES, LANES), 0)
    c = lax.broadcasted_iota(jnp.int32, (LANES, LANES), 1)
    ones_bd = jnp.where((r // RWKV_HEAD) == (c // RWKV_HEAD), 1.0, 0.0).astype(BF16)
    parts = _split3(x)
    cols = []
    for j in range(x.shape[1] // LANES):
        acc = None
        for part in parts:
            t = jnp.dot(part[:, j * LANES:(j + 1) * LANES], ones_bd, preferred_element_type=F32)
            acc = t if acc is None else acc + t
        cols.append(acc)
    return jnp.concatenate(cols, axis=1)


def _rope128(x, cos_t, sin_t):
    lane = lax.broadcasted_iota(jnp.int32, x.shape, 1)
    swapped = jnp.where((lane % 64) < 32, pltpu.roll(x, 96, 1), pltpu.roll(x, 32, 1))
    return x * cos_t + swapped * sin_t


def _matmul_kernel(*refs, nk, has_bias, has_res, gate_row, act):
    a_ref, w_ref = refs[0], refs[1]
    pos = 2
    bias_ref = res_ref = mod_ref = None
    if has_bias:
        bias_ref = refs[pos]; pos += 1
    if has_res:
        res_ref = refs[pos]; mod_ref = refs[pos + 1]; pos += 2
    o_ref, acc_ref = refs[pos], refs[pos + 1]
    k = pl.program_id(2)

    @pl.when(k == 0)
    def _():
        acc_ref[...] = jnp.zeros_like(acc_ref)

    acc_ref[...] += _dot(a_ref[...], w_ref[...])

    @pl.when(k == nk - 1)
    def _():
        y = acc_ref[...]
        if has_bias:
            y = y + bias_ref[...]
        if act == "relu2":
            y = jnp.square(jnp.maximum(y, 0.0))
        if has_res:
            y = res_ref[...] + mod_ref[0, gate_row:gate_row + 1, :] * y
        o_ref[...] = y.astype(o_ref.dtype)


def _matmul(a, w, *, tm, tn, tk, out_dtype=F32, bias=None, res=None, mod=None, gate_row=None,
            mod_index=None, act=None, rows=None, name=None):
    m = a.shape[0] if rows is None else rows
    kdim, n = w.shape
    assert a.shape[1] == kdim and m % tm == 0 and n % tn == 0 and kdim % tk == 0
    nk = kdim // tk
    in_specs = [pl.BlockSpec((tm, tk), lambda i, j, k: (i, k)),
                pl.BlockSpec((tk, tn), lambda i, j, k: (k, j))]
    args = [a, w]
    vmem = 2 * tm * tk * a.dtype.itemsize + 2 * tk * tn * w.dtype.itemsize + tm * tn * 4
    vmem += 2 * tm * tn * jnp.dtype(out_dtype).itemsize
    if bias is not None:
        in_specs.append(pl.BlockSpec((1, tn), lambda i, j, k: (0, j)))
        args.append(bias.reshape(1, n).astype(F32))
    if res is not None:
        in_specs.append(pl.BlockSpec((tm, tn), lambda i, j, k: (i, j)))
        in_specs.append(pl.BlockSpec((1, N_MOD, tn), lambda i, j, k: (mod_index(i), 0, j)))
        args += [res, mod]
        vmem += 2 * tm * tn * 4 + 2 * 8 * tn * 4
    kern = functools.partial(_matmul_kernel, nk=nk, has_bias=bias is not None,
                             has_res=res is not None, gate_row=gate_row, act=act)
    return pl.pallas_call(
        kern,
        grid=(m // tm, n // tn, nk),
        in_specs=in_specs,
        out_specs=pl.BlockSpec((tm, tn), lambda i, j, k: (i, j)),
        out_shape=jax.ShapeDtypeStruct((m, n), out_dtype),
        scratch_shapes=[pltpu.VMEM((tm, tn), F32)],
        compiler_params=_cparams(("parallel", "parallel", "arbitrary"), vmem),
        name=name,
    )(*args)


def _nm_matmul_kernel(*refs, has_mod, shift_row, scale_row, act):
    if has_mod:
        x_ref, g_ref, mod_ref, w_ref, o_ref, h_ref = refs
    else:
        x_ref, g_ref, w_ref, o_ref, h_ref = refs

    @pl.when(pl.program_id(1) == 0)
    def _():
        x = x_ref[...].astype(F32)
        y = x * lax.rsqrt(jnp.mean(x * x, axis=-1, keepdims=True) + NORM_EPS) * g_ref[...]
        if has_mod:
            y = y * (1.0 + mod_ref[0, scale_row:scale_row + 1, :]) + mod_ref[0, shift_row:shift_row + 1, :]
        h_ref[...] = y.astype(BF16)

    y = jnp.dot(h_ref[...], w_ref[...].astype(BF16), preferred_element_type=F32)
    if act == "relu2":
        y = jnp.square(jnp.maximum(y, 0.0))
    o_ref[...] = y.astype(o_ref.dtype)


def _nm_matmul(x, g, w, *, tm, tn, out_dtype, x_col_block=0, kdim=None, mod=None, shift_row=None,
               scale_row=None, mod_index=None, act=None, rows=None, name=None):
    m = x.shape[0] if rows is None else rows
    kdim = x.shape[1] if kdim is None else kdim
    n = w.shape[1]
    assert w.shape[0] == kdim and m % tm == 0 and n % tn == 0
    in_specs = [pl.BlockSpec((tm, kdim), lambda i, j: (i, x_col_block)),
                pl.BlockSpec((1, kdim), lambda i, j: (0, 0))]
    args = [x, g.reshape(1, kdim).astype(F32)]
    if mod is not None:
        in_specs.append(pl.BlockSpec((1, N_MOD, kdim), lambda i, j: (mod_index(i), 0, 0)))
        args.append(mod)
    in_specs.append(pl.BlockSpec((kdim, tn), lambda i, j: (0, j)))
    args.append(w)
    vmem = (2 * tm * kdim * x.dtype.itemsize + tm * kdim * 2 + 2 * kdim * tn * w.dtype.itemsize
            + 2 * tm * tn * jnp.dtype(out_dtype).itemsize + tm * tn * 4 + 4 * 8 * kdim * 4)
    kern = functools.partial(_nm_matmul_kernel, has_mod=mod is not None, shift_row=shift_row,
                             scale_row=scale_row, act=act)
    return pl.pallas_call(
        kern,
        grid=(m // tm, n // tn),
        in_specs=in_specs,
        out_specs=pl.BlockSpec((tm, tn), lambda i, j: (i, j)),
        out_shape=jax.ShapeDtypeStruct((m, n), out_dtype),
        scratch_shapes=[pltpu.VMEM((tm, kdim), BF16)],
        compiler_params=_cparams(("parallel", "arbitrary"), vmem),
        name=name,
    )(*args)


def _norm_mod_kernel(x_ref, g_ref, mod_ref, h_ref, *, shift_row, scale_row):
    x = x_ref[...]
    y = x * lax.rsqrt(jnp.mean(x * x, axis=-1, keepdims=True) + NORM_EPS) * g_ref[...]
    y = y * (1.0 + mod_ref[0, scale_row:scale_row + 1, :]) + mod_ref[0, shift_row:shift_row + 1, :]
    h_ref[...] = y.astype(h_ref.dtype)


def _norm_mod(xs, g, mod, *, shift_row, scale_row, mod_index, tm, rows):
    dm = xs.shape[1]
    return pl.pallas_call(
        functools.partial(_norm_mod_kernel, shift_row=shift_row, scale_row=scale_row),
        grid=(rows // tm,),
        in_specs=[pl.BlockSpec((tm, dm), lambda i: (i, 0)), pl.BlockSpec((1, dm), lambda i: (0, 0)),
                  pl.BlockSpec((1, N_MOD, dm), lambda i: (mod_index(i), 0, 0))],
        out_specs=pl.BlockSpec((tm, dm), lambda i: (i, 0)),
        out_shape=jax.ShapeDtypeStruct((rows, dm), BF16),
        compiler_params=_cparams(("parallel",), 2 * tm * dm * 6 + 4 * tm * dm * 4),
        name="norm_mod",
    )(xs, g.reshape(1, dm), mod)


def _fullk_kernel(*refs, has_res, n_cast, gate_row, act):
    refs = list(refs)
    if n_cast:
        cast_dsts = refs[-n_cast:]
        cast_srcs = refs[-2 * n_cast - 1:-n_cast - 1]
        refs = refs[:-2 * n_cast - 1] + [refs[-n_cast - 1]]
        for src, dst in zip(cast_srcs, cast_dsts):
            dst[...] = src[...].astype(dst.dtype)
    if has_res:
        a_ref, w_ref, res_ref, mod_ref, o_ref = refs
    else:
        a_ref, w_ref, o_ref = refs
    y = jnp.dot(a_ref[...], w_ref[...], preferred_element_type=F32)
    if act == "relu2":
        y = jnp.square(jnp.maximum(y, 0.0))
    if has_res:
        y = res_ref[...] + mod_ref[0, gate_row:gate_row + 1, :] * y
    o_ref[...] = y.astype(o_ref.dtype)


def _cast_rows_per_step(n_rows, n_steps):
    rows = 16
    while n_rows % rows or n_rows // rows > n_steps:
        rows *= 2
    return rows


def _fullk_matmul(a, w, *, tm, tn, out_dtype, act=None, res=None, mod=None, gate_row=None, mod_index=None,
                  rows=None, name=None, layer=None, cast_srcs=(), cast_layer=None):
    m = a.shape[0] if rows is None else rows
    _, kdim, n = w.shape
    assert a.shape[1] == kdim and m % tm == 0 and n % tn == 0 and a.dtype == BF16 and w.dtype == BF16
    in_specs = [pl.BlockSpec((tm, kdim), lambda i, j: (i, 0)),
                pl.BlockSpec((None, kdim, tn), lambda i, j: (layer, 0, j))]
    args = [a, w]
    vmem = 2 * tm * kdim * 2 + 2 * kdim * tn * 2 + 2 * tm * tn * jnp.dtype(out_dtype).itemsize + 2 * tm * tn * 4
    if res is not None:
        in_specs += [pl.BlockSpec((tm, tn), lambda i, j: (i, j)),
                     pl.BlockSpec((1, N_MOD, tn), lambda i, j: (mod_index(i), 0, j))]
        args += [res, mod]
        vmem += 2 * tm * tn * 4
    out_specs = [pl.BlockSpec((tm, tn), lambda i, j: (i, j))]
    out_shape = [jax.ShapeDtypeStruct((m, n), out_dtype)]
    nj = n // tn
    cast_layer = layer if cast_layer is None else cast_layer
    for cast_src in cast_srcs:
        _, c_rows, c_cols = cast_src.shape
        cr = _cast_rows_per_step(c_rows, (m // tm) * nj)
        cast_blk = lambda i, j, last=c_rows // cr - 1: jnp.minimum(i * nj + j, last)
        in_specs.append(pl.BlockSpec((None, cr, c_cols), lambda i, j, blk=cast_blk: (cast_layer, blk(i, j), 0)))
        args.append(cast_src)
        out_specs.append(pl.BlockSpec((cr, c_cols), lambda i, j, blk=cast_blk: (blk(i, j), 0)))
        out_shape.append(jax.ShapeDtypeStruct((c_rows, c_cols), BF16))
        vmem += 2 * cr * c_cols * 6
    return pl.pallas_call(
        functools.partial(_fullk_kernel, has_res=res is not None, n_cast=len(cast_srcs),
                          gate_row=gate_row, act=act),
        grid=(m // tm, n // tn),
        in_specs=in_specs,
        out_specs=out_specs,
        out_shape=out_shape,
        compiler_params=_cparams(("arbitrary", "arbitrary"), vmem),
        name=name,
    )(*args)


def _merge_kernel(gl_ref, y0_ref, y1_ref, y2_ref, y3_ref, wb_ref, gu_ref, gb_ref, o_ref):
    gl = gl_ref[...].astype(BF16)
    bw = wb_ref.shape[0] // 4
    acc = None
    for i, y_ref in enumerate((y0_ref, y1_ref, y2_ref, y3_ref)):
        gate = jax.nn.sigmoid(jnp.dot(gl, gu_ref[i], preferred_element_type=F32) + gb_ref[i:i + 1, :])
        term = gate * jnp.dot(y_ref[...], wb_ref[i * bw:(i + 1) * bw, :], preferred_element_type=F32)
        acc = term if acc is None else acc + term
    o_ref[...] = acc.astype(o_ref.dtype)


def _merge(p, ys, wb, gu, gb, *, layer, tm, tn, rows):
    n = wb.shape[1]
    bw = wb.shape[0] // 4
    gr = gu.shape[2]
    y_spec = pl.BlockSpec((tm, bw), lambda i, j: (i, 0))
    vmem = (2 * tm * gr * 4 + 4 * 2 * tm * bw * 2 + 2 * 4 * bw * tn * 2 + 2 * 4 * gr * tn * 2
            + 2 * tm * tn * 2 + 3 * tm * tn * 4)
    return pl.pallas_call(
        _merge_kernel,
        grid=(rows // tm, n // tn),
        in_specs=[pl.BlockSpec((tm, gr), lambda i, j: (i, C_GL // GATE_RANK)),
                  y_spec, y_spec, y_spec, y_spec,
                  pl.BlockSpec((4 * bw, tn), lambda i, j: (0, j)),
                  pl.BlockSpec((None, 4, gr, tn), lambda i, j: (layer, 0, 0, j)),
                  pl.BlockSpec((None, 4, tn), lambda i, j: (layer, 0, j))],
        out_specs=pl.BlockSpec((tm, tn), lambda i, j: (i, j)),
        out_shape=jax.ShapeDtypeStruct((rows, n), BF16),
        compiler_params=_cparams(("parallel", "arbitrary"), vmem),
        name="merge",
    )(p, *ys, wb, gu, gb)


def _softmax_parts(s_list, scale):
    m = None
    for s in s_list:
        sm = jnp.max(s, axis=-1, keepdims=True)
        m = sm if m is None else jnp.maximum(m, sm)
    e_list = [jnp.exp2((s - m) * (scale * LOG2_E)) for s in s_list]
    l = None
    for e in e_list:
        es = jnp.sum(e, axis=-1, keepdims=True)
        l = es if l is None else l + es
    return e_list, l


def _attend(q, ks, vs, scale):
    e_list, l = _softmax_parts([_dot_nt(q, k) for k in ks], scale)
    o = None
    for e, v in zip(e_list, vs):
        t = jnp.dot(e.astype(BF16), v, preferred_element_type=F32)
        o = t if o is None else o + t
    return o / l


def _mla_kernel(cos_ref, sin_ref, q_ref, knx_ref, knc_ref, krx_ref, krc_ref, vx_ref, vc_ref, o_ref,
                kx_s, kc_s, *, tq, scale):
    i = pl.program_id(2)

    @pl.when(i == 0)
    def _():
        kx_s[:, :MLA_NOPE] = knx_ref[...]
        kx_s[:, MLA_NOPE:] = _rope128(krx_ref[...], cos_ref[...], sin_ref[...]).astype(BF16)
        kc_s[:, :MLA_NOPE] = knc_ref[...]
        kc_s[:, MLA_NOPE:] = krc_ref[...].astype(BF16)

    row0 = pl.multiple_of(i * tq, tq)
    q = q_ref[...]
    q_rope = _rope128(q[:, MLA_NOPE:].astype(F32), cos_ref[pl.ds(row0, tq), :], sin_ref[pl.ds(row0, tq), :])
    q = jnp.concatenate([q[:, :MLA_NOPE], q_rope.astype(BF16)], axis=1)
    o = _attend(q, (kc_s[...], kx_s[...]), (vc_ref[...], vx_ref[...]), scale)
    o_ref[...] = o.astype(o_ref.dtype)


def _mla_ctx_kernel(q_ref, knc_ref, krc_ref, vc_ref, y_hbm, o_ref, *, scale):
    del y_hbm
    kc = jnp.concatenate([knc_ref[...], krc_ref[...].astype(BF16)], axis=1)
    o = _attend(q_ref[...], (kc,), (vc_ref[...],), scale)
    o_ref[...] = o.astype(o_ref.dtype)


def _diff_finish(o1, o2, lam_ref, g_ref, out_scale):
    o = o1 - lam_ref[...] * o2
    y = o * lax.rsqrt(jnp.mean(o * o, axis=-1, keepdims=True) + NORM_EPS) * g_ref[...]
    return y * out_scale


def _diff_halves(q):
    lane = lax.broadcasted_iota(jnp.int32, q.shape, 1)
    return jnp.where(lane < DIFF_QK, q, 0.0).astype(BF16), jnp.where(lane < DIFF_QK, 0.0, q).astype(BF16)


def _diff_kernel(cos_ref, sin_ref, lam_ref, g_ref, q_ref, kx_ref, kc_ref, vx_ref, vc_ref, o_ref,
                 kx_s, kc_s, vx_s, vc_s, *, tq, scale, out_scale):
    i = pl.program_id(2)

    @pl.when(i == 0)
    def _():
        kx_s[...] = _rope128(kx_ref[...], cos_ref[...], sin_ref[...]).astype(BF16)
        kc_s[...] = kc_ref[...].astype(BF16)
        vx_s[...] = vx_ref[...].astype(BF16)
        vc_s[...] = vc_ref[...].astype(BF16)

    row0 = pl.multiple_of(i * tq, tq)
    q1, q2 = _diff_halves(_rope128(q_ref[...], cos_ref[pl.ds(row0, tq), :], sin_ref[pl.ds(row0, tq), :]))
    ks, vs = (kc_s[...], kx_s[...]), (vc_s[...], vx_s[...])
    y = _diff_finish(_attend(q1, ks, vs, scale), _attend(q2, ks, vs, scale), lam_ref, g_ref, out_scale)
    o_ref[...] = y.astype(o_ref.dtype)


def _diff_ctx_kernel(lam_ref, g_ref, q_ref, kc_ref, vc_ref, y_hbm, o_ref, *, scale, out_scale):
    del y_hbm
    q1, q2 = _diff_halves(q_ref[...])
    ks, vs = (kc_ref[...].astype(BF16),), (vc_ref[...].astype(BF16),)
    y = _diff_finish(_attend(q1, ks, vs, scale), _attend(q2, ks, vs, scale), lam_ref, g_ref, out_scale)
    o_ref[...] = y.astype(o_ref.dtype)


def _mla_attention(q, kv, p, cos_t, sin_t, *, n_batch, t_len, c_len, tq, with_ctx):
    n_x = n_batch * t_len
    nq = t_len // tq
    out_rows = n_x + (n_batch * c_len if with_ctx else 0)
    cblk0 = n_x // c_len
    scale = (MLA_NOPE + MLA_ROPE) ** -0.5
    qmap = lambda b, h, i: (b * nq + i, h)
    full = lambda b, h, i: (0, 0)
    vmem = (4 * t_len * LANES * 4 + 2 * tq * 256 * 2 + 2 * (t_len + c_len) * LANES * (2 + 4 + 2)
            + (t_len + c_len) * 256 * 2 + 2 * tq * LANES * 2 + 6 * tq * (t_len + c_len) * 4)
    y = pl.pallas_call(
        functools.partial(_mla_kernel, tq=tq, scale=scale),
        grid=(n_batch, MLA_HEADS, nq),
        in_specs=[pl.BlockSpec((t_len, LANES), full), pl.BlockSpec((t_len, LANES), full),
                  pl.BlockSpec((tq, MLA_QK_PAD), qmap),
                  pl.BlockSpec((t_len, MLA_NOPE), lambda b, h, i: (b, h)),
                  pl.BlockSpec((c_len, MLA_NOPE), lambda b, h, i: (cblk0 + b, h)),
                  pl.BlockSpec((t_len, LANES), lambda b, h, i: (b, C_KROPE // LANES)),
                  pl.BlockSpec((c_len, LANES), lambda b, h, i: (cblk0 + b, C_KROPE // LANES)),
                  pl.BlockSpec((t_len, MLA_V), lambda b, h, i: (b, MLA_HEADS + h)),
                  pl.BlockSpec((c_len, MLA_V), lambda b, h, i: (cblk0 + b, MLA_HEADS + h))],
        out_specs=pl.BlockSpec((tq, MLA_V), qmap),
        out_shape=jax.ShapeDtypeStruct((out_rows, MLA_HEADS * MLA_V), BF16),
        scratch_shapes=[pltpu.VMEM((t_len, MLA_QK_PAD), BF16), pltpu.VMEM((c_len, MLA_QK_PAD), BF16)],
        compiler_params=_cparams(("parallel", "parallel", "arbitrary"), vmem),
        name="mla_attention",
    )(cos_t, sin_t, q, kv, kv, p, p, kv, kv)
    if not with_ctx:
        return y
    cmap = lambda b, h: (cblk0 + b, h)
    return pl.pallas_call(
        functools.partial(_mla_ctx_kernel, scale=scale),
        grid=(n_batch, MLA_HEADS),
        in_specs=[pl.BlockSpec((c_len, MLA_QK_PAD), cmap),
                  pl.BlockSpec((c_len, MLA_NOPE), cmap),
                  pl.BlockSpec((c_len, LANES), lambda b, h: (cblk0 + b, C_KROPE // LANES)),
                  pl.BlockSpec((c_len, MLA_V), lambda b, h: (cblk0 + b, MLA_HEADS + h)),
                  pl.BlockSpec(memory_space=pl.ANY)],
        out_specs=pl.BlockSpec((c_len, MLA_V), cmap),
        out_shape=jax.ShapeDtypeStruct(y.shape, y.dtype),
        input_output_aliases={4: 0},
        compiler_params=_cparams(("parallel", "parallel"), 16 * c_len * c_len * 4 + 8 * c_len * 256 * 4),
        name="mla_attention_ctx",
    )(q, kv, p, kv, y)


def _diff_attention(p, cos_t, sin_t, lam_row, g_row, *, n_batch, t_len, c_len, tq, with_ctx, out_scale):
    n_x = n_batch * t_len
    nq = t_len // tq
    out_rows = n_x + (n_batch * c_len if with_ctx else 0)
    cblk0 = n_x // c_len
    scale = DIFF_QK ** -0.5
    full = lambda b, h, i: (0, 0)
    qblk, kblk, vblk = C_DQ // LANES, C_DK // LANES, C_DV // LANES
    vmem = (4 * t_len * LANES * 4 + 2 * tq * LANES * 4 + 4 * (t_len + c_len) * LANES * 4
            + 2 * (t_len + c_len) * LANES * 2 + 2 * tq * LANES * 2 + 8 * tq * (t_len + c_len) * 4)
    y = pl.pallas_call(
        functools.partial(_diff_kernel, tq=tq, scale=scale, out_scale=out_scale),
        grid=(n_batch, DIFF_HEADS, nq),
        in_specs=[pl.BlockSpec((t_len, LANES), full), pl.BlockSpec((t_len, LANES), full),
                  pl.BlockSpec((1, DIFF_V), full), pl.BlockSpec((1, DIFF_V), full),
                  pl.BlockSpec((tq, LANES), lambda b, h, i: (b * nq + i, qblk + h)),
                  pl.BlockSpec((t_len, LANES), lambda b, h, i: (b, kblk + h)),
                  pl.BlockSpec((c_len, LANES), lambda b, h, i: (cblk0 + b, kblk + h)),
                  pl.BlockSpec((t_len, LANES), lambda b, h, i: (b, vblk + h)),
                  pl.BlockSpec((c_len, LANES), lambda b, h, i: (cblk0 + b, vblk + h))],
        out_specs=pl.BlockSpec((tq, DIFF_V), lambda b, h, i: (b * nq + i, h)),
        out_shape=jax.ShapeDtypeStruct((out_rows, DIFF_HEADS * DIFF_V), BF16),
        scratch_shapes=[pltpu.VMEM((t_len, LANES), BF16), pltpu.VMEM((c_len, LANES), BF16),
                        pltpu.VMEM((t_len, LANES), BF16), pltpu.VMEM((c_len, LANES), BF16)],
        compiler_params=_cparams(("parallel", "parallel", "arbitrary"), vmem),
        name="diff_attention",
    )(cos_t, sin_t, lam_row, g_row, p, p, p, p, p)
    if not with_ctx:
        return y
    one = lambda b, h: (0, 0)
    return pl.pallas_call(
        functools.partial(_diff_ctx_kernel, scale=scale, out_scale=out_scale),
        grid=(n_batch, DIFF_HEADS),
        in_specs=[pl.BlockSpec((1, DIFF_V), one), pl.BlockSpec((1, DIFF_V), one),
                  pl.BlockSpec((c_len, LANES), lambda b, h: (cblk0 + b, qblk + h)),
                  pl.BlockSpec((c_len, LANES), lambda b, h: (cblk0 + b, kblk + h)),
                  pl.BlockSpec((c_len, LANES), lambda b, h: (cblk0 + b, vblk + h)),
                  pl.BlockSpec(memory_space=pl.ANY)],
        out_specs=pl.BlockSpec((c_len, DIFF_V), lambda b, h: (cblk0 + b, h)),
        out_shape=jax.ShapeDtypeStruct(y.shape, y.dtype),
        input_output_aliases={5: 0},
        compiler_params=_cparams(("parallel", "parallel"), 24 * c_len * c_len * 4 + 8 * c_len * LANES * 4),
        name="diff_attention_ctx",
    )(lam_row, g_row, p, p, p, y)


def _prep_kernel(r_ref, k_ref, v_ref, cb_ref, cc_ref, cu_ref, lo_ref, hp_ref, hn_ref,
                 mu_ref, mul_ref, kk_w_ref, ka_w_ref, rk_w_ref, w0_ref, a0_ref, w2_ref, a2_ref, g2_ref,
                 cw_ref,
                 r_o, kk_o, v_o, lw_o, kd_o, ka_o, bonus_o, g_o, conv_o, *, tr, n_x, t_len, c_len):
    i = pl.program_id(0)
    g0 = i * tr
    seq = jnp.where(g0 < n_x, t_len, c_len)
    has_prev = ((g0 % seq) != 0).astype(F32)
    has_next = (((g0 + tr) % seq) != 0).astype(F32)
    row = lax.broadcasted_iota(jnp.int32, (tr, 1), 0)

    def neighbours(x, col0):
        width = x.shape[1]
        before = hp_ref[HALO - 1:HALO, col0:col0 + width] * has_prev
        after = hn_ref[0:1, col0:col0 + width] * has_next
        prev = jnp.where(row == 0, before, pltpu.roll(x, 1, 0))
        nxt = jnp.where(row == tr - 1, after, pltpu.roll(x, tr - 1, 0))
        return prev, nxt

    def shifted(x, col0, mu0, mu1):
        prev, nxt = neighbours(x, col0)
        return x + mu0 * (prev - x) + mu1 * (nxt - x)

    r = shifted(r_ref[...], C_R, mu_ref[0:1, :BRANCH_W], mu_ref[1:2, :BRANCH_W])
    k = shifted(k_ref[...], C_K, mu_ref[0:1, BRANCH_W:2 * BRANCH_W], mu_ref[1:2, BRANCH_W:2 * BRANCH_W])
    v = shifted(v_ref[...], C_V, mu_ref[0:1, 2 * BRANCH_W:], mu_ref[1:2, 2 * BRANCH_W:])
    lo = shifted(lo_ref[...], C_LORA, mul_ref[0:1, :], mul_ref[1:2, :])
    wd = jnp.tanh(lo[:, :2 * RWKV_LORA])
    ad = lo[:, 2 * RWKV_LORA:4 * RWKV_LORA]
    gd = jax.nn.sigmoid(lo[:, 4 * RWKV_LORA:])
    w_pre = _dot(wd, w2_ref[...]) + w0_ref[...]
    a_sig = jax.nn.sigmoid(_dot(ad, a2_ref[...]) + a0_ref[...])
    g_o[...] = _dot(gd, g2_ref[...])
    w_log = -(jnp.maximum(-w_pre, 0.0) + jnp.log(1.0 + jnp.exp(-jnp.abs(w_pre)))) - 0.5
    lw = -jnp.exp(w_log)
    kkf = k * kk_w_ref[...]
    kk = kkf * lax.rsqrt(_head_sums(kkf * kkf) + 1e-12)
    r_o[...] = r
    kk_o[...] = kk
    v_o[...] = v
    k_sum = None
    for d in range(2):
        a_d = a_sig[:, d * BRANCH_W:(d + 1) * BRANCH_W]
        k_d = k * (1.0 + (a_d - 1.0) * ka_w_ref[...])
        lw_o[d] = lw[:, d * BRANCH_W:(d + 1) * BRANCH_W]
        kd_o[d] = k_d
        ka_o[d] = kk * a_d
        k_sum = k_d if k_sum is None else k_sum + k_d
    bonus_o[...] = _head_sums(r * k_sum * rk_w_ref[...]) * v

    z = cc_ref[...] * cu_ref[...]
    z_before = hp_ref[HALO - 1:HALO, C_CC:C_CC + BRANCH_W] * hp_ref[HALO - 1:HALO, C_CU:C_CU + BRANCH_W] * has_prev
    z_after = hn_ref[0:1, C_CC:C_CC + BRANCH_W] * hn_ref[0:1, C_CU:C_CU + BRANCH_W] * has_next
    z_prev = jnp.where(row == 0, z_before, pltpu.roll(z, 1, 0))
    z_next = jnp.where(row == tr - 1, z_after, pltpu.roll(z, tr - 1, 0))
    y = cb_ref[...] * (cw_ref[0:1, :] * z_prev + cw_ref[1:2, :] * z + cw_ref[2:3, :] * z_next)
    conv_o[...] = y.astype(conv_o.dtype)


def _mixer_prep(p, mu_rkv, mu_lora, kk_w, ka_w, rk_w, w0, a0, w2bd, a2bd, g2p, conv_w, *, tr, n_x, t_len,
                c_len):
    n_rows = p.shape[0]
    bw = BRANCH_W
    last_halo = n_rows // HALO - 1
    col = lambda c: (lambda i: (i, c))
    const = lambda i: (0, 0)
    main = [pl.BlockSpec((tr, bw), col(C_R // bw)), pl.BlockSpec((tr, bw), col(C_K // bw)),
            pl.BlockSpec((tr, bw), col(C_V // bw)), pl.BlockSpec((tr, bw), col(C_CB // bw)),
            pl.BlockSpec((tr, bw), col(C_CC // bw)), pl.BlockSpec((tr, bw), col(C_CU // bw)),
            pl.BlockSpec((tr, RWKV_LORA_PAD), col(C_LORA // RWKV_LORA_PAD)),
            pl.BlockSpec((HALO, P_COLS), lambda i: (jnp.maximum(i * (tr // HALO) - 1, 0), 0)),
            pl.BlockSpec((HALO, P_COLS), lambda i: (jnp.minimum((i + 1) * (tr // HALO), last_halo), 0))]
    params = [mu_rkv, mu_lora, kk_w, ka_w, rk_w, w0, a0, w2bd, a2bd, g2p, conv_w]
    param_specs = [pl.BlockSpec(a.shape, const) for a in params]
    row_spec = pl.BlockSpec((tr, bw), lambda i: (i, 0))
    dir_spec = pl.BlockSpec((2, tr, bw), lambda i: (0, i, 0))
    f32_rows = jax.ShapeDtypeStruct((n_rows, bw), F32)
    f32_dirs = jax.ShapeDtypeStruct((2, n_rows, bw), F32)
    vmem = 2 * (7 * tr * bw * 4 + 2 * HALO * P_COLS * 4 + 12 * tr * bw * 4) + 30 * tr * bw * 4
    return pl.pallas_call(
        functools.partial(_prep_kernel, tr=tr, n_x=n_x, t_len=t_len, c_len=c_len),
        grid=(n_rows // tr,),
        in_specs=main + param_specs,
        out_specs=[row_spec, row_spec, row_spec, dir_spec, dir_spec, dir_spec, row_spec, row_spec, row_spec],
        out_shape=[f32_rows, f32_rows, f32_rows, f32_dirs, f32_dirs, f32_dirs, f32_rows, f32_rows,
                   jax.ShapeDtypeStruct((n_rows, bw), BF16)],
        compiler_params=_cparams(("parallel",), vmem),
        name="mixer_prep",
    )(p, p, p, p, p, p, p, p, p, *params)


def _rwkv_kernel(r0_ref, kk0_ref, v0_ref, r1_ref, kk1_ref, v1_ref, lw0_ref, kd0_ref, ka0_ref,
                 lw1_ref, kd1_ref, ka1_ref, o0_ref, o1_ref, s_ref):
    C = RWKV_CHUNK
    W = 2 * RWKV_HEAD
    n_pairs = RWKV_HEADS // 2

    @pl.when(pl.program_id(1) == 0)
    def _():
        s_ref[...] = jnp.zeros_like(s_ref)

    mm, mm_nt = _dot, _dot_nt
    lane = lax.broadcasted_iota(jnp.int32, (C, W), 1)
    row = lax.broadcasted_iota(jnp.int32, (C, W), 0)
    first = lane < RWKV_HEAD
    rc = lax.broadcasted_iota(jnp.int32, (C, C), 0)
    cc = lax.broadcasted_iota(jnp.int32, (C, C), 1)

    def bdiag(x):
        return jnp.concatenate([jnp.where(first, x, jnp.zeros_like(x)), jnp.where(first, jnp.zeros_like(x), x)],
                               axis=0)

    def direction(sign, r_ref, kk_ref, v_ref, lw_ref, kd_ref, ka_ref):
        m_incl = jnp.where((rc - cc) * sign >= 0, 1.0, 0.0).astype(BF16)
        lw = lw_ref[0]
        cum = _dot_exact_lhs(m_incl, lw)
        tot = jnp.sum(lw, axis=0, keepdims=True)
        e_ninc = jnp.exp(-cum)
        e_rem = jnp.exp(tot - cum)
        kk, ka, kd = kk_ref[...], ka_ref[0], kd_ref[0]
        order = (row - (lane % RWKV_HEAD)) * sign
        return dict(a_t=kk * jnp.exp(cum - lw), b_t=-ka * e_ninc, k_t=kd * e_ninc, r_t=r_ref[...] * jnp.exp(cum),
                    b_h=-ka * e_rem, k_h=kd * e_rem, v=v_ref[...], e_tot=jnp.exp(tot), strict=order > 0,
                    incl=order >= 0, eye=jnp.where(order == 0, 1.0, 0.0).astype(F32))

    dirs = (direction(1, r0_ref, kk0_ref, v0_ref, lw0_ref, kd0_ref, ka0_ref),
            direction(-1, r1_ref, kk1_ref, v1_ref, lw1_ref, kd1_ref, ka1_ref))
    o_refs = (o0_ref, o1_ref)

    chains = [(d, p) for d in range(2) for p in range(n_pairs)]
    n = range(len(chains))
    sl = [slice(W * p, W * (p + 1)) for (_, p) in chains]
    dd = [dirs[d] for (d, _) in chains]
    ar = [jnp.concatenate([dd[c]["a_t"][:, sl[c]], dd[c]["r_t"][:, sl[c]]], axis=0).astype(BF16) for c in n]
    a_bd = [bdiag(dd[c]["a_t"][:, sl[c]].astype(BF16)) for c in n]
    bk_bd = [jnp.concatenate([bdiag(dd[c]["b_t"][:, sl[c]].astype(BF16)), bdiag(dd[c]["k_t"][:, sl[c]].astype(BF16))],
                             axis=0) for c in n]
    v_bd = [bdiag(dd[c]["v"][:, sl[c]].astype(BF16)) for c in n]
    g = [mm_nt(ar[c], bk_bd[c]) for c in n]
    l_pow = [jnp.where(dd[c]["strict"], g[c][:C, :W], 0.0) for c in n]
    m_ak = [jnp.where(dd[c]["strict"], g[c][:C, W:], 0.0).astype(BF16) for c in n]
    a_r = [jnp.where(jnp.concatenate([dd[c]["incl"], dd[c]["incl"]], axis=1), g[c][C:], 0.0).astype(BF16) for c in n]
    mv = [mm(m_ak[c], v_bd[c]) for c in n]
    l_bd = [bdiag(l_pow[c].astype(BF16)) for c in n]
    t_inv = [dd[c]["eye"] + l_pow[c] for c in n]
    l_pow = [mm(l_pow[c], l_bd[c]) for c in n]
    for _ in range(int(math.log2(C)) - 2):
        l_bd = [bdiag(l_pow[c].astype(BF16)) for c in n]
        lt = [mm(jnp.concatenate([l_pow[c], t_inv[c]], axis=0), l_bd[c]) for c in n]
        l_pow = [lt[c][:C] for c in n]
        t_inv = [t_inv[c] + lt[c][C:] for c in n]
    t_inv = [t_inv[c] + mm(t_inv[c], bdiag(l_pow[c].astype(BF16))) for c in n]
    wu = [mm(t_inv[c], jnp.concatenate([a_bd[c], bdiag(mv[c].astype(BF16))], axis=1)) for c in n]
    s_bd = [s_ref[d, p] for (d, p) in chains]
    wr = [mm(jnp.concatenate([wu[c][:, :W], dd[c]["r_t"][:, sl[c]]], axis=0), s_bd[c]) for c in n]
    zv = [jnp.concatenate([bdiag((wr[c][:C] + wu[c][:, W:]).astype(BF16)), v_bd[c]], axis=0) for c in n]
    for c, (d, p) in enumerate(chains):
        o_refs[d][:, sl[c]] = wr[c][C:] + mm(a_r[c], zv[c])
    for c, (d, p) in enumerate(chains):
        bk_h = jnp.concatenate([bdiag(dd[c]["b_h"][:, sl[c]]), bdiag(dd[c]["k_h"][:, sl[c]])], axis=0)
        tot_col = jnp.broadcast_to(dd[c]["e_tot"][:, sl[c]], (W, W)).T
        s_ref[d, p] = tot_col * s_bd[c] + mm(bk_h.T, zv[c])


def _rwkv_scan(r, kk, v, lw, kd, ka, *, n_batch, t_len, c_len):
    C = RWKV_CHUNK
    rows, width = r.shape
    nct, nxt = c_len // C, t_len // C
    ctx_blk0 = n_batch * t_len // C

    def blk(b, d, s):
        j_c = s if d == 0 else nct - 1 - s
        j_x = s - nct if d == 0 else nxt - 1 - (s - nct)
        return jnp.where(s < nct, ctx_blk0 + b * nct + j_c, b * nxt + j_x)

    def shared(d):
        return pl.BlockSpec((C, width), lambda b, s: (blk(b, d, s), 0))

    def per_dir(d):
        return pl.BlockSpec((1, C, width), lambda b, s: (d, blk(b, d, s), 0))

    vmem = 2 * 14 * C * width * 4 + RWKV_HEADS * 128 * 128 * 4 + 128 * C * width * 4
    out = jax.ShapeDtypeStruct((rows, width), F32)
    return pl.pallas_call(
        _rwkv_kernel,
        grid=(n_batch, nct + nxt),
        in_specs=[shared(0), shared(0), shared(0), shared(1), shared(1), shared(1),
                  per_dir(0), per_dir(0), per_dir(0), per_dir(1), per_dir(1), per_dir(1)],
        out_specs=[shared(0), shared(1)],
        out_shape=[out, out],
        scratch_shapes=[pltpu.VMEM((2, RWKV_HEADS // 2, 128, 128), F32)],
        compiler_params=_cparams(("parallel", "arbitrary"), vmem),
        name="rwkv_scan",
    )(r, kk, v, r, kk, v, lw, kd, ka, lw, kd, ka)


def _rwkv_readout_kernel(o0_ref, o1_ref, bonus_ref, g_ref, lng_ref, lnb_ref, y_ref):
    o = o0_ref[...] + o1_ref[...]
    mean = _head_sums(o) * (1.0 / RWKV_HEAD)
    cen = o - mean
    var = _head_sums(cen * cen) * (1.0 / RWKV_HEAD)
    o_n = cen * lax.rsqrt(var + RWKV_GN_EPS) * lng_ref[...] + lnb_ref[...]
    y_ref[...] = ((o_n + bonus_ref[...]) * g_ref[...]).astype(y_ref.dtype)


def _rwkv_readout(o_dirs, bonus, g_out, ln_g, ln_b, *, tr):
    n_rows, bw = bonus.shape
    row_spec = pl.BlockSpec((tr, bw), lambda i: (i, 0))
    const = pl.BlockSpec((1, bw), lambda i: (0, 0))
    return pl.pallas_call(
        _rwkv_readout_kernel,
        grid=(n_rows // tr,),
        in_specs=[row_spec, row_spec, row_spec, row_spec, const, const],
        out_specs=row_spec,
        out_shape=jax.ShapeDtypeStruct((n_rows, bw), BF16),
        compiler_params=_cparams(("parallel",), 2 * 5 * tr * bw * 4 + 16 * tr * bw * 4),
        name="rwkv_readout",
    )(o_dirs[0], o_dirs[1], bonus, g_out, ln_g.reshape(1, bw), ln_b.reshape(1, bw))


def _rmsnorm_kernel(x_ref, g_ref, o_ref):
    x = x_ref[...]
    o_ref[...] = x * lax.rsqrt(jnp.mean(x * x, axis=-1, keepdims=True) + NORM_EPS) * g_ref[...]


def _final_norm(xs, g, *, rows, tm):
    dm = xs.shape[1]
    return pl.pallas_call(
        _rmsnorm_kernel,
        grid=(rows // tm,),
        in_specs=[pl.BlockSpec((tm, dm), lambda i: (i, 0)), pl.BlockSpec((1, dm), lambda i: (0, 0))],
        out_specs=pl.BlockSpec((tm, dm), lambda i: (i, 0)),
        out_shape=jax.ShapeDtypeStruct((rows, dm), F32),
        compiler_params=_cparams(("parallel",), 4 * tm * dm * 4),
        name="final_norm",
    )(xs, g.reshape(1, dm))


def _rope_tables128(n_tokens):
    rows = n_tokens // GRID_W
    row = jnp.repeat(jnp.arange(rows, dtype=F32), GRID_W)
    col = jnp.tile(jnp.arange(GRID_W, dtype=F32), rows)
    n_freq = 64 // 4
    inv = ROPE_BASE ** (-jnp.arange(n_freq, dtype=F32) / n_freq)
    ang = jnp.concatenate([row[:, None] * inv, col[:, None] * inv], axis=-1)
    cos, sin = jnp.cos(ang), jnp.sin(ang)
    return jnp.concatenate([cos, cos, cos, cos], axis=-1), jnp.concatenate([-sin, sin, -sin, sin], axis=-1)


def _block_diag2(w2):
    z = jnp.zeros_like(w2[0])
    return jnp.concatenate([jnp.concatenate([w2[0], z], axis=1), jnp.concatenate([z, w2[1]], axis=1)], axis=0)


def kernel(x, c, ctx, c_ctx, norm1_g, norm2_g, mod_down, mod_up, mod_b, w_in, mla_q_norm_g, mla_w_uq,
           mla_kv_norm_g, mla_w_ukv, rwkv_mu, rwkv_w0, rwkv_w2, rwkv_a0, rwkv_a2, rwkv_g2, rwkv_k_k,
           rwkv_k_a, rwkv_r_k, rwkv_ln_g, rwkv_ln_b, conv_w, diff_lambda, diff_norm_g, w_branch, gate_down,
           gate_up, gate_b, w_out, mlp_w1, mlp_w2, final_norm_g):
    n_batch, t_len, dm = x.shape
    c_len = ctx.shape[1]
    depth = w_in.shape[0]
    bw = BRANCH_W
    n_x = n_batch * t_len
    n_c = n_batch * c_len
    n_rows = n_x + n_c
    tm = 512 if (t_len % 512 == 0 and n_c % 512 == 0) else 256
    tq = 512 if t_len % 512 == 0 else min(256, c_len)
    tr = min(256, c_len)
    assert dm == D_MODEL and t_len % tm == 0 and n_c % tm == 0 and n_x % c_len == 0
    assert t_len % tq == 0 and c_len % RWKV_CHUNK == 0 and t_len % c_len == 0
    tm_big = 1024 if (t_len % 1024 == 0 and n_c % 1024 == 0) else tm
    mod_index = functools.partial(_mod_index, tm=tm, n_x_rows=n_x, t_len=t_len, n_batch=n_batch)
    mod_index_big = functools.partial(_mod_index, tm=tm_big, n_x_rows=n_x, t_len=t_len, n_batch=n_batch)
    mod_index_nm = functools.partial(_mod_index, tm=256, n_x_rows=n_x, t_len=t_len, n_batch=n_batch)

    zeros = lambda *s: jnp.zeros(s, F32)
    lora_w = 4 * RWKV_LORA + RWKV_GATE_LORA
    rw0 = MLA_Q_LORA + MLA_KV_LORA + MLA_ROPE
    cv0 = rw0 + 3 * bw + lora_w
    w_in_p = jnp.concatenate(
        [w_in[:, :, :C_R], w_in[:, :, rw0:rw0 + 3 * bw], w_in[:, :, cv0:],
         w_in[:, :, rw0 + 3 * bw:cv0], zeros(depth, dm, RWKV_LORA_PAD - lora_w), gate_down,
         w_in[:, :, C_R:rw0], zeros(depth, dm, P_COLS - C_KROPE - MLA_ROPE)], axis=-1).astype(BF16)
    w_uq_p = jnp.pad(mla_w_uq.reshape(depth, MLA_Q_LORA, MLA_HEADS, MLA_NOPE + MLA_ROPE),
                     ((0, 0), (0, 0), (0, 0), (0, MLA_QK_PAD - MLA_NOPE - MLA_ROPE))
                     ).reshape(depth, MLA_Q_LORA, MLA_HEADS * MLA_QK_PAD).astype(BF16)
    w_ukv_r = mla_w_ukv.reshape(depth, MLA_KV_LORA, MLA_HEADS, MLA_NOPE + MLA_V)
    w_ukv_p = jnp.concatenate([w_ukv_r[..., :MLA_NOPE].reshape(depth, MLA_KV_LORA, -1),
                               w_ukv_r[..., MLA_NOPE:].reshape(depth, MLA_KV_LORA, -1)], axis=-1).astype(BF16)
    w_branch_2d = w_branch.reshape(depth, 4 * bw, dm)
    gate_up_b = jnp.moveaxis(gate_up, 2, 1).astype(BF16)
    g2_p = jnp.pad(rwkv_g2, ((0, 0), (0, RWKV_LORA_PAD - 4 * RWKV_LORA - RWKV_GATE_LORA), (0, 0))).astype(BF16)
    mu_lora = jnp.pad(rwkv_mu[:, :, 3 * bw:], ((0, 0), (0, 0), (0, RWKV_LORA_PAD - lora_w)))

    cond = jnp.concatenate([c, c_ctx[None, :], zeros(16 - n_batch - 1, dm)], axis=0)
    cond = jax.nn.silu(cond)
    mods = []
    for l in range(depth):
        low = _matmul(cond, mod_down[l], tm=16, tn=MOD_RANK, tk=dm, name="mod_down")
        up = _matmul(low, mod_up[l], tm=16, tn=2048, tk=MOD_RANK, bias=mod_b[l], name="mod_up")
        mods.append(up.reshape(16, N_MOD, dm))

    cos_t, sin_t = _rope_tables128(t_len)
    xs = jnp.concatenate([x.reshape(n_x, dm), ctx.reshape(n_c, dm)], axis=0)

    for l in range(depth):
        need_ctx = l < depth - 1
        mod = mods[l]
        lam_init = 0.8 - 0.6 * math.exp(-0.3 * l)
        lq1, lk1, lq2, lk2 = diff_lambda[l]
        lam = jnp.exp(jnp.sum(lq1 * lk1)) - jnp.exp(jnp.sum(lq2 * lk2)) + lam_init
        lam_row = jnp.full((1, DIFF_V), 1.0, F32) * lam

        h1 = _norm_mod(xs, norm1_g[l], mod, shift_row=0, scale_row=1, mod_index=mod_index_nm, tm=256,
                       rows=n_rows)
        p, w1_b, w_out_b, w_branch_b = _fullk_matmul(h1, w_in_p, layer=l, tm=tm_big, tn=512, out_dtype=F32,
                                                     name="in_proj", cast_srcs=(mlp_w1, w_out, w_branch_2d))

        q = _nm_matmul(p, mla_q_norm_g[l], w_uq_p[l], tm=tm, tn=512, out_dtype=BF16,
                       x_col_block=C_CQ // MLA_Q_LORA, kdim=MLA_Q_LORA, name="mla_q")
        kv = _nm_matmul(p, mla_kv_norm_g[l], w_ukv_p[l], tm=tm, tn=512, out_dtype=BF16,
                        x_col_block=C_CKV // MLA_KV_LORA, kdim=MLA_KV_LORA, name="mla_kv")
        y_mla = _mla_attention(q, kv, p, cos_t, sin_t, n_batch=n_batch, t_len=t_len, c_len=c_len, tq=tq,
                               with_ctx=need_ctx)

        y_diff = _diff_attention(p, cos_t, sin_t, lam_row, diff_norm_g[l].reshape(1, DIFF_V),
                                 n_batch=n_batch, t_len=t_len, c_len=c_len, tq=tq, with_ctx=need_ctx,
                                 out_scale=1.0 - lam_init)

        r_, kk, v_, lw, k_dir, kka, bonus, g_out, y_conv = _mixer_prep(
            p, rwkv_mu[l, :, :3 * bw], mu_lora[l], rwkv_k_k[l].reshape(1, bw), rwkv_k_a[l].reshape(1, bw),
            rwkv_r_k[l].reshape(1, bw), rwkv_w0[l].reshape(1, 2 * bw), rwkv_a0[l].reshape(1, 2 * bw),
            _block_diag2(rwkv_w2[l]).astype(BF16), _block_diag2(rwkv_a2[l]).astype(BF16), g2_p[l],
            conv_w[l], tr=min(128, tr), n_x=n_x, t_len=t_len, c_len=c_len)
        o_dirs = _rwkv_scan(r_, kk, v_, lw, k_dir, kka, n_batch=n_batch, t_len=t_len, c_len=c_len)
        y_rwkv = _rwkv_readout(o_dirs, bonus, g_out, rwkv_ln_g[l], rwkv_ln_b[l], tr=tr)

        rows = n_rows if need_ctx else n_x
        acc = _merge(p, (y_mla, y_rwkv, y_conv, y_diff), w_branch_b, gate_up_b, gate_b, layer=l,
                     tm=tm_big, tn=512, rows=rows)
        (xs_new,) = _fullk_matmul(acc, w_out_b[None], layer=0, tm=tm_big, tn=512, out_dtype=F32, res=xs, mod=mod,
                                  gate_row=2, mod_index=mod_index_big, rows=rows, name="out_proj")

        h2 = _norm_mod(xs_new, norm2_g[l], mod, shift_row=3, scale_row=4, mod_index=mod_index_nm, tm=256,
                       rows=rows)
        hid, w2_b = _fullk_matmul(h2, w1_b[None], layer=0, tm=tm_big, tn=TN_UP, out_dtype=BF16, act="relu2",
                                  name="mlp_up", cast_srcs=(mlp_w2,), cast_layer=l)
        xs = _matmul(hid, w2_b, tm=tm_big, tn=1024, tk=2048, res=xs_new, mod=mod, gate_row=5,
                     mod_index=mod_index_big, rows=rows, name="mlp_down")

    out = _final_norm(xs, final_norm_g, rows=n_x, tm=tm)
    return out.reshape(n_batch, t_len, dm)
```

```python
import functools
import math

import jax
import jax.numpy as jnp
from jax import lax
from jax.experimental import pallas as pl
from jax.experimental.pallas import tpu as pltpu

F32 = jnp.float32
BF16 = jnp.bfloat16

D_MODEL = 4096
BRANCH_W = 1024
GRID_W = 64
ROPE_BASE = 10000.0
NORM_EPS = 1e-6
N_MOD = 6
LANES = 128
HALO = 8
LOG2_E = 1.4426950408889634

MLA_HEADS = 8
MLA_NOPE = 128
MLA_ROPE = 64
MLA_V = 128
MLA_Q_LORA = 768
MLA_KV_LORA = 256
MLA_QK_PAD = 256

RWKV_HEAD = 64
RWKV_HEADS = 16
RWKV_LORA = 64
RWKV_GATE_LORA = 160
RWKV_GN_EPS = 64e-5
RWKV_CHUNK = 64
RWKV_LORA_PAD = 512

DIFF_HEADS = 8
DIFF_QK = 64
DIFF_V = 128
GATE_RANK = 256
MOD_RANK = 256

C_CQ = 0
C_CKV = 768
C_R = 1024
C_K = 2048
C_V = 3072
C_CB = 4096
C_CC = 5120
C_CU = 6144
C_DQ = 7168
C_DK = 8192
C_DV = 9216
C_LORA = 10240
C_GL = 10752
C_KROPE = 11008
P_COLS = 11264

TN_UP = 1024
ATTN_SUB = 128
VMEM_CAP = 56 * 1024 * 1024


def _cparams(sem, vmem_bytes):
    limit = int(min(VMEM_CAP, max(vmem_bytes * 1.5 + (4 << 20), 16 << 20)))
    return pltpu.CompilerParams(dimension_semantics=sem, vmem_limit_bytes=limit)


def _mod_index(i, tm, n_x_rows, t_len, n_batch):
    return jnp.where(i < n_x_rows // tm, i // (t_len // tm), n_batch)


def _dot(a, b):
    return jnp.dot(a.astype(BF16), b.astype(BF16), preferred_element_type=F32)


def _dot_nt(a, b):
    return lax.dot_general(a.astype(BF16), b.astype(BF16), (((1,), (1,)), ((), ())),
                           preferred_element_type=F32)


def _split3(x):
    hi = x.astype(BF16)
    r1 = x - hi.astype(F32)
    mid = r1.astype(BF16)
    lo = (r1 - mid.astype(F32)).astype(BF16)
    return hi, mid, lo


def _dot_exact_lhs(m_bf16, x):
    out = None
    for part in _split3(x):
        t = jnp.dot(m_bf16, part, preferred_element_type=F32)
        out = t if out is None else out + t
    return out


def _head_sums(x):
    r = lax.broadcasted_iota(jnp.int32, (LANES, LANES), 0)
    c = lax.broadcasted_iota(jnp.int32, (LANES, LANES), 1)
    ones_bd = jnp.where((r // RWKV_HEAD) == (c // RWKV_HEAD), 1.0, 0.0).astype(BF16)
    parts = _split3(x)
    cols = []
    for j in range(x.shape[1] // LANES):
        acc = None
        for part in parts:
            t = jnp.dot(part[:, j * LANES:(j + 1) * LANES], ones_bd, preferred_element_type=F32)
            acc = t if acc is None else acc + t
        cols.append(acc)
    return jnp.concatenate(cols, axis=1)


def _rope128(x, cos_t, sin_t):
    lane = lax.broadcasted_iota(jnp.int32, x.shape, 1)
    swapped = jnp.where((lane % 64) < 32, pltpu.roll(x, 96, 1), pltpu.roll(x, 32, 1))
    return x * cos_t + swapped * sin_t


def _matmul_kernel(*refs, nk, has_bias, has_res, gate_row, act):
    a_ref, w_ref = refs[0], refs[1]
    pos = 2
    bias_ref = res_ref = mod_ref = None
    if has_bias:
        bias_ref = refs[pos]; pos += 1
    if has_res:
        res_ref = refs[pos]; mod_ref = refs[pos + 1]; pos += 2
    o_ref, acc_ref = refs[pos], refs[pos + 1]
    k = pl.program_id(2)

    @pl.when(k == 0)
    def _():
        acc_ref[...] = jnp.zeros_like(acc_ref)

    acc_ref[...] += _dot(a_ref[...], w_ref[...])

    @pl.when(k == nk - 1)
    def _():
        y = acc_ref[...]
        if has_bias:
            y = y + bias_ref[...]
        if act == "relu2":
            y = jnp.square(jnp.maximum(y, 0.0))
        if has_res:
            y = res_ref[...] + mod_ref[0, gate_row:gate_row + 1, :] * y
        o_ref[...] = y.astype(o_ref.dtype)


def _matmul(a, w, *, tm, tn, tk, out_dtype=F32, bias=None, res=None, mod=None, gate_row=None,
            mod_index=None, act=None, rows=None, name=None):
    m = a.shape[0] if rows is None else rows
    kdim, n = w.shape
    assert a.shape[1] == kdim and m % tm == 0 and n % tn == 0 and kdim % tk == 0
    nk = kdim // tk
    in_specs = [pl.BlockSpec((tm, tk), lambda i, j, k: (i, k)),
                pl.BlockSpec((tk, tn), lambda i, j, k: (k, j))]
    args = [a, w]
    vmem = 2 * tm * tk * a.dtype.itemsize + 2 * tk * tn * w.dtype.itemsize + tm * tn * 4
    vmem += 2 * tm * tn * jnp.dtype(out_dtype).itemsize
    if bias is not None:
        in_specs.append(pl.BlockSpec((1, tn), lambda i, j, k: (0, j)))
        args.append(bias.reshape(1, n).astype(F32))
    if res is not None:
        in_specs.append(pl.BlockSpec((tm, tn), lambda i, j, k: (i, j)))
        in_specs.append(pl.BlockSpec((1, N_MOD, tn), lambda i, j, k: (mod_index(i), 0, j)))
        args += [res, mod]
        vmem += 2 * tm * tn * 4 + 2 * 8 * tn * 4
    kern = functools.partial(_matmul_kernel, nk=nk, has_bias=bias is not None,
                             has_res=res is not None, gate_row=gate_row, act=act)
    return pl.pallas_call(
        kern,
        grid=(m // tm, n // tn, nk),
        in_specs=in_specs,
        out_specs=pl.BlockSpec((tm, tn), lambda i, j, k: (i, j)),
        out_shape=jax.ShapeDtypeStruct((m, n), out_dtype),
        scratch_shapes=[pltpu.VMEM((tm, tn), F32)],
        compiler_params=_cparams(("parallel", "parallel", "arbitrary"), vmem),
        name=name,
    )(*args)


def _nm_matmul_kernel(*refs, has_mod, shift_row, scale_row, act):
    if has_mod:
        x_ref, g_ref, mod_ref, w_ref, o_ref, h_ref = refs
    else:
        x_ref, g_ref, w_ref, o_ref, h_ref = refs

    @pl.when(pl.program_id(1) == 0)
    def _():
        x = x_ref[...].astype(F32)
        y = x * lax.rsqrt(jnp.mean(x * x, axis=-1, keepdims=True) + NORM_EPS) * g_ref[...]
        if has_mod:
            y = y * (1.0 + mod_ref[0, scale_row:scale_row + 1, :]) + mod_ref[0, shift_row:shift_row + 1, :]
        h_ref[...] = y.astype(BF16)

    y = jnp.dot(h_ref[...], w_ref[...].astype(BF16), preferred_element_type=F32)
    if act == "relu2":
        y = jnp.square(jnp.maximum(y, 0.0))
    o_ref[...] = y.astype(o_ref.dtype)


def _nm_matmul(x, g, w, *, tm, tn, out_dtype, x_col_block=0, kdim=None, mod=None, shift_row=None,
               scale_row=None, mod_index=None, act=None, rows=None, name=None):
    m = x.shape[0] if rows is None else rows
    kdim = x.shape[1] if kdim is None else kdim
    n = w.shape[1]
    assert w.shape[0] == kdim and m % tm == 0 and n % tn == 0
    in_specs = [pl.BlockSpec((tm, kdim), lambda i, j: (i, x_col_block)),
                pl.BlockSpec((1, kdim), lambda i, j: (0, 0))]
    args = [x, g.reshape(1, kdim).astype(F32)]
    if mod is not None:
        in_specs.append(pl.BlockSpec((1, N_MOD, kdim), lambda i, j: (mod_index(i), 0, 0)))
        args.append(mod)
    in_specs.append(pl.BlockSpec((kdim, tn), lambda i, j: (0, j)))
    args.append(w)
    vmem = (2 * tm * kdim * x.dtype.itemsize + tm * kdim * 2 + 2 * kdim * tn * w.dtype.itemsize
            + 2 * tm * tn * jnp.dtype(out_dtype).itemsize + tm * tn * 4 + 4 * 8 * kdim * 4)
    kern = functools.partial(_nm_matmul_kernel, has_mod=mod is not None, shift_row=shift_row,
                             scale_row=scale_row, act=act)
    return pl.pallas_call(
        kern,
        grid=(m // tm, n // tn),
        in_specs=in_specs,
        out_specs=pl.BlockSpec((tm, tn), lambda i, j: (i, j)),
        out_shape=jax.ShapeDtypeStruct((m, n), out_dtype),
        scratch_shapes=[pltpu.VMEM((tm, kdim), BF16)],
        compiler_params=_cparams(("parallel", "arbitrary"), vmem),
        name=name,
    )(*args)


def _norm_mod_kernel(x_ref, g_ref, mod_ref, h_ref, *, shift_row, scale_row):
    x = x_ref[...]
    y = x * lax.rsqrt(jnp.mean(x * x, axis=-1, keepdims=True) + NORM_EPS) * g_ref[...]
    y = y * (1.0 + mod_ref[0, scale_row:scale_row + 1, :]) + mod_ref[0, shift_row:shift_row + 1, :]
    h_ref[...] = y.astype(h_ref.dtype)


def _norm_mod(xs, g, mod, *, shift_row, scale_row, mod_index, tm, rows):
    dm = xs.shape[1]
    return pl.pallas_call(
        functools.partial(_norm_mod_kernel, shift_row=shift_row, scale_row=scale_row),
        grid=(rows // tm,),
        in_specs=[pl.BlockSpec((tm, dm), lambda i: (i, 0)), pl.BlockSpec((1, dm), lambda i: (0, 0)),
                  pl.BlockSpec((1, N_MOD, dm), lambda i: (mod_index(i), 0, 0))],
        out_specs=pl.BlockSpec((tm, dm), lambda i: (i, 0)),
        out_shape=jax.ShapeDtypeStruct((rows, dm), BF16),
        compiler_params=_cparams(("parallel",), 2 * tm * dm * 6 + 4 * tm * dm * 4),
        name="norm_mod",
    )(xs, g.reshape(1, dm), mod)


def _fullk_kernel(*refs, has_res, n_cast, gate_row, act):
    refs = list(refs)
    if n_cast:
        cast_dsts = refs[-n_cast:]
        cast_srcs = refs[-2 * n_cast - 1:-n_cast - 1]
        refs = refs[:-2 * n_cast - 1] + [refs[-n_cast - 1]]
        for src, dst in zip(cast_srcs, cast_dsts):
            dst[...] = src[...].astype(dst.dtype)
    if has_res:
        a_ref, w_ref, res_ref, mod_ref, o_ref = refs
    else:
        a_ref, w_ref, o_ref = refs
    y = jnp.dot(a_ref[...], w_ref[...], preferred_element_type=F32)
    if act == "relu2":
        y = jnp.square(jnp.maximum(y, 0.0))
    if has_res:
        y = res_ref[...] + mod_ref[0, gate_row:gate_row + 1, :] * y
    o_ref[...] = y.astype(o_ref.dtype)


def _cast_rows_per_step(n_rows, n_steps):
    rows = 16
    while n_rows % rows or n_rows // rows > n_steps:
        rows *= 2
    return rows


def _fullk_matmul(a, w, *, tm, tn, out_dtype, act=None, res=None, mod=None, gate_row=None, mod_index=None,
                  rows=None, name=None, layer=None, cast_srcs=(), cast_layer=None):
    m = a.shape[0] if rows is None else rows
    _, kdim, n = w.shape
    assert a.shape[1] == kdim and m % tm == 0 and n % tn == 0 and a.dtype == BF16 and w.dtype == BF16
    in_specs = [pl.BlockSpec((tm, kdim), lambda i, j: (i, 0)),
                pl.BlockSpec((None, kdim, tn), lambda i, j: (layer, 0, j))]
    args = [a, w]
    vmem = 2 * tm * kdim * 2 + 2 * kdim * tn * 2 + 2 * tm * tn * jnp.dtype(out_dtype).itemsize + 2 * tm * tn * 4
    if res is not None:
        in_specs += [pl.BlockSpec((tm, tn), lambda i, j: (i, j)),
                     pl.BlockSpec((1, N_MOD, tn), lambda i, j: (mod_index(i), 0, j))]
        args += [res, mod]
        vmem += 2 * tm * tn * 4
    out_specs = [pl.BlockSpec((tm, tn), lambda i, j: (i, j))]
    out_shape = [jax.ShapeDtypeStruct((m, n), out_dtype)]
    nj = n // tn
    cast_layer = layer if cast_layer is None else cast_layer
    for cast_src in cast_srcs:
        _, c_rows, c_cols = cast_src.shape
        cr = _cast_rows_per_step(c_rows, (m // tm) * nj)
        cast_blk = lambda i, j, last=c_rows // cr - 1: jnp.minimum(i * nj + j, last)
        in_specs.append(pl.BlockSpec((None, cr, c_cols), lambda i, j, blk=cast_blk: (cast_layer, blk(i, j), 0)))
        args.append(cast_src)
        out_specs.append(pl.BlockSpec((cr, c_cols), lambda i, j, blk=cast_blk: (blk(i, j), 0)))
        out_shape.append(jax.ShapeDtypeStruct((c_rows, c_cols), BF16))
        vmem += 2 * cr * c_cols * 6
    return pl.pallas_call(
        functools.partial(_fullk_kernel, has_res=res is not None, n_cast=len(cast_srcs),
                          gate_row=gate_row, act=act),
        grid=(m // tm, n // tn),
        in_specs=in_specs,
        out_specs=out_specs,
        out_shape=out_shape,
        compiler_params=_cparams(("arbitrary", "arbitrary"), vmem),
        name=name,
    )(*args)


def _merge_kernel(gl_ref, y0_ref, y1_ref, y2_ref, y3_ref, wb_ref, gu_ref, gb_ref, o_ref):
    gl = gl_ref[...].astype(BF16)
    bw = wb_ref.shape[0] // 4
    acc = None
    for i, y_ref in enumerate((y0_ref, y1_ref, y2_ref, y3_ref)):
        gate = jax.nn.sigmoid(jnp.dot(gl, gu_ref[i], preferred_element_type=F32) + gb_ref[i:i + 1, :])
        term = gate * jnp.dot(y_ref[...], wb_ref[i * bw:(i + 1) * bw, :], preferred_element_type=F32)
        acc = term if acc is None else acc + term
    o_ref[...] = acc.astype(o_ref.dtype)


def _merge(p, ys, wb, gu, gb, *, layer, tm, tn, rows):
    n = wb.shape[1]
    bw = wb.shape[0] // 4
    gr = gu.shape[2]
    y_spec = pl.BlockSpec((tm, bw), lambda i, j: (i, 0))
    vmem = (2 * tm * gr * 4 + 4 * 2 * tm * bw * 2 + 2 * 4 * bw * tn * 2 + 2 * 4 * gr * tn * 2
            + 2 * tm * tn * 2 + 3 * tm * tn * 4)
    return pl.pallas_call(
        _merge_kernel,
        grid=(rows // tm, n // tn),
        in_specs=[pl.BlockSpec((tm, gr), lambda i, j: (i, C_GL // GATE_RANK)),
                  y_spec, y_spec, y_spec, y_spec,
                  pl.BlockSpec((4 * bw, tn), lambda i, j: (0, j)),
                  pl.BlockSpec((None, 4, gr, tn), lambda i, j: (layer, 0, 0, j)),
                  pl.BlockSpec((None, 4, tn), lambda i, j: (layer, 0, j))],
        out_specs=pl.BlockSpec((tm, tn), lambda i, j: (i, j)),
        out_shape=jax.ShapeDtypeStruct((rows, n), BF16),
        compiler_params=_cparams(("parallel", "arbitrary"), vmem),
        name="merge",
    )(p, *ys, wb, gu, gb)


def _softmax_parts(s_list, scale):
    m = None
    for s in s_list:
        sm = jnp.max(s, axis=-1, keepdims=True)
        m = sm if m is None else jnp.maximum(m, sm)
    e_list = [jnp.exp2((s - m) * (scale * LOG2_E)) for s in s_list]
    l = None
    for e in e_list:
        es = jnp.sum(e, axis=-1, keepdims=True)
        l = es if l is None else l + es
    return e_list, l


def _attend(q, ks, vs, scale):
    e_list, l = _softmax_parts([_dot_nt(q, k) for k in ks], scale)
    o = None
    for e, v in zip(e_list, vs):
        t = jnp.dot(e.astype(BF16), v, preferred_element_type=F32)
        o = t if o is None else o + t
    return o / l


def _attend_many(qs, ks, vs, scale, sub):
    parts = [q[r:r + sub] for q in qs for r in range(0, q.shape[0], sub)]
    n = len(parts)
    scores, soft, outs = {}, {}, []
    for t in range(n + 2):
        if t < n:
            scores[t] = [_dot_nt(parts[t], k) for k in ks]
        if 0 <= t - 1 < n:
            soft[t - 1] = _softmax_parts(scores.pop(t - 1), scale)
        if 0 <= t - 2 < n:
            e_list, l = soft.pop(t - 2)
            o = None
            for e, v in zip(e_list, vs):
                pv = jnp.dot(e.astype(BF16), v, preferred_element_type=F32)
                o = pv if o is None else o + pv
            outs.append(o / l)
    per_q = len(parts) // len(qs)
    return [jnp.concatenate(outs[i * per_q:(i + 1) * per_q], axis=0) for i in range(len(qs))]


def _mla_kernel(cos_ref, sin_ref, q_ref, knx_ref, knc_ref, krx_ref, krc_ref, vx_ref, vc_ref, o_ref,
                kx_s, kc_s, *, tq, scale):
    i = pl.program_id(2)

    @pl.when(i == 0)
    def _():
        kx_s[:, :MLA_NOPE] = knx_ref[...]
        kx_s[:, MLA_NOPE:] = _rope128(krx_ref[...], cos_ref[...], sin_ref[...]).astype(BF16)
        kc_s[:, :MLA_NOPE] = knc_ref[...]
        kc_s[:, MLA_NOPE:] = krc_ref[...].astype(BF16)

    row0 = pl.multiple_of(i * tq, tq)
    q = q_ref[...]
    q_rope = _rope128(q[:, MLA_NOPE:].astype(F32), cos_ref[pl.ds(row0, tq), :], sin_ref[pl.ds(row0, tq), :])
    q = jnp.concatenate([q[:, :MLA_NOPE], q_rope.astype(BF16)], axis=1)
    (o,) = _attend_many((q,), (kc_s[...], kx_s[...]), (vc_ref[...], vx_ref[...]), scale, ATTN_SUB)
    o_ref[...] = o.astype(o_ref.dtype)


def _mla_ctx_kernel(q_ref, knc_ref, krc_ref, vc_ref, y_hbm, o_ref, *, scale):
    del y_hbm
    kc = jnp.concatenate([knc_ref[...], krc_ref[...].astype(BF16)], axis=1)
    o = _attend(q_ref[...], (kc,), (vc_ref[...],), scale)
    o_ref[...] = o.astype(o_ref.dtype)


def _diff_finish(o1, o2, lam_ref, g_ref, out_scale):
    o = o1 - lam_ref[...] * o2
    y = o * lax.rsqrt(jnp.mean(o * o, axis=-1, keepdims=True) + NORM_EPS) * g_ref[...]
    return y * out_scale


def _diff_halves(q):
    lane = lax.broadcasted_iota(jnp.int32, q.shape, 1)
    return jnp.where(lane < DIFF_QK, q, 0.0).astype(BF16), jnp.where(lane < DIFF_QK, 0.0, q).astype(BF16)


def _diff_kernel(cos_ref, sin_ref, lam_ref, g_ref, q_ref, kx_ref, kc_ref, vx_ref, vc_ref, o_ref,
                 kx_s, kc_s, vx_s, vc_s, *, tq, scale, out_scale):
    i = pl.program_id(2)

    @pl.when(i == 0)
    def _():
        kx_s[...] = _rope128(kx_ref[...], cos_ref[...], sin_ref[...]).astype(BF16)
        kc_s[...] = kc_ref[...].astype(BF16)
        vx_s[...] = vx_ref[...].astype(BF16)
        vc_s[...] = vc_ref[...].astype(BF16)

    row0 = pl.multiple_of(i * tq, tq)
    q1, q2 = _diff_halves(_rope128(q_ref[...], cos_ref[pl.ds(row0, tq), :], sin_ref[pl.ds(row0, tq), :]))
    ks, vs = (kc_s[...], kx_s[...]), (vc_s[...], vx_s[...])
    o1, o2 = _attend_many((q1, q2), ks, vs, scale, ATTN_SUB)
    y = _diff_finish(o1, o2, lam_ref, g_ref, out_scale)
    o_ref[...] = y.astype(o_ref.dtype)


def _diff_ctx_kernel(lam_ref, g_ref, q_ref, kc_ref, vc_ref, y_hbm, o_ref, *, scale, out_scale):
    del y_hbm
    q1, q2 = _diff_halves(q_ref[...])
    ks, vs = (kc_ref[...].astype(BF16),), (vc_ref[...].astype(BF16),)
    y = _diff_finish(_attend(q1, ks, vs, scale), _attend(q2, ks, vs, scale), lam_ref, g_ref, out_scale)
    o_ref[...] = y.astype(o_ref.dtype)


def _mla_attention(q, kv, p, cos_t, sin_t, *, n_batch, t_len, c_len, tq, with_ctx):
    n_x = n_batch * t_len
    nq = t_len // tq
    out_rows = n_x + (n_batch * c_len if with_ctx else 0)
    cblk0 = n_x // c_len
    scale = (MLA_NOPE + MLA_ROPE) ** -0.5
    qmap = lambda b, h, i: (b * nq + i, h)
    full = lambda b, h, i: (0, 0)
    vmem = (4 * t_len * LANES * 4 + 2 * tq * 256 * 2 + 2 * (t_len + c_len) * LANES * (2 + 4 + 2)
            + (t_len + c_len) * 256 * 2 + 2 * tq * LANES * 2 + 6 * tq * (t_len + c_len) * 4)
    y = pl.pallas_call(
        functools.partial(_mla_kernel, tq=tq, scale=scale),
        grid=(n_batch, MLA_HEADS, nq),
        in_specs=[pl.BlockSpec((t_len, LANES), full), pl.BlockSpec((t_len, LANES), full),
                  pl.BlockSpec((tq, MLA_QK_PAD), qmap),
                  pl.BlockSpec((t_len, MLA_NOPE), lambda b, h, i: (b, h)),
                  pl.BlockSpec((c_len, MLA_NOPE), lambda b, h, i: (cblk0 + b, h)),
                  pl.BlockSpec((t_len, LANES), lambda b, h, i: (b, C_KROPE // LANES)),
                  pl.BlockSpec((c_len, LANES), lambda b, h, i: (cblk0 + b, C_KROPE // LANES)),
                  pl.BlockSpec((t_len, MLA_V), lambda b, h, i: (b, MLA_HEADS + h)),
                  pl.BlockSpec((c_len, MLA_V), lambda b, h, i: (cblk0 + b, MLA_HEADS + h))],
        out_specs=pl.BlockSpec((tq, MLA_V), qmap),
        out_shape=jax.ShapeDtypeStruct((out_rows, MLA_HEADS * MLA_V), BF16),
        scratch_shapes=[pltpu.VMEM((t_len, MLA_QK_PAD), BF16), pltpu.VMEM((c_len, MLA_QK_PAD), BF16)],
        compiler_params=_cparams(("parallel", "parallel", "arbitrary"), vmem),
        name="mla_attention",
    )(cos_t, sin_t, q, kv, kv, p, p, kv, kv)
    if not with_ctx:
        return y
    cmap = lambda b, h: (cblk0 + b, h)
    return pl.pallas_call(
        functools.partial(_mla_ctx_kernel, scale=scale),
        grid=(n_batch, MLA_HEADS),
        in_specs=[pl.BlockSpec((c_len, MLA_QK_PAD), cmap),
                  pl.BlockSpec((c_len, MLA_NOPE), cmap),
                  pl.BlockSpec((c_len, LANES), lambda b, h: (cblk0 + b, C_KROPE // LANES)),
                  pl.BlockSpec((c_len, MLA_V), lambda b, h: (cblk0 + b, MLA_HEADS + h)),
                  pl.BlockSpec(memory_space=pl.ANY)],
        out_specs=pl.BlockSpec((c_len, MLA_V), cmap),
        out_shape=jax.ShapeDtypeStruct(y.shape, y.dtype),
        input_output_aliases={4: 0},
        compiler_params=_cparams(("parallel", "parallel"), 16 * c_len * c_len * 4 + 8 * c_len * 256 * 4),
        name="mla_attention_ctx",
    )(q, kv, p, kv, y)


def _diff_attention(p, cos_t, sin_t, lam_row, g_row, *, n_batch, t_len, c_len, tq, with_ctx, out_scale):
    n_x = n_batch * t_len
    nq = t_len // tq
    out_rows = n_x + (n_batch * c_len if with_ctx else 0)
    cblk0 = n_x // c_len
    scale = DIFF_QK ** -0.5
    full = lambda b, h, i: (0, 0)
    qblk, kblk, vblk = C_DQ // LANES, C_DK // LANES, C_DV // LANES
    vmem = (4 * t_len * LANES * 4 + 2 * tq * LANES * 4 + 4 * (t_len + c_len) * LANES * 4
            + 2 * (t_len + c_len) * LANES * 2 + 2 * tq * LANES * 2 + 8 * tq * (t_len + c_len) * 4)
    y = pl.pallas_call(
        functools.partial(_diff_kernel, tq=tq, scale=scale, out_scale=out_scale),
        grid=(n_batch, DIFF_HEADS, nq),
        in_specs=[pl.BlockSpec((t_len, LANES), full), pl.BlockSpec((t_len, LANES), full),
                  pl.BlockSpec((1, DIFF_V), full), pl.BlockSpec((1, DIFF_V), full),
                  pl.BlockSpec((tq, LANES), lambda b, h, i: (b * nq + i, qblk + h)),
                  pl.BlockSpec((t_len, LANES), lambda b, h, i: (b, kblk + h)),
                  pl.BlockSpec((c_len, LANES), lambda b, h, i: (cblk0 + b, kblk + h)),
                  pl.BlockSpec((t_len, LANES), lambda b, h, i: (b, vblk + h)),
                  pl.BlockSpec((c_len, LANES), lambda b, h, i: (cblk0 + b, vblk + h))],
        out_specs=pl.BlockSpec((tq, DIFF_V), lambda b, h, i: (b * nq + i, h)),
        out_shape=jax.ShapeDtypeStruct((out_rows, DIFF_HEADS * DIFF_V), BF16),
        scratch_shapes=[pltpu.VMEM((t_len, LANES), BF16), pltpu.VMEM((c_len, LANES), BF16),
                        pltpu.VMEM((t_len, LANES), BF16), pltpu.VMEM((c_len, LANES), BF16)],
        compiler_params=_cparams(("parallel", "parallel", "arbitrary"), vmem),
        name="diff_attention",
    )(cos_t, sin_t, lam_row, g_row, p, p, p, p, p)
    if not with_ctx:
        return y
    one = lambda b, h: (0, 0)
    return pl.pallas_call(
        functools.partial(_diff_ctx_kernel, scale=scale, out_scale=out_scale),
        grid=(n_batch, DIFF_HEADS),
        in_specs=[pl.BlockSpec((1, DIFF_V), one), pl.BlockSpec((1, DIFF_V), one),
                  pl.BlockSpec((c_len, LANES), lambda b, h: (cblk0 + b, qblk + h)),
                  pl.BlockSpec((c_len, LANES), lambda b, h: (cblk0 + b, kblk + h)),
                  pl.BlockSpec((c_len, LANES), lambda b, h: (cblk0 + b, vblk + h)),
                  pl.BlockSpec(memory_space=pl.ANY)],
        out_specs=pl.BlockSpec((c_len, DIFF_V), lambda b, h: (cblk0 + b, h)),
        out_shape=jax.ShapeDtypeStruct(y.shape, y.dtype),
        input_output_aliases={5: 0},
        compiler_params=_cparams(("parallel", "parallel"), 24 * c_len * c_len * 4 + 8 * c_len * LANES * 4),
        name="diff_attention_ctx",
    )(lam_row, g_row, p, p, p, y)


def _prep_kernel(r_ref, k_ref, v_ref, cb_ref, cc_ref, cu_ref, lo_ref, hp_ref, hn_ref,
                 mu_ref, mul_ref, kk_w_ref, ka_w_ref, rk_w_ref, w0_ref, a0_ref, w2_ref, a2_ref, g2_ref,
                 cw_ref,
                 r_o, kk_o, v_o, lw_o, kd_o, ka_o, bonus_o, g_o, conv_o, *, tr, n_x, t_len, c_len):
    i = pl.program_id(0)
    g0 = i * tr
    seq = jnp.where(g0 < n_x, t_len, c_len)
    has_prev = ((g0 % seq) != 0).astype(F32)
    has_next = (((g0 + tr) % seq) != 0).astype(F32)
    row = lax.broadcasted_iota(jnp.int32, (tr, 1), 0)

    def neighbours(x, col0):
        width = x.shape[1]
        before = hp_ref[HALO - 1:HALO, col0:col0 + width] * has_prev
        after = hn_ref[0:1, col0:col0 + width] * has_next
        prev = jnp.where(row == 0, before, pltpu.roll(x, 1, 0))
        nxt = jnp.where(row == tr - 1, after, pltpu.roll(x, tr - 1, 0))
        return prev, nxt

    def shifted(x, col0, mu0, mu1):
        prev, nxt = neighbours(x, col0)
        return x + mu0 * (prev - x) + mu1 * (nxt - x)

    r = shifted(r_ref[...], C_R, mu_ref[0:1, :BRANCH_W], mu_ref[1:2, :BRANCH_W])
    k = shifted(k_ref[...], C_K, mu_ref[0:1, BRANCH_W:2 * BRANCH_W], mu_ref[1:2, BRANCH_W:2 * BRANCH_W])
    v = shifted(v_ref[...], C_V, mu_ref[0:1, 2 * BRANCH_W:], mu_ref[1:2, 2 * BRANCH_W:])
    lo = shifted(lo_ref[...], C_LORA, mul_ref[0:1, :], mul_ref[1:2, :])
    wd = jnp.tanh(lo[:, :2 * RWKV_LORA])
    ad = lo[:, 2 * RWKV_LORA:4 * RWKV_LORA]
    gd = jax.nn.sigmoid(lo[:, 4 * RWKV_LORA:])
    w_pre = _dot(wd, w2_ref[...]) + w0_ref[...]
    a_sig = jax.nn.sigmoid(_dot(ad, a2_ref[...]) + a0_ref[...])
    g_o[...] = _dot(gd, g2_ref[...])
    w_log = -(jnp.maximum(-w_pre, 0.0) + jnp.log(1.0 + jnp.exp(-jnp.abs(w_pre)))) - 0.5
    lw = -jnp.exp(w_log)
    kkf = k * kk_w_ref[...]
    kk = kkf * lax.rsqrt(_head_sums(kkf * kkf) + 1e-12)
    r_o[...] = r
    kk_o[...] = kk
    v_o[...] = v
    k_sum = None
    for d in range(2):
        a_d = a_sig[:, d * BRANCH_W:(d + 1) * BRANCH_W]
        k_d = k * (1.0 + (a_d - 1.0) * ka_w_ref[...])
        lw_o[d] = lw[:, d * BRANCH_W:(d + 1) * BRANCH_W]
        kd_o[d] = k_d
        ka_o[d] = kk * a_d
        k_sum = k_d if k_sum is None else k_sum + k_d
    bonus_o[...] = _head_sums(r * k_sum * rk_w_ref[...]) * v

    z = cc_ref[...] * cu_ref[...]
    z_before = hp_ref[HALO - 1:HALO, C_CC:C_CC + BRANCH_W] * hp_ref[HALO - 1:HALO, C_CU:C_CU + BRANCH_W] * has_prev
    z_after = hn_ref[0:1, C_CC:C_CC + BRANCH_W] * hn_ref[0:1, C_CU:C_CU + BRANCH_W] * has_next
    z_prev = jnp.where(row == 0, z_before, pltpu.roll(z, 1, 0))
    z_next = jnp.where(row == tr - 1, z_after, pltpu.roll(z, tr - 1, 0))
    y = cb_ref[...] * (cw_ref[0:1, :] * z_prev + cw_ref[1:2, :] * z + cw_ref[2:3, :] * z_next)
    conv_o[...] = y.astype(conv_o.dtype)


def _mixer_prep(p, mu_rkv, mu_lora, kk_w, ka_w, rk_w, w0, a0, w2bd, a2bd, g2p, conv_w, *, tr, n_x, t_len,
                c_len):
    n_rows = p.shape[0]
    bw = BRANCH_W
    last_halo = n_rows // HALO - 1
    col = lambda c: (lambda i: (i, c))
    const = lambda i: (0, 0)
    main = [pl.BlockSpec((tr, bw), col(C_R // bw)), pl.BlockSpec((tr, bw), col(C_K // bw)),
            pl.BlockSpec((tr, bw), col(C_V // bw)), pl.BlockSpec((tr, bw), col(C_CB // bw)),
            pl.BlockSpec((tr, bw), col(C_CC // bw)), pl.BlockSpec((tr, bw), col(C_CU // bw)),
            pl.BlockSpec((tr, RWKV_LORA_PAD), col(C_LORA // RWKV_LORA_PAD)),
            pl.BlockSpec((HALO, P_COLS), lambda i: (jnp.maximum(i * (tr // HALO) - 1, 0), 0)),
            pl.BlockSpec((HALO, P_COLS), lambda i: (jnp.minimum((i + 1) * (tr // HALO), last_halo), 0))]
    params = [mu_rkv, mu_lora, kk_w, ka_w, rk_w, w0, a0, w2bd, a2bd, g2p, conv_w]
    param_specs = [pl.BlockSpec(a.shape, const) for a in params]
    row_spec = pl.BlockSpec((tr, bw), lambda i: (i, 0))
    dir_spec = pl.BlockSpec((2, tr, bw), lambda i: (0, i, 0))
    f32_rows = jax.ShapeDtypeStruct((n_rows, bw), F32)
    f32_dirs = jax.ShapeDtypeStruct((2, n_rows, bw), F32)
    vmem = 2 * (7 * tr * bw * 4 + 2 * HALO * P_COLS * 4 + 12 * tr * bw * 4) + 30 * tr * bw * 4
    return pl.pallas_call(
        functools.partial(_prep_kernel, tr=tr, n_x=n_x, t_len=t_len, c_len=c_len),
        grid=(n_rows // tr,),
        in_specs=main + param_specs,
        out_specs=[row_spec, row_spec, row_spec, dir_spec, dir_spec, dir_spec, row_spec, row_spec, row_spec],
        out_shape=[f32_rows, f32_rows, f32_rows, f32_dirs, f32_dirs, f32_dirs, f32_rows, f32_rows,
                   jax.ShapeDtypeStruct((n_rows, bw), BF16)],
        compiler_params=_cparams(("parallel",), vmem),
        name="mixer_prep",
    )(p, p, p, p, p, p, p, p, p, *params)


def _rwkv_kernel(r0_ref, kk0_ref, v0_ref, r1_ref, kk1_ref, v1_ref, lw0_ref, kd0_ref, ka0_ref,
                 lw1_ref, kd1_ref, ka1_ref, o0_ref, o1_ref, s_ref):
    C = RWKV_CHUNK
    W = 2 * RWKV_HEAD
    n_pairs = RWKV_HEADS // 2

    @pl.when(pl.program_id(1) == 0)
    def _():
        s_ref[...] = jnp.zeros_like(s_ref)

    mm, mm_nt = _dot, _dot_nt
    lane = lax.broadcasted_iota(jnp.int32, (C, W), 1)
    row = lax.broadcasted_iota(jnp.int32, (C, W), 0)
    first = lane < RWKV_HEAD
    rc = lax.broadcasted_iota(jnp.int32, (C, C), 0)
    cc = lax.broadcasted_iota(jnp.int32, (C, C), 1)

    def bdiag(x):
        return jnp.concatenate([jnp.where(first, x, jnp.zeros_like(x)), jnp.where(first, jnp.zeros_like(x), x)],
                               axis=0)

    def direction(sign, r_ref, kk_ref, v_ref, lw_ref, kd_ref, ka_ref):
        m_incl = jnp.where((rc - cc) * sign >= 0, 1.0, 0.0).astype(BF16)
        lw = lw_ref[0]
        cum = _dot_exact_lhs(m_incl, lw)
        tot = jnp.sum(lw, axis=0, keepdims=True)
        e_ninc = jnp.exp(-cum)
        e_rem = jnp.exp(tot - cum)
        kk, ka, kd = kk_ref[...], ka_ref[0], kd_ref[0]
        order = (row - (lane % RWKV_HEAD)) * sign
        return dict(a_t=kk * jnp.exp(cum - lw), b_t=-ka * e_ninc, k_t=kd * e_ninc, r_t=r_ref[...] * jnp.exp(cum),
                    b_h=-ka * e_rem, k_h=kd * e_rem, v=v_ref[...], e_tot=jnp.exp(tot), strict=order > 0,
                    incl=order >= 0, eye=jnp.where(order == 0, 1.0, 0.0).astype(F32))

    dirs = (direction(1, r0_ref, kk0_ref, v0_ref, lw0_ref, kd0_ref, ka0_ref),
            direction(-1, r1_ref, kk1_ref, v1_ref, lw1_ref, kd1_ref, ka1_ref))
    o_refs = (o0_ref, o1_ref)

    chains = [(d, p) for d in range(2) for p in range(n_pairs)]
    n = range(len(chains))
    sl = [slice(W * p, W * (p + 1)) for (_, p) in chains]
    dd = [dirs[d] for (d, _) in chains]
    ar = [jnp.concatenate([dd[c]["a_t"][:, sl[c]], dd[c]["r_t"][:, sl[c]]], axis=0).astype(BF16) for c in n]
    a_bd = [bdiag(dd[c]["a_t"][:, sl[c]].astype(BF16)) for c in n]
    bk_bd = [jnp.concatenate([bdiag(dd[c]["b_t"][:, sl[c]].astype(BF16)), bdiag(dd[c]["k_t"][:, sl[c]].astype(BF16))],
                             axis=0) for c in n]
    v_bd = [bdiag(dd[c]["v"][:, sl[c]].astype(BF16)) for c in n]
    g = [mm_nt(ar[c], bk_bd[c]) for c in n]
    l_pow = [jnp.where(dd[c]["strict"], g[c][:C, :W], 0.0) for c in n]
    m_ak = [jnp.where(dd[c]["strict"], g[c][:C, W:], 0.0).astype(BF16) for c in n]
    a_r = [jnp.where(jnp.concatenate([dd[c]["incl"], dd[c]["incl"]], axis=1), g[c][C:], 0.0).astype(BF16) for c in n]
    mv = [mm(m_ak[c], v_bd[c]) for c in n]
    l_bd = [bdiag(l_pow[c].astype(BF16)) for c in n]
    t_inv = [dd[c]["eye"] + l_pow[c] for c in n]
    l_pow = [mm(l_pow[c], l_bd[c]) for c in n]
    for _ in range(int(math.log2(C)) - 2):
        l_bd = [bdiag(l_pow[c].astype(BF16)) for c in n]
        lt = [mm(jnp.concatenate([l_pow[c], t_inv[c]], axis=0), l_bd[c]) for c in n]
        l_pow = [lt[c][:C] for c in n]
        t_inv = [t_inv[c] + lt[c][C:] for c in n]
    t_inv = [t_inv[c] + mm(t_inv[c], bdiag(l_pow[c].astype(BF16))) for c in n]
    wu = [mm(t_inv[c], jnp.concatenate([a_bd[c], bdiag(mv[c].astype(BF16))], axis=1)) for c in n]
    s_bd = [s_ref[d, p] for (d, p) in chains]
    wr = [mm(jnp.concatenate([wu[c][:, :W], dd[c]["r_t"][:, sl[c]]], axis=0), s_bd[c]) for c in n]
    zv = [jnp.concatenate([bdiag((wr[c][:C] + wu[c][:, W:]).astype(BF16)), v_bd[c]], axis=0) for c in n]
    for c, (d, p) in enumerate(chains):
        o_refs[d][:, sl[c]] = wr[c][C:] + mm(a_r[c], zv[c])
    for c, (d, p) in enumerate(chains):
        bk_h = jnp.concatenate([bdiag(dd[c]["b_h"][:, sl[c]]), bdiag(dd[c]["k_h"][:, sl[c]])], axis=0)
        tot_col = jnp.broadcast_to(dd[c]["e_tot"][:, sl[c]], (W, W)).T
        s_ref[d, p] = tot_col * s_bd[c] + mm(bk_h.T, zv[c])


def _rwkv_scan(r, kk, v, lw, kd, ka, *, n_batch, t_len, c_len):
    C = RWKV_CHUNK
    rows, width = r.shape
    nct, nxt = c_len // C, t_len // C
    ctx_blk0 = n_batch * t_len // C

    def blk(b, d, s):
        j_c = s if d == 0 else nct - 1 - s
        j_x = s - nct if d == 0 else nxt - 1 - (s - nct)
        return jnp.where(s < nct, ctx_blk0 + b * nct + j_c, b * nxt + j_x)

    def shared(d):
        return pl.BlockSpec((C, width), lambda b, s: (blk(b, d, s), 0))

    def per_dir(d):
        return pl.BlockSpec((1, C, width), lambda b, s: (d, blk(b, d, s), 0))

    vmem = 2 * 14 * C * width * 4 + RWKV_HEADS * 128 * 128 * 4 + 128 * C * width * 4
    out = jax.ShapeDtypeStruct((rows, width), F32)
    return pl.pallas_call(
        _rwkv_kernel,
        grid=(n_batch, nct + nxt),
        in_specs=[shared(0), shared(0), shared(0), shared(1), shared(1), shared(1),
                  per_dir(0), per_dir(0), per_dir(0), per_dir(1), per_dir(1), per_dir(1)],
        out_specs=[shared(0), shared(1)],
        out_shape=[out, out],
        scratch_shapes=[pltpu.VMEM((2, RWKV_HEADS // 2, 128, 128), F32)],
        compiler_params=_cparams(("parallel", "arbitrary"), vmem),
        name="rwkv_scan",
    )(r, kk, v, r, kk, v, lw, kd, ka, lw, kd, ka)


def _rwkv_readout_kernel(o0_ref, o1_ref, bonus_ref, g_ref, lng_ref, lnb_ref, y_ref):
    o = o0_ref[...] + o1_ref[...]
    mean = _head_sums(o) * (1.0 / RWKV_HEAD)
    cen = o - mean
    var = _head_sums(cen * cen) * (1.0 / RWKV_HEAD)
    o_n = cen * lax.rsqrt(var + RWKV_GN_EPS) * lng_ref[...] + lnb_ref[...]
    y_ref[...] = ((o_n + bonus_ref[...]) * g_ref[...]).astype(y_ref.dtype)


def _rwkv_readout(o_dirs, bonus, g_out, ln_g, ln_b, *, tr):
    n_rows, bw = bonus.shape
    row_spec = pl.BlockSpec((tr, bw), lambda i: (i, 0))
    const = pl.BlockSpec((1, bw), lambda i: (0, 0))
    return pl.pallas_call(
        _rwkv_readout_kernel,
        grid=(n_rows // tr,),
        in_specs=[row_spec, row_spec, row_spec, row_spec, const, const],
        out_specs=row_spec,
        out_shape=jax.ShapeDtypeStruct((n_rows, bw), BF16),
        compiler_params=_cparams(("parallel",), 2 * 5 * tr * bw * 4 + 16 * tr * bw * 4),
        name="rwkv_readout",
    )(o_dirs[0], o_dirs[1], bonus, g_out, ln_g.reshape(1, bw), ln_b.reshape(1, bw))


def _rmsnorm_kernel(x_ref, g_ref, o_ref):
    x = x_ref[...]
    o_ref[...] = x * lax.rsqrt(jnp.mean(x * x, axis=-1, keepdims=True) + NORM_EPS) * g_ref[...]


def _final_norm(xs, g, *, rows, tm):
    dm = xs.shape[1]
    return pl.pallas_call(
        _rmsnorm_kernel,
        grid=(rows // tm,),
        in_specs=[pl.BlockSpec((tm, dm), lambda i: (i, 0)), pl.BlockSpec((1, dm), lambda i: (0, 0))],
        out_specs=pl.BlockSpec((tm, dm), lambda i: (i, 0)),
        out_shape=jax.ShapeDtypeStruct((rows, dm), F32),
        compiler_params=_cparams(("parallel",), 4 * tm * dm * 4),
        name="final_norm",
    )(xs, g.reshape(1, dm))


def _rope_tables128(n_tokens):
    rows = n_tokens // GRID_W
    row = jnp.repeat(jnp.arange(rows, dtype=F32), GRID_W)
    col = jnp.tile(jnp.arange(GRID_W, dtype=F32), rows)
    n_freq = 64 // 4
    inv = ROPE_BASE ** (-jnp.arange(n_freq, dtype=F32) / n_freq)
    ang = jnp.concatenate([row[:, None] * inv, col[:, None] * inv], axis=-1)
    cos, sin = jnp.cos(ang), jnp.sin(ang)
    return jnp.concatenate([cos, cos, cos, cos], axis=-1), jnp.concatenate([-sin, sin, -sin, sin], axis=-1)


def _block_diag2(w2):
    z = jnp.zeros_like(w2[0])
    return jnp.concatenate([jnp.concatenate([w2[0], z], axis=1), jnp.concatenate([z, w2[1]], axis=1)], axis=0)


def kernel(x, c, ctx, c_ctx, norm1_g, norm2_g, mod_down, mod_up, mod_b, w_in, mla_q_norm_g, mla_w_uq,
           mla_kv_norm_g, mla_w_ukv, rwkv_mu, rwkv_w0, rwkv_w2, rwkv_a0, rwkv_a2, rwkv_g2, rwkv_k_k,
           rwkv_k_a, rwkv_r_k, rwkv_ln_g, rwkv_ln_b, conv_w, diff_lambda, diff_norm_g, w_branch, gate_down,
           gate_up, gate_b, w_out, mlp_w1, mlp_w2, final_norm_g):
    n_batch, t_len, dm = x.shape
    c_len = ctx.shape[1]
    depth = w_in.shape[0]
    bw = BRANCH_W
    n_x = n_batch * t_len
    n_c = n_batch * c_len
    n_rows = n_x + n_c
    tm = 512 if (t_len % 512 == 0 and n_c % 512 == 0) else 256
    tq = 512 if t_len % 512 == 0 else min(256, c_len)
    tr = min(256, c_len)
    assert dm == D_MODEL and t_len % tm == 0 and n_c % tm == 0 and n_x % c_len == 0
    assert t_len % tq == 0 and c_len % RWKV_CHUNK == 0 and t_len % c_len == 0
    tm_big = 1024 if (t_len % 1024 == 0 and n_c % 1024 == 0) else tm
    mod_index = functools.partial(_mod_index, tm=tm, n_x_rows=n_x, t_len=t_len, n_batch=n_batch)
    mod_index_big = functools.partial(_mod_index, tm=tm_big, n_x_rows=n_x, t_len=t_len, n_batch=n_batch)
    mod_index_nm = functools.partial(_mod_index, tm=256, n_x_rows=n_x, t_len=t_len, n_batch=n_batch)

    zeros = lambda *s: jnp.zeros(s, F32)
    lora_w = 4 * RWKV_LORA + RWKV_GATE_LORA
    rw0 = MLA_Q_LORA + MLA_KV_LORA + MLA_ROPE
    cv0 = rw0 + 3 * bw + lora_w
    w_in_p = jnp.concatenate(
        [w_in[:, :, :C_R], w_in[:, :, rw0:rw0 + 3 * bw], w_in[:, :, cv0:],
         w_in[:, :, rw0 + 3 * bw:cv0], zeros(depth, dm, RWKV_LORA_PAD - lora_w), gate_down,
         w_in[:, :, C_R:rw0], zeros(depth, dm, P_COLS - C_KROPE - MLA_ROPE)], axis=-1).astype(BF16)
    w_uq_p = jnp.pad(mla_w_uq.reshape(depth, MLA_Q_LORA, MLA_HEADS, MLA_NOPE + MLA_ROPE),
                     ((0, 0), (0, 0), (0, 0), (0, MLA_QK_PAD - MLA_NOPE - MLA_ROPE))
                     ).reshape(depth, MLA_Q_LORA, MLA_HEADS * MLA_QK_PAD).astype(BF16)
    w_ukv_r = mla_w_ukv.reshape(depth, MLA_KV_LORA, MLA_HEADS, MLA_NOPE + MLA_V)
    w_ukv_p = jnp.concatenate([w_ukv_r[..., :MLA_NOPE].reshape(depth, MLA_KV_LORA, -1),
                               w_ukv_r[..., MLA_NOPE:].reshape(depth, MLA_KV_LORA, -1)], axis=-1).astype(BF16)
    w_branch_2d = w_branch.reshape(depth, 4 * bw, dm)
    gate_up_b = jnp.moveaxis(gate_up, 2, 1).astype(BF16)
    g2_p = jnp.pad(rwkv_g2, ((0, 0), (0, RWKV_LORA_PAD - 4 * RWKV_LORA - RWKV_GATE_LORA), (0, 0))).astype(BF16)
    mu_lora = jnp.pad(rwkv_mu[:, :, 3 * bw:], ((0, 0), (0, 0), (0, RWKV_LORA_PAD - lora_w)))

    cond = jnp.concatenate([c, c_ctx[None, :], zeros(16 - n_batch - 1, dm)], axis=0)
    cond = jax.nn.silu(cond)
    mods = []
    for l in range(depth):
        low = _matmul(cond, mod_down[l], tm=16, tn=MOD_RANK, tk=dm, name="mod_down")
        up = _matmul(low, mod_up[l], tm=16, tn=2048, tk=MOD_RANK, bias=mod_b[l], name="mod_up")
        mods.append(up.reshape(16, N_MOD, dm))

    cos_t, sin_t = _rope_tables128(t_len)
    xs = jnp.concatenate([x.reshape(n_x, dm), ctx.reshape(n_c, dm)], axis=0)

    for l in range(depth):
        need_ctx = l < depth - 1
        mod = mods[l]
        lam_init = 0.8 - 0.6 * math.exp(-0.3 * l)
        lq1, lk1, lq2, lk2 = diff_lambda[l]
        lam = jnp.exp(jnp.sum(lq1 * lk1)) - jnp.exp(jnp.sum(lq2 * lk2)) + lam_init
        lam_row = jnp.full((1, DIFF_V), 1.0, F32) * lam

        h1 = _norm_mod(xs, norm1_g[l], mod, shift_row=0, scale_row=1, mod_index=mod_index_nm, tm=256,
                       rows=n_rows)
        p, w1_b, w_out_b, w_branch_b = _fullk_matmul(h1, w_in_p, layer=l, tm=tm_big, tn=512, out_dtype=F32,
                                                     name="in_proj", cast_srcs=(mlp_w1, w_out, w_branch_2d))

        q = _nm_matmul(p, mla_q_norm_g[l], w_uq_p[l], tm=tm, tn=512, out_dtype=BF16,
                       x_col_block=C_CQ // MLA_Q_LORA, kdim=MLA_Q_LORA, name="mla_q")
        kv = _nm_matmul(p, mla_kv_norm_g[l], w_ukv_p[l], tm=tm, tn=512, out_dtype=BF16,
                        x_col_block=C_CKV // MLA_KV_LORA, kdim=MLA_KV_LORA, name="mla_kv")
        y_mla = _mla_attention(q, kv, p, cos_t, sin_t, n_batch=n_batch, t_len=t_len, c_len=c_len, tq=tq,
                               with_ctx=need_ctx)

        y_diff = _diff_attention(p, cos_t, sin_t, lam_row, diff_norm_g[l].reshape(1, DIFF_V),
                                 n_batch=n_batch, t_len=t_len, c_len=c_len, tq=tq, with_ctx=need_ctx,
                                 out_scale=1.0 - lam_init)

        r_, kk, v_, lw, k_dir, kka, bonus, g_out, y_conv = _mixer_prep(
            p, rwkv_mu[l, :, :3 * bw], mu_lora[l], rwkv_k_k[l].reshape(1, bw), rwkv_k_a[l].reshape(1, bw),
            rwkv_r_k[l].reshape(1, bw), rwkv_w0[l].reshape(1, 2 * bw), rwkv_a0[l].reshape(1, 2 * bw),
            _block_diag2(rwkv_w2[l]).astype(BF16), _block_diag2(rwkv_a2[l]).astype(BF16), g2_p[l],
            conv_w[l], tr=min(128, tr), n_x=n_x, t_len=t_len, c_len=c_len)
        o_dirs = _rwkv_scan(r_, kk, v_, lw, k_dir, kka, n_batch=n_batch, t_len=t_len, c_len=c_len)
        y_rwkv = _rwkv_readout(o_dirs, bonus, g_out, rwkv_ln_g[l], rwkv_ln_b[l], tr=tr)

        rows = n_rows if need_ctx else n_x
        acc = _merge(p, (y_mla, y_rwkv, y_conv, y_diff), w_branch_b, gate_up_b, gate_b, layer=l,
                     tm=tm_big, tn=512, rows=rows)
        (xs_new,) = _fullk_matmul(acc, w_out_b[None], layer=0, tm=tm_big, tn=512, out_dtype=F32, res=xs, mod=mod,
                                  gate_row=2, mod_index=mod_index_big, rows=rows, name="out_proj")

        h2 = _norm_mod(xs_new, norm2_g[l], mod, shift_row=3, scale_row=4, mod_index=mod_index_nm, tm=256,
                       rows=rows)
        hid, w2_b = _fullk_matmul(h2, w1_b[None], layer=0, tm=tm_big, tn=TN_UP, out_dtype=BF16, act="relu2",
                                  name="mlp_up", cast_srcs=(mlp_w2,), cast_layer=l)
        xs = _matmul(hid, w2_b, tm=tm_big, tn=1024, tk=2048, res=xs_new, mod=mod, gate_row=5,
                     mod_index=mod_index_big, rows=rows, name="mlp_down")

    out = _final_norm(xs, final_norm_g, rows=n_x, tm=tm)
    return out.reshape(n_batch, t_len, dm)
```

```python
import functools
import math

import jax
import jax.numpy as jnp
from jax import lax
from jax.experimental import pallas as pl
from jax.experimental.pallas import tpu as pltpu

F32 = jnp.float32
BF16 = jnp.bfloat16

D_MODEL = 4096
BRANCH_W = 1024
GRID_W = 64
ROPE_BASE = 10000.0
NORM_EPS = 1e-6
N_MOD = 6
LANES = 128
HALO = 8
LOG2_E = 1.4426950408889634

MLA_HEADS = 8
MLA_NOPE = 128
MLA_ROPE = 64
MLA_V = 128
MLA_Q_LORA = 768
MLA_KV_LORA = 256
MLA_QK_PAD = 256

RWKV_HEAD = 64
RWKV_HEADS = 16
RWKV_LORA = 64
RWKV_GATE_LORA = 160
RWKV_GN_EPS = 64e-5
RWKV_CHUNK = 64
RWKV_LORA_PAD = 512

DIFF_HEADS = 8
DIFF_QK = 64
DIFF_V = 128
GATE_RANK = 256
MOD_RANK = 256

C_CQ = 0
C_CKV = 768
C_R = 1024
C_K = 2048
C_V = 3072
C_CB = 4096
C_CC = 5120
C_CU = 6144
C_DQ = 7168
C_DK = 8192
C_DV = 9216
C_LORA = 10240
C_GL = 10752
C_KROPE = 11008
P_COLS = 11264

TN_UP = 1024
ATTN_SUB = 128
VMEM_CAP = 56 * 1024 * 1024


def _cparams(sem, vmem_bytes):
    limit = int(min(VMEM_CAP, max(vmem_bytes * 1.5 + (4 << 20), 16 << 20)))
    return pltpu.CompilerParams(dimension_semantics=sem, vmem_limit_bytes=limit)


def _mod_index(i, tm, n_x_rows, t_len, n_batch):
    return jnp.where(i < n_x_rows // tm, i // (t_len // tm), n_batch)


def _dot(a, b):
    return jnp.dot(a.astype(BF16), b.astype(BF16), preferred_element_type=F32)


def _dot_nt(a, b):
    return lax.dot_general(a.astype(BF16), b.astype(BF16), (((1,), (1,)), ((), ())),
                           preferred_element_type=F32)


def _split3(x):
    hi = x.astype(BF16)
    r1 = x - hi.astype(F32)
    mid = r1.astype(BF16)
    lo = (r1 - mid.astype(F32)).astype(BF16)
    return hi, mid, lo


def _dot_exact_lhs(m_bf16, x):
    out = None
    for part in _split3(x):
        t = jnp.dot(m_bf16, part, preferred_element_type=F32)
        out = t if out is None else out + t
    return out


def _head_sums(x):
    r = lax.broadcasted_iota(jnp.int32, (LANES, LANES), 0)
    c = lax.broadcasted_iota(jnp.int32, (LANES, LANES), 1)
    ones_bd = jnp.where((r // RWKV_HEAD) == (c // RWKV_HEAD), 1.0, 0.0).astype(BF16)
    parts = _split3(x)
    cols = []
    for j in range(x.shape[1] // LANES):
        acc = None
        for part in parts:
            t = jnp.dot(part[:, j * LANES:(j + 1) * LANES], ones_bd, preferred_element_type=F32)
            acc = t if acc is None else acc + t
        cols.append(acc)
    return jnp.concatenate(cols, axis=1)


def _rope128(x, cos_t, sin_t):
    lane = lax.broadcasted_iota(jnp.int32, x.shape, 1)
    swapped = jnp.where((lane % 64) < 32, pltpu.roll(x, 96, 1), pltpu.roll(x, 32, 1))
    return x * cos_t + swapped * sin_t


def _matmul_kernel(*refs, nk, has_bias, has_res, gate_row, act):
    a_ref, w_ref = refs[0], refs[1]
    pos = 2
    bias_ref = res_ref = mod_ref = None
    if has_bias:
        bias_ref = refs[pos]; pos += 1
    if has_res:
        res_ref = refs[pos]; mod_ref = refs[pos + 1]; pos += 2
    o_ref, acc_ref = refs[pos], refs[pos + 1]
    k = pl.program_id(2)

    @pl.when(k == 0)
    def _():
        acc_ref[...] = jnp.zeros_like(acc_ref)

    acc_ref[...] += _dot(a_ref[...], w_ref[...])

    @pl.when(k == nk - 1)
    def _():
        y = acc_ref[...]
        if has_bias:
            y = y + bias_ref[...]
        if act == "relu2":
            y = jnp.square(jnp.maximum(y, 0.0))
        if has_res:
            y = res_ref[...] + mod_ref[0, gate_row:gate_row + 1, :] * y
        o_ref[...] = y.astype(o_ref.dtype)


def _matmul(a, w, *, tm, tn, tk, out_dtype=F32, bias=None, res=None, mod=None, gate_row=None,
            mod_index=None, act=None, rows=None, name=None):
    m = a.shape[0] if rows is None else rows
    kdim, n = w.shape
    assert a.shape[1] == kdim and m % tm == 0 and n % tn == 0 and kdim % tk == 0
    nk = kdim // tk
    in_specs = [pl.BlockSpec((tm, tk), lambda i, j, k: (i, k)),
                pl.BlockSpec((tk, tn), lambda i, j, k: (k, j))]
    args = [a, w]
    vmem = 2 * tm * tk * a.dtype.itemsize + 2 * tk * tn * w.dtype.itemsize + tm * tn * 4
    vmem += 2 * tm * tn * jnp.dtype(out_dtype).itemsize
    if bias is not None:
        in_specs.append(pl.BlockSpec((1, tn), lambda i, j, k: (0, j)))
        args.append(bias.reshape(1, n).astype(F32))
    if res is not None:
        in_specs.append(pl.BlockSpec((tm, tn), lambda i, j, k: (i, j)))
        in_specs.append(pl.BlockSpec((1, N_MOD, tn), lambda i, j, k: (mod_index(i), 0, j)))
        args += [res, mod]
        vmem += 2 * tm * tn * 4 + 2 * 8 * tn * 4
    kern = functools.partial(_matmul_kernel, nk=nk, has_bias=bias is not None,
                             has_res=res is not None, gate_row=gate_row, act=act)
    return pl.pallas_call(
        kern,
        grid=(m // tm, n // tn, nk),
        in_specs=in_specs,
        out_specs=pl.BlockSpec((tm, tn), lambda i, j, k: (i, j)),
        out_shape=jax.ShapeDtypeStruct((m, n), out_dtype),
        scratch_shapes=[pltpu.VMEM((tm, tn), F32)],
        compiler_params=_cparams(("parallel", "parallel", "arbitrary"), vmem),
        name=name,
    )(*args)


def _nm_matmul_kernel(*refs, has_mod, shift_row, scale_row, act):
    if has_mod:
        x_ref, g_ref, mod_ref, w_ref, o_ref, h_ref = refs
    else:
        x_ref, g_ref, w_ref, o_ref, h_ref = refs

    @pl.when(pl.program_id(1) == 0)
    def _():
        x = x_ref[...].astype(F32)
        y = x * lax.rsqrt(jnp.mean(x * x, axis=-1, keepdims=True) + NORM_EPS) * g_ref[...]
        if has_mod:
            y = y * (1.0 + mod_ref[0, scale_row:scale_row + 1, :]) + mod_ref[0, shift_row:shift_row + 1, :]
        h_ref[...] = y.astype(BF16)

    y = jnp.dot(h_ref[...], w_ref[...].astype(BF16), preferred_element_type=F32)
    if act == "relu2":
        y = jnp.square(jnp.maximum(y, 0.0))
    o_ref[...] = y.astype(o_ref.dtype)


def _nm_matmul(x, g, w, *, tm, tn, out_dtype, x_col_block=0, kdim=None, mod=None, shift_row=None,
               scale_row=None, mod_index=None, act=None, rows=None, name=None):
    m = x.shape[0] if rows is None else rows
    kdim = x.shape[1] if kdim is None else kdim
    n = w.shape[1]
    assert w.shape[0] == kdim and m % tm == 0 and n % tn == 0
    in_specs = [pl.BlockSpec((tm, kdim), lambda i, j: (i, x_col_block)),
                pl.BlockSpec((1, kdim), lambda i, j: (0, 0))]
    args = [x, g.reshape(1, kdim).astype(F32)]
    if mod is not None:
        in_specs.append(pl.BlockSpec((1, N_MOD, kdim), lambda i, j: (mod_index(i), 0, 0)))
        args.append(mod)
    in_specs.append(pl.BlockSpec((kdim, tn), lambda i, j: (0, j)))
    args.append(w)
    vmem = (2 * tm * kdim * x.dtype.itemsize + tm * kdim * 2 + 2 * kdim * tn * w.dtype.itemsize
            + 2 * tm * tn * jnp.dtype(out_dtype).itemsize + tm * tn * 4 + 4 * 8 * kdim * 4)
    kern = functools.partial(_nm_matmul_kernel, has_mod=mod is not None, shift_row=shift_row,
                             scale_row=scale_row, act=act)
    return pl.pallas_call(
        kern,
        grid=(m // tm, n // tn),
        in_specs=in_specs,
        out_specs=pl.BlockSpec((tm, tn), lambda i, j: (i, j)),
        out_shape=jax.ShapeDtypeStruct((m, n), out_dtype),
        scratch_shapes=[pltpu.VMEM((tm, kdim), BF16)],
        compiler_params=_cparams(("parallel", "arbitrary"), vmem),
        name=name,
    )(*args)


def _norm_mod_kernel(x_ref, g_ref, mod_ref, h_ref, *, shift_row, scale_row):
    x = x_ref[...]
    y = x * lax.rsqrt(jnp.mean(x * x, axis=-1, keepdims=True) + NORM_EPS) * g_ref[...]
    y = y * (1.0 + mod_ref[0, scale_row:scale_row + 1, :]) + mod_ref[0, shift_row:shift_row + 1, :]
    h_ref[...] = y.astype(h_ref.dtype)


def _norm_mod(xs, g, mod, *, shift_row, scale_row, mod_index, tm, rows):
    dm = xs.shape[1]
    return pl.pallas_call(
        functools.partial(_norm_mod_kernel, shift_row=shift_row, scale_row=scale_row),
        grid=(rows // tm,),
        in_specs=[pl.BlockSpec((tm, dm), lambda i: (i, 0)), pl.BlockSpec((1, dm), lambda i: (0, 0)),
                  pl.BlockSpec((1, N_MOD, dm), lambda i: (mod_index(i), 0, 0))],
        out_specs=pl.BlockSpec((tm, dm), lambda i: (i, 0)),
        out_shape=jax.ShapeDtypeStruct((rows, dm), BF16),
        compiler_params=_cparams(("parallel",), 2 * tm * dm * 6 + 4 * tm * dm * 4),
        name="norm_mod",
    )(xs, g.reshape(1, dm), mod)


def _fullk_kernel(*refs, has_res, n_cast, gate_row, act, w_rows_are_outputs):
    refs = list(refs)
    if n_cast:
        cast_dsts = refs[-n_cast:]
        cast_srcs = refs[-2 * n_cast - 1:-n_cast - 1]
        refs = refs[:-2 * n_cast - 1] + [refs[-n_cast - 1]]
        for src, dst in zip(cast_srcs, cast_dsts):
            dst[...] = src[...].astype(dst.dtype)
    if has_res:
        a_ref, w_ref, res_ref, mod_ref, o_ref = refs
    else:
        a_ref, w_ref, o_ref = refs
    if w_rows_are_outputs:
        y = lax.dot_general(a_ref[...], w_ref[...], (((1,), (1,)), ((), ())), preferred_element_type=F32)
    else:
        y = jnp.dot(a_ref[...], w_ref[...], preferred_element_type=F32)
    if act == "relu2":
        y = jnp.square(jnp.maximum(y, 0.0))
    if has_res:
        y = res_ref[...] + mod_ref[0, gate_row:gate_row + 1, :] * y
    o_ref[...] = y.astype(o_ref.dtype)


def _cast_rows_per_step(n_rows, n_steps):
    rows = 16
    while n_rows % rows or n_rows // rows > n_steps:
        rows *= 2
    return rows


def _fullk_matmul(a, w, *, tm, tn, out_dtype, act=None, res=None, mod=None, gate_row=None, mod_index=None,
                  rows=None, name=None, layer=None, cast_srcs=(), cast_layer=None, w_rows_are_outputs=False):
    m = a.shape[0] if rows is None else rows
    if w_rows_are_outputs:
        _, n, kdim = w.shape
        w_spec = pl.BlockSpec((None, tn, kdim), lambda i, j: (layer, j, 0))
    else:
        _, kdim, n = w.shape
        w_spec = pl.BlockSpec((None, kdim, tn), lambda i, j: (layer, 0, j))
    assert a.shape[1] == kdim and m % tm == 0 and n % tn == 0 and a.dtype == BF16 and w.dtype == BF16
    in_specs = [pl.BlockSpec((tm, kdim), lambda i, j: (i, 0)), w_spec]
    args = [a, w]
    vmem = 2 * tm * kdim * 2 + 2 * kdim * tn * 2 + 2 * tm * tn * jnp.dtype(out_dtype).itemsize + 2 * tm * tn * 4
    if res is not None:
        in_specs += [pl.BlockSpec((tm, tn), lambda i, j: (i, j)),
                     pl.BlockSpec((1, N_MOD, tn), lambda i, j: (mod_index(i), 0, j))]
        args += [res, mod]
        vmem += 2 * tm * tn * 4
    out_specs = [pl.BlockSpec((tm, tn), lambda i, j: (i, j))]
    out_shape = [jax.ShapeDtypeStruct((m, n), out_dtype)]
    nj = n // tn
    cast_layer = layer if cast_layer is None else cast_layer
    for cast_src in cast_srcs:
        _, c_rows, c_cols = cast_src.shape
        cr = _cast_rows_per_step(c_rows, (m // tm) * nj)
        cast_blk = lambda i, j, last=c_rows // cr - 1: jnp.minimum(i * nj + j, last)
        in_specs.append(pl.BlockSpec((None, cr, c_cols), lambda i, j, blk=cast_blk: (cast_layer, blk(i, j), 0)))
        args.append(cast_src)
        out_specs.append(pl.BlockSpec((cr, c_cols), lambda i, j, blk=cast_blk: (blk(i, j), 0)))
        out_shape.append(jax.ShapeDtypeStruct((c_rows, c_cols), BF16))
        vmem += 2 * cr * c_cols * 6
    return pl.pallas_call(
        functools.partial(_fullk_kernel, has_res=res is not None, n_cast=len(cast_srcs),
                          gate_row=gate_row, act=act, w_rows_are_outputs=w_rows_are_outputs),
        grid=(m // tm, n // tn),
        in_specs=in_specs,
        out_specs=out_specs,
        out_shape=out_shape,
        compiler_params=_cparams(("arbitrary", "arbitrary"), vmem),
        name=name,
    )(*args)


def _merge_kernel(gl_ref, y0_ref, y1_ref, y2_ref, y3_ref, wb_ref, gu_ref, gb_ref, o_ref):
    gl = gl_ref[...].astype(BF16)
    bw = wb_ref.shape[0] // 4
    acc = None
    for i, y_ref in enumerate((y0_ref, y1_ref, y2_ref, y3_ref)):
        gate = jax.nn.sigmoid(jnp.dot(gl, gu_ref[i], preferred_element_type=F32) + gb_ref[i:i + 1, :])
        term = gate * jnp.dot(y_ref[...], wb_ref[i * bw:(i + 1) * bw, :], preferred_element_type=F32)
        acc = term if acc is None else acc + term
    o_ref[...] = acc.astype(o_ref.dtype)


def _merge(p, ys, wb, gu, gb, *, layer, tm, tn, rows):
    n = wb.shape[1]
    bw = wb.shape[0] // 4
    gr = gu.shape[2]
    y_spec = pl.BlockSpec((tm, bw), lambda i, j: (i, 0))
    vmem = (2 * tm * gr * 4 + 4 * 2 * tm * bw * 2 + 2 * 4 * bw * tn * 2 + 2 * 4 * gr * tn * 2
            + 2 * tm * tn * 2 + 3 * tm * tn * 4)
    return pl.pallas_call(
        _merge_kernel,
        grid=(rows // tm, n // tn),
        in_specs=[pl.BlockSpec((tm, gr), lambda i, j: (i, C_GL // GATE_RANK)),
                  y_spec, y_spec, y_spec, y_spec,
                  pl.BlockSpec((4 * bw, tn), lambda i, j: (0, j)),
                  pl.BlockSpec((None, 4, gr, tn), lambda i, j: (layer, 0, 0, j)),
                  pl.BlockSpec((None, 4, tn), lambda i, j: (layer, 0, j))],
        out_specs=pl.BlockSpec((tm, tn), lambda i, j: (i, j)),
        out_shape=jax.ShapeDtypeStruct((rows, n), BF16),
        compiler_params=_cparams(("parallel", "arbitrary"), vmem),
        name="merge",
    )(p, *ys, wb, gu, gb)


def _softmax_parts(s_list, scale):
    m = None
    for s in s_list:
        sm = jnp.max(s, axis=-1, keepdims=True)
        m = sm if m is None else jnp.maximum(m, sm)
    e_list = [jnp.exp2((s - m) * (scale * LOG2_E)) for s in s_list]
    l = None
    for e in e_list:
        es = jnp.sum(e, axis=-1, keepdims=True)
        l = es if l is None else l + es
    return e_list, l


def _attend(q, ks, vs, scale):
    e_list, l = _softmax_parts([_dot_nt(q, k) for k in ks], scale)
    o = None
    for e, v in zip(e_list, vs):
        t = jnp.dot(e.astype(BF16), v, preferred_element_type=F32)
        o = t if o is None else o + t
    return o / l


def _attend_many(qs, ks, vs, scale, sub):
    parts = [q[r:r + sub] for q in qs for r in range(0, q.shape[0], sub)]
    n = len(parts)
    scores, soft, outs = {}, {}, []
    for t in range(n + 2):
        if t < n:
            scores[t] = [_dot_nt(parts[t], k) for k in ks]
        if 0 <= t - 1 < n:
            soft[t - 1] = _softmax_parts(scores.pop(t - 1), scale)
        if 0 <= t - 2 < n:
            e_list, l = soft.pop(t - 2)
            o = None
            for e, v in zip(e_list, vs):
                pv = jnp.dot(e.astype(BF16), v, preferred_element_type=F32)
                o = pv if o is None else o + pv
            outs.append(o / l)
    per_q = len(parts) // len(qs)
    return [jnp.concatenate(outs[i * per_q:(i + 1) * per_q], axis=0) for i in range(len(qs))]


def _mla_kernel(cos_ref, sin_ref, q_ref, knx_ref, knc_ref, krx_ref, krc_ref, vx_ref, vc_ref, o_ref,
                kx_s, kc_s, *, tq, scale):
    i = pl.program_id(2)

    @pl.when(i == 0)
    def _():
        kx_s[:, :MLA_NOPE] = knx_ref[...]
        kx_s[:, MLA_NOPE:] = _rope128(krx_ref[...], cos_ref[...], sin_ref[...]).astype(BF16)
        kc_s[:, :MLA_NOPE] = knc_ref[...]
        kc_s[:, MLA_NOPE:] = krc_ref[...].astype(BF16)

    row0 = pl.multiple_of(i * tq, tq)
    q = q_ref[...]
    q_rope = _rope128(q[:, MLA_NOPE:].astype(F32), cos_ref[pl.ds(row0, tq), :], sin_ref[pl.ds(row0, tq), :])
    q = jnp.concatenate([q[:, :MLA_NOPE], q_rope.astype(BF16)], axis=1)
    (o,) = _attend_many((q,), (kc_s[...], kx_s[...]), (vc_ref[...], vx_ref[...]), scale, ATTN_SUB)
    o_ref[...] = o.astype(o_ref.dtype)


def _mla_ctx_kernel(q_ref, knc_ref, krc_ref, vc_ref, y_hbm, o_ref, *, scale):
    del y_hbm
    kc = jnp.concatenate([knc_ref[...], krc_ref[...].astype(BF16)], axis=1)
    o = _attend(q_ref[...], (kc,), (vc_ref[...],), scale)
    o_ref[...] = o.astype(o_ref.dtype)


def _diff_finish(o1, o2, lam_ref, g_ref, out_scale):
    o = o1 - lam_ref[...] * o2
    y = o * lax.rsqrt(jnp.mean(o * o, axis=-1, keepdims=True) + NORM_EPS) * g_ref[...]
    return y * out_scale


def _diff_halves(q):
    lane = lax.broadcasted_iota(jnp.int32, q.shape, 1)
    return jnp.where(lane < DIFF_QK, q, 0.0).astype(BF16), jnp.where(lane < DIFF_QK, 0.0, q).astype(BF16)


def _diff_kernel(cos_ref, sin_ref, lam_ref, g_ref, q_ref, kx_ref, kc_ref, vx_ref, vc_ref, o_ref,
                 kx_s, kc_s, vx_s, vc_s, *, tq, scale, out_scale):
    i = pl.program_id(2)

    @pl.when(i == 0)
    def _():
        kx_s[...] = _rope128(kx_ref[...], cos_ref[...], sin_ref[...]).astype(BF16)
        kc_s[...] = kc_ref[...].astype(BF16)
        vx_s[...] = vx_ref[...].astype(BF16)
        vc_s[...] = vc_ref[...].astype(BF16)

    row0 = pl.multiple_of(i * tq, tq)
    q1, q2 = _diff_halves(_rope128(q_ref[...], cos_ref[pl.ds(row0, tq), :], sin_ref[pl.ds(row0, tq), :]))
    ks, vs = (kc_s[...], kx_s[...]), (vc_s[...], vx_s[...])
    o1, o2 = _attend_many((q1, q2), ks, vs, scale, ATTN_SUB)
    y = _diff_finish(o1, o2, lam_ref, g_ref, out_scale)
    o_ref[...] = y.astype(o_ref.dtype)


def _diff_ctx_kernel(lam_ref, g_ref, q_ref, kc_ref, vc_ref, y_hbm, o_ref, *, scale, out_scale):
    del y_hbm
    q1, q2 = _diff_halves(q_ref[...])
    ks, vs = (kc_ref[...].astype(BF16),), (vc_ref[...].astype(BF16),)
    y = _diff_finish(_attend(q1, ks, vs, scale), _attend(q2, ks, vs, scale), lam_ref, g_ref, out_scale)
    o_ref[...] = y.astype(o_ref.dtype)


def _mla_attention(q, kv, p, cos_t, sin_t, *, n_batch, t_len, c_len, tq, with_ctx):
    n_x = n_batch * t_len
    nq = t_len // tq
    out_rows = n_x + (n_batch * c_len if with_ctx else 0)
    cblk0 = n_x // c_len
    scale = (MLA_NOPE + MLA_ROPE) ** -0.5
    qmap = lambda b, h, i: (b * nq + i, h)
    full = lambda b, h, i: (0, 0)
    vmem = (4 * t_len * LANES * 4 + 2 * tq * 256 * 2 + 2 * (t_len + c_len) * LANES * (2 + 4 + 2)
            + (t_len + c_len) * 256 * 2 + 2 * tq * LANES * 2 + 6 * tq * (t_len + c_len) * 4)
    y = pl.pallas_call(
        functools.partial(_mla_kernel, tq=tq, scale=scale),
        grid=(n_batch, MLA_HEADS, nq),
        in_specs=[pl.BlockSpec((t_len, LANES), full), pl.BlockSpec((t_len, LANES), full),
                  pl.BlockSpec((tq, MLA_QK_PAD), qmap),
                  pl.BlockSpec((t_len, MLA_NOPE), lambda b, h, i: (b, h)),
                  pl.BlockSpec((c_len, MLA_NOPE), lambda b, h, i: (cblk0 + b, h)),
                  pl.BlockSpec((t_len, LANES), lambda b, h, i: (b, C_KROPE // LANES)),
                  pl.BlockSpec((c_len, LANES), lambda b, h, i: (cblk0 + b, C_KROPE // LANES)),
                  pl.BlockSpec((t_len, MLA_V), lambda b, h, i: (b, MLA_HEADS + h)),
                  pl.BlockSpec((c_len, MLA_V), lambda b, h, i: (cblk0 + b, MLA_HEADS + h))],
        out_specs=pl.BlockSpec((tq, MLA_V), qmap),
        out_shape=jax.ShapeDtypeStruct((out_rows, MLA_HEADS * MLA_V), BF16),
        scratch_shapes=[pltpu.VMEM((t_len, MLA_QK_PAD), BF16), pltpu.VMEM((c_len, MLA_QK_PAD), BF16)],
        compiler_params=_cparams(("parallel", "parallel", "arbitrary"), vmem),
        name="mla_attention",
    )(cos_t, sin_t, q, kv, kv, p, p, kv, kv)
    if not with_ctx:
        return y
    cmap = lambda b, h: (cblk0 + b, h)
    return pl.pallas_call(
        functools.partial(_mla_ctx_kernel, scale=scale),
        grid=(n_batch, MLA_HEADS),
        in_specs=[pl.BlockSpec((c_len, MLA_QK_PAD), cmap),
                  pl.BlockSpec((c_len, MLA_NOPE), cmap),
                  pl.BlockSpec((c_len, LANES), lambda b, h: (cblk0 + b, C_KROPE // LANES)),
                  pl.BlockSpec((c_len, MLA_V), lambda b, h: (cblk0 + b, MLA_HEADS + h)),
                  pl.BlockSpec(memory_space=pl.ANY)],
        out_specs=pl.BlockSpec((c_len, MLA_V), cmap),
        out_shape=jax.ShapeDtypeStruct(y.shape, y.dtype),
        input_output_aliases={4: 0},
        compiler_params=_cparams(("parallel", "parallel"), 16 * c_len * c_len * 4 + 8 * c_len * 256 * 4),
        name="mla_attention_ctx",
    )(q, kv, p, kv, y)


def _diff_attention(p, cos_t, sin_t, lam_row, g_row, *, n_batch, t_len, c_len, tq, with_ctx, out_scale):
    n_x = n_batch * t_len
    nq = t_len // tq
    out_rows = n_x + (n_batch * c_len if with_ctx else 0)
    cblk0 = n_x // c_len
    scale = DIFF_QK ** -0.5
    full = lambda b, h, i: (0, 0)
    qblk, kblk, vblk = C_DQ // LANES, C_DK // LANES, C_DV // LANES
    vmem = (4 * t_len * LANES * 4 + 2 * tq * LANES * 4 + 4 * (t_len + c_len) * LANES * 4
            + 2 * (t_len + c_len) * LANES * 2 + 2 * tq * LANES * 2 + 8 * tq * (t_len + c_len) * 4)
    y = pl.pallas_call(
        functools.partial(_diff_kernel, tq=tq, scale=scale, out_scale=out_scale),
        grid=(n_batch, DIFF_HEADS, nq),
        in_specs=[pl.BlockSpec((t_len, LANES), full), pl.BlockSpec((t_len, LANES), full),
                  pl.BlockSpec((1, DIFF_V), full), pl.BlockSpec((1, DIFF_V), full),
                  pl.BlockSpec((tq, LANES), lambda b, h, i: (b * nq + i, qblk + h)),
                  pl.BlockSpec((t_len, LANES), lambda b, h, i: (b, kblk + h)),
                  pl.BlockSpec((c_len, LANES), lambda b, h, i: (cblk0 + b, kblk + h)),
                  pl.BlockSpec((t_len, LANES), lambda b, h, i: (b, vblk + h)),
                  pl.BlockSpec((c_len, LANES), lambda b, h, i: (cblk0 + b, vblk + h))],
        out_specs=pl.BlockSpec((tq, DIFF_V), lambda b, h, i: (b * nq + i, h)),
        out_shape=jax.ShapeDtypeStruct((out_rows, DIFF_HEADS * DIFF_V), BF16),
        scratch_shapes=[pltpu.VMEM((t_len, LANES), BF16), pltpu.VMEM((c_len, LANES), BF16),
                        pltpu.VMEM((t_len, LANES), BF16), pltpu.VMEM((c_len, LANES), BF16)],
        compiler_params=_cparams(("parallel", "parallel", "arbitrary"), vmem),
        name="diff_attention",
    )(cos_t, sin_t, lam_row, g_row, p, p, p, p, p)
    if not with_ctx:
        return y
    one = lambda b, h: (0, 0)
    return pl.pallas_call(
        functools.partial(_diff_ctx_kernel, scale=scale, out_scale=out_scale),
        grid=(n_batch, DIFF_HEADS),
        in_specs=[pl.BlockSpec((1, DIFF_V), one), pl.BlockSpec((1, DIFF_V), one),
                  pl.BlockSpec((c_len, LANES), lambda b, h: (cblk0 + b, qblk + h)),
                  pl.BlockSpec((c_len, LANES), lambda b, h: (cblk0 + b, kblk + h)),
                  pl.BlockSpec((c_len, LANES), lambda b, h: (cblk0 + b, vblk + h)),
                  pl.BlockSpec(memory_space=pl.ANY)],
        out_specs=pl.BlockSpec((c_len, DIFF_V), lambda b, h: (cblk0 + b, h)),
        out_shape=jax.ShapeDtypeStruct(y.shape, y.dtype),
        input_output_aliases={5: 0},
        compiler_params=_cparams(("parallel", "parallel"), 24 * c_len * c_len * 4 + 8 * c_len * LANES * 4),
        name="diff_attention_ctx",
    )(lam_row, g_row, p, p, p, y)


def _prep_kernel(r_ref, k_ref, v_ref, cb_ref, cc_ref, cu_ref, lo_ref, hp_ref, hn_ref,
                 mu_ref, mul_ref, kk_w_ref, ka_w_ref, rk_w_ref, w0_ref, a0_ref, w2_ref, a2_ref, g2_ref,
                 cw_ref,
                 r_o, kk_o, v_o, lw_o, kd_o, ka_o, bonus_o, g_o, conv_o, *, tr, n_x, t_len, c_len):
    i = pl.program_id(0)
    g0 = i * tr
    seq = jnp.where(g0 < n_x, t_len, c_len)
    has_prev = ((g0 % seq) != 0).astype(F32)
    has_next = (((g0 + tr) % seq) != 0).astype(F32)
    row = lax.broadcasted_iota(jnp.int32, (tr, 1), 0)

    def neighbours(x, col0):
        width = x.shape[1]
        before = hp_ref[HALO - 1:HALO, col0:col0 + width] * has_prev
        after = hn_ref[0:1, col0:col0 + width] * has_next
        prev = jnp.where(row == 0, before, pltpu.roll(x, 1, 0))
        nxt = jnp.where(row == tr - 1, after, pltpu.roll(x, tr - 1, 0))
        return prev, nxt

    def shifted(x, col0, mu0, mu1):
        prev, nxt = neighbours(x, col0)
        return x + mu0 * (prev - x) + mu1 * (nxt - x)

    r = shifted(r_ref[...], C_R, mu_ref[0:1, :BRANCH_W], mu_ref[1:2, :BRANCH_W])
    k = shifted(k_ref[...], C_K, mu_ref[0:1, BRANCH_W:2 * BRANCH_W], mu_ref[1:2, BRANCH_W:2 * BRANCH_W])
    v = shifted(v_ref[...], C_V, mu_ref[0:1, 2 * BRANCH_W:], mu_ref[1:2, 2 * BRANCH_W:])
    lo = shifted(lo_ref[...], C_LORA, mul_ref[0:1, :], mul_ref[1:2, :])
    wd = jnp.tanh(lo[:, :2 * RWKV_LORA])
    ad = lo[:, 2 * RWKV_LORA:4 * RWKV_LORA]
    gd = jax.nn.sigmoid(lo[:, 4 * RWKV_LORA:])
    w_pre = _dot(wd, w2_ref[...]) + w0_ref[...]
    a_sig = jax.nn.sigmoid(_dot(ad, a2_ref[...]) + a0_ref[...])
    g_o[...] = _dot(gd, g2_ref[...])
    w_log = -(jnp.maximum(-w_pre, 0.0) + jnp.log(1.0 + jnp.exp(-jnp.abs(w_pre)))) - 0.5
    lw = -jnp.exp(w_log)
    kkf = k * kk_w_ref[...]
    kk = kkf * lax.rsqrt(_head_sums(kkf * kkf) + 1e-12)
    r_o[...] = r
    kk_o[...] = kk
    v_o[...] = v
    k_sum = None
    for d in range(2):
        a_d = a_sig[:, d * BRANCH_W:(d + 1) * BRANCH_W]
        k_d = k * (1.0 + (a_d - 1.0) * ka_w_ref[...])
        lw_o[d] = lw[:, d * BRANCH_W:(d + 1) * BRANCH_W]
        kd_o[d] = k_d
        ka_o[d] = kk * a_d
        k_sum = k_d if k_sum is None else k_sum + k_d
    bonus_o[...] = _head_sums(r * k_sum * rk_w_ref[...]) * v

    z = cc_ref[...] * cu_ref[...]
    z_before = hp_ref[HALO - 1:HALO, C_CC:C_CC + BRANCH_W] * hp_ref[HALO - 1:HALO, C_CU:C_CU + BRANCH_W] * has_prev
    z_after = hn_ref[0:1, C_CC:C_CC + BRANCH_W] * hn_ref[0:1, C_CU:C_CU + BRANCH_W] * has_next
    z_prev = jnp.where(row == 0, z_before, pltpu.roll(z, 1, 0))
    z_next = jnp.where(row == tr - 1, z_after, pltpu.roll(z, tr - 1, 0))
    y = cb_ref[...] * (cw_ref[0:1, :] * z_prev + cw_ref[1:2, :] * z + cw_ref[2:3, :] * z_next)
    conv_o[...] = y.astype(conv_o.dtype)


def _mixer_prep(p, mu_rkv, mu_lora, kk_w, ka_w, rk_w, w0, a0, w2bd, a2bd, g2p, conv_w, *, tr, n_x, t_len,
                c_len):
    n_rows = p.shape[0]
    bw = BRANCH_W
    last_halo = n_rows // HALO - 1
    col = lambda c: (lambda i: (i, c))
    const = lambda i: (0, 0)
    main = [pl.BlockSpec((tr, bw), col(C_R // bw)), pl.BlockSpec((tr, bw), col(C_K // bw)),
            pl.BlockSpec((tr, bw), col(C_V // bw)), pl.BlockSpec((tr, bw), col(C_CB // bw)),
            pl.BlockSpec((tr, bw), col(C_CC // bw)), pl.BlockSpec((tr, bw), col(C_CU // bw)),
            pl.BlockSpec((tr, RWKV_LORA_PAD), col(C_LORA // RWKV_LORA_PAD)),
            pl.BlockSpec((HALO, P_COLS), lambda i: (jnp.maximum(i * (tr // HALO) - 1, 0), 0)),
            pl.BlockSpec((HALO, P_COLS), lambda i: (jnp.minimum((i + 1) * (tr // HALO), last_halo), 0))]
    params = [mu_rkv, mu_lora, kk_w, ka_w, rk_w, w0, a0, w2bd, a2bd, g2p, conv_w]
    param_specs = [pl.BlockSpec(a.shape, const) for a in params]
    row_spec = pl.BlockSpec((tr, bw), lambda i: (i, 0))
    dir_spec = pl.BlockSpec((2, tr, bw), lambda i: (0, i, 0))
    f32_rows = jax.ShapeDtypeStruct((n_rows, bw), F32)
    f32_dirs = jax.ShapeDtypeStruct((2, n_rows, bw), F32)
    vmem = 2 * (7 * tr * bw * 4 + 2 * HALO * P_COLS * 4 + 12 * tr * bw * 4) + 30 * tr * bw * 4
    return pl.pallas_call(
        functools.partial(_prep_kernel, tr=tr, n_x=n_x, t_len=t_len, c_len=c_len),
        grid=(n_rows // tr,),
        in_specs=main + param_specs,
        out_specs=[row_spec, row_spec, row_spec, dir_spec, dir_spec, dir_spec, row_spec, row_spec, row_spec],
        out_shape=[f32_rows, f32_rows, f32_rows, f32_dirs, f32_dirs, f32_dirs, f32_rows, f32_rows,
                   jax.ShapeDtypeStruct((n_rows, bw), BF16)],
        compiler_params=_cparams(("parallel",), vmem),
        name="mixer_prep",
    )(p, p, p, p, p, p, p, p, p, *params)


def _rwkv_kernel(r0_ref, kk0_ref, v0_ref, r1_ref, kk1_ref, v1_ref, lw0_ref, kd0_ref, ka0_ref,
                 lw1_ref, kd1_ref, ka1_ref, o0_ref, o1_ref, s_ref):
    C = RWKV_CHUNK
    W = 2 * RWKV_HEAD
    n_pairs = RWKV_HEADS // 2

    @pl.when(pl.program_id(1) == 0)
    def _():
        s_ref[...] = jnp.zeros_like(s_ref)

    mm, mm_nt = _dot, _dot_nt
    lane = lax.broadcasted_iota(jnp.int32, (C, W), 1)
    row = lax.broadcasted_iota(jnp.int32, (C, W), 0)
    first = lane < RWKV_HEAD
    rc = lax.broadcasted_iota(jnp.int32, (C, C), 0)
    cc = lax.broadcasted_iota(jnp.int32, (C, C), 1)

    def bdiag(x):
        return jnp.concatenate([jnp.where(first, x, jnp.zeros_like(x)), jnp.where(first, jnp.zeros_like(x), x)],
                               axis=0)

    def direction(sign, r_ref, kk_ref, v_ref, lw_ref, kd_ref, ka_ref):
        m_incl = jnp.where((rc - cc) * sign >= 0, 1.0, 0.0).astype(BF16)
        lw = lw_ref[0]
        cum = _dot_exact_lhs(m_incl, lw)
        tot = jnp.sum(lw, axis=0, keepdims=True)
        e_ninc = jnp.exp(-cum)
        e_rem = jnp.exp(tot - cum)
        kk, ka, kd = kk_ref[...], ka_ref[0], kd_ref[0]
        order = (row - (lane % RWKV_HEAD)) * sign
        return dict(a_t=kk * jnp.exp(cum - lw), b_t=-ka * e_ninc, k_t=kd * e_ninc, r_t=r_ref[...] * jnp.exp(cum),
                    b_h=-ka * e_rem, k_h=kd * e_rem, v=v_ref[...], e_tot=jnp.exp(tot), strict=order > 0,
                    incl=order >= 0, eye=jnp.where(order == 0, 1.0, 0.0).astype(F32))

    dirs = (direction(1, r0_ref, kk0_ref, v0_ref, lw0_ref, kd0_ref, ka0_ref),
            direction(-1, r1_ref, kk1_ref, v1_ref, lw1_ref, kd1_ref, ka1_ref))
    o_refs = (o0_ref, o1_ref)

    chains = [(d, p) for d in range(2) for p in range(n_pairs)]
    n = range(len(chains))
    sl = [slice(W * p, W * (p + 1)) for (_, p) in chains]
    dd = [dirs[d] for (d, _) in chains]
    ar = [jnp.concatenate([dd[c]["a_t"][:, sl[c]], dd[c]["r_t"][:, sl[c]]], axis=0).astype(BF16) for c in n]
    a_bd = [bdiag(dd[c]["a_t"][:, sl[c]].astype(BF16)) for c in n]
    bk_bd = [jnp.concatenate([bdiag(dd[c]["b_t"][:, sl[c]].astype(BF16)), bdiag(dd[c]["k_t"][:, sl[c]].astype(BF16))],
                             axis=0) for c in n]
    v_bd = [bdiag(dd[c]["v"][:, sl[c]].astype(BF16)) for c in n]
    g = [mm_nt(ar[c], bk_bd[c]) for c in n]
    l_pow = [jnp.where(dd[c]["strict"], g[c][:C, :W], 0.0) for c in n]
    m_ak = [jnp.where(dd[c]["strict"], g[c][:C, W:], 0.0).astype(BF16) for c in n]
    a_r = [jnp.where(jnp.concatenate([dd[c]["incl"], dd[c]["incl"]], axis=1), g[c][C:], 0.0).astype(BF16) for c in n]
    mv = [mm(m_ak[c], v_bd[c]) for c in n]
    l_bd = [bdiag(l_pow[c].astype(BF16)) for c in n]
    t_inv = [dd[c]["eye"] + l_pow[c] for c in n]
    l_pow = [mm(l_pow[c], l_bd[c]) for c in n]
    for _ in range(int(math.log2(C)) - 2):
        l_bd = [bdiag(l_pow[c].astype(BF16)) for c in n]
        lt = [mm(jnp.concatenate([l_pow[c], t_inv[c]], axis=0), l_bd[c]) for c in n]
        l_pow = [lt[c][:C] for c in n]
        t_inv = [t_inv[c] + lt[c][C:] for c in n]
    t_inv = [t_inv[c] + mm(t_inv[c], bdiag(l_pow[c].astype(BF16))) for c in n]
    wu = [mm(t_inv[c], jnp.concatenate([a_bd[c], bdiag(mv[c].astype(BF16))], axis=1)) for c in n]
    s_bd = [s_ref[d, p] for (d, p) in chains]
    wr = [mm(jnp.concatenate([wu[c][:, :W], dd[c]["r_t"][:, sl[c]]], axis=0), s_bd[c]) for c in n]
    zv = [jnp.concatenate([bdiag((wr[c][:C] + wu[c][:, W:]).astype(BF16)), v_bd[c]], axis=0) for c in n]
    for c, (d, p) in enumerate(chains):
        o_refs[d][:, sl[c]] = wr[c][C:] + mm(a_r[c], zv[c])
    for c, (d, p) in enumerate(chains):
        bk_h = jnp.concatenate([bdiag(dd[c]["b_h"][:, sl[c]]), bdiag(dd[c]["k_h"][:, sl[c]])], axis=0)
        tot_col = jnp.broadcast_to(dd[c]["e_tot"][:, sl[c]], (W, W)).T
        s_ref[d, p] = tot_col * s_bd[c] + mm(bk_h.T, zv[c])


def _rwkv_scan(r, kk, v, lw, kd, ka, *, n_batch, t_len, c_len):
    C = RWKV_CHUNK
    rows, width = r.shape
    nct, nxt = c_len // C, t_len // C
    ctx_blk0 = n_batch * t_len // C

    def blk(b, d, s):
        j_c = s if d == 0 else nct - 1 - s
        j_x = s - nct if d == 0 else nxt - 1 - (s - nct)
        return jnp.where(s < nct, ctx_blk0 + b * nct + j_c, b * nxt + j_x)

    def shared(d):
        return pl.BlockSpec((C, width), lambda b, s: (blk(b, d, s), 0))

    def per_dir(d):
        return pl.BlockSpec((1, C, width), lambda b, s: (d, blk(b, d, s), 0))

    vmem = 2 * 14 * C * width * 4 + RWKV_HEADS * 128 * 128 * 4 + 128 * C * width * 4
    out = jax.ShapeDtypeStruct((rows, width), F32)
    return pl.pallas_call(
        _rwkv_kernel,
        grid=(n_batch, nct + nxt),
        in_specs=[shared(0), shared(0), shared(0), shared(1), shared(1), shared(1),
                  per_dir(0), per_dir(0), per_dir(0), per_dir(1), per_dir(1), per_dir(1)],
        out_specs=[shared(0), shared(1)],
        out_shape=[out, out],
        scratch_shapes=[pltpu.VMEM((2, RWKV_HEADS // 2, 128, 128), F32)],
        compiler_params=_cparams(("parallel", "arbitrary"), vmem),
        name="rwkv_scan",
    )(r, kk, v, r, kk, v, lw, kd, ka, lw, kd, ka)


def _rwkv_readout_kernel(o0_ref, o1_ref, bonus_ref, g_ref, lng_ref, lnb_ref, y_ref):
    o = o0_ref[...] + o1_ref[...]
    mean = _head_sums(o) * (1.0 / RWKV_HEAD)
    cen = o - mean
    var = _head_sums(cen * cen) * (1.0 / RWKV_HEAD)
    o_n = cen * lax.rsqrt(var + RWKV_GN_EPS) * lng_ref[...] + lnb_ref[...]
    y_ref[...] = ((o_n + bonus_ref[...]) * g_ref[...]).astype(y_ref.dtype)


def _rwkv_readout(o_dirs, bonus, g_out, ln_g, ln_b, *, tr):
    n_rows, bw = bonus.shape
    row_spec = pl.BlockSpec((tr, bw), lambda i: (i, 0))
    const = pl.BlockSpec((1, bw), lambda i: (0, 0))
    return pl.pallas_call(
        _rwkv_readout_kernel,
        grid=(n_rows // tr,),
        in_specs=[row_spec, row_spec, row_spec, row_spec, const, const],
        out_specs=row_spec,
        out_shape=jax.ShapeDtypeStruct((n_rows, bw), BF16),
        compiler_params=_cparams(("parallel",), 2 * 5 * tr * bw * 4 + 16 * tr * bw * 4),
        name="rwkv_readout",
    )(o_dirs[0], o_dirs[1], bonus, g_out, ln_g.reshape(1, bw), ln_b.reshape(1, bw))


def _rmsnorm_kernel(x_ref, g_ref, o_ref):
    x = x_ref[...]
    o_ref[...] = x * lax.rsqrt(jnp.mean(x * x, axis=-1, keepdims=True) + NORM_EPS) * g_ref[...]


def _final_norm(xs, g, *, rows, tm):
    dm = xs.shape[1]
    return pl.pallas_call(
        _rmsnorm_kernel,
        grid=(rows // tm,),
        in_specs=[pl.BlockSpec((tm, dm), lambda i: (i, 0)), pl.BlockSpec((1, dm), lambda i: (0, 0))],
        out_specs=pl.BlockSpec((tm, dm), lambda i: (i, 0)),
        out_shape=jax.ShapeDtypeStruct((rows, dm), F32),
        compiler_params=_cparams(("parallel",), 4 * tm * dm * 4),
        name="final_norm",
    )(xs, g.reshape(1, dm))


def _rope_tables128(n_tokens):
    rows = n_tokens // GRID_W
    row = jnp.repeat(jnp.arange(rows, dtype=F32), GRID_W)
    col = jnp.tile(jnp.arange(GRID_W, dtype=F32), rows)
    n_freq = 64 // 4
    inv = ROPE_BASE ** (-jnp.arange(n_freq, dtype=F32) / n_freq)
    ang = jnp.concatenate([row[:, None] * inv, col[:, None] * inv], axis=-1)
    cos, sin = jnp.cos(ang), jnp.sin(ang)
    return jnp.concatenate([cos, cos, cos, cos], axis=-1), jnp.concatenate([-sin, sin, -sin, sin], axis=-1)


def _block_diag2(w2):
    z = jnp.zeros_like(w2[0])
    return jnp.concatenate([jnp.concatenate([w2[0], z], axis=1), jnp.concatenate([z, w2[1]], axis=1)], axis=0)


def kernel(x, c, ctx, c_ctx, norm1_g, norm2_g, mod_down, mod_up, mod_b, w_in, mla_q_norm_g, mla_w_uq,
           mla_kv_norm_g, mla_w_ukv, rwkv_mu, rwkv_w0, rwkv_w2, rwkv_a0, rwkv_a2, rwkv_g2, rwkv_k_k,
           rwkv_k_a, rwkv_r_k, rwkv_ln_g, rwkv_ln_b, conv_w, diff_lambda, diff_norm_g, w_branch, gate_down,
           gate_up, gate_b, w_out, mlp_w1, mlp_w2, final_norm_g):
    n_batch, t_len, dm = x.shape
    c_len = ctx.shape[1]
    depth = w_in.shape[0]
    bw = BRANCH_W
    n_x = n_batch * t_len
    n_c = n_batch * c_len
    n_rows = n_x + n_c
    tm = 512 if (t_len % 512 == 0 and n_c % 512 == 0) else 256
    tq = 512 if t_len % 512 == 0 else min(256, c_len)
    tr = min(256, c_len)
    assert dm == D_MODEL and t_len % tm == 0 and n_c % tm == 0 and n_x % c_len == 0
    assert t_len % tq == 0 and c_len % RWKV_CHUNK == 0 and t_len % c_len == 0
    tm_big = 1024 if (t_len % 1024 == 0 and n_c % 1024 == 0) else tm
    mod_index = functools.partial(_mod_index, tm=tm, n_x_rows=n_x, t_len=t_len, n_batch=n_batch)
    mod_index_big = functools.partial(_mod_index, tm=tm_big, n_x_rows=n_x, t_len=t_len, n_batch=n_batch)
    mod_index_nm = functools.partial(_mod_index, tm=256, n_x_rows=n_x, t_len=t_len, n_batch=n_batch)

    zeros = lambda *s: jnp.zeros(s, F32)
    lora_w = 4 * RWKV_LORA + RWKV_GATE_LORA
    rw0 = MLA_Q_LORA + MLA_KV_LORA + MLA_ROPE
    cv0 = rw0 + 3 * bw + lora_w
    w_in_t = jnp.swapaxes(w_in, 1, 2)
    w_in_p = jnp.concatenate(
        [w_in_t[:, :C_R], w_in_t[:, rw0:rw0 + 3 * bw], w_in_t[:, cv0:],
         w_in_t[:, rw0 + 3 * bw:cv0], zeros(depth, RWKV_LORA_PAD - lora_w, dm), jnp.swapaxes(gate_down, 1, 2),
         w_in_t[:, C_R:rw0], zeros(depth, P_COLS - C_KROPE - MLA_ROPE, dm)], axis=1).astype(BF16)
    w_uq_p = jnp.pad(mla_w_uq.reshape(depth, MLA_Q_LORA, MLA_HEADS, MLA_NOPE + MLA_ROPE),
                     ((0, 0), (0, 0), (0, 0), (0, MLA_QK_PAD - MLA_NOPE - MLA_ROPE))
                     ).reshape(depth, MLA_Q_LORA, MLA_HEADS * MLA_QK_PAD).astype(BF16)
    w_ukv_r = mla_w_ukv.reshape(depth, MLA_KV_LORA, MLA_HEADS, MLA_NOPE + MLA_V)
    w_ukv_p = jnp.concatenate([w_ukv_r[..., :MLA_NOPE].reshape(depth, MLA_KV_LORA, -1),
                               w_ukv_r[..., MLA_NOPE:].reshape(depth, MLA_KV_LORA, -1)], axis=-1).astype(BF16)
    w_branch_2d = w_branch.reshape(depth, 4 * bw, dm)
    gate_up_b = jnp.moveaxis(gate_up, 2, 1).astype(BF16)
    g2_p = jnp.pad(rwkv_g2, ((0, 0), (0, RWKV_LORA_PAD - 4 * RWKV_LORA - RWKV_GATE_LORA), (0, 0))).astype(BF16)
    mu_lora = jnp.pad(rwkv_mu[:, :, 3 * bw:], ((0, 0), (0, 0), (0, RWKV_LORA_PAD - lora_w)))

    cond = jnp.concatenate([c, c_ctx[None, :], zeros(16 - n_batch - 1, dm)], axis=0)
    cond = jax.nn.silu(cond)
    mods = []
    for l in range(depth):
        low = _matmul(cond, mod_down[l], tm=16, tn=MOD_RANK, tk=dm, name="mod_down")
        up = _matmul(low, mod_up[l], tm=16, tn=2048, tk=MOD_RANK, bias=mod_b[l], name="mod_up")
        mods.append(up.reshape(16, N_MOD, dm))

    cos_t, sin_t = _rope_tables128(t_len)
    xs = jnp.concatenate([x.reshape(n_x, dm), ctx.reshape(n_c, dm)], axis=0)

    for l in range(depth):
        need_ctx = l < depth - 1
        mod = mods[l]
        lam_init = 0.8 - 0.6 * math.exp(-0.3 * l)
        lq1, lk1, lq2, lk2 = diff_lambda[l]
        lam = jnp.exp(jnp.sum(lq1 * lk1)) - jnp.exp(jnp.sum(lq2 * lk2)) + lam_init
        lam_row = jnp.full((1, DIFF_V), 1.0, F32) * lam

        h1 = _norm_mod(xs, norm1_g[l], mod, shift_row=0, scale_row=1, mod_index=mod_index_nm, tm=256,
                       rows=n_rows)
        p, w1_b, w_out_b, w_branch_b = _fullk_matmul(h1, w_in_p, layer=l, tm=tm_big, tn=512, out_dtype=F32,
                                                     name="in_proj", cast_srcs=(mlp_w1, w_out, w_branch_2d),
                                                     w_rows_are_outputs=True)

        q = _nm_matmul(p, mla_q_norm_g[l], w_uq_p[l], tm=tm, tn=512, out_dtype=BF16,
                       x_col_block=C_CQ // MLA_Q_LORA, kdim=MLA_Q_LORA, name="mla_q")
        kv = _nm_matmul(p, mla_kv_norm_g[l], w_ukv_p[l], tm=tm, tn=512, out_dtype=BF16,
                        x_col_block=C_CKV // MLA_KV_LORA, kdim=MLA_KV_LORA, name="mla_kv")
        y_mla = _mla_attention(q, kv, p, cos_t, sin_t, n_batch=n_batch, t_len=t_len, c_len=c_len, tq=tq,
                               with_ctx=need_ctx)

        y_diff = _diff_attention(p, cos_t, sin_t, lam_row, diff_norm_g[l].reshape(1, DIFF_V),
                                 n_batch=n_batch, t_len=t_len, c_len=c_len, tq=tq, with_ctx=need_ctx,
                                 out_scale=1.0 - lam_init)

        r_, kk, v_, lw, k_dir, kka, bonus, g_out, y_conv = _mixer_prep(
            p, rwkv_mu[l, :, :3 * bw], mu_lora[l], rwkv_k_k[l].reshape(1, bw), rwkv_k_a[l].reshape(1, bw),
            rwkv_r_k[l].reshape(1, bw), rwkv_w0[l].reshape(1, 2 * bw), rwkv_a0[l].reshape(1, 2 * bw),
            _block_diag2(rwkv_w2[l]).astype(BF16), _block_diag2(rwkv_a2[l]).astype(BF16), g2_p[l],
            conv_w[l], tr=min(128, tr), n_x=n_x, t_len=t_len, c_len=c_len)
        o_dirs = _rwkv_scan(r_, kk, v_, lw, k_dir, kka, n_batch=n_batch, t_len=t_len, c_len=c_len)
        y_rwkv = _rwkv_readout(o_dirs, bonus, g_out, rwkv_ln_g[l], rwkv_ln_b[l], tr=tr)

        rows = n_rows if need_ctx else n_x
        acc = _merge(p, (y_mla, y_rwkv, y_conv, y_diff), w_branch_b, gate_up_b, gate_b, layer=l,
                     tm=tm_big, tn=512, rows=rows)
        (xs_new,) = _fullk_matmul(acc, w_out_b[None], layer=0, tm=tm_big, tn=512, out_dtype=F32, res=xs, mod=mod,
                                  gate_row=2, mod_index=mod_index_big, rows=rows, name="out_proj")

        h2 = _norm_mod(xs_new, norm2_g[l], mod, shift_row=3, scale_row=4, mod_index=mod_index_nm, tm=256,
                       rows=rows)
        hid, w2_b = _fullk_matmul(h2, w1_b[None], layer=0, tm=tm_big, tn=TN_UP, out_dtype=BF16, act="relu2",
                                  name="mlp_up", cast_srcs=(mlp_w2,), cast_layer=l)
        xs = _matmul(hid, w2_b, tm=tm_big, tn=1024, tk=2048, res=xs_new, mod=mod, gate_row=5,
                     mod_index=mod_index_big, rows=rows, name="mlp_down")

    out = _final_norm(xs, final_norm_g, rows=n_x, tm=tm)
    return out.reshape(n_batch, t_len, dm)
```

```python
import functools
import math

import jax
import jax.numpy as jnp
from jax import lax
from jax.experimental import pallas as pl
from jax.experimental.pallas import tpu as pltpu

F32 = jnp.float32
BF16 = jnp.bfloat16

D_MODEL = 4096
BRANCH_W = 1024
GRID_W = 64
ROPE_BASE = 10000.0
NORM_EPS = 1e-6
N_MOD = 6
LANES = 128
HALO = 8
LOG2_E = 1.4426950408889634

MLA_HEADS = 8
MLA_NOPE = 128
MLA_ROPE = 64
MLA_V = 128
MLA_Q_LORA = 768
MLA_KV_LORA = 256
MLA_QK_PAD = 256

RWKV_HEAD = 64
RWKV_HEADS = 16
RWKV_LORA = 64
RWKV_GATE_LORA = 160
RWKV_GN_EPS = 64e-5
RWKV_CHUNK = 64
RWKV_LORA_PAD = 512

DIFF_HEADS = 8
DIFF_QK = 64
DIFF_V = 128
GATE_RANK = 256
MOD_RANK = 256

C_CQ = 0
C_CKV = 768
C_R = 1024
C_K = 2048
C_V = 3072
C_CB = 4096
C_CC = 5120
C_CU = 6144
C_DQ = 7168
C_DK = 8192
C_DV = 9216
C_LORA = 10240
C_GL = 10752
C_KROPE = 11008
P_COLS = 11264

TN_UP = 1024
ATTN_SUB = 128
VMEM_CAP = 56 * 1024 * 1024


def _cparams(sem, vmem_bytes):
    limit = int(min(VMEM_CAP, max(vmem_bytes * 1.5 + (4 << 20), 16 << 20)))
    return pltpu.CompilerParams(dimension_semantics=sem, vmem_limit_bytes=limit)


def _mod_index(i, tm, n_x_rows, t_len, n_batch):
    return jnp.where(i < n_x_rows // tm, i // (t_len // tm), n_batch)


def _dot(a, b):
    return jnp.dot(a.astype(BF16), b.astype(BF16), preferred_element_type=F32)


def _dot_nt(a, b):
    return lax.dot_general(a.astype(BF16), b.astype(BF16), (((1,), (1,)), ((), ())),
                           preferred_element_type=F32)


def _split3(x):
    hi = x.astype(BF16)
    r1 = x - hi.astype(F32)
    mid = r1.astype(BF16)
    lo = (r1 - mid.astype(F32)).astype(BF16)
    return hi, mid, lo


def _dot_exact_lhs(m_bf16, x):
    out = None
    for part in _split3(x):
        t = jnp.dot(m_bf16, part, preferred_element_type=F32)
        out = t if out is None else out + t
    return out


def _head_sums(x):
    r = lax.broadcasted_iota(jnp.int32, (LANES, LANES), 0)
    c = lax.broadcasted_iota(jnp.int32, (LANES, LANES), 1)
    ones_bd = jnp.where((r // RWKV_HEAD) == (c // RWKV_HEAD), 1.0, 0.0).astype(BF16)
    parts = _split3(x)
    cols = []
    for j in range(x.shape[1] // LANES):
        acc = None
        for part in parts:
            t = jnp.dot(part[:, j * LANES:(j + 1) * LANES], ones_bd, preferred_element_type=F32)
            acc = t if acc is None else acc + t
        cols.append(acc)
    return jnp.concatenate(cols, axis=1)


def _rope128(x, cos_t, sin_t):
    lane = lax.broadcasted_iota(jnp.int32, x.shape, 1)
    swapped = jnp.where((lane % 64) < 32, pltpu.roll(x, 96, 1), pltpu.roll(x, 32, 1))
    return x * cos_t + swapped * sin_t


def _matmul_kernel(*refs, nk, has_bias, has_res, gate_row, act):
    a_ref, w_ref = refs[0], refs[1]
    pos = 2
    bias_ref = res_ref = mod_ref = None
    if has_bias:
        bias_ref = refs[pos]; pos += 1
    if has_res:
        res_ref = refs[pos]; mod_ref = refs[pos + 1]; pos += 2
    o_ref, acc_ref = refs[pos], refs[pos + 1]
    k = pl.program_id(2)

    @pl.when(k == 0)
    def _():
        acc_ref[...] = jnp.zeros_like(acc_ref)

    acc_ref[...] += _dot(a_ref[...], w_ref[...])

    @pl.when(k == nk - 1)
    def _():
        y = acc_ref[...]
        if has_bias:
            y = y + bias_ref[...]
        if act == "relu2":
            y = jnp.square(jnp.maximum(y, 0.0))
        if has_res:
            y = res_ref[...] + mod_ref[0, gate_row:gate_row + 1, :] * y
        o_ref[...] = y.astype(o_ref.dtype)


def _matmul(a, w, *, tm, tn, tk, out_dtype=F32, bias=None, res=None, mod=None, gate_row=None,
            mod_index=None, act=None, rows=None, name=None):
    m = a.shape[0] if rows is None else rows
    kdim, n = w.shape
    assert a.shape[1] == kdim and m % tm == 0 and n % tn == 0 and kdim % tk == 0
    nk = kdim // tk
    in_specs = [pl.BlockSpec((tm, tk), lambda i, j, k: (i, k)),
                pl.BlockSpec((tk, tn), lambda i, j, k: (k, j))]
    args = [a, w]
    vmem = 2 * tm * tk * a.dtype.itemsize + 2 * tk * tn * w.dtype.itemsize + tm * tn * 4
    vmem += 2 * tm * tn * jnp.dtype(out_dtype).itemsize
    if bias is not None:
        in_specs.append(pl.BlockSpec((1, tn), lambda i, j, k: (0, j)))
        args.append(bias.reshape(1, n).astype(F32))
    if res is not None:
        in_specs.append(pl.BlockSpec((tm, tn), lambda i, j, k: (i, j)))
        in_specs.append(pl.BlockSpec((1, N_MOD, tn), lambda i, j, k: (mod_index(i), 0, j)))
        args += [res, mod]
        vmem += 2 * tm * tn * 4 + 2 * 8 * tn * 4
    kern = functools.partial(_matmul_kernel, nk=nk, has_bias=bias is not None,
                             has_res=res is not None, gate_row=gate_row, act=act)
    return pl.pallas_call(
        kern,
        grid=(m // tm, n // tn, nk),
        in_specs=in_specs,
        out_specs=pl.BlockSpec((tm, tn), lambda i, j, k: (i, j)),
        out_shape=jax.ShapeDtypeStruct((m, n), out_dtype),
        scratch_shapes=[pltpu.VMEM((tm, tn), F32)],
        compiler_params=_cparams(("parallel", "parallel", "arbitrary"), vmem),
        name=name,
    )(*args)


def _nm_matmul_kernel(*refs, has_mod, shift_row, scale_row, act):
    if has_mod:
        x_ref, g_ref, mod_ref, w_ref, o_ref, h_ref = refs
    else:
        x_ref, g_ref, w_ref, o_ref, h_ref = refs

    @pl.when(pl.program_id(1) == 0)
    def _():
        x = x_ref[...].astype(F32)
        y = x * lax.rsqrt(jnp.mean(x * x, axis=-1, keepdims=True) + NORM_EPS) * g_ref[...]
        if has_mod:
            y = y * (1.0 + mod_ref[0, scale_row:scale_row + 1, :]) + mod_ref[0, shift_row:shift_row + 1, :]
        h_ref[...] = y.astype(BF16)

    y = jnp.dot(h_ref[...], w_ref[...].astype(BF16), preferred_element_type=F32)
    if act == "relu2":
        y = jnp.square(jnp.maximum(y, 0.0))
    o_ref[...] = y.astype(o_ref.dtype)


def _nm_matmul(x, g, w, *, tm, tn, out_dtype, x_col_block=0, kdim=None, mod=None, shift_row=None,
               scale_row=None, mod_index=None, act=None, rows=None, name=None):
    m = x.shape[0] if rows is None else rows
    kdim = x.shape[1] if kdim is None else kdim
    n = w.shape[1]
    assert w.shape[0] == kdim and m % tm == 0 and n % tn == 0
    in_specs = [pl.BlockSpec((tm, kdim), lambda i, j: (i, x_col_block)),
                pl.BlockSpec((1, kdim), lambda i, j: (0, 0))]
    args = [x, g.reshape(1, kdim).astype(F32)]
    if mod is not None:
        in_specs.append(pl.BlockSpec((1, N_MOD, kdim), lambda i, j: (mod_index(i), 0, 0)))
        args.append(mod)
    in_specs.append(pl.BlockSpec((kdim, tn), lambda i, j: (0, j)))
    args.append(w)
    vmem = (2 * tm * kdim * x.dtype.itemsize + tm * kdim * 2 + 2 * kdim * tn * w.dtype.itemsize
            + 2 * tm * tn * jnp.dtype(out_dtype).itemsize + tm * tn * 4 + 4 * 8 * kdim * 4)
    kern = functools.partial(_nm_matmul_kernel, has_mod=mod is not None, shift_row=shift_row,
                             scale_row=scale_row, act=act)
    return pl.pallas_call(
        kern,
        grid=(m // tm, n // tn),
        in_specs=in_specs,
        out_specs=pl.BlockSpec((tm, tn), lambda i, j: (i, j)),
        out_shape=jax.ShapeDtypeStruct((m, n), out_dtype),
        scratch_shapes=[pltpu.VMEM((tm, kdim), BF16)],
        compiler_params=_cparams(("parallel", "arbitrary"), vmem),
        name=name,
    )(*args)


def _norm_mod_kernel(x_ref, g_ref, mod_ref, h_ref, *, shift_row, scale_row):
    x = x_ref[...]
    y = x * lax.rsqrt(jnp.mean(x * x, axis=-1, keepdims=True) + NORM_EPS) * g_ref[...]
    y = y * (1.0 + mod_ref[0, scale_row:scale_row + 1, :]) + mod_ref[0, shift_row:shift_row + 1, :]
    h_ref[...] = y.astype(h_ref.dtype)


def _norm_mod(xs, g, mod, *, shift_row, scale_row, mod_index, tm, rows):
    dm = xs.shape[1]
    return pl.pallas_call(
        functools.partial(_norm_mod_kernel, shift_row=shift_row, scale_row=scale_row),
        grid=(rows // tm,),
        in_specs=[pl.BlockSpec((tm, dm), lambda i: (i, 0)), pl.BlockSpec((1, dm), lambda i: (0, 0)),
                  pl.BlockSpec((1, N_MOD, dm), lambda i: (mod_index(i), 0, 0))],
        out_specs=pl.BlockSpec((tm, dm), lambda i: (i, 0)),
        out_shape=jax.ShapeDtypeStruct((rows, dm), BF16),
        compiler_params=_cparams(("parallel",), 2 * tm * dm * 6 + 4 * tm * dm * 4),
        name="norm_mod",
    )(xs, g.reshape(1, dm), mod)


def _fullk_kernel(*refs, has_res, n_cast, gate_row, act, w_rows_are_outputs):
    refs = list(refs)
    if n_cast:
        cast_dsts = refs[-n_cast:]
        cast_srcs = refs[-2 * n_cast - 1:-n_cast - 1]
        refs = refs[:-2 * n_cast - 1] + [refs[-n_cast - 1]]
        for src, dst in zip(cast_srcs, cast_dsts):
            dst[...] = src[...].astype(dst.dtype)
    if has_res:
        a_ref, w_ref, res_ref, mod_ref, o_ref = refs
    else:
        a_ref, w_ref, o_ref = refs
    if w_rows_are_outputs:
        y = lax.dot_general(a_ref[...], w_ref[...], (((1,), (1,)), ((), ())), preferred_element_type=F32)
    else:
        y = jnp.dot(a_ref[...], w_ref[...], preferred_element_type=F32)
    if act == "relu2":
        y = jnp.square(jnp.maximum(y, 0.0))
    if has_res:
        y = res_ref[...] + mod_ref[0, gate_row:gate_row + 1, :] * y
    o_ref[...] = y.astype(o_ref.dtype)


def _cast_rows_per_step(n_rows, n_steps):
    rows = 16
    while n_rows % rows or n_rows // rows > n_steps:
        rows *= 2
    return rows


def _fullk_matmul(a, w, *, tm, tn, out_dtype, act=None, res=None, mod=None, gate_row=None, mod_index=None,
                  rows=None, name=None, layer=None, cast_srcs=(), cast_layer=None, w_rows_are_outputs=False):
    m = a.shape[0] if rows is None else rows
    if w_rows_are_outputs:
        _, n, kdim = w.shape
        w_spec = pl.BlockSpec((None, tn, kdim), lambda i, j: (layer, j, 0))
    else:
        _, kdim, n = w.shape
        w_spec = pl.BlockSpec((None, kdim, tn), lambda i, j: (layer, 0, j))
    assert a.shape[1] == kdim and m % tm == 0 and n % tn == 0 and a.dtype == BF16 and w.dtype == BF16
    in_specs = [pl.BlockSpec((tm, kdim), lambda i, j: (i, 0)), w_spec]
    args = [a, w]
    vmem = 2 * tm * kdim * 2 + 2 * kdim * tn * 2 + 2 * tm * tn * jnp.dtype(out_dtype).itemsize + 2 * tm * tn * 4
    if res is not None:
        in_specs += [pl.BlockSpec((tm, tn), lambda i, j: (i, j)),
                     pl.BlockSpec((1, N_MOD, tn), lambda i, j: (mod_index(i), 0, j))]
        args += [res, mod]
        vmem += 2 * tm * tn * 4
    out_specs = [pl.BlockSpec((tm, tn), lambda i, j: (i, j))]
    out_shape = [jax.ShapeDtypeStruct((m, n), out_dtype)]
    nj = n // tn
    cast_layer = layer if cast_layer is None else cast_layer
    for cast_src in cast_srcs:
        _, c_rows, c_cols = cast_src.shape
        cr = _cast_rows_per_step(c_rows, (m // tm) * nj)
        cast_blk = lambda i, j, last=c_rows // cr - 1: jnp.minimum(i * nj + j, last)
        in_specs.append(pl.BlockSpec((None, cr, c_cols), lambda i, j, blk=cast_blk: (cast_layer, blk(i, j), 0)))
        args.append(cast_src)
        out_specs.append(pl.BlockSpec((cr, c_cols), lambda i, j, blk=cast_blk: (blk(i, j), 0)))
        out_shape.append(jax.ShapeDtypeStruct((c_rows, c_cols), BF16))
        vmem += 2 * cr * c_cols * 6
    return pl.pallas_call(
        functools.partial(_fullk_kernel, has_res=res is not None, n_cast=len(cast_srcs),
                          gate_row=gate_row, act=act, w_rows_are_outputs=w_rows_are_outputs),
        grid=(m // tm, n // tn),
        in_specs=in_specs,
        out_specs=out_specs,
        out_shape=out_shape,
        compiler_params=_cparams(("arbitrary", "arbitrary"), vmem),
        name=name,
    )(*args)


def _merge_kernel(gl_ref, y0_ref, y1_ref, y2_ref, y3_ref, wb_ref, gu_ref, gb_ref, o_ref):
    gl = gl_ref[...].astype(BF16)
    bw = wb_ref.shape[0] // 4
    acc = None
    for i, y_ref in enumerate((y0_ref, y1_ref, y2_ref, y3_ref)):
        gate = jax.nn.sigmoid(jnp.dot(gl, gu_ref[i], preferred_element_type=F32) + gb_ref[i:i + 1, :])
        term = gate * jnp.dot(y_ref[...], wb_ref[i * bw:(i + 1) * bw, :], preferred_element_type=F32)
        acc = term if acc is None else acc + term
    o_ref[...] = acc.astype(o_ref.dtype)


def _merge(p, ys, wb, gu, gb, *, layer, tm, tn, rows):
    n = wb.shape[1]
    bw = wb.shape[0] // 4
    gr = gu.shape[2]
    y_spec = pl.BlockSpec((tm, bw), lambda i, j: (i, 0))
    vmem = (2 * tm * gr * 4 + 4 * 2 * tm * bw * 2 + 2 * 4 * bw * tn * 2 + 2 * 4 * gr * tn * 2
            + 2 * tm * tn * 2 + 3 * tm * tn * 4)
    return pl.pallas_call(
        _merge_kernel,
        grid=(rows // tm, n // tn),
        in_specs=[pl.BlockSpec((tm, gr), lambda i, j: (i, C_GL // GATE_RANK)),
                  y_spec, y_spec, y_spec, y_spec,
                  pl.BlockSpec((4 * bw, tn), lambda i, j: (0, j)),
                  pl.BlockSpec((None, 4, gr, tn), lambda i, j: (layer, 0, 0, j)),
                  pl.BlockSpec((None, 4, tn), lambda i, j: (layer, 0, j))],
        out_specs=pl.BlockSpec((tm, tn), lambda i, j: (i, j)),
        out_shape=jax.ShapeDtypeStruct((rows, n), BF16),
        compiler_params=_cparams(("parallel", "arbitrary"), vmem),
        name="merge",
    )(p, *ys, wb, gu, gb)


def _softmax_parts(s_list, scale):
    m = None
    for s in s_list:
        sm = jnp.max(s, axis=-1, keepdims=True)
        m = sm if m is None else jnp.maximum(m, sm)
    e_list = [jnp.exp2((s - m) * (scale * LOG2_E)) for s in s_list]
    l = None
    for e in e_list:
        es = jnp.sum(e, axis=-1, keepdims=True)
        l = es if l is None else l + es
    return e_list, l


def _attend(q, ks, vs, scale):
    e_list, l = _softmax_parts([_dot_nt(q, k) for k in ks], scale)
    o = None
    for e, v in zip(e_list, vs):
        t = jnp.dot(e.astype(BF16), v, preferred_element_type=F32)
        o = t if o is None else o + t
    return o / l


def _attend_many(qs, ks, vs, scale, sub):
    parts = [q[r:r + sub] for q in qs for r in range(0, q.shape[0], sub)]
    n = len(parts)
    scores, soft, outs = {}, {}, []
    for t in range(n + 2):
        if t < n:
            scores[t] = [_dot_nt(parts[t], k) for k in ks]
        if 0 <= t - 1 < n:
            soft[t - 1] = _softmax_parts(scores.pop(t - 1), scale)
        if 0 <= t - 2 < n:
            e_list, l = soft.pop(t - 2)
            o = None
            for e, v in zip(e_list, vs):
                pv = jnp.dot(e.astype(BF16), v, preferred_element_type=F32)
                o = pv if o is None else o + pv
            outs.append(o / l)
    per_q = len(parts) // len(qs)
    return [jnp.concatenate(outs[i * per_q:(i + 1) * per_q], axis=0) for i in range(len(qs))]


def _mla_kernel(cos_ref, sin_ref, q_ref, knx_ref, knc_ref, krx_ref, krc_ref, vx_ref, vc_ref, o_ref,
                kx_s, kc_s, *, tq, scale):
    i = pl.program_id(2)

    @pl.when(i == 0)
    def _():
        kx_s[:, :MLA_NOPE] = knx_ref[...]
        kx_s[:, MLA_NOPE:] = _rope128(krx_ref[...], cos_ref[...], sin_ref[...]).astype(BF16)
        kc_s[:, :MLA_NOPE] = knc_ref[...]
        kc_s[:, MLA_NOPE:] = krc_ref[...].astype(BF16)

    row0 = pl.multiple_of(i * tq, tq)
    q = q_ref[...]
    q_rope = _rope128(q[:, MLA_NOPE:].astype(F32), cos_ref[pl.ds(row0, tq), :], sin_ref[pl.ds(row0, tq), :])
    q = jnp.concatenate([q[:, :MLA_NOPE], q_rope.astype(BF16)], axis=1)
    (o,) = _attend_many((q,), (kc_s[...], kx_s[...]), (vc_ref[...], vx_ref[...]), scale, ATTN_SUB)
    o_ref[...] = o.astype(o_ref.dtype)


def _mla_ctx_kernel(q_ref, knc_ref, krc_ref, vc_ref, y_hbm, o_ref, *, scale):
    del y_hbm
    kc = jnp.concatenate([knc_ref[...], krc_ref[...].astype(BF16)], axis=1)
    o = _attend(q_ref[...], (kc,), (vc_ref[...],), scale)
    o_ref[...] = o.astype(o_ref.dtype)


def _diff_finish(o1, o2, lam_ref, g_ref, out_scale):
    o = o1 - lam_ref[...] * o2
    y = o * lax.rsqrt(jnp.mean(o * o, axis=-1, keepdims=True) + NORM_EPS) * g_ref[...]
    return y * out_scale


def _diff_halves(q):
    lane = lax.broadcasted_iota(jnp.int32, q.shape, 1)
    return jnp.where(lane < DIFF_QK, q, 0.0).astype(BF16), jnp.where(lane < DIFF_QK, 0.0, q).astype(BF16)


def _diff_kernel(cos_ref, sin_ref, lam_ref, g_ref, q_ref, kx_ref, kc_ref, vx_ref, vc_ref, o_ref,
                 kx_s, kc_s, vx_s, vc_s, *, tq, scale, out_scale):
    i = pl.program_id(2)

    @pl.when(i == 0)
    def _():
        kx_s[...] = _rope128(kx_ref[...], cos_ref[...], sin_ref[...]).astype(BF16)
        kc_s[...] = kc_ref[...].astype(BF16)
        vx_s[...] = vx_ref[...].astype(BF16)
        vc_s[...] = vc_ref[...].astype(BF16)

    row0 = pl.multiple_of(i * tq, tq)
    q1, q2 = _diff_halves(_rope128(q_ref[...], cos_ref[pl.ds(row0, tq), :], sin_ref[pl.ds(row0, tq), :]))
    ks, vs = (kc_s[...], kx_s[...]), (vc_s[...], vx_s[...])
    o1, o2 = _attend_many((q1, q2), ks, vs, scale, ATTN_SUB)
    y = _diff_finish(o1, o2, lam_ref, g_ref, out_scale)
    o_ref[...] = y.astype(o_ref.dtype)


def _diff_ctx_kernel(lam_ref, g_ref, q_ref, kc_ref, vc_ref, y_hbm, o_ref, *, scale, out_scale):
    del y_hbm
    q1, q2 = _diff_halves(q_ref[...])
    ks, vs = (kc_ref[...].astype(BF16),), (vc_ref[...].astype(BF16),)
    y = _diff_finish(_attend(q1, ks, vs, scale), _attend(q2, ks, vs, scale), lam_ref, g_ref, out_scale)
    o_ref[...] = y.astype(o_ref.dtype)


def _mla_attention(q, kv, p, cos_t, sin_t, *, n_batch, t_len, c_len, tq, with_ctx):
    n_x = n_batch * t_len
    nq = t_len // tq
    out_rows = n_x + (n_batch * c_len if with_ctx else 0)
    cblk0 = n_x // c_len
    scale = (MLA_NOPE + MLA_ROPE) ** -0.5
    qmap = lambda b, h, i: (b * nq + i, h)
    full = lambda b, h, i: (0, 0)
    vmem = (4 * t_len * LANES * 4 + 2 * tq * 256 * 2 + 2 * (t_len + c_len) * LANES * (2 + 4 + 2)
            + (t_len + c_len) * 256 * 2 + 2 * tq * LANES * 2 + 6 * tq * (t_len + c_len) * 4)
    y = pl.pallas_call(
        functools.partial(_mla_kernel, tq=tq, scale=scale),
        grid=(n_batch, MLA_HEADS, nq),
        in_specs=[pl.BlockSpec((t_len, LANES), full), pl.BlockSpec((t_len, LANES), full),
                  pl.BlockSpec((tq, MLA_QK_PAD), qmap),
                  pl.BlockSpec((t_len, MLA_NOPE), lambda b, h, i: (b, h)),
                  pl.BlockSpec((c_len, MLA_NOPE), lambda b, h, i: (cblk0 + b, h)),
                  pl.BlockSpec((t_len, LANES), lambda b, h, i: (b, C_KROPE // LANES)),
                  pl.BlockSpec((c_len, LANES), lambda b, h, i: (cblk0 + b, C_KROPE // LANES)),
                  pl.BlockSpec((t_len, MLA_V), lambda b, h, i: (b, MLA_HEADS + h)),
                  pl.BlockSpec((c_len, MLA_V), lambda b, h, i: (cblk0 + b, MLA_HEADS + h))],
        out_specs=pl.BlockSpec((tq, MLA_V), qmap),
        out_shape=jax.ShapeDtypeStruct((out_rows, MLA_HEADS * MLA_V), BF16),
        scratch_shapes=[pltpu.VMEM((t_len, MLA_QK_PAD), BF16), pltpu.VMEM((c_len, MLA_QK_PAD), BF16)],
        compiler_params=_cparams(("parallel", "parallel", "arbitrary"), vmem),
        name="mla_attention",
    )(cos_t, sin_t, q, kv, kv, p, p, kv, kv)
    if not with_ctx:
        return y
    cmap = lambda b, h: (cblk0 + b, h)
    return pl.pallas_call(
        functools.partial(_mla_ctx_kernel, scale=scale),
        grid=(n_batch, MLA_HEADS),
        in_specs=[pl.BlockSpec((c_len, MLA_QK_PAD), cmap),
                  pl.BlockSpec((c_len, MLA_NOPE), cmap),
                  pl.BlockSpec((c_len, LANES), lambda b, h: (cblk0 + b, C_KROPE // LANES)),
                  pl.BlockSpec((c_len, MLA_V), lambda b, h: (cblk0 + b, MLA_HEADS + h)),
                  pl.BlockSpec(memory_space=pl.ANY)],
        out_specs=pl.BlockSpec((c_len, MLA_V), cmap),
        out_shape=jax.ShapeDtypeStruct(y.shape, y.dtype),
        input_output_aliases={4: 0},
        compiler_params=_cparams(("parallel", "parallel"), 16 * c_len * c_len * 4 + 8 * c_len * 256 * 4),
        name="mla_attention_ctx",
    )(q, kv, p, kv, y)


def _diff_attention(p, cos_t, sin_t, lam_row, g_row, *, n_batch, t_len, c_len, tq, with_ctx, out_scale):
    n_x = n_batch * t_len
    nq = t_len // tq
    out_rows = n_x + (n_batch * c_len if with_ctx else 0)
    cblk0 = n_x // c_len
    scale = DIFF_QK ** -0.5
    full = lambda b, h, i: (0, 0)
    qblk, kblk, vblk = C_DQ // LANES, C_DK // LANES, C_DV // LANES
    vmem = (4 * t_len * LANES * 4 + 2 * tq * LANES * 4 + 4 * (t_len + c_len) * LANES * 4
            + 2 * (t_len + c_len) * LANES * 2 + 2 * tq * LANES * 2 + 8 * tq * (t_len + c_len) * 4)
    y = pl.pallas_call(
        functools.partial(_diff_kernel, tq=tq, scale=scale, out_scale=out_scale),
        grid=(n_batch, DIFF_HEADS, nq),
        in_specs=[pl.BlockSpec((t_len, LANES), full), pl.BlockSpec((t_len, LANES), full),
                  pl.BlockSpec((1, DIFF_V), full), pl.BlockSpec((1, DIFF_V), full),
                  pl.BlockSpec((tq, LANES), lambda b, h, i: (b * nq + i, qblk + h)),
                  pl.BlockSpec((t_len, LANES), lambda b, h, i: (b, kblk + h)),
                  pl.BlockSpec((c_len, LANES), lambda b, h, i: (cblk0 + b, kblk + h)),
                  pl.BlockSpec((t_len, LANES), lambda b, h, i: (b, vblk + h)),
                  pl.BlockSpec((c_len, LANES), lambda b, h, i: (cblk0 + b, vblk + h))],
        out_specs=pl.BlockSpec((tq, DIFF_V), lambda b, h, i: (b * nq + i, h)),
        out_shape=jax.ShapeDtypeStruct((out_rows, DIFF_HEADS * DIFF_V), BF16),
        scratch_shapes=[pltpu.VMEM((t_len, LANES), BF16), pltpu.VMEM((c_len, LANES), BF16),
                        pltpu.VMEM((t_len, LANES), BF16), pltpu.VMEM((c_len, LANES), BF16)],
        compiler_params=_cparams(("parallel", "parallel", "arbitrary"), vmem),
        name="diff_attention",
    )(cos_t, sin_t, lam_row, g_row, p, p, p, p, p)
    if not with_ctx:
        return y
    one = lambda b, h: (0, 0)
    return pl.pallas_call(
        functools.partial(_diff_ctx_kernel, scale=scale, out_scale=out_scale),
        grid=(n_batch, DIFF_HEADS),
        in_specs=[pl.BlockSpec((1, DIFF_V), one), pl.BlockSpec((1, DIFF_V), one),
                  pl.BlockSpec((c_len, LANES), lambda b, h: (cblk0 + b, qblk + h)),
                  pl.BlockSpec((c_len, LANES), lambda b, h: (cblk0 + b, kblk + h)),
                  pl.BlockSpec((c_len, LANES), lambda b, h: (cblk0 + b, vblk + h)),
                  pl.BlockSpec(memory_space=pl.ANY)],
        out_specs=pl.BlockSpec((c_len, DIFF_V), lambda b, h: (cblk0 + b, h)),
        out_shape=jax.ShapeDtypeStruct(y.shape, y.dtype),
        input_output_aliases={5: 0},
        compiler_params=_cparams(("parallel", "parallel"), 24 * c_len * c_len * 4 + 8 * c_len * LANES * 4),
        name="diff_attention_ctx",
    )(lam_row, g_row, p, p, p, y)


def _prep_kernel(r_ref, k_ref, v_ref, cb_ref, cc_ref, cu_ref, lo_ref, hp_ref, hn_ref,
                 mu_ref, mul_ref, kk_w_ref, ka_w_ref, rk_w_ref, w0_ref, a0_ref, w2_ref, a2_ref, g2_ref,
                 cw_ref,
                 r_o, kk_o, v_o, lw_o, kd_o, ka_o, bonus_o, g_o, conv_o, *, tr, n_x, t_len, c_len):
    i = pl.program_id(0)
    g0 = i * tr
    seq = jnp.where(g0 < n_x, t_len, c_len)
    has_prev = ((g0 % seq) != 0).astype(F32)
    has_next = (((g0 + tr) % seq) != 0).astype(F32)
    row = lax.broadcasted_iota(jnp.int32, (tr, 1), 0)

    def neighbours(x, col0):
        width = x.shape[1]
        before = hp_ref[HALO - 1:HALO, col0:col0 + width] * has_prev
        after = hn_ref[0:1, col0:col0 + width] * has_next
        prev = jnp.where(row == 0, before, pltpu.roll(x, 1, 0))
        nxt = jnp.where(row == tr - 1, after, pltpu.roll(x, tr - 1, 0))
        return prev, nxt

    def shifted(x, col0, mu0, mu1):
        prev, nxt = neighbours(x, col0)
        return x + mu0 * (prev - x) + mu1 * (nxt - x)

    r = shifted(r_ref[...], C_R, mu_ref[0:1, :BRANCH_W], mu_ref[1:2, :BRANCH_W])
    k = shifted(k_ref[...], C_K, mu_ref[0:1, BRANCH_W:2 * BRANCH_W], mu_ref[1:2, BRANCH_W:2 * BRANCH_W])
    v = shifted(v_ref[...], C_V, mu_ref[0:1, 2 * BRANCH_W:], mu_ref[1:2, 2 * BRANCH_W:])
    lo = shifted(lo_ref[...], C_LORA, mul_ref[0:1, :], mul_ref[1:2, :])
    wd = jnp.tanh(lo[:, :2 * RWKV_LORA])
    ad = lo[:, 2 * RWKV_LORA:4 * RWKV_LORA]
    gd = jax.nn.sigmoid(lo[:, 4 * RWKV_LORA:])
    w_pre = _dot(wd, w2_ref[...]) + w0_ref[...]
    a_sig = jax.nn.sigmoid(_dot(ad, a2_ref[...]) + a0_ref[...])
    g_o[...] = _dot(gd, g2_ref[...])
    w_log = -(jnp.maximum(-w_pre, 0.0) + jnp.log(1.0 + jnp.exp(-jnp.abs(w_pre)))) - 0.5
    lw = -jnp.exp(w_log)
    kkf = k * kk_w_ref[...]
    kk = kkf * lax.rsqrt(_head_sums(kkf * kkf) + 1e-12)
    r_o[...] = r.astype(r_o.dtype)
    kk_o[...] = kk.astype(kk_o.dtype)
    v_o[...] = v.astype(v_o.dtype)
    k_sum = None
    for d in range(2):
        a_d = a_sig[:, d * BRANCH_W:(d + 1) * BRANCH_W]
        k_d = k * (1.0 + (a_d - 1.0) * ka_w_ref[...])
        lw_o[d] = lw[:, d * BRANCH_W:(d + 1) * BRANCH_W]
        kd_o[d] = k_d.astype(kd_o.dtype)
        ka_o[d] = (kk * a_d).astype(ka_o.dtype)
        k_sum = k_d if k_sum is None else k_sum + k_d
    bonus_o[...] = _head_sums(r * k_sum * rk_w_ref[...]) * v

    z = cc_ref[...] * cu_ref[...]
    z_before = hp_ref[HALO - 1:HALO, C_CC:C_CC + BRANCH_W] * hp_ref[HALO - 1:HALO, C_CU:C_CU + BRANCH_W] * has_prev
    z_after = hn_ref[0:1, C_CC:C_CC + BRANCH_W] * hn_ref[0:1, C_CU:C_CU + BRANCH_W] * has_next
    z_prev = jnp.where(row == 0, z_before, pltpu.roll(z, 1, 0))
    z_next = jnp.where(row == tr - 1, z_after, pltpu.roll(z, tr - 1, 0))
    y = cb_ref[...] * (cw_ref[0:1, :] * z_prev + cw_ref[1:2, :] * z + cw_ref[2:3, :] * z_next)
    conv_o[...] = y.astype(conv_o.dtype)


def _mixer_prep(p, mu_rkv, mu_lora, kk_w, ka_w, rk_w, w0, a0, w2bd, a2bd, g2p, conv_w, *, tr, n_x, t_len,
                c_len):
    n_rows = p.shape[0]
    bw = BRANCH_W
    last_halo = n_rows // HALO - 1
    col = lambda c: (lambda i: (i, c))
    const = lambda i: (0, 0)
    main = [pl.BlockSpec((tr, bw), col(C_R // bw)), pl.BlockSpec((tr, bw), col(C_K // bw)),
            pl.BlockSpec((tr, bw), col(C_V // bw)), pl.BlockSpec((tr, bw), col(C_CB // bw)),
            pl.BlockSpec((tr, bw), col(C_CC // bw)), pl.BlockSpec((tr, bw), col(C_CU // bw)),
            pl.BlockSpec((tr, RWKV_LORA_PAD), col(C_LORA // RWKV_LORA_PAD)),
            pl.BlockSpec((HALO, P_COLS), lambda i: (jnp.maximum(i * (tr // HALO) - 1, 0), 0)),
            pl.BlockSpec((HALO, P_COLS), lambda i: (jnp.minimum((i + 1) * (tr // HALO), last_halo), 0))]
    params = [mu_rkv, mu_lora, kk_w, ka_w, rk_w, w0, a0, w2bd, a2bd, g2p, conv_w]
    param_specs = [pl.BlockSpec(a.shape, const) for a in params]
    row_spec = pl.BlockSpec((tr, bw), lambda i: (i, 0))
    dir_spec = pl.BlockSpec((2, tr, bw), lambda i: (0, i, 0))
    f32_rows = jax.ShapeDtypeStruct((n_rows, bw), F32)
    f32_dirs = jax.ShapeDtypeStruct((2, n_rows, bw), F32)
    bf16_rows = jax.ShapeDtypeStruct((n_rows, bw), BF16)
    bf16_dirs = jax.ShapeDtypeStruct((2, n_rows, bw), BF16)
    vmem = 2 * (7 * tr * bw * 4 + 2 * HALO * P_COLS * 4 + 12 * tr * bw * 4) + 30 * tr * bw * 4
    return pl.pallas_call(
        functools.partial(_prep_kernel, tr=tr, n_x=n_x, t_len=t_len, c_len=c_len),
        grid=(n_rows // tr,),
        in_specs=main + param_specs,
        out_specs=[row_spec, row_spec, row_spec, dir_spec, dir_spec, dir_spec, row_spec, row_spec, row_spec],
        out_shape=[bf16_rows, bf16_rows, bf16_rows, f32_dirs, bf16_dirs, bf16_dirs, f32_rows, f32_rows, bf16_rows],
        compiler_params=_cparams(("parallel",), vmem),
        name="mixer_prep",
    )(p, p, p, p, p, p, p, p, p, *params)


def _rwkv_kernel(r0_ref, kk0_ref, v0_ref, r1_ref, kk1_ref, v1_ref, lw0_ref, kd0_ref, ka0_ref,
                 lw1_ref, kd1_ref, ka1_ref, o0_ref, o1_ref, s_ref):
    C = RWKV_CHUNK
    W = 2 * RWKV_HEAD
    n_pairs = RWKV_HEADS // 2

    @pl.when(pl.program_id(1) == 0)
    def _():
        s_ref[...] = jnp.zeros_like(s_ref)

    mm, mm_nt = _dot, _dot_nt
    lane = lax.broadcasted_iota(jnp.int32, (C, W), 1)
    row = lax.broadcasted_iota(jnp.int32, (C, W), 0)
    first = lane < RWKV_HEAD
    rc = lax.broadcasted_iota(jnp.int32, (C, C), 0)
    cc = lax.broadcasted_iota(jnp.int32, (C, C), 1)

    def bdiag(x):
        return jnp.concatenate([jnp.where(first, x, jnp.zeros_like(x)), jnp.where(first, jnp.zeros_like(x), x)],
                               axis=0)

    def direction(sign, r_ref, kk_ref, v_ref, lw_ref, kd_ref, ka_ref):
        m_incl = jnp.where((rc - cc) * sign >= 0, 1.0, 0.0).astype(BF16)
        lw = lw_ref[0]
        cum = _dot_exact_lhs(m_incl, lw)
        tot = jnp.sum(lw, axis=0, keepdims=True)
        e_ninc = jnp.exp(-cum)
        e_rem = jnp.exp(tot - cum)
        kk, ka, kd = kk_ref[...], ka_ref[0], kd_ref[0]
        order = (row - (lane % RWKV_HEAD)) * sign
        return dict(a_t=kk * jnp.exp(cum - lw), b_t=-ka * e_ninc, k_t=kd * e_ninc, r_t=r_ref[...] * jnp.exp(cum),
                    b_h=-ka * e_rem, k_h=kd * e_rem, v=v_ref[...], e_tot=jnp.exp(tot), strict=order > 0,
                    incl=order >= 0, eye=jnp.where(order == 0, 1.0, 0.0).astype(F32))

    dirs = (direction(1, r0_ref, kk0_ref, v0_ref, lw0_ref, kd0_ref, ka0_ref),
            direction(-1, r1_ref, kk1_ref, v1_ref, lw1_ref, kd1_ref, ka1_ref))
    o_refs = (o0_ref, o1_ref)

    chains = [(d, p) for d in range(2) for p in range(n_pairs)]
    n = range(len(chains))
    sl = [slice(W * p, W * (p + 1)) for (_, p) in chains]
    dd = [dirs[d] for (d, _) in chains]
    ar = [jnp.concatenate([dd[c]["a_t"][:, sl[c]], dd[c]["r_t"][:, sl[c]]], axis=0).astype(BF16) for c in n]
    a_bd = [bdiag(dd[c]["a_t"][:, sl[c]].astype(BF16)) for c in n]
    bk_bd = [jnp.concatenate([bdiag(dd[c]["b_t"][:, sl[c]].astype(BF16)), bdiag(dd[c]["k_t"][:, sl[c]].astype(BF16))],
                             axis=0) for c in n]
    v_bd = [bdiag(dd[c]["v"][:, sl[c]].astype(BF16)) for c in n]
    g = [mm_nt(ar[c], bk_bd[c]) for c in n]
    l_pow = [jnp.where(dd[c]["strict"], g[c][:C, :W], 0.0) for c in n]
    m_ak = [jnp.where(dd[c]["strict"], g[c][:C, W:], 0.0).astype(BF16) for c in n]
    a_r = [jnp.where(jnp.concatenate([dd[c]["incl"], dd[c]["incl"]], axis=1), g[c][C:], 0.0).astype(BF16) for c in n]
    mv = [mm(m_ak[c], v_bd[c]) for c in n]
    l_bd = [bdiag(l_pow[c].astype(BF16)) for c in n]
    t_inv = [dd[c]["eye"] + l_pow[c] for c in n]
    l_pow = [mm(l_pow[c], l_bd[c]) for c in n]
    for _ in range(int(math.log2(C)) - 2):
        l_bd = [bdiag(l_pow[c].astype(BF16)) for c in n]
        lt = [mm(jnp.concatenate([l_pow[c], t_inv[c]], axis=0), l_bd[c]) for c in n]
        l_pow = [lt[c][:C] for c in n]
        t_inv = [t_inv[c] + lt[c][C:] for c in n]
    t_inv = [t_inv[c] + mm(t_inv[c], bdiag(l_pow[c].astype(BF16))) for c in n]
    wu = [mm(t_inv[c], jnp.concatenate([a_bd[c], bdiag(mv[c].astype(BF16))], axis=1)) for c in n]
    s_bd = [s_ref[d, p] for (d, p) in chains]
    wr = [mm(jnp.concatenate([wu[c][:, :W], dd[c]["r_t"][:, sl[c]]], axis=0), s_bd[c]) for c in n]
    zv = [jnp.concatenate([bdiag((wr[c][:C] + wu[c][:, W:]).astype(BF16)), v_bd[c]], axis=0) for c in n]
    for c, (d, p) in enumerate(chains):
        o_refs[d][:, sl[c]] = wr[c][C:] + mm(a_r[c], zv[c])
    for c, (d, p) in enumerate(chains):
        bk_h = jnp.concatenate([bdiag(dd[c]["b_h"][:, sl[c]]), bdiag(dd[c]["k_h"][:, sl[c]])], axis=0)
        tot_col = jnp.broadcast_to(dd[c]["e_tot"][:, sl[c]], (W, W)).T
        s_ref[d, p] = tot_col * s_bd[c] + mm(bk_h.T, zv[c])


def _rwkv_scan(r, kk, v, lw, kd, ka, *, n_batch, t_len, c_len):
    C = RWKV_CHUNK
    rows, width = r.shape
    nct, nxt = c_len // C, t_len // C
    ctx_blk0 = n_batch * t_len // C

    def blk(b, d, s):
        j_c = s if d == 0 else nct - 1 - s
        j_x = s - nct if d == 0 else nxt - 1 - (s - nct)
        return jnp.where(s < nct, ctx_blk0 + b * nct + j_c, b * nxt + j_x)

    def shared(d):
        return pl.BlockSpec((C, width), lambda b, s: (blk(b, d, s), 0))

    def per_dir(d):
        return pl.BlockSpec((1, C, width), lambda b, s: (d, blk(b, d, s), 0))

    vmem = 2 * 14 * C * width * 4 + RWKV_HEADS * 128 * 128 * 4 + 128 * C * width * 4
    out = jax.ShapeDtypeStruct((rows, width), F32)
    return pl.pallas_call(
        _rwkv_kernel,
        grid=(n_batch, nct + nxt),
        in_specs=[shared(0), shared(0), shared(0), shared(1), shared(1), shared(1),
                  per_dir(0), per_dir(0), per_dir(0), per_dir(1), per_dir(1), per_dir(1)],
        out_specs=[shared(0), shared(1)],
        out_shape=[out, out],
        scratch_shapes=[pltpu.VMEM((2, RWKV_HEADS // 2, 128, 128), F32)],
        compiler_params=_cparams(("parallel", "arbitrary"), vmem),
        name="rwkv_scan",
    )(r, kk, v, r, kk, v, lw, kd, ka, lw, kd, ka)


def _rwkv_readout_kernel(o0_ref, o1_ref, bonus_ref, g_ref, lng_ref, lnb_ref, y_ref):
    o = o0_ref[...] + o1_ref[...]
    mean = _head_sums(o) * (1.0 / RWKV_HEAD)
    cen = o - mean
    var = _head_sums(cen * cen) * (1.0 / RWKV_HEAD)
    o_n = cen * lax.rsqrt(var + RWKV_GN_EPS) * lng_ref[...] + lnb_ref[...]
    y_ref[...] = ((o_n + bonus_ref[...]) * g_ref[...]).astype(y_ref.dtype)


def _rwkv_readout(o_dirs, bonus, g_out, ln_g, ln_b, *, tr):
    n_rows, bw = bonus.shape
    row_spec = pl.BlockSpec((tr, bw), lambda i: (i, 0))
    const = pl.BlockSpec((1, bw), lambda i: (0, 0))
    return pl.pallas_call(
        _rwkv_readout_kernel,
        grid=(n_rows // tr,),
        in_specs=[row_spec, row_spec, row_spec, row_spec, const, const],
        out_specs=row_spec,
        out_shape=jax.ShapeDtypeStruct((n_rows, bw), BF16),
        compiler_params=_cparams(("parallel",), 2 * 5 * tr * bw * 4 + 16 * tr * bw * 4),
        name="rwkv_readout",
    )(o_dirs[0], o_dirs[1], bonus, g_out, ln_g.reshape(1, bw), ln_b.reshape(1, bw))


def _rmsnorm_kernel(x_ref, g_ref, o_ref):
    x = x_ref[...]
    o_ref[...] = x * lax.rsqrt(jnp.mean(x * x, axis=-1, keepdims=True) + NORM_EPS) * g_ref[...]


def _final_norm(xs, g, *, rows, tm):
    dm = xs.shape[1]
    return pl.pallas_call(
        _rmsnorm_kernel,
        grid=(rows // tm,),
        in_specs=[pl.BlockSpec((tm, dm), lambda i: (i, 0)), pl.BlockSpec((1, dm), lambda i: (0, 0))],
        out_specs=pl.BlockSpec((tm, dm), lambda i: (i, 0)),
        out_shape=jax.ShapeDtypeStruct((rows, dm), F32),
        compiler_params=_cparams(("parallel",), 4 * tm * dm * 4),
        name="final_norm",
    )(xs, g.reshape(1, dm))


def _rope_tables128(n_tokens):
    rows = n_tokens // GRID_W
    row = jnp.repeat(jnp.arange(rows, dtype=F32), GRID_W)
    col = jnp.tile(jnp.arange(GRID_W, dtype=F32), rows)
    n_freq = 64 // 4
    inv = ROPE_BASE ** (-jnp.arange(n_freq, dtype=F32) / n_freq)
    ang = jnp.concatenate([row[:, None] * inv, col[:, None] * inv], axis=-1)
    cos, sin = jnp.cos(ang), jnp.sin(ang)
    return jnp.concatenate([cos, cos, cos, cos], axis=-1), jnp.concatenate([-sin, sin, -sin, sin], axis=-1)


def _block_diag2(w2):
    z = jnp.zeros_like(w2[0])
    return jnp.concatenate([jnp.concatenate([w2[0], z], axis=1), jnp.concatenate([z, w2[1]], axis=1)], axis=0)


def kernel(x, c, ctx, c_ctx, norm1_g, norm2_g, mod_down, mod_up, mod_b, w_in, mla_q_norm_g, mla_w_uq,
           mla_kv_norm_g, mla_w_ukv, rwkv_mu, rwkv_w0, rwkv_w2, rwkv_a0, rwkv_a2, rwkv_g2, rwkv_k_k,
           rwkv_k_a, rwkv_r_k, rwkv_ln_g, rwkv_ln_b, conv_w, diff_lambda, diff_norm_g, w_branch, gate_down,
           gate_up, gate_b, w_out, mlp_w1, mlp_w2, final_norm_g):
    n_batch, t_len, dm = x.shape
    c_len = ctx.shape[1]
    depth = w_in.shape[0]
    bw = BRANCH_W
    n_x = n_batch * t_len
    n_c = n_batch * c_len
    n_rows = n_x + n_c
    tm = 512 if (t_len % 512 == 0 and n_c % 512 == 0) else 256
    tq = 512 if t_len % 512 == 0 else min(256, c_len)
    tr = min(256, c_len)
    assert dm == D_MODEL and t_len % tm == 0 and n_c % tm == 0 and n_x % c_len == 0
    assert t_len % tq == 0 and c_len % RWKV_CHUNK == 0 and t_len % c_len == 0
    tm_big = 1024 if (t_len % 1024 == 0 and n_c % 1024 == 0) else tm
    mod_index = functools.partial(_mod_index, tm=tm, n_x_rows=n_x, t_len=t_len, n_batch=n_batch)
    mod_index_big = functools.partial(_mod_index, tm=tm_big, n_x_rows=n_x, t_len=t_len, n_batch=n_batch)
    mod_index_nm = functools.partial(_mod_index, tm=256, n_x_rows=n_x, t_len=t_len, n_batch=n_batch)

    zeros = lambda *s: jnp.zeros(s, F32)
    lora_w = 4 * RWKV_LORA + RWKV_GATE_LORA
    rw0 = MLA_Q_LORA + MLA_KV_LORA + MLA_ROPE
    cv0 = rw0 + 3 * bw + lora_w
    w_in_t = jnp.swapaxes(w_in, 1, 2)
    w_in_p = jnp.concatenate(
        [w_in_t[:, :C_R], w_in_t[:, rw0:rw0 + 3 * bw], w_in_t[:, cv0:],
         w_in_t[:, rw0 + 3 * bw:cv0], zeros(depth, RWKV_LORA_PAD - lora_w, dm), jnp.swapaxes(gate_down, 1, 2),
         w_in_t[:, C_R:rw0], zeros(depth, P_COLS - C_KROPE - MLA_ROPE, dm)], axis=1).astype(BF16)
    w_uq_p = jnp.pad(mla_w_uq.reshape(depth, MLA_Q_LORA, MLA_HEADS, MLA_NOPE + MLA_ROPE),
                     ((0, 0), (0, 0), (0, 0), (0, MLA_QK_PAD - MLA_NOPE - MLA_ROPE))
                     ).reshape(depth, MLA_Q_LORA, MLA_HEADS * MLA_QK_PAD).astype(BF16)
    w_ukv_r = mla_w_ukv.reshape(depth, MLA_KV_LORA, MLA_HEADS, MLA_NOPE + MLA_V)
    w_ukv_p = jnp.concatenate([w_ukv_r[..., :MLA_NOPE].reshape(depth, MLA_KV_LORA, -1),
                               w_ukv_r[..., MLA_NOPE:].reshape(depth, MLA_KV_LORA, -1)], axis=-1).astype(BF16)
    w_branch_2d = w_branch.reshape(depth, 4 * bw, dm)
    gate_up_b = jnp.moveaxis(gate_up, 2, 1).astype(BF16)
    g2_p = jnp.pad(rwkv_g2, ((0, 0), (0, RWKV_LORA_PAD - 4 * RWKV_LORA - RWKV_GATE_LORA), (0, 0))).astype(BF16)
    mu_lora = jnp.pad(rwkv_mu[:, :, 3 * bw:], ((0, 0), (0, 0), (0, RWKV_LORA_PAD - lora_w)))

    cond = jnp.concatenate([c, c_ctx[None, :], zeros(16 - n_batch - 1, dm)], axis=0)
    cond = jax.nn.silu(cond)
    mods = []
    for l in range(depth):
        low = _matmul(cond, mod_down[l], tm=16, tn=MOD_RANK, tk=dm, name="mod_down")
        up = _matmul(low, mod_up[l], tm=16, tn=2048, tk=MOD_RANK, bias=mod_b[l], name="mod_up")
        mods.append(up.reshape(16, N_MOD, dm))

    cos_t, sin_t = _rope_tables128(t_len)
    xs = jnp.concatenate([x.reshape(n_x, dm), ctx.reshape(n_c, dm)], axis=0)

    for l in range(depth):
        need_ctx = l < depth - 1
        mod = mods[l]
        lam_init = 0.8 - 0.6 * math.exp(-0.3 * l)
        lq1, lk1, lq2, lk2 = diff_lambda[l]
        lam = jnp.exp(jnp.sum(lq1 * lk1)) - jnp.exp(jnp.sum(lq2 * lk2)) + lam_init
        lam_row = jnp.full((1, DIFF_V), 1.0, F32) * lam

        h1 = _norm_mod(xs, norm1_g[l], mod, shift_row=0, scale_row=1, mod_index=mod_index_nm, tm=256,
                       rows=n_rows)
        p, w1_b, w_out_b, w_branch_b = _fullk_matmul(h1, w_in_p, layer=l, tm=tm_big, tn=512, out_dtype=F32,
                                                     name="in_proj", cast_srcs=(mlp_w1, w_out, w_branch_2d),
                                                     w_rows_are_outputs=True)

        q = _nm_matmul(p, mla_q_norm_g[l], w_uq_p[l], tm=tm, tn=MLA_HEADS * MLA_QK_PAD, out_dtype=BF16,
                       x_col_block=C_CQ // MLA_Q_LORA, kdim=MLA_Q_LORA, name="mla_q")
        kv = _nm_matmul(p, mla_kv_norm_g[l], w_ukv_p[l], tm=tm, tn=MLA_HEADS * (MLA_NOPE + MLA_V), out_dtype=BF16,
                        x_col_block=C_CKV // MLA_KV_LORA, kdim=MLA_KV_LORA, name="mla_kv")
        y_mla = _mla_attention(q, kv, p, cos_t, sin_t, n_batch=n_batch, t_len=t_len, c_len=c_len, tq=tq,
                               with_ctx=need_ctx)

        y_diff = _diff_attention(p, cos_t, sin_t, lam_row, diff_norm_g[l].reshape(1, DIFF_V),
                                 n_batch=n_batch, t_len=t_len, c_len=c_len, tq=tq, with_ctx=need_ctx,
                                 out_scale=1.0 - lam_init)

        r_, kk, v_, lw, k_dir, kka, bonus, g_out, y_conv = _mixer_prep(
            p, rwkv_mu[l, :, :3 * bw], mu_lora[l], rwkv_k_k[l].reshape(1, bw), rwkv_k_a[l].reshape(1, bw),
            rwkv_r_k[l].reshape(1, bw), rwkv_w0[l].reshape(1, 2 * bw), rwkv_a0[l].reshape(1, 2 * bw),
            _block_diag2(rwkv_w2[l]).astype(BF16), _block_diag2(rwkv_a2[l]).astype(BF16), g2_p[l],
            conv_w[l], tr=min(128, tr), n_x=n_x, t_len=t_len, c_len=c_len)
        o_dirs = _rwkv_scan(r_, kk, v_, lw, k_dir, kka, n_batch=n_batch, t_len=t_len, c_len=c_len)
        y_rwkv = _rwkv_readout(o_dirs, bonus, g_out, rwkv_ln_g[l], rwkv_ln_b[l], tr=tr)

        rows = n_rows if need_ctx else n_x
        acc = _merge(p, (y_mla, y_rwkv, y_conv, y_diff), w_branch_b, gate_up_b, gate_b, layer=l,
                     tm=tm_big, tn=512, rows=rows)
        (xs_new,) = _fullk_matmul(acc, w_out_b[None], layer=0, tm=tm_big, tn=512, out_dtype=F32, res=xs, mod=mod,
                                  gate_row=2, mod_index=mod_index_big, rows=rows, name="out_proj")

        h2 = _norm_mod(xs_new, norm2_g[l], mod, shift_row=3, scale_row=4, mod_index=mod_index_nm, tm=256,
                       rows=rows)
        hid, w2_b = _fullk_matmul(h2, w1_b[None], layer=0, tm=tm_big, tn=TN_UP, out_dtype=BF16, act="relu2",
                                  name="mlp_up", cast_srcs=(mlp_w2,), cast_layer=l)
        xs = _matmul(hid, w2_b, tm=tm_big, tn=1024, tk=2048, res=xs_new, mod=mod, gate_row=5,
                     mod_index=mod_index_big, rows=rows, name="mlp_down")

    out = _final_norm(xs, final_norm_g, rows=n_x, tm=tm)
    return out.reshape(n_batch, t_len, dm)
```

```python
import functools
import math

import jax
import jax.numpy as jnp
from jax import lax
from jax.experimental import pallas as pl
from jax.experimental.pallas import tpu as pltpu

F32 = jnp.float32
BF16 = jnp.bfloat16

D_MODEL = 4096
BRANCH_W = 1024
GRID_W = 64
ROPE_BASE = 10000.0
NORM_EPS = 1e-6
N_MOD = 6
LANES = 128
HALO = 8
LOG2_E = 1.4426950408889634

MLA_HEADS = 8
MLA_NOPE = 128
MLA_ROPE = 64
MLA_V = 128
MLA_Q_LORA = 768
MLA_KV_LORA = 256
MLA_QK_PAD = 256

RWKV_HEAD = 64
RWKV_HEADS = 16
RWKV_LORA = 64
RWKV_GATE_LORA = 160
RWKV_GN_EPS = 64e-5
RWKV_CHUNK = 64
RWKV_LORA_PAD = 512

DIFF_HEADS = 8
DIFF_QK = 64
DIFF_V = 128
GATE_RANK = 256
MOD_RANK = 256

C_CQ = 0
C_CKV = 768
C_R = 1024
C_K = 2048
C_V = 3072
C_CB = 4096
C_CC = 5120
C_CU = 6144
C_DQ = 7168
C_DK = 8192
C_DV = 9216
C_LORA = 10240
C_GL = 10752
C_KROPE = 11008
P_COLS = 11264

TN_UP = 1024
ATTN_SUB = 128
VMEM_CAP = 56 * 1024 * 1024


def _cparams(sem, vmem_bytes):
    limit = int(min(VMEM_CAP, max(vmem_bytes * 1.5 + (4 << 20), 16 << 20)))
    return pltpu.CompilerParams(dimension_semantics=sem, vmem_limit_bytes=limit)


def _mod_index(i, tm, n_x_rows, t_len, n_batch):
    return jnp.where(i < n_x_rows // tm, i // (t_len // tm), n_batch)


def _dot(a, b):
    return jnp.dot(a.astype(BF16), b.astype(BF16), preferred_element_type=F32)


def _dot_nt(a, b):
    return lax.dot_general(a.astype(BF16), b.astype(BF16), (((1,), (1,)), ((), ())),
                           preferred_element_type=F32)


def _split3(x):
    hi = x.astype(BF16)
    r1 = x - hi.astype(F32)
    mid = r1.astype(BF16)
    lo = (r1 - mid.astype(F32)).astype(BF16)
    return hi, mid, lo


def _dot_exact_lhs(m_bf16, x):
    out = None
    for part in _split3(x):
        t = jnp.dot(m_bf16, part, preferred_element_type=F32)
        out = t if out is None else out + t
    return out


def _head_sums(x):
    r = lax.broadcasted_iota(jnp.int32, (LANES, LANES), 0)
    c = lax.broadcasted_iota(jnp.int32, (LANES, LANES), 1)
    ones_bd = jnp.where((r // RWKV_HEAD) == (c // RWKV_HEAD), 1.0, 0.0).astype(BF16)
    parts = _split3(x)
    cols = []
    for j in range(x.shape[1] // LANES):
        acc = None
        for part in parts:
            t = jnp.dot(part[:, j * LANES:(j + 1) * LANES], ones_bd, preferred_element_type=F32)
            acc = t if acc is None else acc + t
        cols.append(acc)
    return jnp.concatenate(cols, axis=1)


def _rope128(x, cos_t, sin_t):
    lane = lax.broadcasted_iota(jnp.int32, x.shape, 1)
    swapped = jnp.where((lane % 64) < 32, pltpu.roll(x, 96, 1), pltpu.roll(x, 32, 1))
    return x * cos_t + swapped * sin_t


def _matmul_kernel(*refs, nk, has_bias, has_res, gate_row, act):
    a_ref, w_ref = refs[0], refs[1]
    pos = 2
    bias_ref = res_ref = mod_ref = None
    if has_bias:
        bias_ref = refs[pos]; pos += 1
    if has_res:
        res_ref = refs[pos]; mod_ref = refs[pos + 1]; pos += 2
    o_ref, acc_ref = refs[pos], refs[pos + 1]
    k = pl.program_id(2)

    @pl.when(k == 0)
    def _():
        acc_ref[...] = jnp.zeros_like(acc_ref)

    acc_ref[...] += _dot(a_ref[...], w_ref[...])

    @pl.when(k == nk - 1)
    def _():
        y = acc_ref[...]
        if has_bias:
            y = y + bias_ref[...]
        if act == "relu2":
            y = jnp.square(jnp.maximum(y, 0.0))
        if has_res:
            y = res_ref[...] + mod_ref[0, gate_row:gate_row + 1, :] * y
        o_ref[...] = y.astype(o_ref.dtype)


def _matmul(a, w, *, tm, tn, tk, out_dtype=F32, bias=None, res=None, mod=None, gate_row=None,
            mod_index=None, act=None, rows=None, name=None):
    m = a.shape[0] if rows is None else rows
    kdim, n = w.shape
    assert a.shape[1] == kdim and m % tm == 0 and n % tn == 0 and kdim % tk == 0
    nk = kdim // tk
    in_specs = [pl.BlockSpec((tm, tk), lambda i, j, k: (i, k)),
                pl.BlockSpec((tk, tn), lambda i, j, k: (k, j))]
    args = [a, w]
    vmem = 2 * tm * tk * a.dtype.itemsize + 2 * tk * tn * w.dtype.itemsize + tm * tn * 4
    vmem += 2 * tm * tn * jnp.dtype(out_dtype).itemsize
    if bias is not None:
        in_specs.append(pl.BlockSpec((1, tn), lambda i, j, k: (0, j)))
        args.append(bias.reshape(1, n).astype(F32))
    if res is not None:
        in_specs.append(pl.BlockSpec((tm, tn), lambda i, j, k: (i, j)))
        in_specs.append(pl.BlockSpec((1, N_MOD, tn), lambda i, j, k: (mod_index(i), 0, j)))
        args += [res, mod]
        vmem += 2 * tm * tn * 4 + 2 * 8 * tn * 4
    kern = functools.partial(_matmul_kernel, nk=nk, has_bias=bias is not None,
                             has_res=res is not None, gate_row=gate_row, act=act)
    return pl.pallas_call(
        kern,
        grid=(m // tm, n // tn, nk),
        in_specs=in_specs,
        out_specs=pl.BlockSpec((tm, tn), lambda i, j, k: (i, j)),
        out_shape=jax.ShapeDtypeStruct((m, n), out_dtype),
        scratch_shapes=[pltpu.VMEM((tm, tn), F32)],
        compiler_params=_cparams(("parallel", "parallel", "arbitrary"), vmem),
        name=name,
    )(*args)


def _nm_matmul_kernel(*refs, has_mod, shift_row, scale_row, act):
    if has_mod:
        x_ref, g_ref, mod_ref, w_ref, o_ref, h_ref = refs
    else:
        x_ref, g_ref, w_ref, o_ref, h_ref = refs

    @pl.when(pl.program_id(1) == 0)
    def _():
        x = x_ref[...].astype(F32)
        y = x * lax.rsqrt(jnp.mean(x * x, axis=-1, keepdims=True) + NORM_EPS) * g_ref[...]
        if has_mod:
            y = y * (1.0 + mod_ref[0, scale_row:scale_row + 1, :]) + mod_ref[0, shift_row:shift_row + 1, :]
        h_ref[...] = y.astype(BF16)

    y = jnp.dot(h_ref[...], w_ref[...].astype(BF16), preferred_element_type=F32)
    if act == "relu2":
        y = jnp.square(jnp.maximum(y, 0.0))
    o_ref[...] = y.astype(o_ref.dtype)


def _nm_matmul(x, g, w, *, tm, tn, out_dtype, x_col_block=0, kdim=None, mod=None, shift_row=None,
               scale_row=None, mod_index=None, act=None, rows=None, name=None):
    m = x.shape[0] if rows is None else rows
    kdim = x.shape[1] if kdim is None else kdim
    n = w.shape[1]
    assert w.shape[0] == kdim and m % tm == 0 and n % tn == 0
    in_specs = [pl.BlockSpec((tm, kdim), lambda i, j: (i, x_col_block)),
                pl.BlockSpec((1, kdim), lambda i, j: (0, 0))]
    args = [x, g.reshape(1, kdim).astype(F32)]
    if mod is not None:
        in_specs.append(pl.BlockSpec((1, N_MOD, kdim), lambda i, j: (mod_index(i), 0, 0)))
        args.append(mod)
    in_specs.append(pl.BlockSpec((kdim, tn), lambda i, j: (0, j)))
    args.append(w)
    vmem = (2 * tm * kdim * x.dtype.itemsize + tm * kdim * 2 + 2 * kdim * tn * w.dtype.itemsize
            + 2 * tm * tn * jnp.dtype(out_dtype).itemsize + tm * tn * 4 + 4 * 8 * kdim * 4)
    kern = functools.partial(_nm_matmul_kernel, has_mod=mod is not None, shift_row=shift_row,
                             scale_row=scale_row, act=act)
    return pl.pallas_call(
        kern,
        grid=(m // tm, n // tn),
        in_specs=in_specs,
        out_specs=pl.BlockSpec((tm, tn), lambda i, j: (i, j)),
        out_shape=jax.ShapeDtypeStruct((m, n), out_dtype),
        scratch_shapes=[pltpu.VMEM((tm, kdim), BF16)],
        compiler_params=_cparams(("parallel", "arbitrary"), vmem),
        name=name,
    )(*args)


def _norm_mod_kernel(x_ref, g_ref, mod_ref, h_ref, *, shift_row, scale_row):
    x = x_ref[...]
    y = x * lax.rsqrt(jnp.mean(x * x, axis=-1, keepdims=True) + NORM_EPS) * g_ref[...]
    y = y * (1.0 + mod_ref[0, scale_row:scale_row + 1, :]) + mod_ref[0, shift_row:shift_row + 1, :]
    h_ref[...] = y.astype(h_ref.dtype)


def _norm_mod(xs, g, mod, *, shift_row, scale_row, mod_index, tm, rows):
    dm = xs.shape[1]
    return pl.pallas_call(
        functools.partial(_norm_mod_kernel, shift_row=shift_row, scale_row=scale_row),
        grid=(rows // tm,),
        in_specs=[pl.BlockSpec((tm, dm), lambda i: (i, 0)), pl.BlockSpec((1, dm), lambda i: (0, 0)),
                  pl.BlockSpec((1, N_MOD, dm), lambda i: (mod_index(i), 0, 0))],
        out_specs=pl.BlockSpec((tm, dm), lambda i: (i, 0)),
        out_shape=jax.ShapeDtypeStruct((rows, dm), BF16),
        compiler_params=_cparams(("parallel",), 2 * tm * dm * 6 + 4 * tm * dm * 4),
        name="norm_mod",
    )(xs, g.reshape(1, dm), mod)


def _fullk_kernel(*refs, has_res, n_cast, gate_row, act, w_rows_are_outputs):
    refs = list(refs)
    if n_cast:
        cast_dsts = refs[-n_cast:]
        cast_srcs = refs[-2 * n_cast - 1:-n_cast - 1]
        refs = refs[:-2 * n_cast - 1] + [refs[-n_cast - 1]]
        for src, dst in zip(cast_srcs, cast_dsts):
            dst[...] = src[...].astype(dst.dtype)
    if has_res:
        a_ref, w_ref, res_ref, mod_ref, o_ref = refs
    else:
        a_ref, w_ref, o_ref = refs
    if w_rows_are_outputs:
        y = lax.dot_general(a_ref[...], w_ref[...], (((1,), (1,)), ((), ())), preferred_element_type=F32)
    else:
        y = jnp.dot(a_ref[...], w_ref[...], preferred_element_type=F32)
    if act == "relu2":
        y = jnp.square(jnp.maximum(y, 0.0))
    if has_res:
        y = res_ref[...] + mod_ref[0, gate_row:gate_row + 1, :] * y
    o_ref[...] = y.astype(o_ref.dtype)


def _cast_rows_per_step(n_rows, n_steps):
    rows = 16
    while n_rows % rows or n_rows // rows > n_steps:
        rows *= 2
    return rows


def _fullk_matmul(a, w, *, tm, tn, out_dtype, act=None, res=None, mod=None, gate_row=None, mod_index=None,
                  rows=None, name=None, layer=None, cast_srcs=(), cast_layer=None, w_rows_are_outputs=False):
    m = a.shape[0] if rows is None else rows
    if w_rows_are_outputs:
        _, n, kdim = w.shape
        w_spec = pl.BlockSpec((None, tn, kdim), lambda i, j: (layer, j, 0))
    else:
        _, kdim, n = w.shape
        w_spec = pl.BlockSpec((None, kdim, tn), lambda i, j: (layer, 0, j))
    assert a.shape[1] == kdim and m % tm == 0 and n % tn == 0 and a.dtype == BF16 and w.dtype == BF16
    in_specs = [pl.BlockSpec((tm, kdim), lambda i, j: (i, 0)), w_spec]
    args = [a, w]
    vmem = 2 * tm * kdim * 2 + 2 * kdim * tn * 2 + 2 * tm * tn * jnp.dtype(out_dtype).itemsize + 2 * tm * tn * 4
    if res is not None:
        in_specs += [pl.BlockSpec((tm, tn), lambda i, j: (i, j)),
                     pl.BlockSpec((1, N_MOD, tn), lambda i, j: (mod_index(i), 0, j))]
        args += [res, mod]
        vmem += 2 * tm * tn * 4
    out_specs = [pl.BlockSpec((tm, tn), lambda i, j: (i, j))]
    out_shape = [jax.ShapeDtypeStruct((m, n), out_dtype)]
    nj = n // tn
    cast_layer = layer if cast_layer is None else cast_layer
    for cast_src in cast_srcs:
        _, c_rows, c_cols = cast_src.shape
        cr = _cast_rows_per_step(c_rows, (m // tm) * nj)
        cast_blk = lambda i, j, last=c_rows // cr - 1: jnp.minimum(i * nj + j, last)
        in_specs.append(pl.BlockSpec((None, cr, c_cols), lambda i, j, blk=cast_blk: (cast_layer, blk(i, j), 0)))
        args.append(cast_src)
        out_specs.append(pl.BlockSpec((cr, c_cols), lambda i, j, blk=cast_blk: (blk(i, j), 0)))
        out_shape.append(jax.ShapeDtypeStruct((c_rows, c_cols), BF16))
        vmem += 2 * cr * c_cols * 6
    return pl.pallas_call(
        functools.partial(_fullk_kernel, has_res=res is not None, n_cast=len(cast_srcs),
                          gate_row=gate_row, act=act, w_rows_are_outputs=w_rows_are_outputs),
        grid=(m // tm, n // tn),
        in_specs=in_specs,
        out_specs=out_specs,
        out_shape=out_shape,
        compiler_params=_cparams(("arbitrary", "arbitrary"), vmem),
        name=name,
    )(*args)


def _merge_kernel(gl_ref, y0_ref, y1_ref, y2_ref, y3_ref, wb_ref, gu_ref, gb_ref, o_ref):
    gl = gl_ref[...].astype(BF16)
    bw = wb_ref.shape[0] // 4
    acc = None
    for i, y_ref in enumerate((y0_ref, y1_ref, y2_ref, y3_ref)):
        gate = jax.nn.sigmoid(jnp.dot(gl, gu_ref[i], preferred_element_type=F32) + gb_ref[i:i + 1, :])
        term = gate * jnp.dot(y_ref[...], wb_ref[i * bw:(i + 1) * bw, :], preferred_element_type=F32)
        acc = term if acc is None else acc + term
    o_ref[...] = acc.astype(o_ref.dtype)


def _merge(p, ys, wb, gu, gb, *, layer, tm, tn, rows):
    n = wb.shape[1]
    bw = wb.shape[0] // 4
    gr = gu.shape[2]
    y_spec = pl.BlockSpec((tm, bw), lambda i, j: (i, 0))
    vmem = (2 * tm * gr * 4 + 4 * 2 * tm * bw * 2 + 2 * 4 * bw * tn * 2 + 2 * 4 * gr * tn * 2
            + 2 * tm * tn * 2 + 3 * tm * tn * 4)
    return pl.pallas_call(
        _merge_kernel,
        grid=(rows // tm, n // tn),
        in_specs=[pl.BlockSpec((tm, gr), lambda i, j: (i, C_GL // GATE_RANK)),
                  y_spec, y_spec, y_spec, y_spec,
                  pl.BlockSpec((4 * bw, tn), lambda i, j: (0, j)),
                  pl.BlockSpec((None, 4, gr, tn), lambda i, j: (layer, 0, 0, j)),
                  pl.BlockSpec((None, 4, tn), lambda i, j: (layer, 0, j))],
        out_specs=pl.BlockSpec((tm, tn), lambda i, j: (i, j)),
        out_shape=jax.ShapeDtypeStruct((rows, n), BF16),
        compiler_params=_cparams(("parallel", "arbitrary"), vmem),
        name="merge",
    )(p, *ys, wb, gu, gb)


def _softmax_parts(s_list, scale):
    m = None
    for s in s_list:
        sm = jnp.max(s, axis=-1, keepdims=True)
        m = sm if m is None else jnp.maximum(m, sm)
    e_list = [jnp.exp2((s - m) * (scale * LOG2_E)) for s in s_list]
    l = None
    for e in e_list:
        es = jnp.sum(e, axis=-1, keepdims=True)
        l = es if l is None else l + es
    return e_list, l


def _attend(q, ks, vs, scale):
    e_list, l = _softmax_parts([_dot_nt(q, k) for k in ks], scale)
    o = None
    for e, v in zip(e_list, vs):
        t = jnp.dot(e.astype(BF16), v, preferred_element_type=F32)
        o = t if o is None else o + t
    return o / l


def _attend_many(qs, ks, vs, scale, sub):
    parts = [q[r:r + sub] for q in qs for r in range(0, q.shape[0], sub)]
    n = len(parts)
    scores, soft, outs = {}, {}, []
    for t in range(n + 2):
        if t < n:
            scores[t] = [_dot_nt(parts[t], k) for k in ks]
        if 0 <= t - 1 < n:
            soft[t - 1] = _softmax_parts(scores.pop(t - 1), scale)
        if 0 <= t - 2 < n:
            e_list, l = soft.pop(t - 2)
            o = None
            for e, v in zip(e_list, vs):
                pv = jnp.dot(e.astype(BF16), v, preferred_element_type=F32)
                o = pv if o is None else o + pv
            outs.append(o / l)
    per_q = len(parts) // len(qs)
    return [jnp.concatenate(outs[i * per_q:(i + 1) * per_q], axis=0) for i in range(len(qs))]


def _mla_kernel(cos_ref, sin_ref, q_ref, knx_ref, knc_ref, krx_ref, krc_ref, vx_ref, vc_ref, o_ref,
                kx_s, kc_s, *, tq, scale):
    i = pl.program_id(2)

    @pl.when(i == 0)
    def _():
        kx_s[:, :MLA_NOPE] = knx_ref[...]
        kx_s[:, MLA_NOPE:] = _rope128(krx_ref[...], cos_ref[...], sin_ref[...]).astype(BF16)
        kc_s[:, :MLA_NOPE] = knc_ref[...]
        kc_s[:, MLA_NOPE:] = krc_ref[...].astype(BF16)

    row0 = pl.multiple_of(i * tq, tq)
    q = q_ref[...]
    q_rope = _rope128(q[:, MLA_NOPE:].astype(F32), cos_ref[pl.ds(row0, tq), :], sin_ref[pl.ds(row0, tq), :])
    q = jnp.concatenate([q[:, :MLA_NOPE], q_rope.astype(BF16)], axis=1)
    (o,) = _attend_many((q,), (kc_s[...], kx_s[...]), (vc_ref[...], vx_ref[...]), scale, ATTN_SUB)
    o_ref[...] = o.astype(o_ref.dtype)


def _mla_ctx_kernel(q_ref, knc_ref, krc_ref, vc_ref, y_hbm, o_ref, *, scale):
    del y_hbm
    kc = jnp.concatenate([knc_ref[...], krc_ref[...].astype(BF16)], axis=1)
    o = _attend(q_ref[...], (kc,), (vc_ref[...],), scale)
    o_ref[...] = o.astype(o_ref.dtype)


def _diff_finish(o1, o2, lam_ref, g_ref, out_scale):
    o = o1 - lam_ref[...] * o2
    y = o * lax.rsqrt(jnp.mean(o * o, axis=-1, keepdims=True) + NORM_EPS) * g_ref[...]
    return y * out_scale


def _diff_halves(q):
    lane = lax.broadcasted_iota(jnp.int32, q.shape, 1)
    return jnp.where(lane < DIFF_QK, q, 0.0).astype(BF16), jnp.where(lane < DIFF_QK, 0.0, q).astype(BF16)


def _diff_kernel(cos_ref, sin_ref, lam_ref, g_ref, q_ref, kx_ref, kc_ref, vx_ref, vc_ref, o_ref,
                 kx_s, kc_s, vx_s, vc_s, *, tq, scale, out_scale):
    i = pl.program_id(2)

    @pl.when(i == 0)
    def _():
        kx_s[...] = _rope128(kx_ref[...], cos_ref[...], sin_ref[...]).astype(BF16)
        kc_s[...] = kc_ref[...].astype(BF16)
        vx_s[...] = vx_ref[...].astype(BF16)
        vc_s[...] = vc_ref[...].astype(BF16)

    row0 = pl.multiple_of(i * tq, tq)
    q1, q2 = _diff_halves(_rope128(q_ref[...], cos_ref[pl.ds(row0, tq), :], sin_ref[pl.ds(row0, tq), :]))
    ks, vs = (kc_s[...], kx_s[...]), (vc_s[...], vx_s[...])
    o1, o2 = _attend_many((q1, q2), ks, vs, scale, ATTN_SUB)
    y = _diff_finish(o1, o2, lam_ref, g_ref, out_scale)
    o_ref[...] = y.astype(o_ref.dtype)


def _diff_ctx_kernel(lam_ref, g_ref, q_ref, kc_ref, vc_ref, y_hbm, o_ref, *, scale, out_scale):
    del y_hbm
    q1, q2 = _diff_halves(q_ref[...])
    ks, vs = (kc_ref[...].astype(BF16),), (vc_ref[...].astype(BF16),)
    y = _diff_finish(_attend(q1, ks, vs, scale), _attend(q2, ks, vs, scale), lam_ref, g_ref, out_scale)
    o_ref[...] = y.astype(o_ref.dtype)


def _mla_attention(q, kv, p, cos_t, sin_t, *, n_batch, t_len, c_len, tq, with_ctx):
    n_x = n_batch * t_len
    nq = t_len // tq
    out_rows = n_x + (n_batch * c_len if with_ctx else 0)
    cblk0 = n_x // c_len
    scale = (MLA_NOPE + MLA_ROPE) ** -0.5
    qmap = lambda b, h, i: (b * nq + i, h)
    full = lambda b, h, i: (0, 0)
    vmem = (4 * t_len * LANES * 4 + 2 * tq * 256 * 2 + 2 * (t_len + c_len) * LANES * (2 + 4 + 2)
            + (t_len + c_len) * 256 * 2 + 2 * tq * LANES * 2 + 6 * tq * (t_len + c_len) * 4)
    y = pl.pallas_call(
        functools.partial(_mla_kernel, tq=tq, scale=scale),
        grid=(n_batch, MLA_HEADS, nq),
        in_specs=[pl.BlockSpec((t_len, LANES), full), pl.BlockSpec((t_len, LANES), full),
                  pl.BlockSpec((tq, MLA_QK_PAD), qmap),
                  pl.BlockSpec((t_len, MLA_NOPE), lambda b, h, i: (b, h)),
                  pl.BlockSpec((c_len, MLA_NOPE), lambda b, h, i: (cblk0 + b, h)),
                  pl.BlockSpec((t_len, LANES), lambda b, h, i: (b, C_KROPE // LANES)),
                  pl.BlockSpec((c_len, LANES), lambda b, h, i: (cblk0 + b, C_KROPE // LANES)),
                  pl.BlockSpec((t_len, MLA_V), lambda b, h, i: (b, MLA_HEADS + h)),
                  pl.BlockSpec((c_len, MLA_V), lambda b, h, i: (cblk0 + b, MLA_HEADS + h))],
        out_specs=pl.BlockSpec((tq, MLA_V), qmap),
        out_shape=jax.ShapeDtypeStruct((out_rows, MLA_HEADS * MLA_V), BF16),
        scratch_shapes=[pltpu.VMEM((t_len, MLA_QK_PAD), BF16), pltpu.VMEM((c_len, MLA_QK_PAD), BF16)],
        compiler_params=_cparams(("parallel", "parallel", "arbitrary"), vmem),
        name="mla_attention",
    )(cos_t, sin_t, q, kv, kv, p, p, kv, kv)
    if not with_ctx:
        return y
    cmap = lambda b, h: (cblk0 + b, h)
    return pl.pallas_call(
        functools.partial(_mla_ctx_kernel, scale=scale),
        grid=(n_batch, MLA_HEADS),
        in_specs=[pl.BlockSpec((c_len, MLA_QK_PAD), cmap),
                  pl.BlockSpec((c_len, MLA_NOPE), cmap),
                  pl.BlockSpec((c_len, LANES), lambda b, h: (cblk0 + b, C_KROPE // LANES)),
                  pl.BlockSpec((c_len, MLA_V), lambda b, h: (cblk0 + b, MLA_HEADS + h)),
                  pl.BlockSpec(memory_space=pl.ANY)],
        out_specs=pl.BlockSpec((c_len, MLA_V), cmap),
        out_shape=jax.ShapeDtypeStruct(y.shape, y.dtype),
        input_output_aliases={4: 0},
        compiler_params=_cparams(("parallel", "parallel"), 16 * c_len * c_len * 4 + 8 * c_len * 256 * 4),
        name="mla_attention_ctx",
    )(q, kv, p, kv, y)


def _diff_attention(p, cos_t, sin_t, lam_row, g_row, *, n_batch, t_len, c_len, tq, with_ctx, out_scale):
    n_x = n_batch * t_len
    nq = t_len // tq
    out_rows = n_x + (n_batch * c_len if with_ctx else 0)
    cblk0 = n_x // c_len
    scale = DIFF_QK ** -0.5
    full = lambda b, h, i: (0, 0)
    qblk, kblk, vblk = C_DQ // LANES, C_DK // LANES, C_DV // LANES
    vmem = (4 * t_len * LANES * 4 + 2 * tq * LANES * 4 + 4 * (t_len + c_len) * LANES * 4
            + 2 * (t_len + c_len) * LANES * 2 + 2 * tq * LANES * 2 + 8 * tq * (t_len + c_len) * 4)
    y = pl.pallas_call(
        functools.partial(_diff_kernel, tq=tq, scale=scale, out_scale=out_scale),
        grid=(n_batch, DIFF_HEADS, nq),
        in_specs=[pl.BlockSpec((t_len, LANES), full), pl.BlockSpec((t_len, LANES), full),
                  pl.BlockSpec((1, DIFF_V), full), pl.BlockSpec((1, DIFF_V), full),
                  pl.BlockSpec((tq, LANES), lambda b, h, i: (b * nq + i, qblk + h)),
                  pl.BlockSpec((t_len, LANES), lambda b, h, i: (b, kblk + h)),
                  pl.BlockSpec((c_len, LANES), lambda b, h, i: (cblk0 + b, kblk + h)),
                  pl.BlockSpec((t_len, LANES), lambda b, h, i: (b, vblk + h)),
                  pl.BlockSpec((c_len, LANES), lambda b, h, i: (cblk0 + b, vblk + h))],
        out_specs=pl.BlockSpec((tq, DIFF_V), lambda b, h, i: (b * nq + i, h)),
        out_shape=jax.ShapeDtypeStruct((out_rows, DIFF_HEADS * DIFF_V), BF16),
        scratch_shapes=[pltpu.VMEM((t_len, LANES), BF16), pltpu.VMEM((c_len, LANES), BF16),
                        pltpu.VMEM((t_len, LANES), BF16), pltpu.VMEM((c_len, LANES), BF16)],
        compiler_params=_cparams(("parallel", "parallel", "arbitrary"), vmem),
        name="diff_attention",
    )(cos_t, sin_t, lam_row, g_row, p, p, p, p, p)
    if not with_ctx:
        return y
    one = lambda b, h: (0, 0)
    return pl.pallas_call(
        functools.partial(_diff_ctx_kernel, scale=scale, out_scale=out_scale),
        grid=(n_batch, DIFF_HEADS),
        in_specs=[pl.BlockSpec((1, DIFF_V), one), pl.BlockSpec((1, DIFF_V), one),
                  pl.BlockSpec((c_len, LANES), lambda b, h: (cblk0 + b, qblk + h)),
                  pl.BlockSpec((c_len, LANES), lambda b, h: (cblk0 + b, kblk + h)),
                  pl.BlockSpec((c_len, LANES), lambda b, h: (cblk0 + b, vblk + h)),
                  pl.BlockSpec(memory_space=pl.ANY)],
        out_specs=pl.BlockSpec((c_len, DIFF_V), lambda b, h: (cblk0 + b, h)),
        out_shape=jax.ShapeDtypeStruct(y.shape, y.dtype),
        input_output_aliases={5: 0},
        compiler_params=_cparams(("parallel", "parallel"), 24 * c_len * c_len * 4 + 8 * c_len * LANES * 4),
        name="diff_attention_ctx",
    )(lam_row, g_row, p, p, p, y)


def _prep_kernel(r_ref, k_ref, v_ref, cb_ref, cc_ref, cu_ref, lo_ref, hp_ref, hn_ref,
                 mu_ref, mul_ref, kk_w_ref, ka_w_ref, rk_w_ref, w0_ref, a0_ref, w2_ref, a2_ref, g2_ref,
                 cw_ref,
                 r_o, kk_o, v_o, lw_o, kd_o, ka_o, bonus_o, g_o, conv_o, *, tr, n_x, t_len, c_len):
    i = pl.program_id(0)
    g0 = i * tr
    seq = jnp.where(g0 < n_x, t_len, c_len)
    has_prev = ((g0 % seq) != 0).astype(F32)
    has_next = (((g0 + tr) % seq) != 0).astype(F32)
    row = lax.broadcasted_iota(jnp.int32, (tr, 1), 0)

    def neighbours(x, col0):
        width = x.shape[1]
        before = hp_ref[HALO - 1:HALO, col0:col0 + width] * has_prev
        after = hn_ref[0:1, col0:col0 + width] * has_next
        prev = jnp.where(row == 0, before, pltpu.roll(x, 1, 0))
        nxt = jnp.where(row == tr - 1, after, pltpu.roll(x, tr - 1, 0))
        return prev, nxt

    def shifted(x, col0, mu0, mu1):
        prev, nxt = neighbours(x, col0)
        return x + mu0 * (prev - x) + mu1 * (nxt - x)

    r = shifted(r_ref[...], C_R, mu_ref[0:1, :BRANCH_W], mu_ref[1:2, :BRANCH_W])
    k = shifted(k_ref[...], C_K, mu_ref[0:1, BRANCH_W:2 * BRANCH_W], mu_ref[1:2, BRANCH_W:2 * BRANCH_W])
    v = shifted(v_ref[...], C_V, mu_ref[0:1, 2 * BRANCH_W:], mu_ref[1:2, 2 * BRANCH_W:])
    lo = shifted(lo_ref[...], C_LORA, mul_ref[0:1, :], mul_ref[1:2, :])
    wd = jnp.tanh(lo[:, :2 * RWKV_LORA])
    ad = lo[:, 2 * RWKV_LORA:4 * RWKV_LORA]
    gd = jax.nn.sigmoid(lo[:, 4 * RWKV_LORA:])
    w_pre = _dot(wd, w2_ref[...]) + w0_ref[...]
    a_sig = jax.nn.sigmoid(_dot(ad, a2_ref[...]) + a0_ref[...])
    g_o[...] = _dot(gd, g2_ref[...])
    w_log = -(jnp.maximum(-w_pre, 0.0) + jnp.log(1.0 + jnp.exp(-jnp.abs(w_pre)))) - 0.5
    lw = -jnp.exp(w_log)
    kkf = k * kk_w_ref[...]
    kk = kkf * lax.rsqrt(_head_sums(kkf * kkf) + 1e-12)
    r_o[...] = r
    kk_o[...] = kk
    v_o[...] = v
    k_sum = None
    for d in range(2):
        a_d = a_sig[:, d * BRANCH_W:(d + 1) * BRANCH_W]
        k_d = k * (1.0 + (a_d - 1.0) * ka_w_ref[...])
        lw_o[d] = lw[:, d * BRANCH_W:(d + 1) * BRANCH_W]
        kd_o[d] = k_d
        ka_o[d] = kk * a_d
        k_sum = k_d if k_sum is None else k_sum + k_d
    bonus_o[...] = _head_sums(r * k_sum * rk_w_ref[...]) * v

    z = cc_ref[...] * cu_ref[...]
    z_before = hp_ref[HALO - 1:HALO, C_CC:C_CC + BRANCH_W] * hp_ref[HALO - 1:HALO, C_CU:C_CU + BRANCH_W] * has_prev
    z_after = hn_ref[0:1, C_CC:C_CC + BRANCH_W] * hn_ref[0:1, C_CU:C_CU + BRANCH_W] * has_next
    z_prev = jnp.where(row == 0, z_before, pltpu.roll(z, 1, 0))
    z_next = jnp.where(row == tr - 1, z_after, pltpu.roll(z, tr - 1, 0))
    y = cb_ref[...] * (cw_ref[0:1, :] * z_prev + cw_ref[1:2, :] * z + cw_ref[2:3, :] * z_next)
    conv_o[...] = y.astype(conv_o.dtype)


def _mixer_prep(p, mu_rkv, mu_lora, kk_w, ka_w, rk_w, w0, a0, w2bd, a2bd, g2p, conv_w, *, tr, n_x, t_len,
                c_len):
    n_rows = p.shape[0]
    bw = BRANCH_W
    last_halo = n_rows // HALO - 1
    col = lambda c: (lambda i: (i, c))
    const = lambda i: (0, 0)
    main = [pl.BlockSpec((tr, bw), col(C_R // bw)), pl.BlockSpec((tr, bw), col(C_K // bw)),
            pl.BlockSpec((tr, bw), col(C_V // bw)), pl.BlockSpec((tr, bw), col(C_CB // bw)),
            pl.BlockSpec((tr, bw), col(C_CC // bw)), pl.BlockSpec((tr, bw), col(C_CU // bw)),
            pl.BlockSpec((tr, RWKV_LORA_PAD), col(C_LORA // RWKV_LORA_PAD)),
            pl.BlockSpec((HALO, P_COLS), lambda i: (jnp.maximum(i * (tr // HALO) - 1, 0), 0)),
            pl.BlockSpec((HALO, P_COLS), lambda i: (jnp.minimum((i + 1) * (tr // HALO), last_halo), 0))]
    params = [mu_rkv, mu_lora, kk_w, ka_w, rk_w, w0, a0, w2bd, a2bd, g2p, conv_w]
    param_specs = [pl.BlockSpec(a.shape, const) for a in params]
    row_spec = pl.BlockSpec((tr, bw), lambda i: (i, 0))
    dir_spec = pl.BlockSpec((2, tr, bw), lambda i: (0, i, 0))
    f32_rows = jax.ShapeDtypeStruct((n_rows, bw), F32)
    f32_dirs = jax.ShapeDtypeStruct((2, n_rows, bw), F32)
    vmem = 2 * (7 * tr * bw * 4 + 2 * HALO * P_COLS * 4 + 12 * tr * bw * 4) + 30 * tr * bw * 4
    return pl.pallas_call(
        functools.partial(_prep_kernel, tr=tr, n_x=n_x, t_len=t_len, c_len=c_len),
        grid=(n_rows // tr,),
        in_specs=main + param_specs,
        out_specs=[row_spec, row_spec, row_spec, dir_spec, dir_spec, dir_spec, row_spec, row_spec, row_spec],
        out_shape=[f32_rows, f32_rows, f32_rows, f32_dirs, f32_dirs, f32_dirs, f32_rows, f32_rows,
                   jax.ShapeDtypeStruct((n_rows, bw), BF16)],
        compiler_params=_cparams(("parallel",), vmem),
        name="mixer_prep",
    )(p, p, p, p, p, p, p, p, p, *params)


def _rwkv_kernel(r0_ref, kk0_ref, v0_ref, r1_ref, kk1_ref, v1_ref, lw0_ref, kd0_ref, ka0_ref,
                 lw1_ref, kd1_ref, ka1_ref, o0_ref, o1_ref, s_ref):
    C = RWKV_CHUNK
    W = 2 * RWKV_HEAD
    n_pairs = RWKV_HEADS // 2

    @pl.when(pl.program_id(1) == 0)
    def _():
        s_ref[...] = jnp.zeros_like(s_ref)

    mm, mm_nt = _dot, _dot_nt
    lane = lax.broadcasted_iota(jnp.int32, (C, W), 1)
    row = lax.broadcasted_iota(jnp.int32, (C, W), 0)
    first = lane < RWKV_HEAD
    rc = lax.broadcasted_iota(jnp.int32, (C, C), 0)
    cc = lax.broadcasted_iota(jnp.int32, (C, C), 1)

    def bdiag(x):
        return jnp.concatenate([jnp.where(first, x, jnp.zeros_like(x)), jnp.where(first, jnp.zeros_like(x), x)],
                               axis=0)

    def direction(sign, r_ref, kk_ref, v_ref, lw_ref, kd_ref, ka_ref):
        m_incl = jnp.where((rc - cc) * sign >= 0, 1.0, 0.0).astype(BF16)
        lw = lw_ref[0]
        cum = _dot_exact_lhs(m_incl, lw)
        tot = jnp.sum(lw, axis=0, keepdims=True)
        e_ninc = jnp.exp(-cum)
        e_rem = jnp.exp(tot - cum)
        kk, ka, kd = kk_ref[...], ka_ref[0], kd_ref[0]
        order = (row - (lane % RWKV_HEAD)) * sign
        return dict(a_t=kk * jnp.exp(cum - lw), b_t=-ka * e_ninc, k_t=kd * e_ninc, r_t=r_ref[...] * jnp.exp(cum),
                    b_h=-ka * e_rem, k_h=kd * e_rem, v=v_ref[...], e_tot=jnp.exp(tot), strict=order > 0,
                    incl=order >= 0, eye=jnp.where(order == 0, 1.0, 0.0).astype(F32))

    dirs = (direction(1, r0_ref, kk0_ref, v0_ref, lw0_ref, kd0_ref, ka0_ref),
            direction(-1, r1_ref, kk1_ref, v1_ref, lw1_ref, kd1_ref, ka1_ref))
    o_refs = (o0_ref, o1_ref)

    chains = [(d, p) for d in range(2) for p in range(n_pairs)]
    n = range(len(chains))
    sl = [slice(W * p, W * (p + 1)) for (_, p) in chains]
    dd = [dirs[d] for (d, _) in chains]
    ar = [jnp.concatenate([dd[c]["a_t"][:, sl[c]], dd[c]["r_t"][:, sl[c]]], axis=0).astype(BF16) for c in n]
    a_bd = [bdiag(dd[c]["a_t"][:, sl[c]].astype(BF16)) for c in n]
    bk_bd = [jnp.concatenate([bdiag(dd[c]["b_t"][:, sl[c]].astype(BF16)), bdiag(dd[c]["k_t"][:, sl[c]].astype(BF16))],
                             axis=0) for c in n]
    v_bd = [bdiag(dd[c]["v"][:, sl[c]].astype(BF16)) for c in n]
    g = [mm_nt(ar[c], bk_bd[c]) for c in n]
    l_pow = [jnp.where(dd[c]["strict"], g[c][:C, :W], 0.0) for c in n]
    m_ak = [jnp.where(dd[c]["strict"], g[c][:C, W:], 0.0).astype(BF16) for c in n]
    a_r = [jnp.where(jnp.concatenate([dd[c]["incl"], dd[c]["incl"]], axis=1), g[c][C:], 0.0).astype(BF16) for c in n]
    mv = [mm(m_ak[c], v_bd[c]) for c in n]
    l_bd = [bdiag(l_pow[c].astype(BF16)) for c in n]
    t_inv = [dd[c]["eye"] + l_pow[c] for c in n]
    l_pow = [mm(l_pow[c], l_bd[c]) for c in n]
    for _ in range(int(math.log2(C)) - 2):
        l_bd = [bdiag(l_pow[c].astype(BF16)) for c in n]
        lt = [mm(jnp.concatenate([l_pow[c], t_inv[c]], axis=0), l_bd[c]) for c in n]
        l_pow = [lt[c][:C] for c in n]
        t_inv = [t_inv[c] + lt[c][C:] for c in n]
    t_inv = [t_inv[c] + mm(t_inv[c], bdiag(l_pow[c].astype(BF16))) for c in n]
    wu = [mm(t_inv[c], jnp.concatenate([a_bd[c], bdiag(mv[c].astype(BF16))], axis=1)) for c in n]
    s_bd = [s_ref[d, p] for (d, p) in chains]
    wr = [mm(jnp.concatenate([wu[c][:, :W], dd[c]["r_t"][:, sl[c]]], axis=0), s_bd[c]) for c in n]
    zv = [jnp.concatenate([bdiag((wr[c][:C] + wu[c][:, W:]).astype(BF16)), v_bd[c]], axis=0) for c in n]
    for c, (d, p) in enumerate(chains):
        o_refs[d][:, sl[c]] = wr[c][C:] + mm(a_r[c], zv[c])
    for c, (d, p) in enumerate(chains):
        bk_h = jnp.concatenate([bdiag(dd[c]["b_h"][:, sl[c]]), bdiag(dd[c]["k_h"][:, sl[c]])], axis=0)
        tot_col = jnp.broadcast_to(dd[c]["e_tot"][:, sl[c]], (W, W)).T
        s_ref[d, p] = tot_col * s_bd[c] + mm(bk_h.T, zv[c])


def _rwkv_scan(r, kk, v, lw, kd, ka, *, n_batch, t_len, c_len):
    C = RWKV_CHUNK
    rows, width = r.shape
    nct, nxt = c_len // C, t_len // C
    ctx_blk0 = n_batch * t_len // C

    def blk(b, d, s):
        j_c = s if d == 0 else nct - 1 - s
        j_x = s - nct if d == 0 else nxt - 1 - (s - nct)
        return jnp.where(s < nct, ctx_blk0 + b * nct + j_c, b * nxt + j_x)

    def shared(d):
        return pl.BlockSpec((C, width), lambda b, s: (blk(b, d, s), 0))

    def per_dir(d):
        return pl.BlockSpec((1, C, width), lambda b, s: (d, blk(b, d, s), 0))

    vmem = 2 * 14 * C * width * 4 + RWKV_HEADS * 128 * 128 * 4 + 128 * C * width * 4
    out = jax.ShapeDtypeStruct((rows, width), F32)
    return pl.pallas_call(
        _rwkv_kernel,
        grid=(n_batch, nct + nxt),
        in_specs=[shared(0), shared(0), shared(0), shared(1), shared(1), shared(1),
                  per_dir(0), per_dir(0), per_dir(0), per_dir(1), per_dir(1), per_dir(1)],
        out_specs=[shared(0), shared(1)],
        out_shape=[out, out],
        scratch_shapes=[pltpu.VMEM((2, RWKV_HEADS // 2, 128, 128), F32)],
        compiler_params=_cparams(("parallel", "arbitrary"), vmem),
        name="rwkv_scan",
    )(r, kk, v, r, kk, v, lw, kd, ka, lw, kd, ka)


def _rwkv_readout_kernel(o0_ref, o1_ref, bonus_ref, g_ref, lng_ref, lnb_ref, y_ref):
    o = o0_ref[...] + o1_ref[...]
    mean = _head_sums(o) * (1.0 / RWKV_HEAD)
    cen = o - mean
    var = _head_sums(cen * cen) * (1.0 / RWKV_HEAD)
    o_n = cen * lax.rsqrt(var + RWKV_GN_EPS) * lng_ref[...] + lnb_ref[...]
    y_ref[...] = ((o_n + bonus_ref[...]) * g_ref[...]).astype(y_ref.dtype)


def _rwkv_readout(o_dirs, bonus, g_out, ln_g, ln_b, *, tr):
    n_rows, bw = bonus.shape
    row_spec = pl.BlockSpec((tr, bw), lambda i: (i, 0))
    const = pl.BlockSpec((1, bw), lambda i: (0, 0))
    return pl.pallas_call(
        _rwkv_readout_kernel,
        grid=(n_rows // tr,),
        in_specs=[row_spec, row_spec, row_spec, row_spec, const, const],
        out_specs=row_spec,
        out_shape=jax.ShapeDtypeStruct((n_rows, bw), BF16),
        compiler_params=_cparams(("parallel",), 2 * 5 * tr * bw * 4 + 16 * tr * bw * 4),
        name="rwkv_readout",
    )(o_dirs[0], o_dirs[1], bonus, g_out, ln_g.reshape(1, bw), ln_b.reshape(1, bw))


def _rmsnorm_kernel(x_ref, g_ref, o_ref):
    x = x_ref[...]
    o_ref[...] = x * lax.rsqrt(jnp.mean(x * x, axis=-1, keepdims=True) + NORM_EPS) * g_ref[...]


def _final_norm(xs, g, *, rows, tm):
    dm = xs.shape[1]
    return pl.pallas_call(
        _rmsnorm_kernel,
        grid=(rows // tm,),
        in_specs=[pl.BlockSpec((tm, dm), lambda i: (i, 0)), pl.BlockSpec((1, dm), lambda i: (0, 0))],
        out_specs=pl.BlockSpec((tm, dm), lambda i: (i, 0)),
        out_shape=jax.ShapeDtypeStruct((rows, dm), F32),
        compiler_params=_cparams(("parallel",), 4 * tm * dm * 4),
        name="final_norm",
    )(xs, g.reshape(1, dm))


def _rope_tables128(n_tokens):
    rows = n_tokens // GRID_W
    row = jnp.repeat(jnp.arange(rows, dtype=F32), GRID_W)
    col = jnp.tile(jnp.arange(GRID_W, dtype=F32), rows)
    n_freq = 64 // 4
    inv = ROPE_BASE ** (-jnp.arange(n_freq, dtype=F32) / n_freq)
    ang = jnp.concatenate([row[:, None] * inv, col[:, None] * inv], axis=-1)
    cos, sin = jnp.cos(ang), jnp.sin(ang)
    return jnp.concatenate([cos, cos, cos, cos], axis=-1), jnp.concatenate([-sin, sin, -sin, sin], axis=-1)


def _block_diag2(w2):
    z = jnp.zeros_like(w2[0])
    return jnp.concatenate([jnp.concatenate([w2[0], z], axis=1), jnp.concatenate([z, w2[1]], axis=1)], axis=0)


def kernel(x, c, ctx, c_ctx, norm1_g, norm2_g, mod_down, mod_up, mod_b, w_in, mla_q_norm_g, mla_w_uq,
           mla_kv_norm_g, mla_w_ukv, rwkv_mu, rwkv_w0, rwkv_w2, rwkv_a0, rwkv_a2, rwkv_g2, rwkv_k_k,
           rwkv_k_a, rwkv_r_k, rwkv_ln_g, rwkv_ln_b, conv_w, diff_lambda, diff_norm_g, w_branch, gate_down,
           gate_up, gate_b, w_out, mlp_w1, mlp_w2, final_norm_g):
    n_batch, t_len, dm = x.shape
    c_len = ctx.shape[1]
    depth = w_in.shape[0]
    bw = BRANCH_W
    n_x = n_batch * t_len
    n_c = n_batch * c_len
    n_rows = n_x + n_c
    tm = 512 if (t_len % 512 == 0 and n_c % 512 == 0) else 256
    tq = 512 if t_len % 512 == 0 else min(256, c_len)
    tr = min(256, c_len)
    assert dm == D_MODEL and t_len % tm == 0 and n_c % tm == 0 and n_x % c_len == 0
    assert t_len % tq == 0 and c_len % RWKV_CHUNK == 0 and t_len % c_len == 0
    tm_big = 1024 if (t_len % 1024 == 0 and n_c % 1024 == 0) else tm
    mod_index = functools.partial(_mod_index, tm=tm, n_x_rows=n_x, t_len=t_len, n_batch=n_batch)
    mod_index_big = functools.partial(_mod_index, tm=tm_big, n_x_rows=n_x, t_len=t_len, n_batch=n_batch)

    zeros = lambda *s: jnp.zeros(s, F32)
    lora_w = 4 * RWKV_LORA + RWKV_GATE_LORA
    rw0 = MLA_Q_LORA + MLA_KV_LORA + MLA_ROPE
    cv0 = rw0 + 3 * bw + lora_w
    w_in_t = jnp.swapaxes(w_in, 1, 2)
    w_in_p = jnp.concatenate(
        [w_in_t[:, :C_R], w_in_t[:, rw0:rw0 + 3 * bw], w_in_t[:, cv0:],
         w_in_t[:, rw0 + 3 * bw:cv0], zeros(depth, RWKV_LORA_PAD - lora_w, dm), jnp.swapaxes(gate_down, 1, 2),
         w_in_t[:, C_R:rw0], zeros(depth, P_COLS - C_KROPE - MLA_ROPE, dm)], axis=1).astype(BF16)
    w_uq_p = jnp.pad(mla_w_uq.reshape(depth, MLA_Q_LORA, MLA_HEADS, MLA_NOPE + MLA_ROPE),
                     ((0, 0), (0, 0), (0, 0), (0, MLA_QK_PAD - MLA_NOPE - MLA_ROPE))
                     ).reshape(depth, MLA_Q_LORA, MLA_HEADS * MLA_QK_PAD).astype(BF16)
    w_ukv_r = mla_w_ukv.reshape(depth, MLA_KV_LORA, MLA_HEADS, MLA_NOPE + MLA_V)
    w_ukv_p = jnp.concatenate([w_ukv_r[..., :MLA_NOPE].reshape(depth, MLA_KV_LORA, -1),
                               w_ukv_r[..., MLA_NOPE:].reshape(depth, MLA_KV_LORA, -1)], axis=-1).astype(BF16)
    w_branch_2d = w_branch.reshape(depth, 4 * bw, dm)
    gate_up_b = jnp.moveaxis(gate_up, 2, 1).astype(BF16)
    g2_p = jnp.pad(rwkv_g2, ((0, 0), (0, RWKV_LORA_PAD - 4 * RWKV_LORA - RWKV_GATE_LORA), (0, 0))).astype(BF16)
    mu_lora = jnp.pad(rwkv_mu[:, :, 3 * bw:], ((0, 0), (0, 0), (0, RWKV_LORA_PAD - lora_w)))

    cond = jnp.concatenate([c, c_ctx[None, :], zeros(16 - n_batch - 1, dm)], axis=0)
    cond = jax.nn.silu(cond)
    mods = []
    for l in range(depth):
        low = _matmul(cond, mod_down[l], tm=16, tn=MOD_RANK, tk=dm, name="mod_down")
        up = _matmul(low, mod_up[l], tm=16, tn=2048, tk=MOD_RANK, bias=mod_b[l], name="mod_up")
        mods.append(up.reshape(16, N_MOD, dm))

    cos_t, sin_t = _rope_tables128(t_len)
    xs = jnp.concatenate([x.reshape(n_x, dm), ctx.reshape(n_c, dm)], axis=0)

    for l in range(depth):
        need_ctx = l < depth - 1
        mod = mods[l]
        lam_init = 0.8 - 0.6 * math.exp(-0.3 * l)
        lq1, lk1, lq2, lk2 = diff_lambda[l]
        lam = jnp.exp(jnp.sum(lq1 * lk1)) - jnp.exp(jnp.sum(lq2 * lk2)) + lam_init
        lam_row = jnp.full((1, DIFF_V), 1.0, F32) * lam

        h1 = _norm_mod(xs, norm1_g[l], mod, shift_row=0, scale_row=1, mod_index=mod_index, tm=tm,
                       rows=n_rows)
        p, w1_b, w_out_b, w_branch_b = _fullk_matmul(h1, w_in_p, layer=l, tm=tm_big, tn=512, out_dtype=F32,
                                                     name="in_proj", cast_srcs=(mlp_w1, w_out, w_branch_2d),
                                                     w_rows_are_outputs=True)

        q = _nm_matmul(p, mla_q_norm_g[l], w_uq_p[l], tm=tm, tn=MLA_HEADS * MLA_QK_PAD, out_dtype=BF16,
                       x_col_block=C_CQ // MLA_Q_LORA, kdim=MLA_Q_LORA, name="mla_q")
        kv = _nm_matmul(p, mla_kv_norm_g[l], w_ukv_p[l], tm=tm, tn=MLA_HEADS * (MLA_NOPE + MLA_V), out_dtype=BF16,
                        x_col_block=C_CKV // MLA_KV_LORA, kdim=MLA_KV_LORA, name="mla_kv")
        y_mla = _mla_attention(q, kv, p, cos_t, sin_t, n_batch=n_batch, t_len=t_len, c_len=c_len, tq=tq,
                               with_ctx=need_ctx)

        y_diff = _diff_attention(p, cos_t, sin_t, lam_row, diff_norm_g[l].reshape(1, DIFF_V),
                                 n_batch=n_batch, t_len=t_len, c_len=c_len, tq=tq, with_ctx=need_ctx,
                                 out_scale=1.0 - lam_init)

        r_, kk, v_, lw, k_dir, kka, bonus, g_out, y_conv = _mixer_prep(
            p, rwkv_mu[l, :, :3 * bw], mu_lora[l], rwkv_k_k[l].reshape(1, bw), rwkv_k_a[l].reshape(1, bw),
            rwkv_r_k[l].reshape(1, bw), rwkv_w0[l].reshape(1, 2 * bw), rwkv_a0[l].reshape(1, 2 * bw),
            _block_diag2(rwkv_w2[l]).astype(BF16), _block_diag2(rwkv_a2[l]).astype(BF16), g2_p[l],
            conv_w[l], tr=min(128, tr), n_x=n_x, t_len=t_len, c_len=c_len)
        o_dirs = _rwkv_scan(r_, kk, v_, lw, k_dir, kka, n_batch=n_batch, t_len=t_len, c_len=c_len)
        y_rwkv = _rwkv_readout(o_dirs, bonus, g_out, rwkv_ln_g[l], rwkv_ln_b[l], tr=tr)

        rows = n_rows if need_ctx else n_x
        acc = _merge(p, (y_mla, y_rwkv, y_conv, y_diff), w_branch_b, gate_up_b, gate_b, layer=l,
                     tm=tm_big, tn=512, rows=rows)
        (xs_new,) = _fullk_matmul(acc, w_out_b[None], layer=0, tm=tm_big, tn=512, out_dtype=F32, res=xs, mod=mod,
                                  gate_row=2, mod_index=mod_index_big, rows=rows, name="out_proj")

        h2 = _norm_mod(xs_new, norm2_g[l], mod, shift_row=3, scale_row=4, mod_index=mod_index, tm=tm,
                       rows=rows)
        hid, w2_b = _fullk_matmul(h2, w1_b[None], layer=0, tm=tm_big, tn=TN_UP, out_dtype=BF16, act="relu2",
                                  name="mlp_up", cast_srcs=(mlp_w2,), cast_layer=l)
        xs = _matmul(hid, w2_b, tm=tm_big, tn=1024, tk=2048, res=xs_new, mod=mod, gate_row=5,
                     mod_index=mod_index_big, rows=rows, name="mlp_down")

    out = _final_norm(xs, final_norm_g, rows=n_x, tm=tm)
    return out.reshape(n_batch, t_len, dm)
```

```python
import functools
import math

import jax
import jax.numpy as jnp
from jax import lax
from jax.experimental import pallas as pl
from jax.experimental.pallas import tpu as pltpu

F32 = jnp.float32
BF16 = jnp.bfloat16

D_MODEL = 4096
BRANCH_W = 1024
GRID_W = 64
ROPE_BASE = 10000.0
NORM_EPS = 1e-6
N_MOD = 6
LANES = 128
HALO = 8
LOG2_E = 1.4426950408889634

MLA_HEADS = 8
MLA_NOPE = 128
MLA_ROPE = 64
MLA_V = 128
MLA_Q_LORA = 768
MLA_KV_LORA = 256
MLA_QK_PAD = 256

RWKV_HEAD = 64
RWKV_HEADS = 16
RWKV_LORA = 64
RWKV_GATE_LORA = 160
RWKV_GN_EPS = 64e-5
RWKV_CHUNK = 64
RWKV_LORA_PAD = 512

DIFF_HEADS = 8
DIFF_QK = 64
DIFF_V = 128
GATE_RANK = 256
MOD_RANK = 256

C_CQ = 0
C_CKV = 768
C_R = 1024
C_K = 2048
C_V = 3072
C_CB = 4096
C_CC = 5120
C_CU = 6144
C_DQ = 7168
C_DK = 8192
C_DV = 9216
C_LORA = 10240
C_GL = 10752
C_KROPE = 11008
P_COLS = 11264

TN_UP = 1024
ATTN_SUB = 128
VMEM_CAP = 56 * 1024 * 1024


def _cparams(sem, vmem_bytes):
    limit = int(min(VMEM_CAP, max(vmem_bytes * 1.5 + (4 << 20), 16 << 20)))
    return pltpu.CompilerParams(dimension_semantics=sem, vmem_limit_bytes=limit)


def _mod_index(i, tm, n_x_rows, t_len, n_batch):
    return jnp.where(i < n_x_rows // tm, i // (t_len // tm), n_batch)


def _dot(a, b):
    return jnp.dot(a.astype(BF16), b.astype(BF16), preferred_element_type=F32)


def _dot_nt(a, b):
    return lax.dot_general(a.astype(BF16), b.astype(BF16), (((1,), (1,)), ((), ())),
                           preferred_element_type=F32)


def _split3(x):
    hi = x.astype(BF16)
    r1 = x - hi.astype(F32)
    mid = r1.astype(BF16)
    lo = (r1 - mid.astype(F32)).astype(BF16)
    return hi, mid, lo


def _dot_exact_lhs(m_bf16, x):
    out = None
    for part in _split3(x):
        t = jnp.dot(m_bf16, part, preferred_element_type=F32)
        out = t if out is None else out + t
    return out


def _head_sums(x):
    r = lax.broadcasted_iota(jnp.int32, (LANES, LANES), 0)
    c = lax.broadcasted_iota(jnp.int32, (LANES, LANES), 1)
    ones_bd = jnp.where((r // RWKV_HEAD) == (c // RWKV_HEAD), 1.0, 0.0).astype(BF16)
    parts = _split3(x)
    cols = []
    for j in range(x.shape[1] // LANES):
        acc = None
        for part in parts:
            t = jnp.dot(part[:, j * LANES:(j + 1) * LANES], ones_bd, preferred_element_type=F32)
            acc = t if acc is None else acc + t
        cols.append(acc)
    return jnp.concatenate(cols, axis=1)


def _rope128(x, cos_t, sin_t):
    lane = lax.broadcasted_iota(jnp.int32, x.shape, 1)
    swapped = jnp.where((lane % 64) < 32, pltpu.roll(x, 96, 1), pltpu.roll(x, 32, 1))
    return x * cos_t + swapped * sin_t


def _matmul_kernel(*refs, nk, has_bias, has_res, gate_row, act):
    a_ref, w_ref = refs[0], refs[1]
    pos = 2
    bias_ref = res_ref = mod_ref = None
    if has_bias:
        bias_ref = refs[pos]; pos += 1
    if has_res:
        res_ref = refs[pos]; mod_ref = refs[pos + 1]; pos += 2
    o_ref, acc_ref = refs[pos], refs[pos + 1]
    k = pl.program_id(2)

    @pl.when(k == 0)
    def _():
        acc_ref[...] = jnp.zeros_like(acc_ref)

    acc_ref[...] += _dot(a_ref[...], w_ref[...])

    @pl.when(k == nk - 1)
    def _():
        y = acc_ref[...]
        if has_bias:
            y = y + bias_ref[...]
        if act == "relu2":
            y = jnp.square(jnp.maximum(y, 0.0))
        if has_res:
            y = res_ref[...] + mod_ref[0, gate_row:gate_row + 1, :] * y
        o_ref[...] = y.astype(o_ref.dtype)


def _matmul(a, w, *, tm, tn, tk, out_dtype=F32, bias=None, res=None, mod=None, gate_row=None,
            mod_index=None, act=None, rows=None, name=None):
    m = a.shape[0] if rows is None else rows
    kdim, n = w.shape
    assert a.shape[1] == kdim and m % tm == 0 and n % tn == 0 and kdim % tk == 0
    nk = kdim // tk
    in_specs = [pl.BlockSpec((tm, tk), lambda i, j, k: (i, k)),
                pl.BlockSpec((tk, tn), lambda i, j, k: (k, j))]
    args = [a, w]
    vmem = 2 * tm * tk * a.dtype.itemsize + 2 * tk * tn * w.dtype.itemsize + tm * tn * 4
    vmem += 2 * tm * tn * jnp.dtype(out_dtype).itemsize
    if bias is not None:
        in_specs.append(pl.BlockSpec((1, tn), lambda i, j, k: (0, j)))
        args.append(bias.reshape(1, n).astype(F32))
    if res is not None:
        in_specs.append(pl.BlockSpec((tm, tn), lambda i, j, k: (i, j)))
        in_specs.append(pl.BlockSpec((1, N_MOD, tn), lambda i, j, k: (mod_index(i), 0, j)))
        args += [res, mod]
        vmem += 2 * tm * tn * 4 + 2 * 8 * tn * 4
    kern = functools.partial(_matmul_kernel, nk=nk, has_bias=bias is not None,
                             has_res=res is not None, gate_row=gate_row, act=act)
    return pl.pallas_call(
        kern,
        grid=(m // tm, n // tn, nk),
        in_specs=in_specs,
        out_specs=pl.BlockSpec((tm, tn), lambda i, j, k: (i, j)),
        out_shape=jax.ShapeDtypeStruct((m, n), out_dtype),
        scratch_shapes=[pltpu.VMEM((tm, tn), F32)],
        compiler_params=_cparams(("parallel", "parallel", "arbitrary"), vmem),
        name=name,
    )(*args)


def _nm_matmul_kernel(*refs, has_mod, shift_row, scale_row, act, out_scale):
    if has_mod:
        x_ref, g_ref, mod_ref, w_ref, o_ref, h_ref = refs
    else:
        x_ref, g_ref, w_ref, o_ref, h_ref = refs

    @pl.when(pl.program_id(1) == 0)
    def _():
        x = x_ref[...].astype(F32)
        y = x * lax.rsqrt(jnp.mean(x * x, axis=-1, keepdims=True) + NORM_EPS) * g_ref[...]
        if has_mod:
            y = y * (1.0 + mod_ref[0, scale_row:scale_row + 1, :]) + mod_ref[0, shift_row:shift_row + 1, :]
        h_ref[...] = y.astype(BF16)

    y = jnp.dot(h_ref[...], w_ref[...].astype(BF16), preferred_element_type=F32)
    if act == "relu2":
        y = jnp.square(jnp.maximum(y, 0.0))
    if out_scale is not None:
        y = y * out_scale
    o_ref[...] = y.astype(o_ref.dtype)


def _nm_matmul(x, g, w, *, tm, tn, out_dtype, x_col_block=0, kdim=None, mod=None, shift_row=None,
               scale_row=None, mod_index=None, act=None, rows=None, name=None, out_scale=None):
    m = x.shape[0] if rows is None else rows
    kdim = x.shape[1] if kdim is None else kdim
    n = w.shape[1]
    assert w.shape[0] == kdim and m % tm == 0 and n % tn == 0
    in_specs = [pl.BlockSpec((tm, kdim), lambda i, j: (i, x_col_block)),
                pl.BlockSpec((1, kdim), lambda i, j: (0, 0))]
    args = [x, g.reshape(1, kdim).astype(F32)]
    if mod is not None:
        in_specs.append(pl.BlockSpec((1, N_MOD, kdim), lambda i, j: (mod_index(i), 0, 0)))
        args.append(mod)
    in_specs.append(pl.BlockSpec((kdim, tn), lambda i, j: (0, j)))
    args.append(w)
    vmem = (2 * tm * kdim * x.dtype.itemsize + tm * kdim * 2 + 2 * kdim * tn * w.dtype.itemsize
            + 2 * tm * tn * jnp.dtype(out_dtype).itemsize + tm * tn * 4 + 4 * 8 * kdim * 4)
    kern = functools.partial(_nm_matmul_kernel, has_mod=mod is not None, shift_row=shift_row,
                             scale_row=scale_row, act=act, out_scale=out_scale)
    return pl.pallas_call(
        kern,
        grid=(m // tm, n // tn),
        in_specs=in_specs,
        out_specs=pl.BlockSpec((tm, tn), lambda i, j: (i, j)),
        out_shape=jax.ShapeDtypeStruct((m, n), out_dtype),
        scratch_shapes=[pltpu.VMEM((tm, kdim), BF16)],
        compiler_params=_cparams(("parallel", "arbitrary"), vmem),
        name=name,
    )(*args)


def _norm_mod_kernel(x_ref, g_ref, mod_ref, h_ref, *, shift_row, scale_row):
    x = x_ref[...]
    y = x * lax.rsqrt(jnp.mean(x * x, axis=-1, keepdims=True) + NORM_EPS) * g_ref[...]
    y = y * (1.0 + mod_ref[0, scale_row:scale_row + 1, :]) + mod_ref[0, shift_row:shift_row + 1, :]
    h_ref[...] = y.astype(h_ref.dtype)


def _norm_mod(xs, g, mod, *, shift_row, scale_row, mod_index, tm, rows):
    dm = xs.shape[1]
    return pl.pallas_call(
        functools.partial(_norm_mod_kernel, shift_row=shift_row, scale_row=scale_row),
        grid=(rows // tm,),
        in_specs=[pl.BlockSpec((tm, dm), lambda i: (i, 0)), pl.BlockSpec((1, dm), lambda i: (0, 0)),
                  pl.BlockSpec((1, N_MOD, dm), lambda i: (mod_index(i), 0, 0))],
        out_specs=pl.BlockSpec((tm, dm), lambda i: (i, 0)),
        out_shape=jax.ShapeDtypeStruct((rows, dm), BF16),
        compiler_params=_cparams(("parallel",), 2 * tm * dm * 6 + 4 * tm * dm * 4),
        name="norm_mod",
    )(xs, g.reshape(1, dm), mod)


def _fullk_kernel(*refs, has_res, n_cast, gate_row, act, w_rows_are_outputs):
    refs = list(refs)
    if n_cast:
        cast_dsts = refs[-n_cast:]
        cast_srcs = refs[-2 * n_cast - 1:-n_cast - 1]
        refs = refs[:-2 * n_cast - 1] + [refs[-n_cast - 1]]
        for src, dst in zip(cast_srcs, cast_dsts):
            dst[...] = src[...].astype(dst.dtype)
    if has_res:
        a_ref, w_ref, res_ref, mod_ref, o_ref = refs
    else:
        a_ref, w_ref, o_ref = refs
    if w_rows_are_outputs:
        y = lax.dot_general(a_ref[...], w_ref[...], (((1,), (1,)), ((), ())), preferred_element_type=F32)
    else:
        y = jnp.dot(a_ref[...], w_ref[...], preferred_element_type=F32)
    if act == "relu2":
        y = jnp.square(jnp.maximum(y, 0.0))
    if has_res:
        y = res_ref[...] + mod_ref[0, gate_row:gate_row + 1, :] * y
    o_ref[...] = y.astype(o_ref.dtype)


def _cast_rows_per_step(n_rows, n_steps):
    rows = 16
    while n_rows % rows or n_rows // rows > n_steps:
        rows *= 2
    return rows


def _fullk_matmul(a, w, *, tm, tn, out_dtype, act=None, res=None, mod=None, gate_row=None, mod_index=None,
                  rows=None, name=None, layer=None, cast_srcs=(), cast_layer=None, w_rows_are_outputs=False):
    m = a.shape[0] if rows is None else rows
    if w_rows_are_outputs:
        _, n, kdim = w.shape
        w_spec = pl.BlockSpec((None, tn, kdim), lambda i, j: (layer, j, 0))
    else:
        _, kdim, n = w.shape
        w_spec = pl.BlockSpec((None, kdim, tn), lambda i, j: (layer, 0, j))
    assert a.shape[1] == kdim and m % tm == 0 and n % tn == 0 and a.dtype == BF16 and w.dtype == BF16
    in_specs = [pl.BlockSpec((tm, kdim), lambda i, j: (i, 0)), w_spec]
    args = [a, w]
    vmem = 2 * tm * kdim * 2 + 2 * kdim * tn * 2 + 2 * tm * tn * jnp.dtype(out_dtype).itemsize + 2 * tm * tn * 4
    if res is not None:
        in_specs += [pl.BlockSpec((tm, tn), lambda i, j: (i, j)),
                     pl.BlockSpec((1, N_MOD, tn), lambda i, j: (mod_index(i), 0, j))]
        args += [res, mod]
        vmem += 2 * tm * tn * 4
    out_specs = [pl.BlockSpec((tm, tn), lambda i, j: (i, j))]
    out_shape = [jax.ShapeDtypeStruct((m, n), out_dtype)]
    nj = n // tn
    cast_layer = layer if cast_layer is None else cast_layer
    for cast_src in cast_srcs:
        _, c_rows, c_cols = cast_src.shape
        cr = _cast_rows_per_step(c_rows, (m // tm) * nj)
        cast_blk = lambda i, j, last=c_rows // cr - 1: jnp.minimum(i * nj + j, last)
        in_specs.append(pl.BlockSpec((None, cr, c_cols), lambda i, j, blk=cast_blk: (cast_layer, blk(i, j), 0)))
        args.append(cast_src)
        out_specs.append(pl.BlockSpec((cr, c_cols), lambda i, j, blk=cast_blk: (blk(i, j), 0)))
        out_shape.append(jax.ShapeDtypeStruct((c_rows, c_cols), BF16))
        vmem += 2 * cr * c_cols * 6
    return pl.pallas_call(
        functools.partial(_fullk_kernel, has_res=res is not None, n_cast=len(cast_srcs),
                          gate_row=gate_row, act=act, w_rows_are_outputs=w_rows_are_outputs),
        grid=(m // tm, n // tn),
        in_specs=in_specs,
        out_specs=out_specs,
        out_shape=out_shape,
        compiler_params=_cparams(("arbitrary", "arbitrary"), vmem),
        name=name,
    )(*args)


def _merge_kernel(gl_ref, y0_ref, y1_ref, y2_ref, y3_ref, wb_ref, gu_ref, gb_ref, o_ref):
    gl = gl_ref[...].astype(BF16)
    bw = wb_ref.shape[0] // 4
    acc = None
    for i, y_ref in enumerate((y0_ref, y1_ref, y2_ref, y3_ref)):
        gate = jax.nn.sigmoid(jnp.dot(gl, gu_ref[i], preferred_element_type=F32) + gb_ref[i:i + 1, :])
        term = gate * jnp.dot(y_ref[...], wb_ref[i * bw:(i + 1) * bw, :], preferred_element_type=F32)
        acc = term if acc is None else acc + term
    o_ref[...] = acc.astype(o_ref.dtype)


def _merge(p, ys, wb, gu, gb, *, layer, tm, tn, rows):
    n = wb.shape[1]
    bw = wb.shape[0] // 4
    gr = gu.shape[2]
    y_spec = pl.BlockSpec((tm, bw), lambda i, j: (i, 0))
    vmem = (2 * tm * gr * 4 + 4 * 2 * tm * bw * 2 + 2 * 4 * bw * tn * 2 + 2 * 4 * gr * tn * 2
            + 2 * tm * tn * 2 + 3 * tm * tn * 4)
    return pl.pallas_call(
        _merge_kernel,
        grid=(rows // tm, n // tn),
        in_specs=[pl.BlockSpec((tm, gr), lambda i, j: (i, C_GL // GATE_RANK)),
                  y_spec, y_spec, y_spec, y_spec,
                  pl.BlockSpec((4 * bw, tn), lambda i, j: (0, j)),
                  pl.BlockSpec((None, 4, gr, tn), lambda i, j: (layer, 0, 0, j)),
                  pl.BlockSpec((None, 4, tn), lambda i, j: (layer, 0, j))],
        out_specs=pl.BlockSpec((tm, tn), lambda i, j: (i, j)),
        out_shape=jax.ShapeDtypeStruct((rows, n), BF16),
        compiler_params=_cparams(("parallel", "arbitrary"), vmem),
        name="merge",
    )(p, *ys, wb, gu, gb)


def _softmax_parts(s_list):
    m = None
    for s in s_list:
        sm = jnp.max(s, axis=-1, keepdims=True)
        m = sm if m is None else jnp.maximum(m, sm)
    e_list = [jnp.exp2(s - m) for s in s_list]
    l = None
    for e in e_list:
        es = jnp.sum(e, axis=-1, keepdims=True)
        l = es if l is None else l + es
    return e_list, l


def _attend(q, ks, vs):
    e_list, l = _softmax_parts([_dot_nt(q, k) for k in ks])
    o = None
    for e, v in zip(e_list, vs):
        t = jnp.dot(e.astype(BF16), v, preferred_element_type=F32)
        o = t if o is None else o + t
    return o / l


def _attend_many(qs, ks, vs, sub):
    parts = [q[r:r + sub] for q in qs for r in range(0, q.shape[0], sub)]
    n = len(parts)
    scores, soft, outs = {}, {}, []
    for t in range(n + 2):
        if t < n:
            scores[t] = [_dot_nt(parts[t], k) for k in ks]
        if 0 <= t - 1 < n:
            soft[t - 1] = _softmax_parts(scores.pop(t - 1))
        if 0 <= t - 2 < n:
            e_list, l = soft.pop(t - 2)
            o = None
            for e, v in zip(e_list, vs):
                pv = jnp.dot(e.astype(BF16), v, preferred_element_type=F32)
                o = pv if o is None else o + pv
            outs.append(o / l)
    per_q = len(parts) // len(qs)
    return [jnp.concatenate(outs[i * per_q:(i + 1) * per_q], axis=0) for i in range(len(qs))]


def _mla_kernel(cos_ref, sin_ref, q_ref, knx_ref, knc_ref, krx_ref, krc_ref, vx_ref, vc_ref, o_ref,
                kx_s, kc_s, *, tq):
    i = pl.program_id(2)

    @pl.when(i == 0)
    def _():
        kx_s[:, :MLA_NOPE] = knx_ref[...]
        kx_s[:, MLA_NOPE:] = _rope128(krx_ref[...], cos_ref[...], sin_ref[...]).astype(BF16)
        kc_s[:, :MLA_NOPE] = knc_ref[...]
        kc_s[:, MLA_NOPE:] = krc_ref[...].astype(BF16)

    row0 = pl.multiple_of(i * tq, tq)
    q = q_ref[...]
    q_rope = _rope128(q[:, MLA_NOPE:].astype(F32), cos_ref[pl.ds(row0, tq), :], sin_ref[pl.ds(row0, tq), :])
    q = jnp.concatenate([q[:, :MLA_NOPE], q_rope.astype(BF16)], axis=1)
    (o,) = _attend_many((q,), (kc_s[...], kx_s[...]), (vc_ref[...], vx_ref[...]), ATTN_SUB)
    o_ref[...] = o.astype(o_ref.dtype)


def _mla_ctx_kernel(q_ref, knc_ref, krc_ref, vc_ref, y_hbm, o_ref):
    del y_hbm
    kc = jnp.concatenate([knc_ref[...], krc_ref[...].astype(BF16)], axis=1)
    o = _attend(q_ref[...], (kc,), (vc_ref[...],))
    o_ref[...] = o.astype(o_ref.dtype)


def _diff_finish(o1, o2, lam_ref, g_ref, out_scale):
    o = o1 - lam_ref[...] * o2
    y = o * lax.rsqrt(jnp.mean(o * o, axis=-1, keepdims=True) + NORM_EPS) * g_ref[...]
    return y * out_scale


def _diff_halves(q):
    lane = lax.broadcasted_iota(jnp.int32, q.shape, 1)
    return jnp.where(lane < DIFF_QK, q, 0.0).astype(BF16), jnp.where(lane < DIFF_QK, 0.0, q).astype(BF16)


def _diff_kernel(cos_ref, sin_ref, lam_ref, g_ref, q_ref, kx_ref, kc_ref, vx_ref, vc_ref, o_ref,
                 kx_s, kc_s, vx_s, vc_s, *, tq, scale, out_scale):
    i = pl.program_id(2)

    @pl.when(i == 0)
    def _():
        kx_s[...] = _rope128(kx_ref[...], cos_ref[...], sin_ref[...]).astype(BF16)
        kc_s[...] = kc_ref[...].astype(BF16)
        vx_s[...] = vx_ref[...].astype(BF16)
        vc_s[...] = vc_ref[...].astype(BF16)

    row0 = pl.multiple_of(i * tq, tq)
    q = q_ref[...] * (scale * LOG2_E)
    q1, q2 = _diff_halves(_rope128(q, cos_ref[pl.ds(row0, tq), :], sin_ref[pl.ds(row0, tq), :]))
    ks, vs = (kc_s[...], kx_s[...]), (vc_s[...], vx_s[...])
    o1, o2 = _attend_many((q1, q2), ks, vs, ATTN_SUB)
    y = _diff_finish(o1, o2, lam_ref, g_ref, out_scale)
    o_ref[...] = y.astype(o_ref.dtype)


def _diff_ctx_kernel(lam_ref, g_ref, q_ref, kc_ref, vc_ref, y_hbm, o_ref, *, scale, out_scale):
    del y_hbm
    q1, q2 = _diff_halves(q_ref[...] * (scale * LOG2_E))
    ks, vs = (kc_ref[...].astype(BF16),), (vc_ref[...].astype(BF16),)
    y = _diff_finish(_attend(q1, ks, vs), _attend(q2, ks, vs), lam_ref, g_ref, out_scale)
    o_ref[...] = y.astype(o_ref.dtype)


def _mla_attention(q, kv, p, cos_t, sin_t, *, n_batch, t_len, c_len, tq, with_ctx):
    n_x = n_batch * t_len
    nq = t_len // tq
    out_rows = n_x + (n_batch * c_len if with_ctx else 0)
    cblk0 = n_x // c_len
    qmap = lambda b, h, i: (b * nq + i, h)
    full = lambda b, h, i: (0, 0)
    vmem = (4 * t_len * LANES * 4 + 2 * tq * 256 * 2 + 2 * (t_len + c_len) * LANES * (2 + 4 + 2)
            + (t_len + c_len) * 256 * 2 + 2 * tq * LANES * 2 + 6 * tq * (t_len + c_len) * 4)
    y = pl.pallas_call(
        functools.partial(_mla_kernel, tq=tq),
        grid=(n_batch, MLA_HEADS, nq),
        in_specs=[pl.BlockSpec((t_len, LANES), full), pl.BlockSpec((t_len, LANES), full),
                  pl.BlockSpec((tq, MLA_QK_PAD), qmap),
                  pl.BlockSpec((t_len, MLA_NOPE), lambda b, h, i: (b, h)),
                  pl.BlockSpec((c_len, MLA_NOPE), lambda b, h, i: (cblk0 + b, h)),
                  pl.BlockSpec((t_len, LANES), lambda b, h, i: (b, C_KROPE // LANES)),
                  pl.BlockSpec((c_len, LANES), lambda b, h, i: (cblk0 + b, C_KROPE // LANES)),
                  pl.BlockSpec((t_len, MLA_V), lambda b, h, i: (b, MLA_HEADS + h)),
                  pl.BlockSpec((c_len, MLA_V), lambda b, h, i: (cblk0 + b, MLA_HEADS + h))],
        out_specs=pl.BlockSpec((tq, MLA_V), qmap),
        out_shape=jax.ShapeDtypeStruct((out_rows, MLA_HEADS * MLA_V), BF16),
        scratch_shapes=[pltpu.VMEM((t_len, MLA_QK_PAD), BF16), pltpu.VMEM((c_len, MLA_QK_PAD), BF16)],
        compiler_params=_cparams(("parallel", "parallel", "arbitrary"), vmem),
        name="mla_attention",
    )(cos_t, sin_t, q, kv, kv, p, p, kv, kv)
    if not with_ctx:
        return y
    cmap = lambda b, h: (cblk0 + b, h)
    return pl.pallas_call(
        _mla_ctx_kernel,
        grid=(n_batch, MLA_HEADS),
        in_specs=[pl.BlockSpec((c_len, MLA_QK_PAD), cmap),
                  pl.BlockSpec((c_len, MLA_NOPE), cmap),
                  pl.BlockSpec((c_len, LANES), lambda b, h: (cblk0 + b, C_KROPE // LANES)),
                  pl.BlockSpec((c_len, MLA_V), lambda b, h: (cblk0 + b, MLA_HEADS + h)),
                  pl.BlockSpec(memory_space=pl.ANY)],
        out_specs=pl.BlockSpec((c_len, MLA_V), cmap),
        out_shape=jax.ShapeDtypeStruct(y.shape, y.dtype),
        input_output_aliases={4: 0},
        compiler_params=_cparams(("parallel", "parallel"), 16 * c_len * c_len * 4 + 8 * c_len * 256 * 4),
        name="mla_attention_ctx",
    )(q, kv, p, kv, y)


def _diff_attention(p, cos_t, sin_t, lam_row, g_row, *, n_batch, t_len, c_len, tq, with_ctx, out_scale):
    n_x = n_batch * t_len
    nq = t_len // tq
    out_rows = n_x + (n_batch * c_len if with_ctx else 0)
    cblk0 = n_x // c_len
    scale = DIFF_QK ** -0.5
    full = lambda b, h, i: (0, 0)
    qblk, kblk, vblk = C_DQ // LANES, C_DK // LANES, C_DV // LANES
    vmem = (4 * t_len * LANES * 4 + 2 * tq * LANES * 4 + 4 * (t_len + c_len) * LANES * 4
            + 2 * (t_len + c_len) * LANES * 2 + 2 * tq * LANES * 2 + 8 * tq * (t_len + c_len) * 4)
    y = pl.pallas_call(
        functools.partial(_diff_kernel, tq=tq, scale=scale, out_scale=out_scale),
        grid=(n_batch, DIFF_HEADS, nq),
        in_specs=[pl.BlockSpec((t_len, LANES), full), pl.BlockSpec((t_len, LANES), full),
                  pl.BlockSpec((1, DIFF_V), full), pl.BlockSpec((1, DIFF_V), full),
                  pl.BlockSpec((tq, LANES), lambda b, h, i: (b * nq + i, qblk + h)),
                  pl.BlockSpec((t_len, LANES), lambda b, h, i: (b, kblk + h)),
                  pl.BlockSpec((c_len, LANES), lambda b, h, i: (cblk0 + b, kblk + h)),
                  pl.BlockSpec((t_len, LANES), lambda b, h, i: (b, vblk + h)),
                  pl.BlockSpec((c_len, LANES), lambda b, h, i: (cblk0 + b, vblk + h))],
        out_specs=pl.BlockSpec((tq, DIFF_V), lambda b, h, i: (b * nq + i, h)),
        out_shape=jax.ShapeDtypeStruct((out_rows, DIFF_HEADS * DIFF_V), BF16),
        scratch_shapes=[pltpu.VMEM((t_len, LANES), BF16), pltpu.VMEM((c_len, LANES), BF16),
                        pltpu.VMEM((t_len, LANES), BF16), pltpu.VMEM((c_len, LANES), BF16)],
        compiler_params=_cparams(("parallel", "parallel", "arbitrary"), vmem),
        name="diff_attention",
    )(cos_t, sin_t, lam_row, g_row, p, p, p, p, p)
    if not with_ctx:
        return y
    one = lambda b, h: (0, 0)
    return pl.pallas_call(
        functools.partial(_diff_ctx_kernel, scale=scale, out_scale=out_scale),
        grid=(n_batch, DIFF_HEADS),
        in_specs=[pl.BlockSpec((1, DIFF_V), one), pl.BlockSpec((1, DIFF_V), one),
                  pl.BlockSpec((c_len, LANES), lambda b, h: (cblk0 + b, qblk + h)),
                  pl.BlockSpec((c_len, LANES), lambda b, h: (cblk0 + b, kblk + h)),
                  pl.BlockSpec((c_len, LANES), lambda b, h: (cblk0 + b, vblk + h)),
                  pl.BlockSpec(memory_space=pl.ANY)],
        out_specs=pl.BlockSpec((c_len, DIFF_V), lambda b, h: (cblk0 + b, h)),
        out_shape=jax.ShapeDtypeStruct(y.shape, y.dtype),
        input_output_aliases={5: 0},
        compiler_params=_cparams(("parallel", "parallel"), 24 * c_len * c_len * 4 + 8 * c_len * LANES * 4),
        name="diff_attention_ctx",
    )(lam_row, g_row, p, p, p, y)


def _prep_kernel(r_ref, k_ref, v_ref, cb_ref, cc_ref, cu_ref, lo_ref, hp_ref, hn_ref,
                 mu_ref, mul_ref, kk_w_ref, ka_w_ref, rk_w_ref, w0_ref, a0_ref, w2_ref, a2_ref, g2_ref,
                 cw_ref,
                 r_o, kk_o, v_o, lw_o, kd_o, ka_o, bonus_o, g_o, conv_o, *, tr, n_x, t_len, c_len):
    i = pl.program_id(0)
    g0 = i * tr
    seq = jnp.where(g0 < n_x, t_len, c_len)
    has_prev = ((g0 % seq) != 0).astype(F32)
    has_next = (((g0 + tr) % seq) != 0).astype(F32)
    row = lax.broadcasted_iota(jnp.int32, (tr, 1), 0)

    def neighbours(x, col0):
        width = x.shape[1]
        before = hp_ref[HALO - 1:HALO, col0:col0 + width] * has_prev
        after = hn_ref[0:1, col0:col0 + width] * has_next
        prev = jnp.where(row == 0, before, pltpu.roll(x, 1, 0))
        nxt = jnp.where(row == tr - 1, after, pltpu.roll(x, tr - 1, 0))
        return prev, nxt

    def shifted(x, col0, mu0, mu1):
        prev, nxt = neighbours(x, col0)
        return x + mu0 * (prev - x) + mu1 * (nxt - x)

    r = shifted(r_ref[...], C_R, mu_ref[0:1, :BRANCH_W], mu_ref[1:2, :BRANCH_W])
    k = shifted(k_ref[...], C_K, mu_ref[0:1, BRANCH_W:2 * BRANCH_W], mu_ref[1:2, BRANCH_W:2 * BRANCH_W])
    v = shifted(v_ref[...], C_V, mu_ref[0:1, 2 * BRANCH_W:], mu_ref[1:2, 2 * BRANCH_W:])
    lo = shifted(lo_ref[...], C_LORA, mul_ref[0:1, :], mul_ref[1:2, :])
    wd = jnp.tanh(lo[:, :2 * RWKV_LORA])
    ad = lo[:, 2 * RWKV_LORA:4 * RWKV_LORA]
    gd = jax.nn.sigmoid(lo[:, 4 * RWKV_LORA:])
    w_pre = _dot(wd, w2_ref[...]) + w0_ref[...]
    a_sig = jax.nn.sigmoid(_dot(ad, a2_ref[...]) + a0_ref[...])
    g_o[...] = _dot(gd, g2_ref[...])
    w_log = -(jnp.maximum(-w_pre, 0.0) + jnp.log(1.0 + jnp.exp(-jnp.abs(w_pre)))) - 0.5
    lw = -jnp.exp(w_log)
    kkf = k * kk_w_ref[...]
    kk = kkf * lax.rsqrt(_head_sums(kkf * kkf) + 1e-12)
    r_o[...] = r
    kk_o[...] = kk
    v_o[...] = v
    k_sum = None
    for d in range(2):
        a_d = a_sig[:, d * BRANCH_W:(d + 1) * BRANCH_W]
        k_d = k * (1.0 + (a_d - 1.0) * ka_w_ref[...])
        lw_o[d] = lw[:, d * BRANCH_W:(d + 1) * BRANCH_W]
        kd_o[d] = k_d
        ka_o[d] = kk * a_d
        k_sum = k_d if k_sum is None else k_sum + k_d
    bonus_o[...] = _head_sums(r * k_sum * rk_w_ref[...]) * v

    z = cc_ref[...] * cu_ref[...]
    z_before = hp_ref[HALO - 1:HALO, C_CC:C_CC + BRANCH_W] * hp_ref[HALO - 1:HALO, C_CU:C_CU + BRANCH_W] * has_prev
    z_after = hn_ref[0:1, C_CC:C_CC + BRANCH_W] * hn_ref[0:1, C_CU:C_CU + BRANCH_W] * has_next
    z_prev = jnp.where(row == 0, z_before, pltpu.roll(z, 1, 0))
    z_next = jnp.where(row == tr - 1, z_after, pltpu.roll(z, tr - 1, 0))
    y = cb_ref[...] * (cw_ref[0:1, :] * z_prev + cw_ref[1:2, :] * z + cw_ref[2:3, :] * z_next)
    conv_o[...] = y.astype(conv_o.dtype)


def _mixer_prep(p, mu_rkv, mu_lora, kk_w, ka_w, rk_w, w0, a0, w2bd, a2bd, g2p, conv_w, *, tr, n_x, t_len,
                c_len):
    n_rows = p.shape[0]
    bw = BRANCH_W
    last_halo = n_rows // HALO - 1
    col = lambda c: (lambda i: (i, c))
    const = lambda i: (0, 0)
    main = [pl.BlockSpec((tr, bw), col(C_R // bw)), pl.BlockSpec((tr, bw), col(C_K // bw)),
            pl.BlockSpec((tr, bw), col(C_V // bw)), pl.BlockSpec((tr, bw), col(C_CB // bw)),
            pl.BlockSpec((tr, bw), col(C_CC // bw)), pl.BlockSpec((tr, bw), col(C_CU // bw)),
            pl.BlockSpec((tr, RWKV_LORA_PAD), col(C_LORA // RWKV_LORA_PAD)),
            pl.BlockSpec((HALO, P_COLS), lambda i: (jnp.maximum(i * (tr // HALO) - 1, 0), 0)),
            pl.BlockSpec((HALO, P_COLS), lambda i: (jnp.minimum((i + 1) * (tr // HALO), last_halo), 0))]
    params = [mu_rkv, mu_lora, kk_w, ka_w, rk_w, w0, a0, w2bd, a2bd, g2p, conv_w]
    param_specs = [pl.BlockSpec(a.shape, const) for a in params]
    row_spec = pl.BlockSpec((tr, bw), lambda i: (i, 0))
    dir_spec = pl.BlockSpec((2, tr, bw), lambda i: (0, i, 0))
    f32_rows = jax.ShapeDtypeStruct((n_rows, bw), F32)
    f32_dirs = jax.ShapeDtypeStruct((2, n_rows, bw), F32)
    vmem = 2 * (7 * tr * bw * 4 + 2 * HALO * P_COLS * 4 + 12 * tr * bw * 4) + 30 * tr * bw * 4
    return pl.pallas_call(
        functools.partial(_prep_kernel, tr=tr, n_x=n_x, t_len=t_len, c_len=c_len),
        grid=(n_rows // tr,),
        in_specs=main + param_specs,
        out_specs=[row_spec, row_spec, row_spec, dir_spec, dir_spec, dir_spec, row_spec, row_spec, row_spec],
        out_shape=[f32_rows, f32_rows, f32_rows, f32_dirs, f32_dirs, f32_dirs, f32_rows, f32_rows,
                   jax.ShapeDtypeStruct((n_rows, bw), BF16)],
        compiler_params=_cparams(("parallel",), vmem),
        name="mixer_prep",
    )(p, p, p, p, p, p, p, p, p, *params)


def _rwkv_kernel(r0_ref, kk0_ref, v0_ref, r1_ref, kk1_ref, v1_ref, lw0_ref, kd0_ref, ka0_ref,
                 lw1_ref, kd1_ref, ka1_ref, o0_ref, o1_ref, s_ref):
    C = RWKV_CHUNK
    W = 2 * RWKV_HEAD
    n_pairs = RWKV_HEADS // 2

    @pl.when(pl.program_id(1) == 0)
    def _():
        s_ref[...] = jnp.zeros_like(s_ref)

    mm, mm_nt = _dot, _dot_nt
    lane = lax.broadcasted_iota(jnp.int32, (C, W), 1)
    row = lax.broadcasted_iota(jnp.int32, (C, W), 0)
    first = lane < RWKV_HEAD
    rc = lax.broadcasted_iota(jnp.int32, (C, C), 0)
    cc = lax.broadcasted_iota(jnp.int32, (C, C), 1)

    def bdiag(x):
        return jnp.concatenate([jnp.where(first, x, jnp.zeros_like(x)), jnp.where(first, jnp.zeros_like(x), x)],
                               axis=0)

    def direction(sign, r_ref, kk_ref, v_ref, lw_ref, kd_ref, ka_ref):
        m_incl = jnp.where((rc - cc) * sign >= 0, 1.0, 0.0).astype(BF16)
        lw = lw_ref[0]
        cum = _dot_exact_lhs(m_incl, lw)
        tot = jnp.sum(lw, axis=0, keepdims=True)
        e_ninc = jnp.exp(-cum)
        e_rem = jnp.exp(tot - cum)
        kk, ka, kd = kk_ref[...], ka_ref[0], kd_ref[0]
        order = (row - (lane % RWKV_HEAD)) * sign
        return dict(a_t=kk * jnp.exp(cum - lw), b_t=-ka * e_ninc, k_t=kd * e_ninc, r_t=r_ref[...] * jnp.exp(cum),
                    b_h=-ka * e_rem, k_h=kd * e_rem, v=v_ref[...], e_tot=jnp.exp(tot), strict=order > 0,
                    incl=order >= 0, eye=jnp.where(order == 0, 1.0, 0.0).astype(F32))

    dirs = (direction(1, r0_ref, kk0_ref, v0_ref, lw0_ref, kd0_ref, ka0_ref),
            direction(-1, r1_ref, kk1_ref, v1_ref, lw1_ref, kd1_ref, ka1_ref))
    o_refs = (o0_ref, o1_ref)

    chains = [(d, p) for d in range(2) for p in range(n_pairs)]
    n = range(len(chains))
    sl = [slice(W * p, W * (p + 1)) for (_, p) in chains]
    dd = [dirs[d] for (d, _) in chains]
    ar = [jnp.concatenate([dd[c]["a_t"][:, sl[c]], dd[c]["r_t"][:, sl[c]]], axis=0).astype(BF16) for c in n]
    a_bd = [bdiag(dd[c]["a_t"][:, sl[c]].astype(BF16)) for c in n]
    bk_bd = [jnp.concatenate([bdiag(dd[c]["b_t"][:, sl[c]].astype(BF16)), bdiag(dd[c]["k_t"][:, sl[c]].astype(BF16))],
                             axis=0) for c in n]
    v_bd = [bdiag(dd[c]["v"][:, sl[c]].astype(BF16)) for c in n]
    g = [mm_nt(ar[c], bk_bd[c]) for c in n]
    l_pow = [jnp.where(dd[c]["strict"], g[c][:C, :W], 0.0) for c in n]
    m_ak = [jnp.where(dd[c]["strict"], g[c][:C, W:], 0.0).astype(BF16) for c in n]
    a_r = [jnp.where(jnp.concatenate([dd[c]["incl"], dd[c]["incl"]], axis=1), g[c][C:], 0.0).astype(BF16) for c in n]
    mv = [mm(m_ak[c], v_bd[c]) for c in n]
    l_bd = [bdiag(l_pow[c].astype(BF16)) for c in n]
    t_inv = [dd[c]["eye"] + l_pow[c] for c in n]
    l_pow = [mm(l_pow[c], l_bd[c]) for c in n]
    for _ in range(int(math.log2(C)) - 2):
        l_bd = [bdiag(l_pow[c].astype(BF16)) for c in n]
        lt = [mm(jnp.concatenate([l_pow[c], t_inv[c]], axis=0), l_bd[c]) for c in n]
        l_pow = [lt[c][:C] for c in n]
        t_inv = [t_inv[c] + lt[c][C:] for c in n]
    t_inv = [t_inv[c] + mm(t_inv[c], bdiag(l_pow[c].astype(BF16))) for c in n]
    wu = [mm(t_inv[c], jnp.concatenate([a_bd[c], bdiag(mv[c].astype(BF16))], axis=1)) for c in n]
    s_bd = [s_ref[d, p] for (d, p) in chains]
    wr = [mm(jnp.concatenate([wu[c][:, :W], dd[c]["r_t"][:, sl[c]]], axis=0), s_bd[c]) for c in n]
    zv = [jnp.concatenate([bdiag((wr[c][:C] + wu[c][:, W:]).astype(BF16)), v_bd[c]], axis=0) for c in n]
    for c, (d, p) in enumerate(chains):
        o_refs[d][:, sl[c]] = wr[c][C:] + mm(a_r[c], zv[c])
    for c, (d, p) in enumerate(chains):
        bk_h = jnp.concatenate([bdiag(dd[c]["b_h"][:, sl[c]]), bdiag(dd[c]["k_h"][:, sl[c]])], axis=0)
        tot_col = jnp.broadcast_to(dd[c]["e_tot"][:, sl[c]], (W, W)).T
        s_ref[d, p] = tot_col * s_bd[c] + mm(bk_h.T, zv[c])


def _rwkv_scan(r, kk, v, lw, kd, ka, *, n_batch, t_len, c_len):
    C = RWKV_CHUNK
    rows, width = r.shape
    nct, nxt = c_len // C, t_len // C
    ctx_blk0 = n_batch * t_len // C

    def blk(b, d, s):
        j_c = s if d == 0 else nct - 1 - s
        j_x = s - nct if d == 0 else nxt - 1 - (s - nct)
        return jnp.where(s < nct, ctx_blk0 + b * nct + j_c, b * nxt + j_x)

    def shared(d):
        return pl.BlockSpec((C, width), lambda b, s: (blk(b, d, s), 0))

    def per_dir(d):
        return pl.BlockSpec((1, C, width), lambda b, s: (d, blk(b, d, s), 0))

    vmem = 2 * 14 * C * width * 4 + RWKV_HEADS * 128 * 128 * 4 + 128 * C * width * 4
    out = jax.ShapeDtypeStruct((rows, width), F32)
    return pl.pallas_call(
        _rwkv_kernel,
        grid=(n_batch, nct + nxt),
        in_specs=[shared(0), shared(0), shared(0), shared(1), shared(1), shared(1),
                  per_dir(0), per_dir(0), per_dir(0), per_dir(1), per_dir(1), per_dir(1)],
        out_specs=[shared(0), shared(1)],
        out_shape=[out, out],
        scratch_shapes=[pltpu.VMEM((2, RWKV_HEADS // 2, 128, 128), F32)],
        compiler_params=_cparams(("parallel", "arbitrary"), vmem),
        name="rwkv_scan",
    )(r, kk, v, r, kk, v, lw, kd, ka, lw, kd, ka)


def _rwkv_readout_kernel(o0_ref, o1_ref, bonus_ref, g_ref, lng_ref, lnb_ref, y_ref):
    o = o0_ref[...] + o1_ref[...]
    mean = _head_sums(o) * (1.0 / RWKV_HEAD)
    cen = o - mean
    var = _head_sums(cen * cen) * (1.0 / RWKV_HEAD)
    o_n = cen * lax.rsqrt(var + RWKV_GN_EPS) * lng_ref[...] + lnb_ref[...]
    y_ref[...] = ((o_n + bonus_ref[...]) * g_ref[...]).astype(y_ref.dtype)


def _rwkv_readout(o_dirs, bonus, g_out, ln_g, ln_b, *, tr):
    n_rows, bw = bonus.shape
    row_spec = pl.BlockSpec((tr, bw), lambda i: (i, 0))
    const = pl.BlockSpec((1, bw), lambda i: (0, 0))
    return pl.pallas_call(
        _rwkv_readout_kernel,
        grid=(n_rows // tr,),
        in_specs=[row_spec, row_spec, row_spec, row_spec, const, const],
        out_specs=row_spec,
        out_shape=jax.ShapeDtypeStruct((n_rows, bw), BF16),
        compiler_params=_cparams(("parallel",), 2 * 5 * tr * bw * 4 + 16 * tr * bw * 4),
        name="rwkv_readout",
    )(o_dirs[0], o_dirs[1], bonus, g_out, ln_g.reshape(1, bw), ln_b.reshape(1, bw))


def _rmsnorm_kernel(x_ref, g_ref, o_ref):
    x = x_ref[...]
    o_ref[...] = x * lax.rsqrt(jnp.mean(x * x, axis=-1, keepdims=True) + NORM_EPS) * g_ref[...]


def _final_norm(xs, g, *, rows, tm):
    dm = xs.shape[1]
    return pl.pallas_call(
        _rmsnorm_kernel,
        grid=(rows // tm,),
        in_specs=[pl.BlockSpec((tm, dm), lambda i: (i, 0)), pl.BlockSpec((1, dm), lambda i: (0, 0))],
        out_specs=pl.BlockSpec((tm, dm), lambda i: (i, 0)),
        out_shape=jax.ShapeDtypeStruct((rows, dm), F32),
        compiler_params=_cparams(("parallel",), 4 * tm * dm * 4),
        name="final_norm",
    )(xs, g.reshape(1, dm))


def _rope_tables128(n_tokens):
    rows = n_tokens // GRID_W
    row = jnp.repeat(jnp.arange(rows, dtype=F32), GRID_W)
    col = jnp.tile(jnp.arange(GRID_W, dtype=F32), rows)
    n_freq = 64 // 4
    inv = ROPE_BASE ** (-jnp.arange(n_freq, dtype=F32) / n_freq)
    ang = jnp.concatenate([row[:, None] * inv, col[:, None] * inv], axis=-1)
    cos, sin = jnp.cos(ang), jnp.sin(ang)
    return jnp.concatenate([cos, cos, cos, cos], axis=-1), jnp.concatenate([-sin, sin, -sin, sin], axis=-1)


def _block_diag2(w2):
    z = jnp.zeros_like(w2[0])
    return jnp.concatenate([jnp.concatenate([w2[0], z], axis=1), jnp.concatenate([z, w2[1]], axis=1)], axis=0)


def kernel(x, c, ctx, c_ctx, norm1_g, norm2_g, mod_down, mod_up, mod_b, w_in, mla_q_norm_g, mla_w_uq,
           mla_kv_norm_g, mla_w_ukv, rwkv_mu, rwkv_w0, rwkv_w2, rwkv_a0, rwkv_a2, rwkv_g2, rwkv_k_k,
           rwkv_k_a, rwkv_r_k, rwkv_ln_g, rwkv_ln_b, conv_w, diff_lambda, diff_norm_g, w_branch, gate_down,
           gate_up, gate_b, w_out, mlp_w1, mlp_w2, final_norm_g):
    n_batch, t_len, dm = x.shape
    c_len = ctx.shape[1]
    depth = w_in.shape[0]
    bw = BRANCH_W
    n_x = n_batch * t_len
    n_c = n_batch * c_len
    n_rows = n_x + n_c
    tm = 512 if (t_len % 512 == 0 and n_c % 512 == 0) else 256
    tq = 1024 if t_len % 1024 == 0 else min(256, c_len)
    tr = min(256, c_len)
    assert dm == D_MODEL and t_len % tm == 0 and n_c % tm == 0 and n_x % c_len == 0
    assert t_len % tq == 0 and c_len % RWKV_CHUNK == 0 and t_len % c_len == 0
    tm_big = 1024 if (t_len % 1024 == 0 and n_c % 1024 == 0) else tm
    mod_index = functools.partial(_mod_index, tm=tm, n_x_rows=n_x, t_len=t_len, n_batch=n_batch)
    mod_index_big = functools.partial(_mod_index, tm=tm_big, n_x_rows=n_x, t_len=t_len, n_batch=n_batch)

    zeros = lambda *s: jnp.zeros(s, F32)
    lora_w = 4 * RWKV_LORA + RWKV_GATE_LORA
    rw0 = MLA_Q_LORA + MLA_KV_LORA + MLA_ROPE
    cv0 = rw0 + 3 * bw + lora_w
    w_in_t = jnp.swapaxes(w_in, 1, 2)
    w_in_p = jnp.concatenate(
        [w_in_t[:, :C_R], w_in_t[:, rw0:rw0 + 3 * bw], w_in_t[:, cv0:],
         w_in_t[:, rw0 + 3 * bw:cv0], zeros(depth, RWKV_LORA_PAD - lora_w, dm), jnp.swapaxes(gate_down, 1, 2),
         w_in_t[:, C_R:rw0], zeros(depth, P_COLS - C_KROPE - MLA_ROPE, dm)], axis=1).astype(BF16)
    w_uq_p = jnp.pad(mla_w_uq.reshape(depth, MLA_Q_LORA, MLA_HEADS, MLA_NOPE + MLA_ROPE),
                     ((0, 0), (0, 0), (0, 0), (0, MLA_QK_PAD - MLA_NOPE - MLA_ROPE))
                     ).reshape(depth, MLA_Q_LORA, MLA_HEADS * MLA_QK_PAD).astype(BF16)
    w_ukv_r = mla_w_ukv.reshape(depth, MLA_KV_LORA, MLA_HEADS, MLA_NOPE + MLA_V)
    w_ukv_p = jnp.concatenate([w_ukv_r[..., :MLA_NOPE].reshape(depth, MLA_KV_LORA, -1),
                               w_ukv_r[..., MLA_NOPE:].reshape(depth, MLA_KV_LORA, -1)], axis=-1).astype(BF16)
    w_branch_2d = w_branch.reshape(depth, 4 * bw, dm)
    gate_up_b = jnp.moveaxis(gate_up, 2, 1).astype(BF16)
    g2_p = jnp.pad(rwkv_g2, ((0, 0), (0, RWKV_LORA_PAD - 4 * RWKV_LORA - RWKV_GATE_LORA), (0, 0))).astype(BF16)
    mu_lora = jnp.pad(rwkv_mu[:, :, 3 * bw:], ((0, 0), (0, 0), (0, RWKV_LORA_PAD - lora_w)))

    cond = jnp.concatenate([c, c_ctx[None, :], zeros(16 - n_batch - 1, dm)], axis=0)
    cond = jax.nn.silu(cond)
    mods = []
    for l in range(depth):
        low = _matmul(cond, mod_down[l], tm=16, tn=MOD_RANK, tk=dm, name="mod_down")
        up = _matmul(low, mod_up[l], tm=16, tn=2048, tk=MOD_RANK, bias=mod_b[l], name="mod_up")
        mods.append(up.reshape(16, N_MOD, dm))

    cos_t, sin_t = _rope_tables128(t_len)
    xs = jnp.concatenate([x.reshape(n_x, dm), ctx.reshape(n_c, dm)], axis=0)

    for l in range(depth):
        need_ctx = l < depth - 1
        mod = mods[l]
        lam_init = 0.8 - 0.6 * math.exp(-0.3 * l)
        lq1, lk1, lq2, lk2 = diff_lambda[l]
        lam = jnp.exp(jnp.sum(lq1 * lk1)) - jnp.exp(jnp.sum(lq2 * lk2)) + lam_init
        lam_row = jnp.full((1, DIFF_V), 1.0, F32) * lam

        h1 = _norm_mod(xs, norm1_g[l], mod, shift_row=0, scale_row=1, mod_index=mod_index, tm=tm,
                       rows=n_rows)
        p, w1_b, w_out_b, w_branch_b = _fullk_matmul(h1, w_in_p, layer=l, tm=tm_big, tn=512, out_dtype=F32,
                                                     name="in_proj", cast_srcs=(mlp_w1, w_out, w_branch_2d),
                                                     w_rows_are_outputs=True)

        q = _nm_matmul(p, mla_q_norm_g[l], w_uq_p[l], tm=tm, tn=MLA_HEADS * MLA_QK_PAD, out_dtype=BF16,
                       x_col_block=C_CQ // MLA_Q_LORA, kdim=MLA_Q_LORA, name="mla_q",
                       out_scale=(MLA_NOPE + MLA_ROPE) ** -0.5 * LOG2_E)
        kv = _nm_matmul(p, mla_kv_norm_g[l], w_ukv_p[l], tm=tm, tn=MLA_HEADS * (MLA_NOPE + MLA_V), out_dtype=BF16,
                        x_col_block=C_CKV // MLA_KV_LORA, kdim=MLA_KV_LORA, name="mla_kv")
        y_mla = _mla_attention(q, kv, p, cos_t, sin_t, n_batch=n_batch, t_len=t_len, c_len=c_len, tq=tq,
                               with_ctx=need_ctx)

        y_diff = _diff_attention(p, cos_t, sin_t, lam_row, diff_norm_g[l].reshape(1, DIFF_V),
                                 n_batch=n_batch, t_len=t_len, c_len=c_len, tq=tq, with_ctx=need_ctx,
                                 out_scale=1.0 - lam_init)

        r_, kk, v_, lw, k_dir, kka, bonus, g_out, y_conv = _mixer_prep(
            p, rwkv_mu[l, :, :3 * bw], mu_lora[l], rwkv_k_k[l].reshape(1, bw), rwkv_k_a[l].reshape(1, bw),
            rwkv_r_k[l].reshape(1, bw), rwkv_w0[l].reshape(1, 2 * bw), rwkv_a0[l].reshape(1, 2 * bw),
            _block_diag2(rwkv_w2[l]).astype(BF16), _block_diag2(rwkv_a2[l]).astype(BF16), g2_p[l],
            conv_w[l], tr=min(128, tr), n_x=n_x, t_len=t_len, c_len=c_len)
        o_dirs = _rwkv_scan(r_, kk, v_, lw, k_dir, kka, n_batch=n_batch, t_len=t_len, c_len=c_len)
        y_rwkv = _rwkv_readout(o_dirs, bonus, g_out, rwkv_ln_g[l], rwkv_ln_b[l], tr=tr)

        rows = n_rows if need_ctx else n_x
        acc = _merge(p, (y_mla, y_rwkv, y_conv, y_diff), w_branch_b, gate_up_b, gate_b, layer=l,
                     tm=tm_big, tn=512, rows=rows)
        (xs_new,) = _fullk_matmul(acc, w_out_b[None], layer=0, tm=tm_big, tn=512, out_dtype=F32, res=xs, mod=mod,
                                  gate_row=2, mod_index=mod_index_big, rows=rows, name="out_proj")

        h2 = _norm_mod(xs_new, norm2_g[l], mod, shift_row=3, scale_row=4, mod_index=mod_index, tm=tm,
                       rows=rows)
        hid, w2_b = _fullk_matmul(h2, w1_b[None], layer=0, tm=tm_big, tn=TN_UP, out_dtype=BF16, act="relu2",
                                  name="mlp_up", cast_srcs=(mlp_w2,), cast_layer=l)
        xs = _matmul(hid, w2_b, tm=tm_big, tn=1024, tk=2048, res=xs_new, mod=mod, gate_row=5,
                     mod_index=mod_index_big, rows=rows, name="mlp_down")

    out = _final_norm(xs, final_norm_g, rows=n_x, tm=tm)
    return out.reshape(n_batch, t_len, dm)
```

```python
import functools
import math

import jax
import jax.numpy as jnp
from jax import lax
from jax.experimental import pallas as pl
from jax.experimental.pallas import tpu as pltpu

F32 = jnp.float32
BF16 = jnp.bfloat16

D_MODEL = 4096
BRANCH_W = 1024
GRID_W = 64
ROPE_BASE = 10000.0
NORM_EPS = 1e-6
N_MOD = 6
LANES = 128
HALO = 8
LOG2_E = 1.4426950408889634

MLA_HEADS = 8
MLA_NOPE = 128
MLA_ROPE = 64
MLA_V = 128
MLA_Q_LORA = 768
MLA_KV_LORA = 256
MLA_QK_PAD = 256

RWKV_HEAD = 64
RWKV_HEADS = 16
RWKV_LORA = 64
RWKV_GATE_LORA = 160
RWKV_GN_EPS = 64e-5
RWKV_CHUNK = 64
RWKV_LORA_PAD = 512

DIFF_HEADS = 8
DIFF_QK = 64
DIFF_V = 128
GATE_RANK = 256
MOD_RANK = 256

C_CQ = 0
C_CKV = 768
C_R = 1024
C_K = 2048
C_V = 3072
C_CB = 4096
C_CC = 5120
C_CU = 6144
C_DQ = 7168
C_DK = 8192
C_DV = 9216
C_LORA = 10240
C_GL = 10752
C_KROPE = 11008
P_COLS = 11264

TN_UP = 1024
ATTN_SUB = 128
VMEM_CAP = 56 * 1024 * 1024


def _cparams(sem, vmem_bytes):
    limit = int(min(VMEM_CAP, max(vmem_bytes * 1.5 + (4 << 20), 16 << 20)))
    return pltpu.CompilerParams(dimension_semantics=sem, vmem_limit_bytes=limit)


def _mod_index(i, tm, n_x_rows, t_len, n_batch):
    return jnp.where(i < n_x_rows // tm, i // (t_len // tm), n_batch)


def _dot(a, b):
    return jnp.dot(a.astype(BF16), b.astype(BF16), preferred_element_type=F32)


def _dot_nt(a, b):
    return lax.dot_general(a.astype(BF16), b.astype(BF16), (((1,), (1,)), ((), ())),
                           preferred_element_type=F32)


def _split3(x):
    hi = x.astype(BF16)
    r1 = x - hi.astype(F32)
    mid = r1.astype(BF16)
    lo = (r1 - mid.astype(F32)).astype(BF16)
    return hi, mid, lo


def _dot_exact_lhs(m_bf16, x):
    out = None
    for part in _split3(x):
        t = jnp.dot(m_bf16, part, preferred_element_type=F32)
        out = t if out is None else out + t
    return out


def _head_sums(x):
    r = lax.broadcasted_iota(jnp.int32, (LANES, LANES), 0)
    c = lax.broadcasted_iota(jnp.int32, (LANES, LANES), 1)
    ones_bd = jnp.where((r // RWKV_HEAD) == (c // RWKV_HEAD), 1.0, 0.0).astype(BF16)
    parts = _split3(x)
    cols = []
    for j in range(x.shape[1] // LANES):
        acc = None
        for part in parts:
            t = jnp.dot(part[:, j * LANES:(j + 1) * LANES], ones_bd, preferred_element_type=F32)
            acc = t if acc is None else acc + t
        cols.append(acc)
    return jnp.concatenate(cols, axis=1)


def _rope128(x, cos_t, sin_t):
    lane = lax.broadcasted_iota(jnp.int32, x.shape, 1)
    swapped = jnp.where((lane % 64) < 32, pltpu.roll(x, 96, 1), pltpu.roll(x, 32, 1))
    return x * cos_t + swapped * sin_t


def _matmul_kernel(*refs, nk, has_bias, has_res, gate_row, act):
    a_ref, w_ref = refs[0], refs[1]
    pos = 2
    bias_ref = res_ref = mod_ref = None
    if has_bias:
        bias_ref = refs[pos]; pos += 1
    if has_res:
        res_ref = refs[pos]; mod_ref = refs[pos + 1]; pos += 2
    o_ref, acc_ref = refs[pos], refs[pos + 1]
    k = pl.program_id(2)

    @pl.when(k == 0)
    def _():
        acc_ref[...] = jnp.zeros_like(acc_ref)

    acc_ref[...] += _dot(a_ref[...], w_ref[...])

    @pl.when(k == nk - 1)
    def _():
        y = acc_ref[...]
        if has_bias:
            y = y + bias_ref[...]
        if act == "relu2":
            y = jnp.square(jnp.maximum(y, 0.0))
        if has_res:
            y = res_ref[...] + mod_ref[0, gate_row:gate_row + 1, :] * y
        o_ref[...] = y.astype(o_ref.dtype)


def _matmul(a, w, *, tm, tn, tk, out_dtype=F32, bias=None, res=None, mod=None, gate_row=None,
            mod_index=None, act=None, rows=None, name=None):
    m = a.shape[0] if rows is None else rows
    kdim, n = w.shape
    assert a.shape[1] == kdim and m % tm == 0 and n % tn == 0 and kdim % tk == 0
    nk = kdim // tk
    in_specs = [pl.BlockSpec((tm, tk), lambda i, j, k: (i, k)),
                pl.BlockSpec((tk, tn), lambda i, j, k: (k, j))]
    args = [a, w]
    vmem = 2 * tm * tk * a.dtype.itemsize + 2 * tk * tn * w.dtype.itemsize + tm * tn * 4
    vmem += 2 * tm * tn * jnp.dtype(out_dtype).itemsize
    if bias is not None:
        in_specs.append(pl.BlockSpec((1, tn), lambda i, j, k: (0, j)))
        args.append(bias.reshape(1, n).astype(F32))
    if res is not None:
        in_specs.append(pl.BlockSpec((tm, tn), lambda i, j, k: (i, j)))
        in_specs.append(pl.BlockSpec((1, N_MOD, tn), lambda i, j, k: (mod_index(i), 0, j)))
        args += [res, mod]
        vmem += 2 * tm * tn * 4 + 2 * 8 * tn * 4
    kern = functools.partial(_matmul_kernel, nk=nk, has_bias=bias is not None,
                             has_res=res is not None, gate_row=gate_row, act=act)
    return pl.pallas_call(
        kern,
        grid=(m // tm, n // tn, nk),
        in_specs=in_specs,
        out_specs=pl.BlockSpec((tm, tn), lambda i, j, k: (i, j)),
        out_shape=jax.ShapeDtypeStruct((m, n), out_dtype),
        scratch_shapes=[pltpu.VMEM((tm, tn), F32)],
        compiler_params=_cparams(("parallel", "parallel", "arbitrary"), vmem),
        name=name,
    )(*args)


def _nm_matmul_kernel(*refs, has_mod, shift_row, scale_row, act, out_scale):
    if has_mod:
        x_ref, g_ref, mod_ref, w_ref, o_ref, h_ref = refs
    else:
        x_ref, g_ref, w_ref, o_ref, h_ref = refs

    @pl.when(pl.program_id(1) == 0)
    def _():
        x = x_ref[...].astype(F32)
        y = x * lax.rsqrt(jnp.mean(x * x, axis=-1, keepdims=True) + NORM_EPS) * g_ref[...]
        if has_mod:
            y = y * (1.0 + mod_ref[0, scale_row:scale_row + 1, :]) + mod_ref[0, shift_row:shift_row + 1, :]
        h_ref[...] = y.astype(BF16)

    y = jnp.dot(h_ref[...], w_ref[...].astype(BF16), preferred_element_type=F32)
    if act == "relu2":
        y = jnp.square(jnp.maximum(y, 0.0))
    if out_scale is not None:
        y = y * out_scale
    o_ref[...] = y.astype(o_ref.dtype)


def _nm_matmul(x, g, w, *, tm, tn, out_dtype, x_col_block=0, kdim=None, mod=None, shift_row=None,
               scale_row=None, mod_index=None, act=None, rows=None, name=None, out_scale=None):
    m = x.shape[0] if rows is None else rows
    kdim = x.shape[1] if kdim is None else kdim
    n = w.shape[1]
    assert w.shape[0] == kdim and m % tm == 0 and n % tn == 0
    in_specs = [pl.BlockSpec((tm, kdim), lambda i, j: (i, x_col_block)),
                pl.BlockSpec((1, kdim), lambda i, j: (0, 0))]
    args = [x, g.reshape(1, kdim).astype(F32)]
    if mod is not None:
        in_specs.append(pl.BlockSpec((1, N_MOD, kdim), lambda i, j: (mod_index(i), 0, 0)))
        args.append(mod)
    in_specs.append(pl.BlockSpec((kdim, tn), lambda i, j: (0, j)))
    args.append(w)
    vmem = (2 * tm * kdim * x.dtype.itemsize + tm * kdim * 2 + 2 * kdim * tn * w.dtype.itemsize
            + 2 * tm * tn * jnp.dtype(out_dtype).itemsize + tm * tn * 4 + 4 * 8 * kdim * 4)
    kern = functools.partial(_nm_matmul_kernel, has_mod=mod is not None, shift_row=shift_row,
                             scale_row=scale_row, act=act, out_scale=out_scale)
    return pl.pallas_call(
        kern,
        grid=(m // tm, n // tn),
        in_specs=in_specs,
        out_specs=pl.BlockSpec((tm, tn), lambda i, j: (i, j)),
        out_shape=jax.ShapeDtypeStruct((m, n), out_dtype),
        scratch_shapes=[pltpu.VMEM((tm, kdim), BF16)],
        compiler_params=_cparams(("parallel", "arbitrary"), vmem),
        name=name,
    )(*args)


def _norm_mod_kernel(x_ref, g_ref, mod_ref, h_ref, *, shift_row, scale_row):
    x = x_ref[...]
    y = x * lax.rsqrt(jnp.mean(x * x, axis=-1, keepdims=True) + NORM_EPS) * g_ref[...]
    y = y * (1.0 + mod_ref[0, scale_row:scale_row + 1, :]) + mod_ref[0, shift_row:shift_row + 1, :]
    h_ref[...] = y.astype(h_ref.dtype)


def _norm_mod(xs, g, mod, *, shift_row, scale_row, mod_index, tm, rows):
    dm = xs.shape[1]
    return pl.pallas_call(
        functools.partial(_norm_mod_kernel, shift_row=shift_row, scale_row=scale_row),
        grid=(rows // tm,),
        in_specs=[pl.BlockSpec((tm, dm), lambda i: (i, 0)), pl.BlockSpec((1, dm), lambda i: (0, 0)),
                  pl.BlockSpec((1, N_MOD, dm), lambda i: (mod_index(i), 0, 0))],
        out_specs=pl.BlockSpec((tm, dm), lambda i: (i, 0)),
        out_shape=jax.ShapeDtypeStruct((rows, dm), BF16),
        compiler_params=_cparams(("parallel",), 2 * tm * dm * 6 + 4 * tm * dm * 4),
        name="norm_mod",
    )(xs, g.reshape(1, dm), mod)


def _fullk_kernel(*refs, has_res, n_cast, gate_row, act, w_rows_are_outputs):
    refs = list(refs)
    if n_cast:
        cast_dsts = refs[-n_cast:]
        cast_srcs = refs[-2 * n_cast - 1:-n_cast - 1]
        refs = refs[:-2 * n_cast - 1] + [refs[-n_cast - 1]]
        for src, dst in zip(cast_srcs, cast_dsts):
            dst[...] = src[...].astype(dst.dtype)
    if has_res:
        a_ref, w_ref, res_ref, mod_ref, o_ref = refs
    else:
        a_ref, w_ref, o_ref = refs
    if w_rows_are_outputs:
        y = lax.dot_general(a_ref[...], w_ref[...], (((1,), (1,)), ((), ())), preferred_element_type=F32)
    else:
        y = jnp.dot(a_ref[...], w_ref[...], preferred_element_type=F32)
    if act == "relu2":
        y = jnp.square(jnp.maximum(y, 0.0))
    if has_res:
        y = res_ref[...] + mod_ref[0, gate_row:gate_row + 1, :] * y
    o_ref[...] = y.astype(o_ref.dtype)


def _cast_rows_per_step(n_rows, n_steps):
    rows = 16
    while n_rows % rows or n_rows // rows > n_steps:
        rows *= 2
    return rows


def _fullk_matmul(a, w, *, tm, tn, out_dtype, act=None, res=None, mod=None, gate_row=None, mod_index=None,
                  rows=None, name=None, layer=None, cast_srcs=(), cast_layer=None, w_rows_are_outputs=False):
    m = a.shape[0] if rows is None else rows
    if w_rows_are_outputs:
        _, n, kdim = w.shape
        w_spec = pl.BlockSpec((None, tn, kdim), lambda i, j: (layer, j, 0))
    else:
        _, kdim, n = w.shape
        w_spec = pl.BlockSpec((None, kdim, tn), lambda i, j: (layer, 0, j))
    assert a.shape[1] == kdim and m % tm == 0 and n % tn == 0 and a.dtype == BF16 and w.dtype == BF16
    in_specs = [pl.BlockSpec((tm, kdim), lambda i, j: (i, 0)), w_spec]
    args = [a, w]
    vmem = 2 * tm * kdim * 2 + 2 * kdim * tn * 2 + 2 * tm * tn * jnp.dtype(out_dtype).itemsize + 2 * tm * tn * 4
    if res is not None:
        in_specs += [pl.BlockSpec((tm, tn), lambda i, j: (i, j)),
                     pl.BlockSpec((1, N_MOD, tn), lambda i, j: (mod_index(i), 0, j))]
        args += [res, mod]
        vmem += 2 * tm * tn * 4
    out_specs = [pl.BlockSpec((tm, tn), lambda i, j: (i, j))]
    out_shape = [jax.ShapeDtypeStruct((m, n), out_dtype)]
    nj = n // tn
    cast_layer = layer if cast_layer is None else cast_layer
    for cast_src in cast_srcs:
        _, c_rows, c_cols = cast_src.shape
        cr = _cast_rows_per_step(c_rows, (m // tm) * nj)
        cast_blk = lambda i, j, last=c_rows // cr - 1: jnp.minimum(i * nj + j, last)
        in_specs.append(pl.BlockSpec((None, cr, c_cols), lambda i, j, blk=cast_blk: (cast_layer, blk(i, j), 0)))
        args.append(cast_src)
        out_specs.append(pl.BlockSpec((cr, c_cols), lambda i, j, blk=cast_blk: (blk(i, j), 0)))
        out_shape.append(jax.ShapeDtypeStruct((c_rows, c_cols), BF16))
        vmem += 2 * cr * c_cols * 6
    return pl.pallas_call(
        functools.partial(_fullk_kernel, has_res=res is not None, n_cast=len(cast_srcs),
                          gate_row=gate_row, act=act, w_rows_are_outputs=w_rows_are_outputs),
        grid=(m // tm, n // tn),
        in_specs=in_specs,
        out_specs=out_specs,
        out_shape=out_shape,
        compiler_params=_cparams(("arbitrary", "arbitrary"), vmem),
        name=name,
    )(*args)


def _merge_kernel(gl_ref, y0_ref, y1_ref, y2_ref, y3_ref, wb_ref, gu_ref, gb_ref, o_ref):
    gl = gl_ref[...].astype(BF16)
    bw = wb_ref.shape[0] // 4
    acc = None
    for i, y_ref in enumerate((y0_ref, y1_ref, y2_ref, y3_ref)):
        gate = jax.nn.sigmoid(jnp.dot(gl, gu_ref[i], preferred_element_type=F32) + gb_ref[i:i + 1, :])
        term = gate * jnp.dot(y_ref[...], wb_ref[i * bw:(i + 1) * bw, :], preferred_element_type=F32)
        acc = term if acc is None else acc + term
    o_ref[...] = acc.astype(o_ref.dtype)


def _merge(p, ys, wb, gu, gb, *, layer, tm, tn, rows):
    n = wb.shape[1]
    bw = wb.shape[0] // 4
    gr = gu.shape[2]
    y_spec = pl.BlockSpec((tm, bw), lambda i, j: (i, 0))
    vmem = (2 * tm * gr * 4 + 4 * 2 * tm * bw * 2 + 2 * 4 * bw * tn * 2 + 2 * 4 * gr * tn * 2
            + 2 * tm * tn * 2 + 3 * tm * tn * 4)
    return pl.pallas_call(
        _merge_kernel,
        grid=(rows // tm, n // tn),
        in_specs=[pl.BlockSpec((tm, gr), lambda i, j: (i, C_GL // GATE_RANK)),
                  y_spec, y_spec, y_spec, y_spec,
                  pl.BlockSpec((4 * bw, tn), lambda i, j: (0, j)),
                  pl.BlockSpec((None, 4, gr, tn), lambda i, j: (layer, 0, 0, j)),
                  pl.BlockSpec((None, 4, tn), lambda i, j: (layer, 0, j))],
        out_specs=pl.BlockSpec((tm, tn), lambda i, j: (i, j)),
        out_shape=jax.ShapeDtypeStruct((rows, n), BF16),
        compiler_params=_cparams(("parallel", "arbitrary"), vmem),
        name="merge",
    )(p, *ys, wb, gu, gb)


def _softmax_parts(s_list):
    m = None
    for s in s_list:
        sm = jnp.max(s, axis=-1, keepdims=True)
        m = sm if m is None else jnp.maximum(m, sm)
    e_list = [jnp.exp2(s - m) for s in s_list]
    l = None
    for e in e_list:
        es = jnp.sum(e, axis=-1, keepdims=True)
        l = es if l is None else l + es
    return e_list, l


def _attend(q, ks, vs):
    e_list, l = _softmax_parts([_dot_nt(q, k) for k in ks])
    o = None
    for e, v in zip(e_list, vs):
        t = jnp.dot(e.astype(BF16), v, preferred_element_type=F32)
        o = t if o is None else o + t
    return o / l


def _attend_many(qs, ks, vs, sub):
    parts = [q[r:r + sub] for q in qs for r in range(0, q.shape[0], sub)]
    n = len(parts)
    scores, soft, outs = {}, {}, []
    for t in range(n + 2):
        if t < n:
            scores[t] = [_dot_nt(parts[t], k) for k in ks]
        if 0 <= t - 1 < n:
            soft[t - 1] = _softmax_parts(scores.pop(t - 1))
        if 0 <= t - 2 < n:
            e_list, l = soft.pop(t - 2)
            o = None
            for e, v in zip(e_list, vs):
                pv = jnp.dot(e.astype(BF16), v, preferred_element_type=F32)
                o = pv if o is None else o + pv
            outs.append(o / l)
    per_q = len(parts) // len(qs)
    return [jnp.concatenate(outs[i * per_q:(i + 1) * per_q], axis=0) for i in range(len(qs))]


def _mla_kernel(cos_ref, sin_ref, q_ref, knx_ref, knc_ref, krx_ref, krc_ref, vx_ref, vc_ref, o_ref,
                kx_s, kc_s, *, tq):
    i = pl.program_id(2)

    @pl.when(i == 0)
    def _():
        kx_s[:, :MLA_NOPE] = knx_ref[...]
        kx_s[:, MLA_NOPE:] = _rope128(krx_ref[...], cos_ref[...], sin_ref[...]).astype(BF16)
        kc_s[:, :MLA_NOPE] = knc_ref[...]
        kc_s[:, MLA_NOPE:] = krc_ref[...].astype(BF16)

    row0 = pl.multiple_of(i * tq, tq)
    q = q_ref[...]
    q_rope = _rope128(q[:, MLA_NOPE:].astype(F32), cos_ref[pl.ds(row0, tq), :], sin_ref[pl.ds(row0, tq), :])
    q = jnp.concatenate([q[:, :MLA_NOPE], q_rope.astype(BF16)], axis=1)
    (o,) = _attend_many((q,), (kc_s[...], kx_s[...]), (vc_ref[...], vx_ref[...]), ATTN_SUB)
    o_ref[...] = o.astype(o_ref.dtype)


def _mla_ctx_kernel(q_ref, knc_ref, krc_ref, vc_ref, y_hbm, o_ref):
    del y_hbm
    kc = jnp.concatenate([knc_ref[...], krc_ref[...].astype(BF16)], axis=1)
    o = _attend(q_ref[...], (kc,), (vc_ref[...],))
    o_ref[...] = o.astype(o_ref.dtype)


def _diff_finish(o1, o2, lam_ref, g_ref, out_scale):
    o = o1 - lam_ref[...] * o2
    y = o * lax.rsqrt(jnp.mean(o * o, axis=-1, keepdims=True) + NORM_EPS) * g_ref[...]
    return y * out_scale


def _diff_halves(q):
    lane = lax.broadcasted_iota(jnp.int32, q.shape, 1)
    return jnp.where(lane < DIFF_QK, q, 0.0).astype(BF16), jnp.where(lane < DIFF_QK, 0.0, q).astype(BF16)


def _diff_kernel(cos_ref, sin_ref, lam_ref, g_ref, q_ref, kx_ref, kc_ref, vx_ref, vc_ref, o_ref,
                 kx_s, kc_s, vx_s, vc_s, *, tq, scale, out_scale):
    i = pl.program_id(2)

    @pl.when(i == 0)
    def _():
        kx_s[...] = _rope128(kx_ref[...], cos_ref[...], sin_ref[...]).astype(BF16)
        kc_s[...] = kc_ref[...].astype(BF16)
        vx_s[...] = vx_ref[...].astype(BF16)
        vc_s[...] = vc_ref[...].astype(BF16)

    row0 = pl.multiple_of(i * tq, tq)
    q = q_ref[...] * (scale * LOG2_E)
    q1, q2 = _diff_halves(_rope128(q, cos_ref[pl.ds(row0, tq), :], sin_ref[pl.ds(row0, tq), :]))
    ks, vs = (kc_s[...], kx_s[...]), (vc_s[...], vx_s[...])
    o1, o2 = _attend_many((q1, q2), ks, vs, ATTN_SUB)
    y = _diff_finish(o1, o2, lam_ref, g_ref, out_scale)
    o_ref[...] = y.astype(o_ref.dtype)


def _diff_ctx_kernel(lam_ref, g_ref, q_ref, kc_ref, vc_ref, y_hbm, o_ref, *, scale, out_scale):
    del y_hbm
    q1, q2 = _diff_halves(q_ref[...] * (scale * LOG2_E))
    ks, vs = (kc_ref[...].astype(BF16),), (vc_ref[...].astype(BF16),)
    y = _diff_finish(_attend(q1, ks, vs), _attend(q2, ks, vs), lam_ref, g_ref, out_scale)
    o_ref[...] = y.astype(o_ref.dtype)


def _mla_attention(q, kv, p, cos_t, sin_t, *, n_batch, t_len, c_len, tq, with_ctx):
    n_x = n_batch * t_len
    nq = t_len // tq
    out_rows = n_x + (n_batch * c_len if with_ctx else 0)
    cblk0 = n_x // c_len
    qmap = lambda b, h, i: (b * nq + i, h)
    full = lambda b, h, i: (0, 0)
    vmem = (4 * t_len * LANES * 4 + 2 * tq * 256 * 2 + 2 * (t_len + c_len) * LANES * (2 + 4 + 2)
            + (t_len + c_len) * 256 * 2 + 2 * tq * LANES * 2 + 6 * tq * (t_len + c_len) * 4)
    y = pl.pallas_call(
        functools.partial(_mla_kernel, tq=tq),
        grid=(n_batch, MLA_HEADS, nq),
        in_specs=[pl.BlockSpec((t_len, LANES), full), pl.BlockSpec((t_len, LANES), full),
                  pl.BlockSpec((tq, MLA_QK_PAD), qmap),
                  pl.BlockSpec((t_len, MLA_NOPE), lambda b, h, i: (b, h)),
                  pl.BlockSpec((c_len, MLA_NOPE), lambda b, h, i: (cblk0 + b, h)),
                  pl.BlockSpec((t_len, LANES), lambda b, h, i: (b, C_KROPE // LANES)),
                  pl.BlockSpec((c_len, LANES), lambda b, h, i: (cblk0 + b, C_KROPE // LANES)),
                  pl.BlockSpec((t_len, MLA_V), lambda b, h, i: (b, MLA_HEADS + h)),
                  pl.BlockSpec((c_len, MLA_V), lambda b, h, i: (cblk0 + b, MLA_HEADS + h))],
        out_specs=pl.BlockSpec((tq, MLA_V), qmap),
        out_shape=jax.ShapeDtypeStruct((out_rows, MLA_HEADS * MLA_V), BF16),
        scratch_shapes=[pltpu.VMEM((t_len, MLA_QK_PAD), BF16), pltpu.VMEM((c_len, MLA_QK_PAD), BF16)],
        compiler_params=_cparams(("parallel", "parallel", "arbitrary"), vmem),
        name="mla_attention",
    )(cos_t, sin_t, q, kv, kv, p, p, kv, kv)
    if not with_ctx:
        return y
    cmap = lambda b, h: (cblk0 + b, h)
    return pl.pallas_call(
        _mla_ctx_kernel,
        grid=(n_batch, MLA_HEADS),
        in_specs=[pl.BlockSpec((c_len, MLA_QK_PAD), cmap),
                  pl.BlockSpec((c_len, MLA_NOPE), cmap),
                  pl.BlockSpec((c_len, LANES), lambda b, h: (cblk0 + b, C_KROPE // LANES)),
                  pl.BlockSpec((c_len, MLA_V), lambda b, h: (cblk0 + b, MLA_HEADS + h)),
                  pl.BlockSpec(memory_space=pl.ANY)],
        out_specs=pl.BlockSpec((c_len, MLA_V), cmap),
        out_shape=jax.ShapeDtypeStruct(y.shape, y.dtype),
        input_output_aliases={4: 0},
        compiler_params=_cparams(("parallel", "parallel"), 16 * c_len * c_len * 4 + 8 * c_len * 256 * 4),
        name="mla_attention_ctx",
    )(q, kv, p, kv, y)


def _diff_attention(p, cos_t, sin_t, lam_row, g_row, *, n_batch, t_len, c_len, tq, with_ctx, out_scale):
    n_x = n_batch * t_len
    nq = t_len // tq
    out_rows = n_x + (n_batch * c_len if with_ctx else 0)
    cblk0 = n_x // c_len
    scale = DIFF_QK ** -0.5
    full = lambda b, h, i: (0, 0)
    qblk, kblk, vblk = C_DQ // LANES, C_DK // LANES, C_DV // LANES
    vmem = (4 * t_len * LANES * 4 + 2 * tq * LANES * 4 + 4 * (t_len + c_len) * LANES * 4
            + 2 * (t_len + c_len) * LANES * 2 + 2 * tq * LANES * 2 + 8 * tq * (t_len + c_len) * 4)
    y = pl.pallas_call(
        functools.partial(_diff_kernel, tq=tq, scale=scale, out_scale=out_scale),
        grid=(n_batch, DIFF_HEADS, nq),
        in_specs=[pl.BlockSpec((t_len, LANES), full), pl.BlockSpec((t_len, LANES), full),
                  pl.BlockSpec((1, DIFF_V), full), pl.BlockSpec((1, DIFF_V), full),
                  pl.BlockSpec((tq, LANES), lambda b, h, i: (b * nq + i, qblk + h)),
                  pl.BlockSpec((t_len, LANES), lambda b, h, i: (b, kblk + h)),
                  pl.BlockSpec((c_len, LANES), lambda b, h, i: (cblk0 + b, kblk + h)),
                  pl.BlockSpec((t_len, LANES), lambda b, h, i: (b, vblk + h)),
                  pl.BlockSpec((c_len, LANES), lambda b, h, i: (cblk0 + b, vblk + h))],
        out_specs=pl.BlockSpec((tq, DIFF_V), lambda b, h, i: (b * nq + i, h)),
        out_shape=jax.ShapeDtypeStruct((out_rows, DIFF_HEADS * DIFF_V), BF16),
        scratch_shapes=[pltpu.VMEM((t_len, LANES), BF16), pltpu.VMEM((c_len, LANES), BF16),
                        pltpu.VMEM((t_len, LANES), BF16), pltpu.VMEM((c_len, LANES), BF16)],
        compiler_params=_cparams(("parallel", "parallel", "arbitrary"), vmem),
        name="diff_attention",
    )(cos_t, sin_t, lam_row, g_row, p, p, p, p, p)
    if not with_ctx:
        return y
    one = lambda b, h: (0, 0)
    return pl.pallas_call(
        functools.partial(_diff_ctx_kernel, scale=scale, out_scale=out_scale),
        grid=(n_batch, DIFF_HEADS),
        in_specs=[pl.BlockSpec((1, DIFF_V), one), pl.BlockSpec((1, DIFF_V), one),
                  pl.BlockSpec((c_len, LANES), lambda b, h: (cblk0 + b, qblk + h)),
                  pl.BlockSpec((c_len, LANES), lambda b, h: (cblk0 + b, kblk + h)),
                  pl.BlockSpec((c_len, LANES), lambda b, h: (cblk0 + b, vblk + h)),
                  pl.BlockSpec(memory_space=pl.ANY)],
        out_specs=pl.BlockSpec((c_len, DIFF_V), lambda b, h: (cblk0 + b, h)),
        out_shape=jax.ShapeDtypeStruct(y.shape, y.dtype),
        input_output_aliases={5: 0},
        compiler_params=_cparams(("parallel", "parallel"), 24 * c_len * c_len * 4 + 8 * c_len * LANES * 4),
        name="diff_attention_ctx",
    )(lam_row, g_row, p, p, p, y)


def _prep_kernel(r_ref, k_ref, v_ref, cb_ref, cc_ref, cu_ref, lo_ref, hp_ref, hn_ref,
                 mu_ref, mul_ref, kk_w_ref, ka_w_ref, rk_w_ref, w0_ref, a0_ref, w2_ref, a2_ref, g2_ref,
                 cw_ref,
                 r_o, kk_o, v_o, lw_o, kd_o, ka_o, bonus_o, g_o, conv_o, *, tr, n_x, t_len, c_len):
    i = pl.program_id(0)
    g0 = i * tr
    seq = jnp.where(g0 < n_x, t_len, c_len)
    has_prev = ((g0 % seq) != 0).astype(F32)
    has_next = (((g0 + tr) % seq) != 0).astype(F32)
    row = lax.broadcasted_iota(jnp.int32, (tr, 1), 0)

    def neighbours(x, col0):
        width = x.shape[1]
        before = hp_ref[HALO - 1:HALO, col0:col0 + width] * has_prev
        after = hn_ref[0:1, col0:col0 + width] * has_next
        prev = jnp.where(row == 0, before, pltpu.roll(x, 1, 0))
        nxt = jnp.where(row == tr - 1, after, pltpu.roll(x, tr - 1, 0))
        return prev, nxt

    def shifted(x, col0, mu0, mu1):
        prev, nxt = neighbours(x, col0)
        return x + mu0 * (prev - x) + mu1 * (nxt - x)

    r = shifted(r_ref[...], C_R, mu_ref[0:1, :BRANCH_W], mu_ref[1:2, :BRANCH_W])
    k = shifted(k_ref[...], C_K, mu_ref[0:1, BRANCH_W:2 * BRANCH_W], mu_ref[1:2, BRANCH_W:2 * BRANCH_W])
    v = shifted(v_ref[...], C_V, mu_ref[0:1, 2 * BRANCH_W:], mu_ref[1:2, 2 * BRANCH_W:])
    lo = shifted(lo_ref[...], C_LORA, mul_ref[0:1, :], mul_ref[1:2, :])
    wd = jnp.tanh(lo[:, :2 * RWKV_LORA])
    ad = lo[:, 2 * RWKV_LORA:4 * RWKV_LORA]
    gd = jax.nn.sigmoid(lo[:, 4 * RWKV_LORA:])
    w_pre = _dot(wd, w2_ref[...]) + w0_ref[...]
    a_sig = jax.nn.sigmoid(_dot(ad, a2_ref[...]) + a0_ref[...])
    g_o[...] = _dot(gd, g2_ref[...])
    w_log = -(jnp.maximum(-w_pre, 0.0) + jnp.log(1.0 + jnp.exp(-jnp.abs(w_pre)))) - 0.5
    lw = -jnp.exp(w_log)
    kkf = k * kk_w_ref[...]
    kk = kkf * lax.rsqrt(_head_sums(kkf * kkf) + 1e-12)
    r_o[...] = r
    kk_o[...] = kk
    v_o[...] = v
    k_sum = None
    for d in range(2):
        a_d = a_sig[:, d * BRANCH_W:(d + 1) * BRANCH_W]
        k_d = k * (1.0 + (a_d - 1.0) * ka_w_ref[...])
        lw_o[d] = lw[:, d * BRANCH_W:(d + 1) * BRANCH_W]
        kd_o[d] = k_d
        ka_o[d] = kk * a_d
        k_sum = k_d if k_sum is None else k_sum + k_d
    bonus_o[...] = _head_sums(r * k_sum * rk_w_ref[...]) * v

    z = cc_ref[...] * cu_ref[...]
    z_before = hp_ref[HALO - 1:HALO, C_CC:C_CC + BRANCH_W] * hp_ref[HALO - 1:HALO, C_CU:C_CU + BRANCH_W] * has_prev
    z_after = hn_ref[0:1, C_CC:C_CC + BRANCH_W] * hn_ref[0:1, C_CU:C_CU + BRANCH_W] * has_next
    z_prev = jnp.where(row == 0, z_before, pltpu.roll(z, 1, 0))
    z_next = jnp.where(row == tr - 1, z_after, pltpu.roll(z, tr - 1, 0))
    y = cb_ref[...] * (cw_ref[0:1, :] * z_prev + cw_ref[1:2, :] * z + cw_ref[2:3, :] * z_next)
    conv_o[...] = y.astype(conv_o.dtype)


def _mixer_prep(p, mu_rkv, mu_lora, kk_w, ka_w, rk_w, w0, a0, w2bd, a2bd, g2p, conv_w, *, tr, n_x, t_len,
                c_len):
    n_rows = p.shape[0]
    bw = BRANCH_W
    last_halo = n_rows // HALO - 1
    col = lambda c: (lambda i: (i, c))
    const = lambda i: (0, 0)
    main = [pl.BlockSpec((tr, bw), col(C_R // bw)), pl.BlockSpec((tr, bw), col(C_K // bw)),
            pl.BlockSpec((tr, bw), col(C_V // bw)), pl.BlockSpec((tr, bw), col(C_CB // bw)),
            pl.BlockSpec((tr, bw), col(C_CC // bw)), pl.BlockSpec((tr, bw), col(C_CU // bw)),
            pl.BlockSpec((tr, RWKV_LORA_PAD), col(C_LORA // RWKV_LORA_PAD)),
            pl.BlockSpec((HALO, P_COLS), lambda i: (jnp.maximum(i * (tr // HALO) - 1, 0), 0)),
            pl.BlockSpec((HALO, P_COLS), lambda i: (jnp.minimum((i + 1) * (tr // HALO), last_halo), 0))]
    params = [mu_rkv, mu_lora, kk_w, ka_w, rk_w, w0, a0, w2bd, a2bd, g2p, conv_w]
    param_specs = [pl.BlockSpec(a.shape, const) for a in params]
    row_spec = pl.BlockSpec((tr, bw), lambda i: (i, 0))
    dir_spec = pl.BlockSpec((2, tr, bw), lambda i: (0, i, 0))
    f32_rows = jax.ShapeDtypeStruct((n_rows, bw), F32)
    f32_dirs = jax.ShapeDtypeStruct((2, n_rows, bw), F32)
    vmem = 2 * (7 * tr * bw * 4 + 2 * HALO * P_COLS * 4 + 12 * tr * bw * 4) + 30 * tr * bw * 4
    return pl.pallas_call(
        functools.partial(_prep_kernel, tr=tr, n_x=n_x, t_len=t_len, c_len=c_len),
        grid=(n_rows // tr,),
        in_specs=main + param_specs,
        out_specs=[row_spec, row_spec, row_spec, dir_spec, dir_spec, dir_spec, row_spec, row_spec, row_spec],
        out_shape=[f32_rows, f32_rows, f32_rows, f32_dirs, f32_dirs, f32_dirs, f32_rows, f32_rows,
                   jax.ShapeDtypeStruct((n_rows, bw), BF16)],
        compiler_params=_cparams(("parallel",), vmem),
        name="mixer_prep",
    )(p, p, p, p, p, p, p, p, p, *params)


def _rwkv_kernel(r0_ref, kk0_ref, v0_ref, r1_ref, kk1_ref, v1_ref, lw0_ref, kd0_ref, ka0_ref,
                 lw1_ref, kd1_ref, ka1_ref, o0_ref, o1_ref, s_ref):
    C = RWKV_CHUNK
    W = 2 * RWKV_HEAD
    n_pairs = RWKV_HEADS // 2

    @pl.when(pl.program_id(1) == 0)
    def _():
        s_ref[...] = jnp.zeros_like(s_ref)

    mm, mm_nt = _dot, _dot_nt
    lane = lax.broadcasted_iota(jnp.int32, (C, W), 1)
    row = lax.broadcasted_iota(jnp.int32, (C, W), 0)
    first = lane < RWKV_HEAD
    rc = lax.broadcasted_iota(jnp.int32, (C, C), 0)
    cc = lax.broadcasted_iota(jnp.int32, (C, C), 1)

    def bdiag(x):
        return jnp.concatenate([jnp.where(first, x, jnp.zeros_like(x)), jnp.where(first, jnp.zeros_like(x), x)],
                               axis=0)

    def direction(sign, r_ref, kk_ref, v_ref, lw_ref, kd_ref, ka_ref):
        m_incl = jnp.where((rc - cc) * sign >= 0, 1.0, 0.0).astype(BF16)
        lw = lw_ref[0]
        cum = _dot_exact_lhs(m_incl, lw)
        tot = jnp.sum(lw, axis=0, keepdims=True)
        e_ninc = jnp.exp(-cum)
        e_rem = jnp.exp(tot - cum)
        kk, ka, kd = kk_ref[...], ka_ref[0], kd_ref[0]
        order = (row - (lane % RWKV_HEAD)) * sign
        return dict(a_t=kk * jnp.exp(cum - lw), b_t=-ka * e_ninc, k_t=kd * e_ninc, r_t=r_ref[...] * jnp.exp(cum),
                    b_h=-ka * e_rem, k_h=kd * e_rem, v=v_ref[...], e_tot=jnp.exp(tot), strict=order > 0,
                    incl=order >= 0, eye=jnp.where(order == 0, 1.0, 0.0).astype(F32))

    dirs = (direction(1, r0_ref, kk0_ref, v0_ref, lw0_ref, kd0_ref, ka0_ref),
            direction(-1, r1_ref, kk1_ref, v1_ref, lw1_ref, kd1_ref, ka1_ref))
    o_refs = (o0_ref, o1_ref)

    chains = [(d, p) for d in range(2) for p in range(n_pairs)]
    n = range(len(chains))
    sl = [slice(W * p, W * (p + 1)) for (_, p) in chains]
    dd = [dirs[d] for (d, _) in chains]
    ar = [jnp.concatenate([dd[c]["a_t"][:, sl[c]], dd[c]["r_t"][:, sl[c]]], axis=0).astype(BF16) for c in n]
    a_bd = [bdiag(dd[c]["a_t"][:, sl[c]].astype(BF16)) for c in n]
    bk_bd = [jnp.concatenate([bdiag(dd[c]["b_t"][:, sl[c]].astype(BF16)), bdiag(dd[c]["k_t"][:, sl[c]].astype(BF16))],
                             axis=0) for c in n]
    v_bd = [bdiag(dd[c]["v"][:, sl[c]].astype(BF16)) for c in n]
    g = [mm_nt(ar[c], bk_bd[c]) for c in n]
    l_pow = [jnp.where(dd[c]["strict"], g[c][:C, :W], 0.0) for c in n]
    m_ak = [jnp.where(dd[c]["strict"], g[c][:C, W:], 0.0).astype(BF16) for c in n]
    a_r = [jnp.where(jnp.concatenate([dd[c]["incl"], dd[c]["incl"]], axis=1), g[c][C:], 0.0).astype(BF16) for c in n]
    mv = [mm(m_ak[c], v_bd[c]) for c in n]
    l_bd = [bdiag(l_pow[c].astype(BF16)) for c in n]
    t_inv = [dd[c]["eye"] + l_pow[c] for c in n]
    l_pow = [mm(l_pow[c], l_bd[c]) for c in n]
    for _ in range(int(math.log2(C)) - 2):
        l_bd = [bdiag(l_pow[c].astype(BF16)) for c in n]
        lt = [mm(jnp.concatenate([l_pow[c], t_inv[c]], axis=0), l_bd[c]) for c in n]
        l_pow = [lt[c][:C] for c in n]
        t_inv = [t_inv[c] + lt[c][C:] for c in n]
    t_inv = [t_inv[c] + mm(t_inv[c], bdiag(l_pow[c].astype(BF16))) for c in n]
    wu = [mm(t_inv[c], jnp.concatenate([a_bd[c], bdiag(mv[c].astype(BF16))], axis=1)) for c in n]
    s_bd = [s_ref[d, p] for (d, p) in chains]
    wr = [mm(jnp.concatenate([wu[c][:, :W], dd[c]["r_t"][:, sl[c]]], axis=0), s_bd[c]) for c in n]
    zv = [jnp.concatenate([bdiag((wr[c][:C] + wu[c][:, W:]).astype(BF16)), v_bd[c]], axis=0) for c in n]
    for c, (d, p) in enumerate(chains):
        o_refs[d][:, sl[c]] = wr[c][C:] + mm(a_r[c], zv[c])
    for c, (d, p) in enumerate(chains):
        bk_h = jnp.concatenate([bdiag(dd[c]["b_h"][:, sl[c]]), bdiag(dd[c]["k_h"][:, sl[c]])], axis=0)
        tot_col = jnp.broadcast_to(dd[c]["e_tot"][:, sl[c]], (W, W)).T
        s_ref[d, p] = tot_col * s_bd[c] + mm(bk_h.T, zv[c])


def _rwkv_scan(r, kk, v, lw, kd, ka, *, n_batch, t_len, c_len):
    C = RWKV_CHUNK
    rows, width = r.shape
    nct, nxt = c_len // C, t_len // C
    ctx_blk0 = n_batch * t_len // C

    def blk(b, d, s):
        j_c = s if d == 0 else nct - 1 - s
        j_x = s - nct if d == 0 else nxt - 1 - (s - nct)
        return jnp.where(s < nct, ctx_blk0 + b * nct + j_c, b * nxt + j_x)

    def shared(d):
        return pl.BlockSpec((C, width), lambda b, s: (blk(b, d, s), 0))

    def per_dir(d):
        return pl.BlockSpec((1, C, width), lambda b, s: (d, blk(b, d, s), 0))

    vmem = 2 * 14 * C * width * 4 + RWKV_HEADS * 128 * 128 * 4 + 128 * C * width * 4
    out = jax.ShapeDtypeStruct((rows, width), F32)
    return pl.pallas_call(
        _rwkv_kernel,
        grid=(n_batch, nct + nxt),
        in_specs=[shared(0), shared(0), shared(0), shared(1), shared(1), shared(1),
                  per_dir(0), per_dir(0), per_dir(0), per_dir(1), per_dir(1), per_dir(1)],
        out_specs=[shared(0), shared(1)],
        out_shape=[out, out],
        scratch_shapes=[pltpu.VMEM((2, RWKV_HEADS // 2, 128, 128), F32)],
        compiler_params=_cparams(("parallel", "arbitrary"), vmem),
        name="rwkv_scan",
    )(r, kk, v, r, kk, v, lw, kd, ka, lw, kd, ka)


def _rwkv_readout_kernel(o0_ref, o1_ref, bonus_ref, g_ref, lng_ref, lnb_ref, y_ref):
    o = o0_ref[...] + o1_ref[...]
    mean = _head_sums(o) * (1.0 / RWKV_HEAD)
    cen = o - mean
    var = _head_sums(cen * cen) * (1.0 / RWKV_HEAD)
    o_n = cen * lax.rsqrt(var + RWKV_GN_EPS) * lng_ref[...] + lnb_ref[...]
    y_ref[...] = ((o_n + bonus_ref[...]) * g_ref[...]).astype(y_ref.dtype)


def _rwkv_readout(o_dirs, bonus, g_out, ln_g, ln_b, *, tr):
    n_rows, bw = bonus.shape
    row_spec = pl.BlockSpec((tr, bw), lambda i: (i, 0))
    const = pl.BlockSpec((1, bw), lambda i: (0, 0))
    return pl.pallas_call(
        _rwkv_readout_kernel,
        grid=(n_rows // tr,),
        in_specs=[row_spec, row_spec, row_spec, row_spec, const, const],
        out_specs=row_spec,
        out_shape=jax.ShapeDtypeStruct((n_rows, bw), BF16),
        compiler_params=_cparams(("parallel",), 2 * 5 * tr * bw * 4 + 16 * tr * bw * 4),
        name="rwkv_readout",
    )(o_dirs[0], o_dirs[1], bonus, g_out, ln_g.reshape(1, bw), ln_b.reshape(1, bw))


def _rmsnorm_kernel(x_ref, g_ref, o_ref):
    x = x_ref[...]
    o_ref[...] = x * lax.rsqrt(jnp.mean(x * x, axis=-1, keepdims=True) + NORM_EPS) * g_ref[...]


def _final_norm(xs, g, *, rows, tm):
    dm = xs.shape[1]
    return pl.pallas_call(
        _rmsnorm_kernel,
        grid=(rows // tm,),
        in_specs=[pl.BlockSpec((tm, dm), lambda i: (i, 0)), pl.BlockSpec((1, dm), lambda i: (0, 0))],
        out_specs=pl.BlockSpec((tm, dm), lambda i: (i, 0)),
        out_shape=jax.ShapeDtypeStruct((rows, dm), F32),
        compiler_params=_cparams(("parallel",), 4 * tm * dm * 4),
        name="final_norm",
    )(xs, g.reshape(1, dm))


def _rope_tables128(n_tokens):
    rows = n_tokens // GRID_W
    row = jnp.repeat(jnp.arange(rows, dtype=F32), GRID_W)
    col = jnp.tile(jnp.arange(GRID_W, dtype=F32), rows)
    n_freq = 64 // 4
    inv = ROPE_BASE ** (-jnp.arange(n_freq, dtype=F32) / n_freq)
    ang = jnp.concatenate([row[:, None] * inv, col[:, None] * inv], axis=-1)
    cos, sin = jnp.cos(ang), jnp.sin(ang)
    return jnp.concatenate([cos, cos, cos, cos], axis=-1), jnp.concatenate([-sin, sin, -sin, sin], axis=-1)


def _block_diag2(w2):
    z = jnp.zeros_like(w2[0])
    return jnp.concatenate([jnp.concatenate([w2[0], z], axis=1), jnp.concatenate([z, w2[1]], axis=1)], axis=0)


def kernel(x, c, ctx, c_ctx, norm1_g, norm2_g, mod_down, mod_up, mod_b, w_in, mla_q_norm_g, mla_w_uq,
           mla_kv_norm_g, mla_w_ukv, rwkv_mu, rwkv_w0, rwkv_w2, rwkv_a0, rwkv_a2, rwkv_g2, rwkv_k_k,
           rwkv_k_a, rwkv_r_k, rwkv_ln_g, rwkv_ln_b, conv_w, diff_lambda, diff_norm_g, w_branch, gate_down,
           gate_up, gate_b, w_out, mlp_w1, mlp_w2, final_norm_g):
    n_batch, t_len, dm = x.shape
    c_len = ctx.shape[1]
    depth = w_in.shape[0]
    bw = BRANCH_W
    n_x = n_batch * t_len
    n_c = n_batch * c_len
    n_rows = n_x + n_c
    tm = 512 if (t_len % 512 == 0 and n_c % 512 == 0) else 256
    tq = 2048 if t_len % 2048 == 0 else min(256, c_len)
    tr = min(256, c_len)
    assert dm == D_MODEL and t_len % tm == 0 and n_c % tm == 0 and n_x % c_len == 0
    assert t_len % tq == 0 and c_len % RWKV_CHUNK == 0 and t_len % c_len == 0
    tm_big = 1024 if (t_len % 1024 == 0 and n_c % 1024 == 0) else tm
    mod_index = functools.partial(_mod_index, tm=tm, n_x_rows=n_x, t_len=t_len, n_batch=n_batch)
    mod_index_big = functools.partial(_mod_index, tm=tm_big, n_x_rows=n_x, t_len=t_len, n_batch=n_batch)

    zeros = lambda *s: jnp.zeros(s, F32)
    lora_w = 4 * RWKV_LORA + RWKV_GATE_LORA
    rw0 = MLA_Q_LORA + MLA_KV_LORA + MLA_ROPE
    cv0 = rw0 + 3 * bw + lora_w
    w_in_t = jnp.swapaxes(w_in, 1, 2)
    w_in_p = jnp.concatenate(
        [w_in_t[:, :C_R], w_in_t[:, rw0:rw0 + 3 * bw], w_in_t[:, cv0:],
         w_in_t[:, rw0 + 3 * bw:cv0], zeros(depth, RWKV_LORA_PAD - lora_w, dm), jnp.swapaxes(gate_down, 1, 2),
         w_in_t[:, C_R:rw0], zeros(depth, P_COLS - C_KROPE - MLA_ROPE, dm)], axis=1).astype(BF16)
    w_uq_p = jnp.pad(mla_w_uq.reshape(depth, MLA_Q_LORA, MLA_HEADS, MLA_NOPE + MLA_ROPE),
                     ((0, 0), (0, 0), (0, 0), (0, MLA_QK_PAD - MLA_NOPE - MLA_ROPE))
                     ).reshape(depth, MLA_Q_LORA, MLA_HEADS * MLA_QK_PAD).astype(BF16)
    w_ukv_r = mla_w_ukv.reshape(depth, MLA_KV_LORA, MLA_HEADS, MLA_NOPE + MLA_V)
    w_ukv_p = jnp.concatenate([w_ukv_r[..., :MLA_NOPE].reshape(depth, MLA_KV_LORA, -1),
                               w_ukv_r[..., MLA_NOPE:].reshape(depth, MLA_KV_LORA, -1)], axis=-1).astype(BF16)
    w_branch_2d = w_branch.reshape(depth, 4 * bw, dm)
    gate_up_b = jnp.moveaxis(gate_up, 2, 1).astype(BF16)
    g2_p = jnp.pad(rwkv_g2, ((0, 0), (0, RWKV_LORA_PAD - 4 * RWKV_LORA - RWKV_GATE_LORA), (0, 0))).astype(BF16)
    mu_lora = jnp.pad(rwkv_mu[:, :, 3 * bw:], ((0, 0), (0, 0), (0, RWKV_LORA_PAD - lora_w)))

    cond = jnp.concatenate([c, c_ctx[None, :], zeros(16 - n_batch - 1, dm)], axis=0)
    cond = jax.nn.silu(cond)
    mods = []
    for l in range(depth):
        low = _matmul(cond, mod_down[l], tm=16, tn=MOD_RANK, tk=dm, name="mod_down")
        up = _matmul(low, mod_up[l], tm=16, tn=2048, tk=MOD_RANK, bias=mod_b[l], name="mod_up")
        mods.append(up.reshape(16, N_MOD, dm))

    cos_t, sin_t = _rope_tables128(t_len)
    xs = jnp.concatenate([x.reshape(n_x, dm), ctx.reshape(n_c, dm)], axis=0)

    for l in range(depth):
        need_ctx = l < depth - 1
        mod = mods[l]
        lam_init = 0.8 - 0.6 * math.exp(-0.3 * l)
        lq1, lk1, lq2, lk2 = diff_lambda[l]
        lam = jnp.exp(jnp.sum(lq1 * lk1)) - jnp.exp(jnp.sum(lq2 * lk2)) + lam_init
        lam_row = jnp.full((1, DIFF_V), 1.0, F32) * lam

        h1 = _norm_mod(xs, norm1_g[l], mod, shift_row=0, scale_row=1, mod_index=mod_index, tm=tm,
                       rows=n_rows)
        p, w1_b, w_out_b, w_branch_b = _fullk_matmul(h1, w_in_p, layer=l, tm=tm_big, tn=512, out_dtype=F32,
                                                     name="in_proj", cast_srcs=(mlp_w1, w_out, w_branch_2d),
                                                     w_rows_are_outputs=True)

        q = _nm_matmul(p, mla_q_norm_g[l], w_uq_p[l], tm=tm, tn=MLA_HEADS * MLA_QK_PAD, out_dtype=BF16,
                       x_col_block=C_CQ // MLA_Q_LORA, kdim=MLA_Q_LORA, name="mla_q",
                       out_scale=(MLA_NOPE + MLA_ROPE) ** -0.5 * LOG2_E)
        kv = _nm_matmul(p, mla_kv_norm_g[l], w_ukv_p[l], tm=tm, tn=MLA_HEADS * (MLA_NOPE + MLA_V), out_dtype=BF16,
                        x_col_block=C_CKV // MLA_KV_LORA, kdim=MLA_KV_LORA, name="mla_kv")
        y_mla = _mla_attention(q, kv, p, cos_t, sin_t, n_batch=n_batch, t_len=t_len, c_len=c_len, tq=tq,
                               with_ctx=need_ctx)

        y_diff = _diff_attention(p, cos_t, sin_t, lam_row, diff_norm_g[l].reshape(1, DIFF_V),
                                 n_batch=n_batch, t_len=t_len, c_len=c_len, tq=tq, with_ctx=need_ctx,
                                 out_scale=1.0 - lam_init)

        r_, kk, v_, lw, k_dir, kka, bonus, g_out, y_conv = _mixer_prep(
            p, rwkv_mu[l, :, :3 * bw], mu_lora[l], rwkv_k_k[l].reshape(1, bw), rwkv_k_a[l].reshape(1, bw),
            rwkv_r_k[l].reshape(1, bw), rwkv_w0[l].reshape(1, 2 * bw), rwkv_a0[l].reshape(1, 2 * bw),
            _block_diag2(rwkv_w2[l]).astype(BF16), _block_diag2(rwkv_a2[l]).astype(BF16), g2_p[l],
            conv_w[l], tr=min(128, tr), n_x=n_x, t_len=t_len, c_len=c_len)
        o_dirs = _rwkv_scan(r_, kk, v_, lw, k_dir, kka, n_batch=n_batch, t_len=t_len, c_len=c_len)
        y_rwkv = _rwkv_readout(o_dirs, bonus, g_out, rwkv_ln_g[l], rwkv_ln_b[l], tr=tr)

        rows = n_rows if need_ctx else n_x
        acc = _merge(p, (y_mla, y_rwkv, y_conv, y_diff), w_branch_b, gate_up_b, gate_b, layer=l,
                     tm=tm_big, tn=512, rows=rows)
        (xs_new,) = _fullk_matmul(acc, w_out_b[None], layer=0, tm=tm_big, tn=512, out_dtype=F32, res=xs, mod=mod,
                                  gate_row=2, mod_index=mod_index_big, rows=rows, name="out_proj")

        h2 = _norm_mod(xs_new, norm2_g[l], mod, shift_row=3, scale_row=4, mod_index=mod_index, tm=tm,
                       rows=rows)
        hid, w2_b = _fullk_matmul(h2, w1_b[None], layer=0, tm=tm_big, tn=TN_UP, out_dtype=BF16, act="relu2",
                                  name="mlp_up", cast_srcs=(mlp_w2,), cast_layer=l)
        xs = _matmul(hid, w2_b, tm=tm_big, tn=1024, tk=2048, res=xs_new, mod=mod, gate_row=5,
                     mod_index=mod_index_big, rows=rows, name="mlp_down")

    out = _final_norm(xs, final_norm_g, rows=n_x, tm=tm)
    return out.reshape(n_batch, t_len, dm)
```

```python
import functools
import math

import jax
import jax.numpy as jnp
from jax import lax
from jax.experimental import pallas as pl
from jax.experimental.pallas import tpu as pltpu

F32 = jnp.float32
BF16 = jnp.bfloat16

D_MODEL = 4096
BRANCH_W = 1024
GRID_W = 64
ROPE_BASE = 10000.0
NORM_EPS = 1e-6
N_MOD = 6
LANES = 128
HALO = 8
LOG2_E = 1.4426950408889634

MLA_HEADS = 8
MLA_NOPE = 128
MLA_ROPE = 64
MLA_V = 128
MLA_Q_LORA = 768
MLA_KV_LORA = 256
MLA_QK_PAD = 256

RWKV_HEAD = 64
RWKV_HEADS = 16
RWKV_LORA = 64
RWKV_GATE_LORA = 160
RWKV_GN_EPS = 64e-5
RWKV_CHUNK = 64
RWKV_LORA_PAD = 512

DIFF_HEADS = 8
DIFF_QK = 64
DIFF_V = 128
GATE_RANK = 256
MOD_RANK = 256

C_CQ = 0
C_CKV = 768
C_R = 1024
C_K = 2048
C_V = 3072
C_CB = 4096
C_CC = 5120
C_CU = 6144
C_DQ = 7168
C_DK = 8192
C_DV = 9216
C_LORA = 10240
C_GL = 10752
C_KROPE = 11008
P_COLS = 11264

TN_UP = 1024
ATTN_SUB = 128
VMEM_CAP = 56 * 1024 * 1024


def _cparams(sem, vmem_bytes):
    limit = int(min(VMEM_CAP, max(vmem_bytes * 1.5 + (4 << 20), 16 << 20)))
    return pltpu.CompilerParams(dimension_semantics=sem, vmem_limit_bytes=limit)


def _mod_index(i, tm, n_x_rows, t_len, n_batch):
    return jnp.where(i < n_x_rows // tm, i // (t_len // tm), n_batch)


def _dot(a, b):
    return jnp.dot(a.astype(BF16), b.astype(BF16), preferred_element_type=F32)


def _dot_nt(a, b):
    return lax.dot_general(a.astype(BF16), b.astype(BF16), (((1,), (1,)), ((), ())),
                           preferred_element_type=F32)


def _split3(x):
    hi = x.astype(BF16)
    r1 = x - hi.astype(F32)
    mid = r1.astype(BF16)
    lo = (r1 - mid.astype(F32)).astype(BF16)
    return hi, mid, lo


def _dot_exact_lhs(m_bf16, x):
    out = None
    for part in _split3(x):
        t = jnp.dot(m_bf16, part, preferred_element_type=F32)
        out = t if out is None else out + t
    return out


def _head_sums(x):
    r = lax.broadcasted_iota(jnp.int32, (LANES, LANES), 0)
    c = lax.broadcasted_iota(jnp.int32, (LANES, LANES), 1)
    ones_bd = jnp.where((r // RWKV_HEAD) == (c // RWKV_HEAD), 1.0, 0.0).astype(BF16)
    parts = _split3(x)
    cols = []
    for j in range(x.shape[1] // LANES):
        acc = None
        for part in parts:
            t = jnp.dot(part[:, j * LANES:(j + 1) * LANES], ones_bd, preferred_element_type=F32)
            acc = t if acc is None else acc + t
        cols.append(acc)
    return jnp.concatenate(cols, axis=1)


def _rope128(x, cos_t, sin_t):
    lane = lax.broadcasted_iota(jnp.int32, x.shape, 1)
    swapped = jnp.where((lane % 64) < 32, pltpu.roll(x, 96, 1), pltpu.roll(x, 32, 1))
    return x * cos_t + swapped * sin_t


def _matmul_kernel(*refs, nk, has_bias, has_res, gate_row):
    a_ref, w_ref = refs[0], refs[1]
    pos = 2
    bias_ref = res_ref = mod_ref = None
    if has_bias:
        bias_ref = refs[pos]; pos += 1
    if has_res:
        res_ref = refs[pos]; mod_ref = refs[pos + 1]; pos += 2
    o_ref, acc_ref = refs[pos], refs[pos + 1]
    k = pl.program_id(2)

    @pl.when(k == 0)
    def _():
        acc_ref[...] = jnp.zeros_like(acc_ref)

    acc_ref[...] += _dot(a_ref[...], w_ref[...])

    @pl.when(k == nk - 1)
    def _():
        y = acc_ref[...]
        if has_bias:
            y = y + bias_ref[...]
        if has_res:
            y = res_ref[...] + mod_ref[0, gate_row:gate_row + 1, :] * y
        o_ref[...] = y.astype(o_ref.dtype)


def _matmul(a, w, *, tm, tn, tk, out_dtype=F32, bias=None, res=None, mod=None, gate_row=None,
            mod_index=None, rows=None, name=None):
    m = a.shape[0] if rows is None else rows
    kdim, n = w.shape
    assert a.shape[1] == kdim and m % tm == 0 and n % tn == 0 and kdim % tk == 0
    nk = kdim // tk
    in_specs = [pl.BlockSpec((tm, tk), lambda i, j, k: (i, k)),
                pl.BlockSpec((tk, tn), lambda i, j, k: (k, j))]
    args = [a, w]
    vmem = 2 * tm * tk * a.dtype.itemsize + 2 * tk * tn * w.dtype.itemsize + tm * tn * 4
    vmem += 2 * tm * tn * jnp.dtype(out_dtype).itemsize
    if bias is not None:
        in_specs.append(pl.BlockSpec((1, tn), lambda i, j, k: (0, j)))
        args.append(bias.reshape(1, n).astype(F32))
    if res is not None:
        in_specs.append(pl.BlockSpec((tm, tn), lambda i, j, k: (i, j)))
        in_specs.append(pl.BlockSpec((1, N_MOD, tn), lambda i, j, k: (mod_index(i), 0, j)))
        args += [res, mod]
        vmem += 2 * tm * tn * 4 + 2 * 8 * tn * 4
    kern = functools.partial(_matmul_kernel, nk=nk, has_bias=bias is not None,
                             has_res=res is not None, gate_row=gate_row)
    return pl.pallas_call(
        kern,
        grid=(m // tm, n // tn, nk),
        in_specs=in_specs,
        out_specs=pl.BlockSpec((tm, tn), lambda i, j, k: (i, j)),
        out_shape=jax.ShapeDtypeStruct((m, n), out_dtype),
        scratch_shapes=[pltpu.VMEM((tm, tn), F32)],
        compiler_params=_cparams(("parallel", "parallel", "arbitrary"), vmem),
        name=name,
    )(*args)


def _nm_matmul_kernel(x_ref, g_ref, w_ref, o_ref, *, out_scale):
    x = x_ref[...]
    h = x * lax.rsqrt(jnp.mean(x * x, axis=-1, keepdims=True) + NORM_EPS) * g_ref[...]
    y = jnp.dot(h.astype(BF16), w_ref[...], preferred_element_type=F32)
    if out_scale is not None:
        y = y * out_scale
    o_ref[...] = y.astype(o_ref.dtype)


def _nm_matmul(x, g, w, *, tm, x_col_block, out_dtype, name, out_scale=None):
    m = x.shape[0]
    kdim, n = w.shape
    assert m % tm == 0
    vmem = 2 * tm * kdim * 4 + 2 * kdim * n * 2 + 2 * tm * n * jnp.dtype(out_dtype).itemsize + 2 * tm * n * 4
    return pl.pallas_call(
        functools.partial(_nm_matmul_kernel, out_scale=out_scale),
        grid=(m // tm,),
        in_specs=[pl.BlockSpec((tm, kdim), lambda i: (i, x_col_block)),
                  pl.BlockSpec((1, kdim), lambda i: (0, 0)),
                  pl.BlockSpec((kdim, n), lambda i: (0, 0))],
        out_specs=pl.BlockSpec((tm, n), lambda i: (i, 0)),
        out_shape=jax.ShapeDtypeStruct((m, n), out_dtype),
        compiler_params=_cparams(("parallel",), vmem),
        name=name,
    )(x, g.reshape(1, kdim), w)


def _norm_mod_kernel(x_ref, g_ref, mod_ref, h_ref, *, shift_row, scale_row):
    x = x_ref[...]
    y = x * lax.rsqrt(jnp.mean(x * x, axis=-1, keepdims=True) + NORM_EPS) * g_ref[...]
    y = y * (1.0 + mod_ref[0, scale_row:scale_row + 1, :]) + mod_ref[0, shift_row:shift_row + 1, :]
    h_ref[...] = y.astype(h_ref.dtype)


def _norm_mod(xs, g, mod, *, shift_row, scale_row, mod_index, tm, rows):
    dm = xs.shape[1]
    return pl.pallas_call(
        functools.partial(_norm_mod_kernel, shift_row=shift_row, scale_row=scale_row),
        grid=(rows // tm,),
        in_specs=[pl.BlockSpec((tm, dm), lambda i: (i, 0)), pl.BlockSpec((1, dm), lambda i: (0, 0)),
                  pl.BlockSpec((1, N_MOD, dm), lambda i: (mod_index(i), 0, 0))],
        out_specs=pl.BlockSpec((tm, dm), lambda i: (i, 0)),
        out_shape=jax.ShapeDtypeStruct((rows, dm), BF16),
        compiler_params=_cparams(("parallel",), 2 * tm * dm * 6 + 4 * tm * dm * 4),
        name="norm_mod",
    )(xs, g.reshape(1, dm), mod)


def _fullk_kernel(*refs, has_res, n_cast, gate_row, act, w_rows_are_outputs):
    refs = list(refs)
    if n_cast:
        cast_dsts = refs[-n_cast:]
        cast_srcs = refs[-2 * n_cast - 1:-n_cast - 1]
        refs = refs[:-2 * n_cast - 1] + [refs[-n_cast - 1]]
        for src, dst in zip(cast_srcs, cast_dsts):
            dst[...] = src[...].astype(dst.dtype)
    if has_res:
        a_ref, w_ref, res_ref, mod_ref, o_ref = refs
    else:
        a_ref, w_ref, o_ref = refs
    if w_rows_are_outputs:
        y = lax.dot_general(a_ref[...], w_ref[...], (((1,), (1,)), ((), ())), preferred_element_type=F32)
    else:
        y = jnp.dot(a_ref[...], w_ref[...], preferred_element_type=F32)
    if act == "relu2":
        y = jnp.square(jnp.maximum(y, 0.0))
    if has_res:
        y = res_ref[...] + mod_ref[0, gate_row:gate_row + 1, :] * y
    o_ref[...] = y.astype(o_ref.dtype)


def _cast_rows_per_step(n_rows, n_steps):
    rows = 16
    while n_rows % rows or n_rows // rows > n_steps:
        rows *= 2
    return rows


def _fullk_matmul(a, w, *, tm, tn, out_dtype, act=None, res=None, mod=None, gate_row=None, mod_index=None,
                  rows=None, name=None, layer=None, cast_srcs=(), cast_layer=None, w_rows_are_outputs=False):
    m = a.shape[0] if rows is None else rows
    if w_rows_are_outputs:
        _, n, kdim = w.shape
        w_spec = pl.BlockSpec((None, tn, kdim), lambda i, j: (layer, j, 0))
    else:
        _, kdim, n = w.shape
        w_spec = pl.BlockSpec((None, kdim, tn), lambda i, j: (layer, 0, j))
    assert a.shape[1] == kdim and m % tm == 0 and n % tn == 0 and a.dtype == BF16 and w.dtype == BF16
    in_specs = [pl.BlockSpec((tm, kdim), lambda i, j: (i, 0)), w_spec]
    args = [a, w]
    vmem = 2 * tm * kdim * 2 + 2 * kdim * tn * 2 + 2 * tm * tn * jnp.dtype(out_dtype).itemsize + 2 * tm * tn * 4
    if res is not None:
        in_specs += [pl.BlockSpec((tm, tn), lambda i, j: (i, j)),
                     pl.BlockSpec((1, N_MOD, tn), lambda i, j: (mod_index(i), 0, j))]
        args += [res, mod]
        vmem += 2 * tm * tn * 4
    out_specs = [pl.BlockSpec((tm, tn), lambda i, j: (i, j))]
    out_shape = [jax.ShapeDtypeStruct((m, n), out_dtype)]
    nj = n // tn
    cast_layer = layer if cast_layer is None else cast_layer
    for cast_src in cast_srcs:
        _, c_rows, c_cols = cast_src.shape
        cr = _cast_rows_per_step(c_rows, (m // tm) * nj)
        cast_blk = lambda i, j, last=c_rows // cr - 1: jnp.minimum(i * nj + j, last)
        in_specs.append(pl.BlockSpec((None, cr, c_cols), lambda i, j, blk=cast_blk: (cast_layer, blk(i, j), 0)))
        args.append(cast_src)
        out_specs.append(pl.BlockSpec((cr, c_cols), lambda i, j, blk=cast_blk: (blk(i, j), 0)))
        out_shape.append(jax.ShapeDtypeStruct((c_rows, c_cols), BF16))
        vmem += 2 * cr * c_cols * 6
    return pl.pallas_call(
        functools.partial(_fullk_kernel, has_res=res is not None, n_cast=len(cast_srcs),
                          gate_row=gate_row, act=act, w_rows_are_outputs=w_rows_are_outputs),
        grid=(m // tm, n // tn),
        in_specs=in_specs,
        out_specs=out_specs,
        out_shape=out_shape,
        compiler_params=_cparams(("arbitrary", "arbitrary"), vmem),
        name=name,
    )(*args)


def _merge_kernel(gl_ref, y0_ref, y1_ref, y2_ref, y3_ref, wb_ref, gu_ref, gb_ref, o_ref):
    gl = gl_ref[...].astype(BF16)
    bw = wb_ref.shape[0] // 4
    acc = None
    for i, y_ref in enumerate((y0_ref, y1_ref, y2_ref, y3_ref)):
        gate = jax.nn.sigmoid(jnp.dot(gl, gu_ref[i], preferred_element_type=F32) + gb_ref[i:i + 1, :])
        term = gate * jnp.dot(y_ref[...], wb_ref[i * bw:(i + 1) * bw, :], preferred_element_type=F32)
        acc = term if acc is None else acc + term
    o_ref[...] = acc.astype(o_ref.dtype)


def _merge(p, ys, wb, gu, gb, *, layer, tm, tn, rows):
    n = wb.shape[1]
    bw = wb.shape[0] // 4
    gr = gu.shape[2]
    y_spec = pl.BlockSpec((tm, bw), lambda i, j: (i, 0))
    vmem = (2 * tm * gr * 4 + 4 * 2 * tm * bw * 2 + 2 * 4 * bw * tn * 2 + 2 * 4 * gr * tn * 2
            + 2 * tm * tn * 2 + 3 * tm * tn * 4)
    return pl.pallas_call(
        _merge_kernel,
        grid=(rows // tm, n // tn),
        in_specs=[pl.BlockSpec((tm, gr), lambda i, j: (i, C_GL // GATE_RANK)),
                  y_spec, y_spec, y_spec, y_spec,
                  pl.BlockSpec((4 * bw, tn), lambda i, j: (0, j)),
                  pl.BlockSpec((None, 4, gr, tn), lambda i, j: (layer, 0, 0, j)),
                  pl.BlockSpec((None, 4, tn), lambda i, j: (layer, 0, j))],
        out_specs=pl.BlockSpec((tm, tn), lambda i, j: (i, j)),
        out_shape=jax.ShapeDtypeStruct((rows, n), BF16),
        compiler_params=_cparams(("parallel", "arbitrary"), vmem),
        name="merge",
    )(p, *ys, wb, gu, gb)


def _softmax_parts(s_list):
    m = None
    for s in s_list:
        sm = jnp.max(s, axis=-1, keepdims=True)
        m = sm if m is None else jnp.maximum(m, sm)
    e_list = [jnp.exp2(s - m) for s in s_list]
    l = None
    for e in e_list:
        es = jnp.sum(e, axis=-1, keepdims=True)
        l = es if l is None else l + es
    return e_list, l


def _attend(q, ks, vs):
    e_list, l = _softmax_parts([_dot_nt(q, k) for k in ks])
    o = None
    for e, v in zip(e_list, vs):
        t = jnp.dot(e.astype(BF16), v, preferred_element_type=F32)
        o = t if o is None else o + t
    return o / l


def _attend_many(qs, ks, vs, sub):
    parts = [q[r:r + sub] for q in qs for r in range(0, q.shape[0], sub)]
    n = len(parts)
    scores, soft, outs = {}, {}, []
    for t in range(n + 2):
        if t < n:
            scores[t] = [_dot_nt(parts[t], k) for k in ks]
        if 0 <= t - 1 < n:
            soft[t - 1] = _softmax_parts(scores.pop(t - 1))
        if 0 <= t - 2 < n:
            e_list, l = soft.pop(t - 2)
            o = None
            for e, v in zip(e_list, vs):
                pv = jnp.dot(e.astype(BF16), v, preferred_element_type=F32)
                o = pv if o is None else o + pv
            outs.append(o / l)
    per_q = len(parts) // len(qs)
    return [jnp.concatenate(outs[i * per_q:(i + 1) * per_q], axis=0) for i in range(len(qs))]


def _mla_kernel(cos_ref, sin_ref, q_ref, knx_ref, knc_ref, krx_ref, krc_ref, vx_ref, vc_ref, o_ref,
                kx_s, kc_s, *, tq):
    i = pl.program_id(2)

    @pl.when(i == 0)
    def _():
        kx_s[:, :MLA_NOPE] = knx_ref[...]
        kx_s[:, MLA_NOPE:] = _rope128(krx_ref[...], cos_ref[...], sin_ref[...]).astype(BF16)
        kc_s[:, :MLA_NOPE] = knc_ref[...]
        kc_s[:, MLA_NOPE:] = krc_ref[...].astype(BF16)

    row0 = pl.multiple_of(i * tq, tq)
    q = q_ref[...]
    q_rope = _rope128(q[:, MLA_NOPE:].astype(F32), cos_ref[pl.ds(row0, tq), :], sin_ref[pl.ds(row0, tq), :])
    q = jnp.concatenate([q[:, :MLA_NOPE], q_rope.astype(BF16)], axis=1)
    (o,) = _attend_many((q,), (kc_s[...], kx_s[...]), (vc_ref[...], vx_ref[...]), ATTN_SUB)
    o_ref[...] = o.astype(o_ref.dtype)


def _mla_ctx_kernel(q_ref, knc_ref, krc_ref, vc_ref, y_hbm, o_ref):
    del y_hbm
    kc = jnp.concatenate([knc_ref[...], krc_ref[...].astype(BF16)], axis=1)
    o = _attend(q_ref[...], (kc,), (vc_ref[...],))
    o_ref[...] = o.astype(o_ref.dtype)


def _diff_finish(o1, o2, lam_ref, g_ref, out_scale):
    o = o1 - lam_ref[...] * o2
    y = o * lax.rsqrt(jnp.mean(o * o, axis=-1, keepdims=True) + NORM_EPS) * g_ref[...]
    return y * out_scale


def _diff_halves(q):
    lane = lax.broadcasted_iota(jnp.int32, q.shape, 1)
    return jnp.where(lane < DIFF_QK, q, 0.0).astype(BF16), jnp.where(lane < DIFF_QK, 0.0, q).astype(BF16)


def _diff_kernel(cos_ref, sin_ref, lam_ref, g_ref, q_ref, kx_ref, kc_ref, vx_ref, vc_ref, o_ref,
                 kx_s, kc_s, vx_s, vc_s, *, tq, scale, out_scale):
    i = pl.program_id(2)

    @pl.when(i == 0)
    def _():
        kx_s[...] = _rope128(kx_ref[...], cos_ref[...], sin_ref[...]).astype(BF16)
        kc_s[...] = kc_ref[...].astype(BF16)
        vx_s[...] = vx_ref[...].astype(BF16)
        vc_s[...] = vc_ref[...].astype(BF16)

    row0 = pl.multiple_of(i * tq, tq)
    q = q_ref[...] * (scale * LOG2_E)
    q1, q2 = _diff_halves(_rope128(q, cos_ref[pl.ds(row0, tq), :], sin_ref[pl.ds(row0, tq), :]))
    ks, vs = (kc_s[...], kx_s[...]), (vc_s[...], vx_s[...])
    o1, o2 = _attend_many((q1, q2), ks, vs, ATTN_SUB)
    y = _diff_finish(o1, o2, lam_ref, g_ref, out_scale)
    o_ref[...] = y.astype(o_ref.dtype)


def _diff_ctx_kernel(lam_ref, g_ref, q_ref, kc_ref, vc_ref, y_hbm, o_ref, *, scale, out_scale):
    del y_hbm
    q1, q2 = _diff_halves(q_ref[...] * (scale * LOG2_E))
    ks, vs = (kc_ref[...].astype(BF16),), (vc_ref[...].astype(BF16),)
    y = _diff_finish(_attend(q1, ks, vs), _attend(q2, ks, vs), lam_ref, g_ref, out_scale)
    o_ref[...] = y.astype(o_ref.dtype)


def _mla_attention(q, kv, p, cos_t, sin_t, *, n_batch, t_len, c_len, tq, with_ctx):
    n_x = n_batch * t_len
    nq = t_len // tq
    out_rows = n_x + (n_batch * c_len if with_ctx else 0)
    cblk0 = n_x // c_len
    qmap = lambda b, h, i: (b * nq + i, h)
    full = lambda b, h, i: (0, 0)
    vmem = (4 * t_len * LANES * 4 + 2 * tq * 256 * 2 + 2 * (t_len + c_len) * LANES * (2 + 4 + 2)
            + (t_len + c_len) * 256 * 2 + 2 * tq * LANES * 2 + 6 * tq * (t_len + c_len) * 4)
    y = pl.pallas_call(
        functools.partial(_mla_kernel, tq=tq),
        grid=(n_batch, MLA_HEADS, nq),
        in_specs=[pl.BlockSpec((t_len, LANES), full), pl.BlockSpec((t_len, LANES), full),
                  pl.BlockSpec((tq, MLA_QK_PAD), qmap),
                  pl.BlockSpec((t_len, MLA_NOPE), lambda b, h, i: (b, h)),
                  pl.BlockSpec((c_len, MLA_NOPE), lambda b, h, i: (cblk0 + b, h)),
                  pl.BlockSpec((t_len, LANES), lambda b, h, i: (b, C_KROPE // LANES)),
                  pl.BlockSpec((c_len, LANES), lambda b, h, i: (cblk0 + b, C_KROPE // LANES)),
                  pl.BlockSpec((t_len, MLA_V), lambda b, h, i: (b, MLA_HEADS + h)),
                  pl.BlockSpec((c_len, MLA_V), lambda b, h, i: (cblk0 + b, MLA_HEADS + h))],
        out_specs=pl.BlockSpec((tq, MLA_V), qmap),
        out_shape=jax.ShapeDtypeStruct((out_rows, MLA_HEADS * MLA_V), BF16),
        scratch_shapes=[pltpu.VMEM((t_len, MLA_QK_PAD), BF16), pltpu.VMEM((c_len, MLA_QK_PAD), BF16)],
        compiler_params=_cparams(("parallel", "parallel", "arbitrary"), vmem),
        name="mla_attention",
    )(cos_t, sin_t, q, kv, kv, p, p, kv, kv)
    if not with_ctx:
        return y
    cmap = lambda b, h: (cblk0 + b, h)
    return pl.pallas_call(
        _mla_ctx_kernel,
        grid=(n_batch, MLA_HEADS),
        in_specs=[pl.BlockSpec((c_len, MLA_QK_PAD), cmap),
                  pl.BlockSpec((c_len, MLA_NOPE), cmap),
                  pl.BlockSpec((c_len, LANES), lambda b, h: (cblk0 + b, C_KROPE // LANES)),
                  pl.BlockSpec((c_len, MLA_V), lambda b, h: (cblk0 + b, MLA_HEADS + h)),
                  pl.BlockSpec(memory_space=pl.ANY)],
        out_specs=pl.BlockSpec((c_len, MLA_V), cmap),
        out_shape=jax.ShapeDtypeStruct(y.shape, y.dtype),
        input_output_aliases={4: 0},
        compiler_params=_cparams(("parallel", "parallel"), 16 * c_len * c_len * 4 + 8 * c_len * 256 * 4),
        name="mla_attention_ctx",
    )(q, kv, p, kv, y)


def _diff_attention(p, cos_t, sin_t, lam_row, g_row, *, n_batch, t_len, c_len, tq, with_ctx, out_scale):
    n_x = n_batch * t_len
    nq = t_len // tq
    out_rows = n_x + (n_batch * c_len if with_ctx else 0)
    cblk0 = n_x // c_len
    scale = DIFF_QK ** -0.5
    full = lambda b, h, i: (0, 0)
    qblk, kblk, vblk = C_DQ // LANES, C_DK // LANES, C_DV // LANES
    vmem = (4 * t_len * LANES * 4 + 2 * tq * LANES * 4 + 4 * (t_len + c_len) * LANES * 4
            + 2 * (t_len + c_len) * LANES * 2 + 2 * tq * LANES * 2 + 8 * tq * (t_len + c_len) * 4)
    y = pl.pallas_call(
        functools.partial(_diff_kernel, tq=tq, scale=scale, out_scale=out_scale),
        grid=(n_batch, DIFF_HEADS, nq),
        in_specs=[pl.BlockSpec((t_len, LANES), full), pl.BlockSpec((t_len, LANES), full),
                  pl.BlockSpec((1, DIFF_V), full), pl.BlockSpec((1, DIFF_V), full),
                  pl.BlockSpec((tq, LANES), lambda b, h, i: (b * nq + i, qblk + h)),
                  pl.BlockSpec((t_len, LANES), lambda b, h, i: (b, kblk + h)),
                  pl.BlockSpec((c_len, LANES), lambda b, h, i: (cblk0 + b, kblk + h)),
                  pl.BlockSpec((t_len, LANES), lambda b, h, i: (b, vblk + h)),
                  pl.BlockSpec((c_len, LANES), lambda b, h, i: (cblk0 + b, vblk + h))],
        out_specs=pl.BlockSpec((tq, DIFF_V), lambda b, h, i: (b * nq + i, h)),
        out_shape=jax.ShapeDtypeStruct((out_rows, DIFF_HEADS * DIFF_V), BF16),
        scratch_shapes=[pltpu.VMEM((t_len, LANES), BF16), pltpu.VMEM((c_len, LANES), BF16),
                        pltpu.VMEM((t_len, LANES), BF16), pltpu.VMEM((c_len, LANES), BF16)],
        compiler_params=_cparams(("parallel", "parallel", "arbitrary"), vmem),
        name="diff_attention",
    )(cos_t, sin_t, lam_row, g_row, p, p, p, p, p)
    if not with_ctx:
        return y
    one = lambda b, h: (0, 0)
    return pl.pallas_call(
        functools.partial(_diff_ctx_kernel, scale=scale, out_scale=out_scale),
        grid=(n_batch, DIFF_HEADS),
        in_specs=[pl.BlockSpec((1, DIFF_V), one), pl.BlockSpec((1, DIFF_V), one),
                  pl.BlockSpec((c_len, LANES), lambda b, h: (cblk0 + b, qblk + h)),
                  pl.BlockSpec((c_len, LANES), lambda b, h: (cblk0 + b, kblk + h)),
                  pl.BlockSpec((c_len, LANES), lambda b, h: (cblk0 + b, vblk + h)),
                  pl.BlockSpec(memory_space=pl.ANY)],
        out_specs=pl.BlockSpec((c_len, DIFF_V), lambda b, h: (cblk0 + b, h)),
        out_shape=jax.ShapeDtypeStruct(y.shape, y.dtype),
        input_output_aliases={5: 0},
        compiler_params=_cparams(("parallel", "parallel"), 24 * c_len * c_len * 4 + 8 * c_len * LANES * 4),
        name="diff_attention_ctx",
    )(lam_row, g_row, p, p, p, y)


def _prep_kernel(r_ref, k_ref, v_ref, cb_ref, cc_ref, cu_ref, lo_ref, hp_ref, hn_ref,
                 mu_ref, mul_ref, kk_w_ref, ka_w_ref, rk_w_ref, w0_ref, a0_ref, w2_ref, a2_ref, g2_ref,
                 cw_ref,
                 r_o, kk_o, v_o, lw_o, kd_o, ka_o, bonus_o, g_o, conv_o, *, tr, n_x, t_len, c_len):
    i = pl.program_id(0)
    g0 = i * tr
    seq = jnp.where(g0 < n_x, t_len, c_len)
    has_prev = ((g0 % seq) != 0).astype(F32)
    has_next = (((g0 + tr) % seq) != 0).astype(F32)
    row = lax.broadcasted_iota(jnp.int32, (tr, 1), 0)

    def neighbours(x, col0):
        width = x.shape[1]
        before = hp_ref[HALO - 1:HALO, col0:col0 + width] * has_prev
        after = hn_ref[0:1, col0:col0 + width] * has_next
        prev = jnp.where(row == 0, before, pltpu.roll(x, 1, 0))
        nxt = jnp.where(row == tr - 1, after, pltpu.roll(x, tr - 1, 0))
        return prev, nxt

    def shifted(x, col0, mu0, mu1):
        prev, nxt = neighbours(x, col0)
        return x + mu0 * (prev - x) + mu1 * (nxt - x)

    r = shifted(r_ref[...], C_R, mu_ref[0:1, :BRANCH_W], mu_ref[1:2, :BRANCH_W])
    k = shifted(k_ref[...], C_K, mu_ref[0:1, BRANCH_W:2 * BRANCH_W], mu_ref[1:2, BRANCH_W:2 * BRANCH_W])
    v = shifted(v_ref[...], C_V, mu_ref[0:1, 2 * BRANCH_W:], mu_ref[1:2, 2 * BRANCH_W:])
    lo = shifted(lo_ref[...], C_LORA, mul_ref[0:1, :], mul_ref[1:2, :])
    wd = jnp.tanh(lo[:, :2 * RWKV_LORA])
    ad = lo[:, 2 * RWKV_LORA:4 * RWKV_LORA]
    gd = jax.nn.sigmoid(lo[:, 4 * RWKV_LORA:])
    w_pre = _dot(wd, w2_ref[...]) + w0_ref[...]
    a_sig = jax.nn.sigmoid(_dot(ad, a2_ref[...]) + a0_ref[...])
    g_o[...] = _dot(gd, g2_ref[...])
    w_log = -(jnp.maximum(-w_pre, 0.0) + jnp.log(1.0 + jnp.exp(-jnp.abs(w_pre)))) - 0.5
    lw = -jnp.exp(w_log)
    kkf = k * kk_w_ref[...]
    kk = kkf * lax.rsqrt(_head_sums(kkf * kkf) + 1e-12)
    r_o[...] = r
    kk_o[...] = kk
    v_o[...] = v
    k_sum = None
    for d in range(2):
        a_d = a_sig[:, d * BRANCH_W:(d + 1) * BRANCH_W]
        k_d = k * (1.0 + (a_d - 1.0) * ka_w_ref[...])
        lw_o[d] = lw[:, d * BRANCH_W:(d + 1) * BRANCH_W]
        kd_o[d] = k_d
        ka_o[d] = kk * a_d
        k_sum = k_d if k_sum is None else k_sum + k_d
    bonus_o[...] = _head_sums(r * k_sum * rk_w_ref[...]) * v

    z = cc_ref[...] * cu_ref[...]
    z_before = hp_ref[HALO - 1:HALO, C_CC:C_CC + BRANCH_W] * hp_ref[HALO - 1:HALO, C_CU:C_CU + BRANCH_W] * has_prev
    z_after = hn_ref[0:1, C_CC:C_CC + BRANCH_W] * hn_ref[0:1, C_CU:C_CU + BRANCH_W] * has_next
    z_prev = jnp.where(row == 0, z_before, pltpu.roll(z, 1, 0))
    z_next = jnp.where(row == tr - 1, z_after, pltpu.roll(z, tr - 1, 0))
    y = cb_ref[...] * (cw_ref[0:1, :] * z_prev + cw_ref[1:2, :] * z + cw_ref[2:3, :] * z_next)
    conv_o[...] = y.astype(conv_o.dtype)


def _mixer_prep(p, mu_rkv, mu_lora, kk_w, ka_w, rk_w, w0, a0, w2bd, a2bd, g2p, conv_w, *, tr, n_x, t_len,
                c_len):
    n_rows = p.shape[0]
    bw = BRANCH_W
    last_halo = n_rows // HALO - 1
    col = lambda c: (lambda i: (i, c))
    const = lambda i: (0, 0)
    main = [pl.BlockSpec((tr, bw), col(C_R // bw)), pl.BlockSpec((tr, bw), col(C_K // bw)),
            pl.BlockSpec((tr, bw), col(C_V // bw)), pl.BlockSpec((tr, bw), col(C_CB // bw)),
            pl.BlockSpec((tr, bw), col(C_CC // bw)), pl.BlockSpec((tr, bw), col(C_CU // bw)),
            pl.BlockSpec((tr, RWKV_LORA_PAD), col(C_LORA // RWKV_LORA_PAD)),
            pl.BlockSpec((HALO, P_COLS), lambda i: (jnp.maximum(i * (tr // HALO) - 1, 0), 0)),
            pl.BlockSpec((HALO, P_COLS), lambda i: (jnp.minimum((i + 1) * (tr // HALO), last_halo), 0))]
    params = [mu_rkv, mu_lora, kk_w, ka_w, rk_w, w0, a0, w2bd, a2bd, g2p, conv_w]
    param_specs = [pl.BlockSpec(a.shape, const) for a in params]
    row_spec = pl.BlockSpec((tr, bw), lambda i: (i, 0))
    dir_spec = pl.BlockSpec((2, tr, bw), lambda i: (0, i, 0))
    f32_rows = jax.ShapeDtypeStruct((n_rows, bw), F32)
    f32_dirs = jax.ShapeDtypeStruct((2, n_rows, bw), F32)
    vmem = 2 * (7 * tr * bw * 4 + 2 * HALO * P_COLS * 4 + 12 * tr * bw * 4) + 30 * tr * bw * 4
    return pl.pallas_call(
        functools.partial(_prep_kernel, tr=tr, n_x=n_x, t_len=t_len, c_len=c_len),
        grid=(n_rows // tr,),
        in_specs=main + param_specs,
        out_specs=[row_spec, row_spec, row_spec, dir_spec, dir_spec, dir_spec, row_spec, row_spec, row_spec],
        out_shape=[f32_rows, f32_rows, f32_rows, f32_dirs, f32_dirs, f32_dirs, f32_rows, f32_rows,
                   jax.ShapeDtypeStruct((n_rows, bw), BF16)],
        compiler_params=_cparams(("parallel",), vmem),
        name="mixer_prep",
    )(p, p, p, p, p, p, p, p, p, *params)


def _rwkv_kernel(r0_ref, kk0_ref, v0_ref, r1_ref, kk1_ref, v1_ref, lw0_ref, kd0_ref, ka0_ref,
                 lw1_ref, kd1_ref, ka1_ref, o0_ref, o1_ref, s_ref):
    C = RWKV_CHUNK
    W = 2 * RWKV_HEAD
    n_pairs = RWKV_HEADS // 2

    @pl.when(pl.program_id(1) == 0)
    def _():
        s_ref[...] = jnp.zeros_like(s_ref)

    mm, mm_nt = _dot, _dot_nt
    lane = lax.broadcasted_iota(jnp.int32, (C, W), 1)
    row = lax.broadcasted_iota(jnp.int32, (C, W), 0)
    first = lane < RWKV_HEAD
    rc = lax.broadcasted_iota(jnp.int32, (C, C), 0)
    cc = lax.broadcasted_iota(jnp.int32, (C, C), 1)

    def bdiag(x):
        return jnp.concatenate([jnp.where(first, x, jnp.zeros_like(x)), jnp.where(first, jnp.zeros_like(x), x)],
                               axis=0)

    def direction(sign, r_ref, kk_ref, v_ref, lw_ref, kd_ref, ka_ref):
        m_incl = jnp.where((rc - cc) * sign >= 0, 1.0, 0.0).astype(BF16)
        lw = lw_ref[0]
        cum = _dot_exact_lhs(m_incl, lw)
        tot = jnp.sum(lw, axis=0, keepdims=True)
        e_ninc = jnp.exp(-cum)
        e_rem = jnp.exp(tot - cum)
        kk, ka, kd = kk_ref[...], ka_ref[0], kd_ref[0]
        order = (row - (lane % RWKV_HEAD)) * sign
        return dict(a_t=kk * jnp.exp(cum - lw), b_t=-ka * e_ninc, k_t=kd * e_ninc, r_t=r_ref[...] * jnp.exp(cum),
                    b_h=-ka * e_rem, k_h=kd * e_rem, v=v_ref[...], e_tot=jnp.exp(tot), strict=order > 0,
                    incl=order >= 0, eye=jnp.where(order == 0, 1.0, 0.0).astype(F32))

    dirs = (direction(1, r0_ref, kk0_ref, v0_ref, lw0_ref, kd0_ref, ka0_ref),
            direction(-1, r1_ref, kk1_ref, v1_ref, lw1_ref, kd1_ref, ka1_ref))
    o_refs = (o0_ref, o1_ref)

    chains = [(d, p) for d in range(2) for p in range(n_pairs)]
    n = range(len(chains))
    sl = [slice(W * p, W * (p + 1)) for (_, p) in chains]
    dd = [dirs[d] for (d, _) in chains]
    ar = [jnp.concatenate([dd[c]["a_t"][:, sl[c]], dd[c]["r_t"][:, sl[c]]], axis=0).astype(BF16) for c in n]
    a_bd = [bdiag(dd[c]["a_t"][:, sl[c]].astype(BF16)) for c in n]
    bk_bd = [jnp.concatenate([bdiag(dd[c]["b_t"][:, sl[c]].astype(BF16)), bdiag(dd[c]["k_t"][:, sl[c]].astype(BF16))],
                             axis=0) for c in n]
    v_bd = [bdiag(dd[c]["v"][:, sl[c]].astype(BF16)) for c in n]
    g = [mm_nt(ar[c], bk_bd[c]) for c in n]
    l_pow = [jnp.where(dd[c]["strict"], g[c][:C, :W], 0.0) for c in n]
    m_ak = [jnp.where(dd[c]["strict"], g[c][:C, W:], 0.0).astype(BF16) for c in n]
    a_r = [jnp.where(jnp.concatenate([dd[c]["incl"], dd[c]["incl"]], axis=1), g[c][C:], 0.0).astype(BF16) for c in n]
    mv = [mm(m_ak[c], v_bd[c]) for c in n]
    l_bd = [bdiag(l_pow[c].astype(BF16)) for c in n]
    t_inv = [dd[c]["eye"] + l_pow[c] for c in n]
    l_pow = [mm(l_pow[c], l_bd[c]) for c in n]
    for _ in range(int(math.log2(C)) - 2):
        l_bd = [bdiag(l_pow[c].astype(BF16)) for c in n]
        lt = [mm(jnp.concatenate([l_pow[c], t_inv[c]], axis=0), l_bd[c]) for c in n]
        l_pow = [lt[c][:C] for c in n]
        t_inv = [t_inv[c] + lt[c][C:] for c in n]
    t_inv = [t_inv[c] + mm(t_inv[c], bdiag(l_pow[c].astype(BF16))) for c in n]
    wu = [mm(t_inv[c], jnp.concatenate([a_bd[c], bdiag(mv[c].astype(BF16))], axis=1)) for c in n]
    s_bd = [s_ref[d, p] for (d, p) in chains]
    wr = [mm(jnp.concatenate([wu[c][:, :W], dd[c]["r_t"][:, sl[c]]], axis=0), s_bd[c]) for c in n]
    zv = [jnp.concatenate([bdiag((wr[c][:C] + wu[c][:, W:]).astype(BF16)), v_bd[c]], axis=0) for c in n]
    for c, (d, p) in enumerate(chains):
        o_refs[d][:, sl[c]] = wr[c][C:] + mm(a_r[c], zv[c])
    for c, (d, p) in enumerate(chains):
        bk_h = jnp.concatenate([bdiag(dd[c]["b_h"][:, sl[c]]), bdiag(dd[c]["k_h"][:, sl[c]])], axis=0)
        tot_col = jnp.broadcast_to(dd[c]["e_tot"][:, sl[c]], (W, W)).T
        s_ref[d, p] = tot_col * s_bd[c] + mm(bk_h.T, zv[c])


def _rwkv_scan(r, kk, v, lw, kd, ka, *, n_batch, t_len, c_len):
    C = RWKV_CHUNK
    rows, width = r.shape
    nct, nxt = c_len // C, t_len // C
    ctx_blk0 = n_batch * t_len // C

    def blk(b, d, s):
        j_c = s if d == 0 else nct - 1 - s
        j_x = s - nct if d == 0 else nxt - 1 - (s - nct)
        return jnp.where(s < nct, ctx_blk0 + b * nct + j_c, b * nxt + j_x)

    def shared(d):
        return pl.BlockSpec((C, width), lambda b, s: (blk(b, d, s), 0))

    def per_dir(d):
        return pl.BlockSpec((1, C, width), lambda b, s: (d, blk(b, d, s), 0))

    vmem = 2 * 14 * C * width * 4 + RWKV_HEADS * 128 * 128 * 4 + 128 * C * width * 4
    out = jax.ShapeDtypeStruct((rows, width), F32)
    return pl.pallas_call(
        _rwkv_kernel,
        grid=(n_batch, nct + nxt),
        in_specs=[shared(0), shared(0), shared(0), shared(1), shared(1), shared(1),
                  per_dir(0), per_dir(0), per_dir(0), per_dir(1), per_dir(1), per_dir(1)],
        out_specs=[shared(0), shared(1)],
        out_shape=[out, out],
        scratch_shapes=[pltpu.VMEM((2, RWKV_HEADS // 2, 2 * RWKV_HEAD, 2 * RWKV_HEAD), F32)],
        compiler_params=_cparams(("parallel", "arbitrary"), vmem),
        name="rwkv_scan",
    )(r, kk, v, r, kk, v, lw, kd, ka, lw, kd, ka)


def _rwkv_readout_kernel(o0_ref, o1_ref, bonus_ref, g_ref, lng_ref, lnb_ref, y_ref):
    o = o0_ref[...] + o1_ref[...]
    mean = _head_sums(o) * (1.0 / RWKV_HEAD)
    cen = o - mean
    var = _head_sums(cen * cen) * (1.0 / RWKV_HEAD)
    o_n = cen * lax.rsqrt(var + RWKV_GN_EPS) * lng_ref[...] + lnb_ref[...]
    y_ref[...] = ((o_n + bonus_ref[...]) * g_ref[...]).astype(y_ref.dtype)


def _rwkv_readout(o_dirs, bonus, g_out, ln_g, ln_b, *, tr):
    n_rows, bw = bonus.shape
    row_spec = pl.BlockSpec((tr, bw), lambda i: (i, 0))
    const = pl.BlockSpec((1, bw), lambda i: (0, 0))
    return pl.pallas_call(
        _rwkv_readout_kernel,
        grid=(n_rows // tr,),
        in_specs=[row_spec, row_spec, row_spec, row_spec, const, const],
        out_specs=row_spec,
        out_shape=jax.ShapeDtypeStruct((n_rows, bw), BF16),
        compiler_params=_cparams(("parallel",), 2 * 5 * tr * bw * 4 + 16 * tr * bw * 4),
        name="rwkv_readout",
    )(o_dirs[0], o_dirs[1], bonus, g_out, ln_g.reshape(1, bw), ln_b.reshape(1, bw))


def _rmsnorm_kernel(x_ref, g_ref, o_ref):
    x = x_ref[...]
    o_ref[...] = x * lax.rsqrt(jnp.mean(x * x, axis=-1, keepdims=True) + NORM_EPS) * g_ref[...]


def _final_norm(xs, g, *, rows, tm):
    dm = xs.shape[1]
    return pl.pallas_call(
        _rmsnorm_kernel,
        grid=(rows // tm,),
        in_specs=[pl.BlockSpec((tm, dm), lambda i: (i, 0)), pl.BlockSpec((1, dm), lambda i: (0, 0))],
        out_specs=pl.BlockSpec((tm, dm), lambda i: (i, 0)),
        out_shape=jax.ShapeDtypeStruct((rows, dm), F32),
        compiler_params=_cparams(("parallel",), 4 * tm * dm * 4),
        name="final_norm",
    )(xs, g.reshape(1, dm))


def _rope_tables128(n_tokens):
    rows = n_tokens // GRID_W
    row = jnp.repeat(jnp.arange(rows, dtype=F32), GRID_W)
    col = jnp.tile(jnp.arange(GRID_W, dtype=F32), rows)
    n_freq = 64 // 4
    inv = ROPE_BASE ** (-jnp.arange(n_freq, dtype=F32) / n_freq)
    ang = jnp.concatenate([row[:, None] * inv, col[:, None] * inv], axis=-1)
    cos, sin = jnp.cos(ang), jnp.sin(ang)
    return jnp.concatenate([cos, cos, cos, cos], axis=-1), jnp.concatenate([-sin, sin, -sin, sin], axis=-1)


def _block_diag2(w2):
    z = jnp.zeros_like(w2[0])
    return jnp.concatenate([jnp.concatenate([w2[0], z], axis=1), jnp.concatenate([z, w2[1]], axis=1)], axis=0)


def kernel(x, c, ctx, c_ctx, norm1_g, norm2_g, mod_down, mod_up, mod_b, w_in, mla_q_norm_g, mla_w_uq,
           mla_kv_norm_g, mla_w_ukv, rwkv_mu, rwkv_w0, rwkv_w2, rwkv_a0, rwkv_a2, rwkv_g2, rwkv_k_k,
           rwkv_k_a, rwkv_r_k, rwkv_ln_g, rwkv_ln_b, conv_w, diff_lambda, diff_norm_g, w_branch, gate_down,
           gate_up, gate_b, w_out, mlp_w1, mlp_w2, final_norm_g):
    n_batch, t_len, dm = x.shape
    c_len = ctx.shape[1]
    depth = w_in.shape[0]
    bw = BRANCH_W
    n_x = n_batch * t_len
    n_c = n_batch * c_len
    n_rows = n_x + n_c
    tm = 512 if (t_len % 512 == 0 and n_c % 512 == 0) else 256
    tq = 2048 if t_len % 2048 == 0 else min(256, c_len)
    tr = min(256, c_len)
    assert dm == D_MODEL and t_len % tm == 0 and n_c % tm == 0 and n_x % c_len == 0
    assert t_len % tq == 0 and c_len % RWKV_CHUNK == 0 and t_len % c_len == 0
    tm_big = 1024 if (t_len % 1024 == 0 and n_c % 1024 == 0) else tm
    mod_index = functools.partial(_mod_index, tm=tm, n_x_rows=n_x, t_len=t_len, n_batch=n_batch)
    mod_index_big = functools.partial(_mod_index, tm=tm_big, n_x_rows=n_x, t_len=t_len, n_batch=n_batch)

    zeros = lambda *s: jnp.zeros(s, F32)
    lora_w = 4 * RWKV_LORA + RWKV_GATE_LORA
    rw0 = MLA_Q_LORA + MLA_KV_LORA + MLA_ROPE
    cv0 = rw0 + 3 * bw + lora_w
    w_in_t = jnp.swapaxes(w_in, 1, 2)
    w_in_p = jnp.concatenate(
        [w_in_t[:, :C_R], w_in_t[:, rw0:rw0 + 3 * bw], w_in_t[:, cv0:],
         w_in_t[:, rw0 + 3 * bw:cv0], zeros(depth, RWKV_LORA_PAD - lora_w, dm), jnp.swapaxes(gate_down, 1, 2),
         w_in_t[:, C_R:rw0], zeros(depth, P_COLS - C_KROPE - MLA_ROPE, dm)], axis=1).astype(BF16)
    w_uq_p = jnp.pad(mla_w_uq.reshape(depth, MLA_Q_LORA, MLA_HEADS, MLA_NOPE + MLA_ROPE),
                     ((0, 0), (0, 0), (0, 0), (0, MLA_QK_PAD - MLA_NOPE - MLA_ROPE))
                     ).reshape(depth, MLA_Q_LORA, MLA_HEADS * MLA_QK_PAD).astype(BF16)
    w_ukv_r = mla_w_ukv.reshape(depth, MLA_KV_LORA, MLA_HEADS, MLA_NOPE + MLA_V)
    w_ukv_p = jnp.concatenate([w_ukv_r[..., :MLA_NOPE].reshape(depth, MLA_KV_LORA, -1),
                               w_ukv_r[..., MLA_NOPE:].reshape(depth, MLA_KV_LORA, -1)], axis=-1).astype(BF16)
    w_branch_2d = w_branch.reshape(depth, 4 * bw, dm)
    gate_up_b = jnp.moveaxis(gate_up, 2, 1).astype(BF16)
    g2_p = jnp.pad(rwkv_g2, ((0, 0), (0, RWKV_LORA_PAD - 4 * RWKV_LORA - RWKV_GATE_LORA), (0, 0))).astype(BF16)
    mu_lora = jnp.pad(rwkv_mu[:, :, 3 * bw:], ((0, 0), (0, 0), (0, RWKV_LORA_PAD - lora_w)))

    cond = jnp.concatenate([c, c_ctx[None, :], zeros(16 - n_batch - 1, dm)], axis=0)
    cond = jax.nn.silu(cond)
    mods = []
    for l in range(depth):
        low = _matmul(cond, mod_down[l], tm=16, tn=MOD_RANK, tk=dm, name="mod_down")
        up = _matmul(low, mod_up[l], tm=16, tn=2048, tk=MOD_RANK, bias=mod_b[l], name="mod_up")
        mods.append(up.reshape(16, N_MOD, dm))

    cos_t, sin_t = _rope_tables128(t_len)
    xs = jnp.concatenate([x.reshape(n_x, dm), ctx.reshape(n_c, dm)], axis=0)

    for l in range(depth):
        need_ctx = l < depth - 1
        mod = mods[l]
        lam_init = 0.8 - 0.6 * math.exp(-0.3 * l)
        lq1, lk1, lq2, lk2 = diff_lambda[l]
        lam = jnp.exp(jnp.sum(lq1 * lk1)) - jnp.exp(jnp.sum(lq2 * lk2)) + lam_init
        lam_row = jnp.full((1, DIFF_V), 1.0, F32) * lam

        h1 = _norm_mod(xs, norm1_g[l], mod, shift_row=0, scale_row=1, mod_index=mod_index, tm=tm,
                       rows=n_rows)
        p, w1_b, w_out_b, w_branch_b = _fullk_matmul(h1, w_in_p, layer=l, tm=tm_big, tn=512, out_dtype=F32,
                                                     name="in_proj", cast_srcs=(mlp_w1, w_out, w_branch_2d),
                                                     w_rows_are_outputs=True)

        q = _nm_matmul(p, mla_q_norm_g[l], w_uq_p[l], tm=tm, x_col_block=C_CQ // MLA_Q_LORA, out_dtype=BF16,
                       name="mla_q",
                       out_scale=(MLA_NOPE + MLA_ROPE) ** -0.5 * LOG2_E)
        kv = _nm_matmul(p, mla_kv_norm_g[l], w_ukv_p[l], tm=tm, x_col_block=C_CKV // MLA_KV_LORA, out_dtype=BF16,
                        name="mla_kv")
        y_mla = _mla_attention(q, kv, p, cos_t, sin_t, n_batch=n_batch, t_len=t_len, c_len=c_len, tq=tq,
                               with_ctx=need_ctx)

        y_diff = _diff_attention(p, cos_t, sin_t, lam_row, diff_norm_g[l].reshape(1, DIFF_V),
                                 n_batch=n_batch, t_len=t_len, c_len=c_len, tq=tq, with_ctx=need_ctx,
                                 out_scale=1.0 - lam_init)

        r_, kk, v_, lw, k_dir, kka, bonus, g_out, y_conv = _mixer_prep(
            p, rwkv_mu[l, :, :3 * bw], mu_lora[l], rwkv_k_k[l].reshape(1, bw), rwkv_k_a[l].reshape(1, bw),
            rwkv_r_k[l].reshape(1, bw), rwkv_w0[l].reshape(1, 2 * bw), rwkv_a0[l].reshape(1, 2 * bw),
            _block_diag2(rwkv_w2[l]).astype(BF16), _block_diag2(rwkv_a2[l]).astype(BF16), g2_p[l],
            conv_w[l], tr=min(128, tr), n_x=n_x, t_len=t_len, c_len=c_len)
        o_dirs = _rwkv_scan(r_, kk, v_, lw, k_dir, kka, n_batch=n_batch, t_len=t_len, c_len=c_len)
        y_rwkv = _rwkv_readout(o_dirs, bonus, g_out, rwkv_ln_g[l], rwkv_ln_b[l], tr=tr)

        rows = n_rows if need_ctx else n_x
        acc = _merge(p, (y_mla, y_rwkv, y_conv, y_diff), w_branch_b, gate_up_b, gate_b, layer=l,
                     tm=tm_big, tn=512, rows=rows)
        (xs_new,) = _fullk_matmul(acc, w_out_b[None], layer=0, tm=tm_big, tn=512, out_dtype=F32, res=xs, mod=mod,
                                  gate_row=2, mod_index=mod_index_big, rows=rows, name="out_proj")

        h2 = _norm_mod(xs_new, norm2_g[l], mod, shift_row=3, scale_row=4, mod_index=mod_index, tm=tm,
                       rows=rows)
        hid, w2_b = _fullk_matmul(h2, w1_b[None], layer=0, tm=tm_big, tn=TN_UP, out_dtype=BF16, act="relu2",
                                  name="mlp_up", cast_srcs=(mlp_w2,), cast_layer=l)
        xs = _matmul(hid, w2_b, tm=tm_big, tn=1024, tk=2048, res=xs_new, mod=mod, gate_row=5,
                     mod_index=mod_index_big, rows=rows, name="mlp_down")

    out = _final_norm(xs, final_norm_g, rows=n_x, tm=tm)
    return out.reshape(n_batch, t_len, dm)
```

```python
import functools
import math

import jax
import jax.numpy as jnp
from jax import lax
from jax.experimental import pallas as pl
from jax.experimental.pallas import tpu as pltpu

F32 = jnp.float32
BF16 = jnp.bfloat16

D_MODEL = 4096
BRANCH_W = 1024
GRID_W = 64
ROPE_BASE = 10000.0
NORM_EPS = 1e-6
N_MOD = 6
LANES = 128
HALO = 8
LOG2_E = 1.4426950408889634

MLA_HEADS = 8
MLA_NOPE = 128
MLA_ROPE = 64
MLA_V = 128
MLA_Q_LORA = 768
MLA_KV_LORA = 256
MLA_QK_PAD = 256

RWKV_HEAD = 64
RWKV_HEADS = 16
RWKV_LORA = 64
RWKV_GATE_LORA = 160
RWKV_GN_EPS = 64e-5
RWKV_CHUNK = 64
RWKV_LORA_PAD = 512

DIFF_HEADS = 8
DIFF_QK = 64
DIFF_V = 128
GATE_RANK = 256
MOD_RANK = 256

C_CQ = 0
C_CKV = 768
C_R = 1024
C_K = 2048
C_V = 3072
C_CB = 4096
C_CC = 5120
C_CU = 6144
C_DQ = 7168
C_DK = 8192
C_DV = 9216
C_LORA = 10240
C_GL = 10752
C_KROPE = 11008
P_COLS = 11264

TN_UP = 1024
ATTN_SUB = 128
VMEM_CAP = 56 * 1024 * 1024


def _cparams(sem, vmem_bytes):
    limit = int(min(VMEM_CAP, max(vmem_bytes * 1.5 + (4 << 20), 16 << 20)))
    return pltpu.CompilerParams(dimension_semantics=sem, vmem_limit_bytes=limit)


def _mod_index(i, tm, n_x_rows, t_len, n_batch):
    return jnp.where(i < n_x_rows // tm, i // (t_len // tm), n_batch)


def _dot(a, b):
    return jnp.dot(a.astype(BF16), b.astype(BF16), preferred_element_type=F32)


def _dot_nt(a, b):
    return lax.dot_general(a.astype(BF16), b.astype(BF16), (((1,), (1,)), ((), ())),
                           preferred_element_type=F32)


def _split3(x):
    hi = x.astype(BF16)
    r1 = x - hi.astype(F32)
    mid = r1.astype(BF16)
    lo = (r1 - mid.astype(F32)).astype(BF16)
    return hi, mid, lo


def _dot_exact_lhs(m_bf16, x):
    out = None
    for part in _split3(x):
        t = jnp.dot(m_bf16, part, preferred_element_type=F32)
        out = t if out is None else out + t
    return out


def _head_sums(x):
    r = lax.broadcasted_iota(jnp.int32, (LANES, LANES), 0)
    c = lax.broadcasted_iota(jnp.int32, (LANES, LANES), 1)
    ones_bd = jnp.where((r // RWKV_HEAD) == (c // RWKV_HEAD), 1.0, 0.0).astype(BF16)
    parts = _split3(x)
    cols = []
    for j in range(x.shape[1] // LANES):
        acc = None
        for part in parts:
            t = jnp.dot(part[:, j * LANES:(j + 1) * LANES], ones_bd, preferred_element_type=F32)
            acc = t if acc is None else acc + t
        cols.append(acc)
    return jnp.concatenate(cols, axis=1)


def _rope128(x, cos_t, sin_t):
    lane = lax.broadcasted_iota(jnp.int32, x.shape, 1)
    swapped = jnp.where((lane % 64) < 32, pltpu.roll(x, 96, 1), pltpu.roll(x, 32, 1))
    return x * cos_t + swapped * sin_t


def _matmul_kernel(*refs, nk, has_bias, has_res, gate_row):
    a_ref, w_ref = refs[0], refs[1]
    pos = 2
    bias_ref = res_ref = mod_ref = None
    if has_bias:
        bias_ref = refs[pos]; pos += 1
    if has_res:
        res_ref = refs[pos]; mod_ref = refs[pos + 1]; pos += 2
    o_ref, acc_ref = refs[pos], refs[pos + 1]
    k = pl.program_id(2)

    @pl.when(k == 0)
    def _():
        acc_ref[...] = jnp.zeros_like(acc_ref)

    acc_ref[...] += _dot(a_ref[...], w_ref[...])

    @pl.when(k == nk - 1)
    def _():
        y = acc_ref[...]
        if has_bias:
            y = y + bias_ref[...]
        if has_res:
            y = res_ref[...] + mod_ref[0, gate_row:gate_row + 1, :] * y
        o_ref[...] = y.astype(o_ref.dtype)


def _matmul(a, w, *, tm, tn, tk, out_dtype=F32, bias=None, res=None, mod=None, gate_row=None,
            mod_index=None, rows=None, name=None):
    m = a.shape[0] if rows is None else rows
    kdim, n = w.shape
    assert a.shape[1] == kdim and m % tm == 0 and n % tn == 0 and kdim % tk == 0
    nk = kdim // tk
    in_specs = [pl.BlockSpec((tm, tk), lambda i, j, k: (i, k)),
                pl.BlockSpec((tk, tn), lambda i, j, k: (k, j))]
    args = [a, w]
    vmem = 2 * tm * tk * a.dtype.itemsize + 2 * tk * tn * w.dtype.itemsize + tm * tn * 4
    vmem += 2 * tm * tn * jnp.dtype(out_dtype).itemsize
    if bias is not None:
        in_specs.append(pl.BlockSpec((1, tn), lambda i, j, k: (0, j)))
        args.append(bias.reshape(1, n).astype(F32))
    if res is not None:
        in_specs.append(pl.BlockSpec((tm, tn), lambda i, j, k: (i, j)))
        in_specs.append(pl.BlockSpec((1, N_MOD, tn), lambda i, j, k: (mod_index(i), 0, j)))
        args += [res, mod]
        vmem += 2 * tm * tn * 4 + 2 * 8 * tn * 4
    kern = functools.partial(_matmul_kernel, nk=nk, has_bias=bias is not None,
                             has_res=res is not None, gate_row=gate_row)
    return pl.pallas_call(
        kern,
        grid=(m // tm, n // tn, nk),
        in_specs=in_specs,
        out_specs=pl.BlockSpec((tm, tn), lambda i, j, k: (i, j)),
        out_shape=jax.ShapeDtypeStruct((m, n), out_dtype),
        scratch_shapes=[pltpu.VMEM((tm, tn), F32)],
        compiler_params=_cparams(("parallel", "parallel", "arbitrary"), vmem),
        name=name,
    )(*args)


def _nm_matmul_kernel(x_ref, g_ref, w_ref, o_ref, *, out_scale):
    x = x_ref[...]
    h = x * lax.rsqrt(jnp.mean(x * x, axis=-1, keepdims=True) + NORM_EPS) * g_ref[...]
    y = jnp.dot(h.astype(BF16), w_ref[...], preferred_element_type=F32)
    if out_scale is not None:
        y = y * out_scale
    o_ref[...] = y.astype(o_ref.dtype)


def _nm_matmul(x, g, w, *, tm, x_col_block, out_dtype, name, out_scale=None):
    m = x.shape[0]
    kdim, n = w.shape
    assert m % tm == 0
    vmem = 2 * tm * kdim * 4 + 2 * kdim * n * 2 + 2 * tm * n * jnp.dtype(out_dtype).itemsize + 2 * tm * n * 4
    return pl.pallas_call(
        functools.partial(_nm_matmul_kernel, out_scale=out_scale),
        grid=(m // tm,),
        in_specs=[pl.BlockSpec((tm, kdim), lambda i: (i, x_col_block)),
                  pl.BlockSpec((1, kdim), lambda i: (0, 0)),
                  pl.BlockSpec((kdim, n), lambda i: (0, 0))],
        out_specs=pl.BlockSpec((tm, n), lambda i: (i, 0)),
        out_shape=jax.ShapeDtypeStruct((m, n), out_dtype),
        compiler_params=_cparams(("parallel",), vmem),
        name=name,
    )(x, g.reshape(1, kdim), w)


def _norm_mod_kernel(x_ref, g_ref, mod_ref, h_ref, *, shift_row, scale_row):
    x = x_ref[...]
    y = x * lax.rsqrt(jnp.mean(x * x, axis=-1, keepdims=True) + NORM_EPS) * g_ref[...]
    y = y * (1.0 + mod_ref[0, scale_row:scale_row + 1, :]) + mod_ref[0, shift_row:shift_row + 1, :]
    h_ref[...] = y.astype(h_ref.dtype)


def _norm_mod(xs, g, mod, *, shift_row, scale_row, mod_index, tm, rows):
    dm = xs.shape[1]
    return pl.pallas_call(
        functools.partial(_norm_mod_kernel, shift_row=shift_row, scale_row=scale_row),
        grid=(rows // tm,),
        in_specs=[pl.BlockSpec((tm, dm), lambda i: (i, 0)), pl.BlockSpec((1, dm), lambda i: (0, 0)),
                  pl.BlockSpec((1, N_MOD, dm), lambda i: (mod_index(i), 0, 0))],
        out_specs=pl.BlockSpec((tm, dm), lambda i: (i, 0)),
        out_shape=jax.ShapeDtypeStruct((rows, dm), BF16),
        compiler_params=_cparams(("parallel",), 2 * tm * dm * 6 + 4 * tm * dm * 4),
        name="norm_mod",
    )(xs, g.reshape(1, dm), mod)


def _fullk_kernel(*refs, has_res, n_cast, gate_row, act, w_rows_are_outputs):
    refs = list(refs)
    if n_cast:
        cast_dsts = refs[-n_cast:]
        cast_srcs = refs[-2 * n_cast - 1:-n_cast - 1]
        refs = refs[:-2 * n_cast - 1] + [refs[-n_cast - 1]]
        for src, dst in zip(cast_srcs, cast_dsts):
            dst[...] = src[...].astype(dst.dtype)
    if has_res:
        a_ref, w_ref, res_ref, mod_ref, o_ref = refs
    else:
        a_ref, w_ref, o_ref = refs
    if w_rows_are_outputs:
        y = lax.dot_general(a_ref[...], w_ref[...], (((1,), (1,)), ((), ())), preferred_element_type=F32)
    else:
        y = jnp.dot(a_ref[...], w_ref[...], preferred_element_type=F32)
    if act == "relu2":
        y = jnp.square(jnp.maximum(y, 0.0))
    if has_res:
        y = res_ref[...] + mod_ref[0, gate_row:gate_row + 1, :] * y
    o_ref[...] = y.astype(o_ref.dtype)


def _cast_rows_per_step(n_rows, n_steps):
    rows = 16
    while n_rows % rows or n_rows // rows > n_steps:
        rows *= 2
    return rows


def _fullk_matmul(a, w, *, tm, tn, out_dtype, act=None, res=None, mod=None, gate_row=None, mod_index=None,
                  rows=None, name=None, layer=None, cast_srcs=(), cast_layer=None, w_rows_are_outputs=False):
    m = a.shape[0] if rows is None else rows
    if w_rows_are_outputs:
        _, n, kdim = w.shape
        w_spec = pl.BlockSpec((None, tn, kdim), lambda i, j: (layer, j, 0))
    else:
        _, kdim, n = w.shape
        w_spec = pl.BlockSpec((None, kdim, tn), lambda i, j: (layer, 0, j))
    assert a.shape[1] == kdim and m % tm == 0 and n % tn == 0 and a.dtype == BF16 and w.dtype == BF16
    in_specs = [pl.BlockSpec((tm, kdim), lambda i, j: (i, 0)), w_spec]
    args = [a, w]
    vmem = 2 * tm * kdim * 2 + 2 * kdim * tn * 2 + 2 * tm * tn * jnp.dtype(out_dtype).itemsize + 2 * tm * tn * 4
    if res is not None:
        in_specs += [pl.BlockSpec((tm, tn), lambda i, j: (i, j)),
                     pl.BlockSpec((1, N_MOD, tn), lambda i, j: (mod_index(i), 0, j))]
        args += [res, mod]
        vmem += 2 * tm * tn * 4
    out_specs = [pl.BlockSpec((tm, tn), lambda i, j: (i, j))]
    out_shape = [jax.ShapeDtypeStruct((m, n), out_dtype)]
    nj = n // tn
    cast_layer = layer if cast_layer is None else cast_layer
    for cast_src in cast_srcs:
        _, c_rows, c_cols = cast_src.shape
        cr = _cast_rows_per_step(c_rows, (m // tm) * nj)
        cast_blk = lambda i, j, last=c_rows // cr - 1: jnp.minimum(i * nj + j, last)
        in_specs.append(pl.BlockSpec((None, cr, c_cols), lambda i, j, blk=cast_blk: (cast_layer, blk(i, j), 0)))
        args.append(cast_src)
        out_specs.append(pl.BlockSpec((cr, c_cols), lambda i, j, blk=cast_blk: (blk(i, j), 0)))
        out_shape.append(jax.ShapeDtypeStruct((c_rows, c_cols), BF16))
        vmem += 2 * cr * c_cols * 6
    return pl.pallas_call(
        functools.partial(_fullk_kernel, has_res=res is not None, n_cast=len(cast_srcs),
                          gate_row=gate_row, act=act, w_rows_are_outputs=w_rows_are_outputs),
        grid=(m // tm, n // tn),
        in_specs=in_specs,
        out_specs=out_specs,
        out_shape=out_shape,
        compiler_params=_cparams(("arbitrary", "arbitrary"), vmem),
        name=name,
    )(*args)


def _merge_kernel(gl_ref, y0_ref, y1_ref, y2_ref, y3_ref, wb_ref, gu_ref, gb_ref, o_ref):
    gl = gl_ref[...].astype(BF16)
    bw = wb_ref.shape[0] // 4
    acc = None
    for i, y_ref in enumerate((y0_ref, y1_ref, y2_ref, y3_ref)):
        gate = jax.nn.sigmoid(jnp.dot(gl, gu_ref[i], preferred_element_type=F32) + gb_ref[i:i + 1, :])
        term = gate * jnp.dot(y_ref[...], wb_ref[i * bw:(i + 1) * bw, :], preferred_element_type=F32)
        acc = term if acc is None else acc + term
    o_ref[...] = acc.astype(o_ref.dtype)


def _merge(p, ys, wb, gu, gb, *, layer, tm, tn, rows):
    n = wb.shape[1]
    bw = wb.shape[0] // 4
    gr = gu.shape[2]
    y_spec = pl.BlockSpec((tm, bw), lambda i, j: (i, 0))
    vmem = (2 * tm * gr * 4 + 4 * 2 * tm * bw * 2 + 2 * 4 * bw * tn * 2 + 2 * 4 * gr * tn * 2
            + 2 * tm * tn * 2 + 3 * tm * tn * 4)
    return pl.pallas_call(
        _merge_kernel,
        grid=(rows // tm, n // tn),
        in_specs=[pl.BlockSpec((tm, gr), lambda i, j: (i, C_GL // GATE_RANK)),
                  y_spec, y_spec, y_spec, y_spec,
                  pl.BlockSpec((4 * bw, tn), lambda i, j: (0, j)),
                  pl.BlockSpec((None, 4, gr, tn), lambda i, j: (layer, 0, 0, j)),
                  pl.BlockSpec((None, 4, tn), lambda i, j: (layer, 0, j))],
        out_specs=pl.BlockSpec((tm, tn), lambda i, j: (i, j)),
        out_shape=jax.ShapeDtypeStruct((rows, n), BF16),
        compiler_params=_cparams(("parallel", "arbitrary"), vmem),
        name="merge",
    )(p, *ys, wb, gu, gb)


def _softmax_parts(s_list):
    m = None
    for s in s_list:
        sm = jnp.max(s, axis=-1, keepdims=True)
        m = sm if m is None else jnp.maximum(m, sm)
    e_list = [jnp.exp2(s - m) for s in s_list]
    l = None
    for e in e_list:
        es = jnp.sum(e, axis=-1, keepdims=True)
        l = es if l is None else l + es
    return e_list, l


def _attend(q, ks, vs):
    e_list, l = _softmax_parts([_dot_nt(q, k) for k in ks])
    o = None
    for e, v in zip(e_list, vs):
        t = jnp.dot(e.astype(BF16), v, preferred_element_type=F32)
        o = t if o is None else o + t
    return o / l


def _attend_many(qs, ks, vs, sub):
    parts = [q[r:r + sub] for q in qs for r in range(0, q.shape[0], sub)]
    n = len(parts)
    scores, soft, outs = {}, {}, []
    for t in range(n + 2):
        if t < n:
            scores[t] = [_dot_nt(parts[t], k) for k in ks]
        if 0 <= t - 1 < n:
            soft[t - 1] = _softmax_parts(scores.pop(t - 1))
        if 0 <= t - 2 < n:
            e_list, l = soft.pop(t - 2)
            o = None
            for e, v in zip(e_list, vs):
                pv = jnp.dot(e.astype(BF16), v, preferred_element_type=F32)
                o = pv if o is None else o + pv
            outs.append(o / l)
    per_q = len(parts) // len(qs)
    return [jnp.concatenate(outs[i * per_q:(i + 1) * per_q], axis=0) for i in range(len(qs))]


def _mla_kernel(cos_ref, sin_ref, q_ref, knx_ref, knc_ref, krx_ref, krc_ref, vx_ref, vc_ref, y_init, o_ref,
                kx_s, kc_s, *, tq):
    del y_init
    i = pl.program_id(2)

    @pl.when(i == 0)
    def _():
        kx_s[:, :MLA_NOPE] = knx_ref[...]
        kx_s[:, MLA_NOPE:] = _rope128(krx_ref[...], cos_ref[...], sin_ref[...]).astype(BF16)
        kc_s[:, :MLA_NOPE] = knc_ref[...]
        kc_s[:, MLA_NOPE:] = krc_ref[...].astype(BF16)

    row0 = pl.multiple_of(i * tq, tq)
    q = q_ref[...]
    q_rope = _rope128(q[:, MLA_NOPE:].astype(F32), cos_ref[pl.ds(row0, tq), :], sin_ref[pl.ds(row0, tq), :])
    q = jnp.concatenate([q[:, :MLA_NOPE], q_rope.astype(BF16)], axis=1)
    (o,) = _attend_many((q,), (kc_s[...], kx_s[...]), (vc_ref[...], vx_ref[...]), ATTN_SUB)
    o_ref[...] = o.astype(o_ref.dtype)


def _mla_ctx_kernel(q_ref, knc_ref, krc_ref, vc_ref, y_hbm, o_ref):
    del y_hbm
    kc = jnp.concatenate([knc_ref[...], krc_ref[...].astype(BF16)], axis=1)
    o = _attend(q_ref[...], (kc,), (vc_ref[...],))
    o_ref[...] = o.astype(o_ref.dtype)


def _diff_finish(o1, o2, lam_ref, g_ref, out_scale):
    o = o1 - lam_ref[...] * o2
    y = o * lax.rsqrt(jnp.mean(o * o, axis=-1, keepdims=True) + NORM_EPS) * g_ref[...]
    return y * out_scale


def _diff_halves(q):
    lane = lax.broadcasted_iota(jnp.int32, q.shape, 1)
    return jnp.where(lane < DIFF_QK, q, 0.0).astype(BF16), jnp.where(lane < DIFF_QK, 0.0, q).astype(BF16)


def _diff_kernel(cos_ref, sin_ref, lam_ref, g_ref, q_ref, kx_ref, kc_ref, vx_ref, vc_ref, y_init, o_ref,
                 kx_s, kc_s, vx_s, vc_s, *, tq, scale, out_scale):
    del y_init
    i = pl.program_id(2)

    @pl.when(i == 0)
    def _():
        kx_s[...] = _rope128(kx_ref[...], cos_ref[...], sin_ref[...]).astype(BF16)
        kc_s[...] = kc_ref[...].astype(BF16)
        vx_s[...] = vx_ref[...].astype(BF16)
        vc_s[...] = vc_ref[...].astype(BF16)

    row0 = pl.multiple_of(i * tq, tq)
    q = q_ref[...] * (scale * LOG2_E)
    q1, q2 = _diff_halves(_rope128(q, cos_ref[pl.ds(row0, tq), :], sin_ref[pl.ds(row0, tq), :]))
    ks, vs = (kc_s[...], kx_s[...]), (vc_s[...], vx_s[...])
    o1, o2 = _attend_many((q1, q2), ks, vs, ATTN_SUB)
    y = _diff_finish(o1, o2, lam_ref, g_ref, out_scale)
    o_ref[...] = y.astype(o_ref.dtype)


def _diff_ctx_kernel(lam_ref, g_ref, q_ref, kc_ref, vc_ref, y_hbm, o_ref, *, scale, out_scale):
    del y_hbm
    q1, q2 = _diff_halves(q_ref[...] * (scale * LOG2_E))
    ks, vs = (kc_ref[...].astype(BF16),), (vc_ref[...].astype(BF16),)
    y = _diff_finish(_attend(q1, ks, vs), _attend(q2, ks, vs), lam_ref, g_ref, out_scale)
    o_ref[...] = y.astype(o_ref.dtype)


def _mla_attention(q, kv, p, cos_t, sin_t, *, n_batch, t_len, c_len, tq, with_ctx):
    n_x = n_batch * t_len
    nq = t_len // tq
    out_rows = n_x + (n_batch * c_len if with_ctx else 0)
    cblk0 = n_x // c_len
    qmap = lambda b, h, i: (b * nq + i, h)
    full = lambda b, h, i: (0, 0)
    vmem = (4 * t_len * LANES * 4 + 2 * tq * 256 * 2 + 2 * (t_len + c_len) * LANES * (2 + 4 + 2)
            + (t_len + c_len) * 256 * 2 + 2 * tq * LANES * 2 + 6 * tq * (t_len + c_len) * 4)
    y = pl.pallas_call(
        functools.partial(_mla_kernel, tq=tq),
        grid=(n_batch, MLA_HEADS, nq),
        in_specs=[pl.BlockSpec((t_len, LANES), full), pl.BlockSpec((t_len, LANES), full),
                  pl.BlockSpec((tq, MLA_QK_PAD), qmap),
                  pl.BlockSpec((t_len, MLA_NOPE), lambda b, h, i: (b, h)),
                  pl.BlockSpec((c_len, MLA_NOPE), lambda b, h, i: (cblk0 + b, h)),
                  pl.BlockSpec((t_len, LANES), lambda b, h, i: (b, C_KROPE // LANES)),
                  pl.BlockSpec((c_len, LANES), lambda b, h, i: (cblk0 + b, C_KROPE // LANES)),
                  pl.BlockSpec((t_len, MLA_V), lambda b, h, i: (b, MLA_HEADS + h)),
                  pl.BlockSpec((c_len, MLA_V), lambda b, h, i: (cblk0 + b, MLA_HEADS + h)),
                  pl.BlockSpec(memory_space=pl.ANY)],
        out_specs=pl.BlockSpec((tq, MLA_V), qmap),
        out_shape=jax.ShapeDtypeStruct((out_rows, MLA_HEADS * MLA_V), BF16),
        input_output_aliases={9: 0},
        scratch_shapes=[pltpu.VMEM((t_len, MLA_QK_PAD), BF16), pltpu.VMEM((c_len, MLA_QK_PAD), BF16)],
        compiler_params=_cparams(("parallel", "parallel", "arbitrary"), vmem),
        name="mla_attention",
    )(cos_t, sin_t, q, kv, kv, p, p, kv, kv, jnp.zeros((out_rows, MLA_HEADS * MLA_V), BF16))
    if not with_ctx:
        return y
    cmap = lambda b, h: (cblk0 + b, h)
    return pl.pallas_call(
        _mla_ctx_kernel,
        grid=(n_batch, MLA_HEADS),
        in_specs=[pl.BlockSpec((c_len, MLA_QK_PAD), cmap),
                  pl.BlockSpec((c_len, MLA_NOPE), cmap),
                  pl.BlockSpec((c_len, LANES), lambda b, h: (cblk0 + b, C_KROPE // LANES)),
                  pl.BlockSpec((c_len, MLA_V), lambda b, h: (cblk0 + b, MLA_HEADS + h)),
                  pl.BlockSpec(memory_space=pl.ANY)],
        out_specs=pl.BlockSpec((c_len, MLA_V), cmap),
        out_shape=jax.ShapeDtypeStruct(y.shape, y.dtype),
        input_output_aliases={4: 0},
        compiler_params=_cparams(("parallel", "parallel"), 16 * c_len * c_len * 4 + 8 * c_len * 256 * 4),
        name="mla_attention_ctx",
    )(q, kv, p, kv, y)


def _diff_attention(p, cos_t, sin_t, lam_row, g_row, *, n_batch, t_len, c_len, tq, with_ctx, out_scale):
    n_x = n_batch * t_len
    nq = t_len // tq
    out_rows = n_x + (n_batch * c_len if with_ctx else 0)
    cblk0 = n_x // c_len
    scale = DIFF_QK ** -0.5
    full = lambda b, h, i: (0, 0)
    qblk, kblk, vblk = C_DQ // LANES, C_DK // LANES, C_DV // LANES
    vmem = (4 * t_len * LANES * 4 + 2 * tq * LANES * 4 + 4 * (t_len + c_len) * LANES * 4
            + 2 * (t_len + c_len) * LANES * 2 + 2 * tq * LANES * 2 + 8 * tq * (t_len + c_len) * 4)
    y = pl.pallas_call(
        functools.partial(_diff_kernel, tq=tq, scale=scale, out_scale=out_scale),
        grid=(n_batch, DIFF_HEADS, nq),
        in_specs=[pl.BlockSpec((t_len, LANES), full), pl.BlockSpec((t_len, LANES), full),
                  pl.BlockSpec((1, DIFF_V), full), pl.BlockSpec((1, DIFF_V), full),
                  pl.BlockSpec((tq, LANES), lambda b, h, i: (b * nq + i, qblk + h)),
                  pl.BlockSpec((t_len, LANES), lambda b, h, i: (b, kblk + h)),
                  pl.BlockSpec((c_len, LANES), lambda b, h, i: (cblk0 + b, kblk + h)),
                  pl.BlockSpec((t_len, LANES), lambda b, h, i: (b, vblk + h)),
                  pl.BlockSpec((c_len, LANES), lambda b, h, i: (cblk0 + b, vblk + h)),
                  pl.BlockSpec(memory_space=pl.ANY)],
        out_specs=pl.BlockSpec((tq, DIFF_V), lambda b, h, i: (b * nq + i, h)),
        out_shape=jax.ShapeDtypeStruct((out_rows, DIFF_HEADS * DIFF_V), BF16),
        input_output_aliases={9: 0},
        scratch_shapes=[pltpu.VMEM((t_len, LANES), BF16), pltpu.VMEM((c_len, LANES), BF16),
                        pltpu.VMEM((t_len, LANES), BF16), pltpu.VMEM((c_len, LANES), BF16)],
        compiler_params=_cparams(("parallel", "parallel", "arbitrary"), vmem),
        name="diff_attention",
    )(cos_t, sin_t, lam_row, g_row, p, p, p, p, p, jnp.zeros((out_rows, DIFF_HEADS * DIFF_V), BF16))
    if not with_ctx:
        return y
    one = lambda b, h: (0, 0)
    return pl.pallas_call(
        functools.partial(_diff_ctx_kernel, scale=scale, out_scale=out_scale),
        grid=(n_batch, DIFF_HEADS),
        in_specs=[pl.BlockSpec((1, DIFF_V), one), pl.BlockSpec((1, DIFF_V), one),
                  pl.BlockSpec((c_len, LANES), lambda b, h: (cblk0 + b, qblk + h)),
                  pl.BlockSpec((c_len, LANES), lambda b, h: (cblk0 + b, kblk + h)),
                  pl.BlockSpec((c_len, LANES), lambda b, h: (cblk0 + b, vblk + h)),
                  pl.BlockSpec(memory_space=pl.ANY)],
        out_specs=pl.BlockSpec((c_len, DIFF_V), lambda b, h: (cblk0 + b, h)),
        out_shape=jax.ShapeDtypeStruct(y.shape, y.dtype),
        input_output_aliases={5: 0},
        compiler_params=_cparams(("parallel", "parallel"), 24 * c_len * c_len * 4 + 8 * c_len * LANES * 4),
        name="diff_attention_ctx",
    )(lam_row, g_row, p, p, p, y)


def _prep_kernel(r_ref, k_ref, v_ref, cb_ref, cc_ref, cu_ref, lo_ref, hp_ref, hn_ref,
                 mu_ref, mul_ref, kk_w_ref, ka_w_ref, rk_w_ref, w0_ref, a0_ref, w2_ref, a2_ref, g2_ref,
                 cw_ref,
                 r_o, kk_o, v_o, lw_o, kd_o, ka_o, bonus_o, g_o, conv_o, *, tr, n_x, t_len, c_len):
    i = pl.program_id(0)
    g0 = i * tr
    seq = jnp.where(g0 < n_x, t_len, c_len)
    has_prev = ((g0 % seq) != 0).astype(F32)
    has_next = (((g0 + tr) % seq) != 0).astype(F32)
    row = lax.broadcasted_iota(jnp.int32, (tr, 1), 0)

    def neighbours(x, col0):
        width = x.shape[1]
        before = hp_ref[HALO - 1:HALO, col0:col0 + width] * has_prev
        after = hn_ref[0:1, col0:col0 + width] * has_next
        prev = jnp.where(row == 0, before, pltpu.roll(x, 1, 0))
        nxt = jnp.where(row == tr - 1, after, pltpu.roll(x, tr - 1, 0))
        return prev, nxt

    def shifted(x, col0, mu0, mu1):
        prev, nxt = neighbours(x, col0)
        return x + mu0 * (prev - x) + mu1 * (nxt - x)

    r = shifted(r_ref[...], C_R, mu_ref[0:1, :BRANCH_W], mu_ref[1:2, :BRANCH_W])
    k = shifted(k_ref[...], C_K, mu_ref[0:1, BRANCH_W:2 * BRANCH_W], mu_ref[1:2, BRANCH_W:2 * BRANCH_W])
    v = shifted(v_ref[...], C_V, mu_ref[0:1, 2 * BRANCH_W:], mu_ref[1:2, 2 * BRANCH_W:])
    lo = shifted(lo_ref[...], C_LORA, mul_ref[0:1, :], mul_ref[1:2, :])
    wd = jnp.tanh(lo[:, :2 * RWKV_LORA])
    ad = lo[:, 2 * RWKV_LORA:4 * RWKV_LORA]
    gd = jax.nn.sigmoid(lo[:, 4 * RWKV_LORA:])
    w_pre = _dot(wd, w2_ref[...]) + w0_ref[...]
    a_sig = jax.nn.sigmoid(_dot(ad, a2_ref[...]) + a0_ref[...])
    g_o[...] = _dot(gd, g2_ref[...])
    w_log = -(jnp.maximum(-w_pre, 0.0) + jnp.log(1.0 + jnp.exp(-jnp.abs(w_pre)))) - 0.5
    lw = -jnp.exp(w_log)
    kkf = k * kk_w_ref[...]
    kk = kkf * lax.rsqrt(_head_sums(kkf * kkf) + 1e-12)
    r_o[...] = r
    kk_o[...] = kk
    v_o[...] = v
    k_sum = None
    for d in range(2):
        a_d = a_sig[:, d * BRANCH_W:(d + 1) * BRANCH_W]
        k_d = k * (1.0 + (a_d - 1.0) * ka_w_ref[...])
        lw_o[d] = lw[:, d * BRANCH_W:(d + 1) * BRANCH_W]
        kd_o[d] = k_d
        ka_o[d] = kk * a_d
        k_sum = k_d if k_sum is None else k_sum + k_d
    bonus_o[...] = _head_sums(r * k_sum * rk_w_ref[...]) * v

    z = cc_ref[...] * cu_ref[...]
    z_before = hp_ref[HALO - 1:HALO, C_CC:C_CC + BRANCH_W] * hp_ref[HALO - 1:HALO, C_CU:C_CU + BRANCH_W] * has_prev
    z_after = hn_ref[0:1, C_CC:C_CC + BRANCH_W] * hn_ref[0:1, C_CU:C_CU + BRANCH_W] * has_next
    z_prev = jnp.where(row == 0, z_before, pltpu.roll(z, 1, 0))
    z_next = jnp.where(row == tr - 1, z_after, pltpu.roll(z, tr - 1, 0))
    y = cb_ref[...] * (cw_ref[0:1, :] * z_prev + cw_ref[1:2, :] * z + cw_ref[2:3, :] * z_next)
    conv_o[...] = y.astype(conv_o.dtype)


def _mixer_prep(p, mu_rkv, mu_lora, kk_w, ka_w, rk_w, w0, a0, w2bd, a2bd, g2p, conv_w, *, tr, n_x, t_len,
                c_len):
    n_rows = p.shape[0]
    bw = BRANCH_W
    last_halo = n_rows // HALO - 1
    col = lambda c: (lambda i: (i, c))
    const = lambda i: (0, 0)
    main = [pl.BlockSpec((tr, bw), col(C_R // bw)), pl.BlockSpec((tr, bw), col(C_K // bw)),
            pl.BlockSpec((tr, bw), col(C_V // bw)), pl.BlockSpec((tr, bw), col(C_CB // bw)),
            pl.BlockSpec((tr, bw), col(C_CC // bw)), pl.BlockSpec((tr, bw), col(C_CU // bw)),
            pl.BlockSpec((tr, RWKV_LORA_PAD), col(C_LORA // RWKV_LORA_PAD)),
            pl.BlockSpec((HALO, P_COLS), lambda i: (jnp.maximum(i * (tr // HALO) - 1, 0), 0)),
            pl.BlockSpec((HALO, P_COLS), lambda i: (jnp.minimum((i + 1) * (tr // HALO), last_halo), 0))]
    params = [mu_rkv, mu_lora, kk_w, ka_w, rk_w, w0, a0, w2bd, a2bd, g2p, conv_w]
    param_specs = [pl.BlockSpec(a.shape, const) for a in params]
    row_spec = pl.BlockSpec((tr, bw), lambda i: (i, 0))
    dir_spec = pl.BlockSpec((2, tr, bw), lambda i: (0, i, 0))
    f32_rows = jax.ShapeDtypeStruct((n_rows, bw), F32)
    f32_dirs = jax.ShapeDtypeStruct((2, n_rows, bw), F32)
    vmem = 2 * (7 * tr * bw * 4 + 2 * HALO * P_COLS * 4 + 12 * tr * bw * 4) + 30 * tr * bw * 4
    return pl.pallas_call(
        functools.partial(_prep_kernel, tr=tr, n_x=n_x, t_len=t_len, c_len=c_len),
        grid=(n_rows // tr,),
        in_specs=main + param_specs,
        out_specs=[row_spec, row_spec, row_spec, dir_spec, dir_spec, dir_spec, row_spec, row_spec, row_spec],
        out_shape=[f32_rows, f32_rows, f32_rows, f32_dirs, f32_dirs, f32_dirs, f32_rows, f32_rows,
                   jax.ShapeDtypeStruct((n_rows, bw), BF16)],
        compiler_params=_cparams(("parallel",), vmem),
        name="mixer_prep",
    )(p, p, p, p, p, p, p, p, p, *params)


def _rwkv_kernel(r0_ref, kk0_ref, v0_ref, r1_ref, kk1_ref, v1_ref, lw0_ref, kd0_ref, ka0_ref,
                 lw1_ref, kd1_ref, ka1_ref, o0_ref, o1_ref, s_ref):
    C = RWKV_CHUNK
    W = 2 * RWKV_HEAD
    n_pairs = RWKV_HEADS // 2

    @pl.when(pl.program_id(1) == 0)
    def _():
        s_ref[...] = jnp.zeros_like(s_ref)

    mm, mm_nt = _dot, _dot_nt
    lane = lax.broadcasted_iota(jnp.int32, (C, W), 1)
    row = lax.broadcasted_iota(jnp.int32, (C, W), 0)
    first = lane < RWKV_HEAD
    rc = lax.broadcasted_iota(jnp.int32, (C, C), 0)
    cc = lax.broadcasted_iota(jnp.int32, (C, C), 1)

    def bdiag(x):
        return jnp.concatenate([jnp.where(first, x, jnp.zeros_like(x)), jnp.where(first, jnp.zeros_like(x), x)],
                               axis=0)

    def direction(sign, r_ref, kk_ref, v_ref, lw_ref, kd_ref, ka_ref):
        m_incl = jnp.where((rc - cc) * sign >= 0, 1.0, 0.0).astype(BF16)
        lw = lw_ref[0]
        cum = _dot_exact_lhs(m_incl, lw)
        tot = jnp.sum(lw, axis=0, keepdims=True)
        e_ninc = jnp.exp(-cum)
        e_rem = jnp.exp(tot - cum)
        kk, ka, kd = kk_ref[...], ka_ref[0], kd_ref[0]
        order = (row - (lane % RWKV_HEAD)) * sign
        return dict(a_t=kk * jnp.exp(cum - lw), b_t=-ka * e_ninc, k_t=kd * e_ninc, r_t=r_ref[...] * jnp.exp(cum),
                    b_h=-ka * e_rem, k_h=kd * e_rem, v=v_ref[...], e_tot=jnp.exp(tot), strict=order > 0,
                    incl=order >= 0, eye=jnp.where(order == 0, 1.0, 0.0).astype(F32))

    dirs = (direction(1, r0_ref, kk0_ref, v0_ref, lw0_ref, kd0_ref, ka0_ref),
            direction(-1, r1_ref, kk1_ref, v1_ref, lw1_ref, kd1_ref, ka1_ref))
    o_refs = (o0_ref, o1_ref)

    chains = [(d, p) for d in range(2) for p in range(n_pairs)]
    n = range(len(chains))
    sl = [slice(W * p, W * (p + 1)) for (_, p) in chains]
    dd = [dirs[d] for (d, _) in chains]
    ar = [jnp.concatenate([dd[c]["a_t"][:, sl[c]], dd[c]["r_t"][:, sl[c]]], axis=0).astype(BF16) for c in n]
    a_bd = [bdiag(dd[c]["a_t"][:, sl[c]].astype(BF16)) for c in n]
    bk_bd = [jnp.concatenate([bdiag(dd[c]["b_t"][:, sl[c]].astype(BF16)), bdiag(dd[c]["k_t"][:, sl[c]].astype(BF16))],
                             axis=0) for c in n]
    v_bd = [bdiag(dd[c]["v"][:, sl[c]].astype(BF16)) for c in n]
    g = [mm_nt(ar[c], bk_bd[c]) for c in n]
    l_pow = [jnp.where(dd[c]["strict"], g[c][:C, :W], 0.0) for c in n]
    m_ak = [jnp.where(dd[c]["strict"], g[c][:C, W:], 0.0).astype(BF16) for c in n]
    a_r = [jnp.where(jnp.concatenate([dd[c]["incl"], dd[c]["incl"]], axis=1), g[c][C:], 0.0).astype(BF16) for c in n]
    mv = [mm(m_ak[c], v_bd[c]) for c in n]
    l_bd = [bdiag(l_pow[c].astype(BF16)) for c in n]
    t_inv = [dd[c]["eye"] + l_pow[c] for c in n]
    l_pow = [mm(l_pow[c], l_bd[c]) for c in n]
    for _ in range(int(math.log2(C)) - 2):
        l_bd = [bdiag(l_pow[c].astype(BF16)) for c in n]
        lt = [mm(jnp.concatenate([l_pow[c], t_inv[c]], axis=0), l_bd[c]) for c in n]
        l_pow = [lt[c][:C] for c in n]
        t_inv = [t_inv[c] + lt[c][C:] for c in n]
    t_inv = [t_inv[c] + mm(t_inv[c], bdiag(l_pow[c].astype(BF16))) for c in n]
    wu = [mm(t_inv[c], jnp.concatenate([a_bd[c], bdiag(mv[c].astype(BF16))], axis=1)) for c in n]
    s_bd = [s_ref[d, p] for (d, p) in chains]
    wr = [mm(jnp.concatenate([wu[c][:, :W], dd[c]["r_t"][:, sl[c]]], axis=0), s_bd[c]) for c in n]
    zv = [jnp.concatenate([bdiag((wr[c][:C] + wu[c][:, W:]).astype(BF16)), v_bd[c]], axis=0) for c in n]
    for c, (d, p) in enumerate(chains):
        o_refs[d][:, sl[c]] = wr[c][C:] + mm(a_r[c], zv[c])
    for c, (d, p) in enumerate(chains):
        bk_h = jnp.concatenate([bdiag(dd[c]["b_h"][:, sl[c]]), bdiag(dd[c]["k_h"][:, sl[c]])], axis=0)
        tot_col = jnp.broadcast_to(dd[c]["e_tot"][:, sl[c]], (W, W)).T
        s_ref[d, p] = tot_col * s_bd[c] + mm(bk_h.T, zv[c])


def _rwkv_scan(r, kk, v, lw, kd, ka, *, n_batch, t_len, c_len):
    C = RWKV_CHUNK
    rows, width = r.shape
    nct, nxt = c_len // C, t_len // C
    ctx_blk0 = n_batch * t_len // C

    def blk(b, d, s):
        j_c = s if d == 0 else nct - 1 - s
        j_x = s - nct if d == 0 else nxt - 1 - (s - nct)
        return jnp.where(s < nct, ctx_blk0 + b * nct + j_c, b * nxt + j_x)

    def shared(d):
        return pl.BlockSpec((C, width), lambda b, s: (blk(b, d, s), 0))

    def per_dir(d):
        return pl.BlockSpec((1, C, width), lambda b, s: (d, blk(b, d, s), 0))

    vmem = 2 * 14 * C * width * 4 + RWKV_HEADS * 128 * 128 * 4 + 128 * C * width * 4
    out = jax.ShapeDtypeStruct((rows, width), F32)
    return pl.pallas_call(
        _rwkv_kernel,
        grid=(n_batch, nct + nxt),
        in_specs=[shared(0), shared(0), shared(0), shared(1), shared(1), shared(1),
                  per_dir(0), per_dir(0), per_dir(0), per_dir(1), per_dir(1), per_dir(1)],
        out_specs=[shared(0), shared(1)],
        out_shape=[out, out],
        scratch_shapes=[pltpu.VMEM((2, RWKV_HEADS // 2, 2 * RWKV_HEAD, 2 * RWKV_HEAD), F32)],
        compiler_params=_cparams(("parallel", "arbitrary"), vmem),
        name="rwkv_scan",
    )(r, kk, v, r, kk, v, lw, kd, ka, lw, kd, ka)


def _rwkv_readout_kernel(o0_ref, o1_ref, bonus_ref, g_ref, lng_ref, lnb_ref, y_ref):
    o = o0_ref[...] + o1_ref[...]
    mean = _head_sums(o) * (1.0 / RWKV_HEAD)
    cen = o - mean
    var = _head_sums(cen * cen) * (1.0 / RWKV_HEAD)
    o_n = cen * lax.rsqrt(var + RWKV_GN_EPS) * lng_ref[...] + lnb_ref[...]
    y_ref[...] = ((o_n + bonus_ref[...]) * g_ref[...]).astype(y_ref.dtype)


def _rwkv_readout(o_dirs, bonus, g_out, ln_g, ln_b, *, tr):
    n_rows, bw = bonus.shape
    row_spec = pl.BlockSpec((tr, bw), lambda i: (i, 0))
    const = pl.BlockSpec((1, bw), lambda i: (0, 0))
    return pl.pallas_call(
        _rwkv_readout_kernel,
        grid=(n_rows // tr,),
        in_specs=[row_spec, row_spec, row_spec, row_spec, const, const],
        out_specs=row_spec,
        out_shape=jax.ShapeDtypeStruct((n_rows, bw), BF16),
        compiler_params=_cparams(("parallel",), 2 * 5 * tr * bw * 4 + 16 * tr * bw * 4),
        name="rwkv_readout",
    )(o_dirs[0], o_dirs[1], bonus, g_out, ln_g.reshape(1, bw), ln_b.reshape(1, bw))


def _rmsnorm_kernel(x_ref, g_ref, o_ref):
    x = x_ref[...]
    o_ref[...] = x * lax.rsqrt(jnp.mean(x * x, axis=-1, keepdims=True) + NORM_EPS) * g_ref[...]


def _final_norm(xs, g, *, rows, tm):
    dm = xs.shape[1]
    return pl.pallas_call(
        _rmsnorm_kernel,
        grid=(rows // tm,),
        in_specs=[pl.BlockSpec((tm, dm), lambda i: (i, 0)), pl.BlockSpec((1, dm), lambda i: (0, 0))],
        out_specs=pl.BlockSpec((tm, dm), lambda i: (i, 0)),
        out_shape=jax.ShapeDtypeStruct((rows, dm), F32),
        compiler_params=_cparams(("parallel",), 4 * tm * dm * 4),
        name="final_norm",
    )(xs, g.reshape(1, dm))


def _rope_tables128(n_tokens):
    rows = n_tokens // GRID_W
    row = jnp.repeat(jnp.arange(rows, dtype=F32), GRID_W)
    col = jnp.tile(jnp.arange(GRID_W, dtype=F32), rows)
    n_freq = 64 // 4
    inv = ROPE_BASE ** (-jnp.arange(n_freq, dtype=F32) / n_freq)
    ang = jnp.concatenate([row[:, None] * inv, col[:, None] * inv], axis=-1)
    cos, sin = jnp.cos(ang), jnp.sin(ang)
    return jnp.concatenate([cos, cos, cos, cos], axis=-1), jnp.concatenate([-sin, sin, -sin, sin], axis=-1)


def _block_diag2(w2):
    z = jnp.zeros_like(w2[0])
    return jnp.concatenate([jnp.concatenate([w2[0], z], axis=1), jnp.concatenate([z, w2[1]], axis=1)], axis=0)


def kernel(x, c, ctx, c_ctx, norm1_g, norm2_g, mod_down, mod_up, mod_b, w_in, mla_q_norm_g, mla_w_uq,
           mla_kv_norm_g, mla_w_ukv, rwkv_mu, rwkv_w0, rwkv_w2, rwkv_a0, rwkv_a2, rwkv_g2, rwkv_k_k,
           rwkv_k_a, rwkv_r_k, rwkv_ln_g, rwkv_ln_b, conv_w, diff_lambda, diff_norm_g, w_branch, gate_down,
           gate_up, gate_b, w_out, mlp_w1, mlp_w2, final_norm_g):
    n_batch, t_len, dm = x.shape
    c_len = ctx.shape[1]
    depth = w_in.shape[0]
    bw = BRANCH_W
    n_x = n_batch * t_len
    n_c = n_batch * c_len
    n_rows = n_x + n_c
    tm = 512 if (t_len % 512 == 0 and n_c % 512 == 0) else 256
    tq = 2048 if t_len % 2048 == 0 else min(256, c_len)
    tr = min(256, c_len)
    assert dm == D_MODEL and t_len % tm == 0 and n_c % tm == 0 and n_x % c_len == 0
    assert t_len % tq == 0 and c_len % RWKV_CHUNK == 0 and t_len % c_len == 0
    tm_big = 1024 if (t_len % 1024 == 0 and n_c % 1024 == 0) else tm
    mod_index = functools.partial(_mod_index, tm=tm, n_x_rows=n_x, t_len=t_len, n_batch=n_batch)
    mod_index_big = functools.partial(_mod_index, tm=tm_big, n_x_rows=n_x, t_len=t_len, n_batch=n_batch)

    zeros = lambda *s: jnp.zeros(s, F32)
    lora_w = 4 * RWKV_LORA + RWKV_GATE_LORA
    rw0 = MLA_Q_LORA + MLA_KV_LORA + MLA_ROPE
    cv0 = rw0 + 3 * bw + lora_w
    w_in_t = jnp.swapaxes(w_in, 1, 2)
    w_in_p = jnp.concatenate(
        [w_in_t[:, :C_R], w_in_t[:, rw0:rw0 + 3 * bw], w_in_t[:, cv0:],
         w_in_t[:, rw0 + 3 * bw:cv0], zeros(depth, RWKV_LORA_PAD - lora_w, dm), jnp.swapaxes(gate_down, 1, 2),
         w_in_t[:, C_R:rw0], zeros(depth, P_COLS - C_KROPE - MLA_ROPE, dm)], axis=1).astype(BF16)
    w_uq_p = jnp.pad(mla_w_uq.reshape(depth, MLA_Q_LORA, MLA_HEADS, MLA_NOPE + MLA_ROPE),
                     ((0, 0), (0, 0), (0, 0), (0, MLA_QK_PAD - MLA_NOPE - MLA_ROPE))
                     ).reshape(depth, MLA_Q_LORA, MLA_HEADS * MLA_QK_PAD).astype(BF16)
    w_ukv_r = mla_w_ukv.reshape(depth, MLA_KV_LORA, MLA_HEADS, MLA_NOPE + MLA_V)
    w_ukv_p = jnp.concatenate([w_ukv_r[..., :MLA_NOPE].reshape(depth, MLA_KV_LORA, -1),
                               w_ukv_r[..., MLA_NOPE:].reshape(depth, MLA_KV_LORA, -1)], axis=-1).astype(BF16)
    w_branch_2d = w_branch.reshape(depth, 4 * bw, dm)
    gate_up_b = jnp.moveaxis(gate_up, 2, 1).astype(BF16)
    g2_p = jnp.pad(rwkv_g2, ((0, 0), (0, RWKV_LORA_PAD - 4 * RWKV_LORA - RWKV_GATE_LORA), (0, 0))).astype(BF16)
    mu_lora = jnp.pad(rwkv_mu[:, :, 3 * bw:], ((0, 0), (0, 0), (0, RWKV_LORA_PAD - lora_w)))

    cond = jnp.concatenate([c, c_ctx[None, :], zeros(16 - n_batch - 1, dm)], axis=0)
    cond = jax.nn.silu(cond)
    mods = []
    for l in range(depth):
        low = _matmul(cond, mod_down[l], tm=16, tn=MOD_RANK, tk=dm, name="mod_down")
        up = _matmul(low, mod_up[l], tm=16, tn=2048, tk=MOD_RANK, bias=mod_b[l], name="mod_up")
        mods.append(up.reshape(16, N_MOD, dm))

    cos_t, sin_t = _rope_tables128(t_len)
    xs = jnp.concatenate([x.reshape(n_x, dm), ctx.reshape(n_c, dm)], axis=0)

    for l in range(depth):
        need_ctx = l < depth - 1
        mod = mods[l]
        lam_init = 0.8 - 0.6 * math.exp(-0.3 * l)
        lq1, lk1, lq2, lk2 = diff_lambda[l]
        lam = jnp.exp(jnp.sum(lq1 * lk1)) - jnp.exp(jnp.sum(lq2 * lk2)) + lam_init
        lam_row = jnp.full((1, DIFF_V), 1.0, F32) * lam

        h1 = _norm_mod(xs, norm1_g[l], mod, shift_row=0, scale_row=1, mod_index=mod_index, tm=tm,
                       rows=n_rows)
        p, w1_b, w_out_b, w_branch_b = _fullk_matmul(h1, w_in_p, layer=l, tm=tm_big, tn=512, out_dtype=F32,
                                                     name="in_proj", cast_srcs=(mlp_w1, w_out, w_branch_2d),
                                                     w_rows_are_outputs=True)

        q = _nm_matmul(p, mla_q_norm_g[l], w_uq_p[l], tm=tm, x_col_block=C_CQ // MLA_Q_LORA, out_dtype=BF16,
                       name="mla_q",
                       out_scale=(MLA_NOPE + MLA_ROPE) ** -0.5 * LOG2_E)
        kv = _nm_matmul(p, mla_kv_norm_g[l], w_ukv_p[l], tm=tm, x_col_block=C_CKV // MLA_KV_LORA, out_dtype=BF16,
                        name="mla_kv")
        y_mla = _mla_attention(q, kv, p, cos_t, sin_t, n_batch=n_batch, t_len=t_len, c_len=c_len, tq=tq,
                               with_ctx=need_ctx)

        y_diff = _diff_attention(p, cos_t, sin_t, lam_row, diff_norm_g[l].reshape(1, DIFF_V),
                                 n_batch=n_batch, t_len=t_len, c_len=c_len, tq=tq, with_ctx=need_ctx,
                                 out_scale=1.0 - lam_init)

        r_, kk, v_, lw, k_dir, kka, bonus, g_out, y_conv = _mixer_prep(
            p, rwkv_mu[l, :, :3 * bw], mu_lora[l], rwkv_k_k[l].reshape(1, bw), rwkv_k_a[l].reshape(1, bw),
            rwkv_r_k[l].reshape(1, bw), rwkv_w0[l].reshape(1, 2 * bw), rwkv_a0[l].reshape(1, 2 * bw),
            _block_diag2(rwkv_w2[l]).astype(BF16), _block_diag2(rwkv_a2[l]).astype(BF16), g2_p[l],
            conv_w[l], tr=min(128, tr), n_x=n_x, t_len=t_len, c_len=c_len)
        o_dirs = _rwkv_scan(r_, kk, v_, lw, k_dir, kka, n_batch=n_batch, t_len=t_len, c_len=c_len)
        y_rwkv = _rwkv_readout(o_dirs, bonus, g_out, rwkv_ln_g[l], rwkv_ln_b[l], tr=tr)

        rows = n_rows if need_ctx else n_x
        acc = _merge(p, (y_mla, y_rwkv, y_conv, y_diff), w_branch_b, gate_up_b, gate_b, layer=l,
                     tm=tm_big, tn=512, rows=rows)
        (xs_new,) = _fullk_matmul(acc, w_out_b[None], layer=0, tm=tm_big, tn=512, out_dtype=F32, res=xs, mod=mod,
                                  gate_row=2, mod_index=mod_index_big, rows=rows, name="out_proj")

        h2 = _norm_mod(xs_new, norm2_g[l], mod, shift_row=3, scale_row=4, mod_index=mod_index, tm=tm,
                       rows=rows)
        hid, w2_b = _fullk_matmul(h2, w1_b[None], layer=0, tm=tm_big, tn=TN_UP, out_dtype=BF16, act="relu2",
                                  name="mlp_up", cast_srcs=(mlp_w2,), cast_layer=l)
        xs = _matmul(hid, w2_b, tm=tm_big, tn=1024, tk=2048, res=xs_new, mod=mod, gate_row=5,
                     mod_index=mod_index_big, rows=rows, name="mlp_down")

    out = _final_norm(xs, final_norm_g, rows=n_x, tm=tm)
    return out.reshape(n_batch, t_len, dm)
```

```python
import functools
import math

import jax
import jax.numpy as jnp
from jax import lax
from jax.experimental import pallas as pl
from jax.experimental.pallas import tpu as pltpu

F32 = jnp.float32
BF16 = jnp.bfloat16

D_MODEL = 4096
BRANCH_W = 1024
GRID_W = 64
ROPE_BASE = 10000.0
NORM_EPS = 1e-6
N_MOD = 6
LANES = 128
HALO = 8
LOG2_E = 1.4426950408889634

MLA_HEADS = 8
MLA_NOPE = 128
MLA_ROPE = 64
MLA_V = 128
MLA_Q_LORA = 768
MLA_KV_LORA = 256
MLA_QK_PAD = 256

RWKV_HEAD = 64
RWKV_HEADS = 16
RWKV_LORA = 64
RWKV_GATE_LORA = 160
RWKV_GN_EPS = 64e-5
RWKV_CHUNK = 64
RWKV_LORA_PAD = 512

DIFF_HEADS = 8
DIFF_QK = 64
DIFF_V = 128
GATE_RANK = 256
MOD_RANK = 256

C_CQ = 0
C_CKV = 768
C_R = 1024
C_K = 2048
C_V = 3072
C_CB = 4096
C_CC = 5120
C_CU = 6144
C_DQ = 7168
C_DK = 8192
C_DV = 9216
C_LORA = 10240
C_GL = 10752
C_KROPE = 11008
P_COLS = 11264

TN_UP = 1024
MLA_SUB = 256
DIFF_SUB = 128
VMEM_CAP = 56 * 1024 * 1024


def _cparams(sem, vmem_bytes):
    limit = int(min(VMEM_CAP, max(vmem_bytes * 1.5 + (4 << 20), 16 << 20)))
    return pltpu.CompilerParams(dimension_semantics=sem, vmem_limit_bytes=limit)


def _mod_index(i, tm, n_x_rows, t_len, n_batch):
    return jnp.where(i < n_x_rows // tm, i // (t_len // tm), n_batch)


def _dot(a, b):
    return jnp.dot(a.astype(BF16), b.astype(BF16), preferred_element_type=F32)


def _dot_nt(a, b):
    return lax.dot_general(a.astype(BF16), b.astype(BF16), (((1,), (1,)), ((), ())),
                           preferred_element_type=F32)


def _split3(x):
    hi = x.astype(BF16)
    r1 = x - hi.astype(F32)
    mid = r1.astype(BF16)
    lo = (r1 - mid.astype(F32)).astype(BF16)
    return hi, mid, lo


def _dot_exact_lhs(m_bf16, x):
    out = None
    for part in _split3(x):
        t = jnp.dot(m_bf16, part, preferred_element_type=F32)
        out = t if out is None else out + t
    return out


def _head_sums(x):
    r = lax.broadcasted_iota(jnp.int32, (LANES, LANES), 0)
    c = lax.broadcasted_iota(jnp.int32, (LANES, LANES), 1)
    ones_bd = jnp.where((r // RWKV_HEAD) == (c // RWKV_HEAD), 1.0, 0.0).astype(BF16)
    parts = _split3(x)
    cols = []
    for j in range(x.shape[1] // LANES):
        acc = None
        for part in parts:
            t = jnp.dot(part[:, j * LANES:(j + 1) * LANES], ones_bd, preferred_element_type=F32)
            acc = t if acc is None else acc + t
        cols.append(acc)
    return jnp.concatenate(cols, axis=1)


def _rope128(x, cos_t, sin_t):
    lane = lax.broadcasted_iota(jnp.int32, x.shape, 1)
    swapped = jnp.where((lane % 64) < 32, pltpu.roll(x, 96, 1), pltpu.roll(x, 32, 1))
    return x * cos_t + swapped * sin_t


def _matmul_kernel(*refs, nk, has_bias, has_res, gate_row):
    a_ref, w_ref = refs[0], refs[1]
    pos = 2
    bias_ref = res_ref = mod_ref = None
    if has_bias:
        bias_ref = refs[pos]; pos += 1
    if has_res:
        res_ref = refs[pos]; mod_ref = refs[pos + 1]; pos += 2
    o_ref, acc_ref = refs[pos], refs[pos + 1]
    k = pl.program_id(2)

    @pl.when(k == 0)
    def _():
        acc_ref[...] = jnp.zeros_like(acc_ref)

    acc_ref[...] += _dot(a_ref[...], w_ref[...])

    @pl.when(k == nk - 1)
    def _():
        y = acc_ref[...]
        if has_bias:
            y = y + bias_ref[...]
        if has_res:
            y = res_ref[...] + mod_ref[0, gate_row:gate_row + 1, :] * y
        o_ref[...] = y.astype(o_ref.dtype)


def _matmul(a, w, *, tm, tn, tk, out_dtype=F32, bias=None, res=None, mod=None, gate_row=None,
            mod_index=None, rows=None, name=None):
    m = a.shape[0] if rows is None else rows
    kdim, n = w.shape
    assert a.shape[1] == kdim and m % tm == 0 and n % tn == 0 and kdim % tk == 0
    nk = kdim // tk
    in_specs = [pl.BlockSpec((tm, tk), lambda i, j, k: (i, k)),
                pl.BlockSpec((tk, tn), lambda i, j, k: (k, j))]
    args = [a, w]
    vmem = 2 * tm * tk * a.dtype.itemsize + 2 * tk * tn * w.dtype.itemsize + tm * tn * 4
    vmem += 2 * tm * tn * jnp.dtype(out_dtype).itemsize
    if bias is not None:
        in_specs.append(pl.BlockSpec((1, tn), lambda i, j, k: (0, j)))
        args.append(bias.reshape(1, n).astype(F32))
    if res is not None:
        in_specs.append(pl.BlockSpec((tm, tn), lambda i, j, k: (i, j)))
        in_specs.append(pl.BlockSpec((1, N_MOD, tn), lambda i, j, k: (mod_index(i), 0, j)))
        args += [res, mod]
        vmem += 2 * tm * tn * 4 + 2 * 8 * tn * 4
    kern = functools.partial(_matmul_kernel, nk=nk, has_bias=bias is not None,
                             has_res=res is not None, gate_row=gate_row)
    return pl.pallas_call(
        kern,
        grid=(m // tm, n // tn, nk),
        in_specs=in_specs,
        out_specs=pl.BlockSpec((tm, tn), lambda i, j, k: (i, j)),
        out_shape=jax.ShapeDtypeStruct((m, n), out_dtype),
        scratch_shapes=[pltpu.VMEM((tm, tn), F32)],
        compiler_params=_cparams(("parallel", "parallel", "arbitrary"), vmem),
        name=name,
    )(*args)


def _nm_matmul_kernel(x_ref, g_ref, w_ref, o_ref, *, out_scale):
    x = x_ref[...]
    h = x * lax.rsqrt(jnp.mean(x * x, axis=-1, keepdims=True) + NORM_EPS) * g_ref[...]
    y = jnp.dot(h.astype(BF16), w_ref[...], preferred_element_type=F32)
    if out_scale is not None:
        y = y * out_scale
    o_ref[...] = y.astype(o_ref.dtype)


def _nm_matmul(x, g, w, *, tm, x_col_block, out_dtype, name, out_scale=None):
    m = x.shape[0]
    kdim, n = w.shape
    assert m % tm == 0
    vmem = 2 * tm * kdim * 4 + 2 * kdim * n * 2 + 2 * tm * n * jnp.dtype(out_dtype).itemsize + 2 * tm * n * 4
    return pl.pallas_call(
        functools.partial(_nm_matmul_kernel, out_scale=out_scale),
        grid=(m // tm,),
        in_specs=[pl.BlockSpec((tm, kdim), lambda i: (i, x_col_block)),
                  pl.BlockSpec((1, kdim), lambda i: (0, 0)),
                  pl.BlockSpec((kdim, n), lambda i: (0, 0))],
        out_specs=pl.BlockSpec((tm, n), lambda i: (i, 0)),
        out_shape=jax.ShapeDtypeStruct((m, n), out_dtype),
        compiler_params=_cparams(("parallel",), vmem),
        name=name,
    )(x, g.reshape(1, kdim), w)


def _norm_mod_kernel(x_ref, g_ref, mod_ref, h_ref, *, shift_row, scale_row):
    x = x_ref[...]
    y = x * lax.rsqrt(jnp.mean(x * x, axis=-1, keepdims=True) + NORM_EPS) * g_ref[...]
    y = y * (1.0 + mod_ref[0, scale_row:scale_row + 1, :]) + mod_ref[0, shift_row:shift_row + 1, :]
    h_ref[...] = y.astype(h_ref.dtype)


def _norm_mod(xs, g, mod, *, shift_row, scale_row, mod_index, tm, rows):
    dm = xs.shape[1]
    return pl.pallas_call(
        functools.partial(_norm_mod_kernel, shift_row=shift_row, scale_row=scale_row),
        grid=(rows // tm,),
        in_specs=[pl.BlockSpec((tm, dm), lambda i: (i, 0)), pl.BlockSpec((1, dm), lambda i: (0, 0)),
                  pl.BlockSpec((1, N_MOD, dm), lambda i: (mod_index(i), 0, 0))],
        out_specs=pl.BlockSpec((tm, dm), lambda i: (i, 0)),
        out_shape=jax.ShapeDtypeStruct((rows, dm), BF16),
        compiler_params=_cparams(("parallel",), 2 * tm * dm * 6 + 4 * tm * dm * 4),
        name="norm_mod",
    )(xs, g.reshape(1, dm), mod)


def _fullk_kernel(*refs, has_res, n_cast, gate_row, act, w_rows_are_outputs):
    refs = list(refs)
    if n_cast:
        cast_dsts = refs[-n_cast:]
        cast_srcs = refs[-2 * n_cast - 1:-n_cast - 1]
        refs = refs[:-2 * n_cast - 1] + [refs[-n_cast - 1]]
        for src, dst in zip(cast_srcs, cast_dsts):
            dst[...] = src[...].astype(dst.dtype)
    if has_res:
        a_ref, w_ref, res_ref, mod_ref, o_ref = refs
    else:
        a_ref, w_ref, o_ref = refs
    if w_rows_are_outputs:
        y = lax.dot_general(a_ref[...], w_ref[...], (((1,), (1,)), ((), ())), preferred_element_type=F32)
    else:
        y = jnp.dot(a_ref[...], w_ref[...], preferred_element_type=F32)
    if act == "relu2":
        y = jnp.square(jnp.maximum(y, 0.0))
    if has_res:
        y = res_ref[...] + mod_ref[0, gate_row:gate_row + 1, :] * y
    o_ref[...] = y.astype(o_ref.dtype)


def _cast_rows_per_step(n_rows, n_steps):
    rows = 16
    while n_rows % rows or n_rows // rows > n_steps:
        rows *= 2
    return rows


def _fullk_matmul(a, w, *, tm, tn, out_dtype, act=None, res=None, mod=None, gate_row=None, mod_index=None,
                  rows=None, name=None, layer=None, cast_srcs=(), cast_layer=None, w_rows_are_outputs=False):
    m = a.shape[0] if rows is None else rows
    if w_rows_are_outputs:
        _, n, kdim = w.shape
        w_spec = pl.BlockSpec((None, tn, kdim), lambda i, j: (layer, j, 0))
    else:
        _, kdim, n = w.shape
        w_spec = pl.BlockSpec((None, kdim, tn), lambda i, j: (layer, 0, j))
    assert a.shape[1] == kdim and m % tm == 0 and n % tn == 0 and a.dtype == BF16 and w.dtype == BF16
    in_specs = [pl.BlockSpec((tm, kdim), lambda i, j: (i, 0)), w_spec]
    args = [a, w]
    vmem = 2 * tm * kdim * 2 + 2 * kdim * tn * 2 + 2 * tm * tn * jnp.dtype(out_dtype).itemsize + 2 * tm * tn * 4
    if res is not None:
        in_specs += [pl.BlockSpec((tm, tn), lambda i, j: (i, j)),
                     pl.BlockSpec((1, N_MOD, tn), lambda i, j: (mod_index(i), 0, j))]
        args += [res, mod]
        vmem += 2 * tm * tn * 4
    out_specs = [pl.BlockSpec((tm, tn), lambda i, j: (i, j))]
    out_shape = [jax.ShapeDtypeStruct((m, n), out_dtype)]
    nj = n // tn
    cast_layer = layer if cast_layer is None else cast_layer
    for cast_src in cast_srcs:
        _, c_rows, c_cols = cast_src.shape
        cr = _cast_rows_per_step(c_rows, (m // tm) * nj)
        cast_blk = lambda i, j, last=c_rows // cr - 1: jnp.minimum(i * nj + j, last)
        in_specs.append(pl.BlockSpec((None, cr, c_cols), lambda i, j, blk=cast_blk: (cast_layer, blk(i, j), 0)))
        args.append(cast_src)
        out_specs.append(pl.BlockSpec((cr, c_cols), lambda i, j, blk=cast_blk: (blk(i, j), 0)))
        out_shape.append(jax.ShapeDtypeStruct((c_rows, c_cols), BF16))
        vmem += 2 * cr * c_cols * 6
    return pl.pallas_call(
        functools.partial(_fullk_kernel, has_res=res is not None, n_cast=len(cast_srcs),
                          gate_row=gate_row, act=act, w_rows_are_outputs=w_rows_are_outputs),
        grid=(m // tm, n // tn),
        in_specs=in_specs,
        out_specs=out_specs,
        out_shape=out_shape,
        compiler_params=_cparams(("arbitrary", "arbitrary"), vmem),
        name=name,
    )(*args)


def _merge_kernel(gl_ref, y0_ref, y1_ref, y2_ref, y3_ref, wb_ref, gu_ref, gb_ref, o_ref):
    gl = gl_ref[...].astype(BF16)
    bw = wb_ref.shape[0] // 4
    acc = None
    for i, y_ref in enumerate((y0_ref, y1_ref, y2_ref, y3_ref)):
        gate = jax.nn.sigmoid(jnp.dot(gl, gu_ref[i], preferred_element_type=F32) + gb_ref[i:i + 1, :])
        term = gate * jnp.dot(y_ref[...], wb_ref[i * bw:(i + 1) * bw, :], preferred_element_type=F32)
        acc = term if acc is None else acc + term
    o_ref[...] = acc.astype(o_ref.dtype)


def _merge(p, ys, wb, gu, gb, *, layer, tm, tn, rows):
    n = wb.shape[1]
    bw = wb.shape[0] // 4
    gr = gu.shape[2]
    y_spec = pl.BlockSpec((tm, bw), lambda i, j: (i, 0))
    vmem = (2 * tm * gr * 4 + 4 * 2 * tm * bw * 2 + 2 * 4 * bw * tn * 2 + 2 * 4 * gr * tn * 2
            + 2 * tm * tn * 2 + 3 * tm * tn * 4)
    return pl.pallas_call(
        _merge_kernel,
        grid=(rows // tm, n // tn),
        in_specs=[pl.BlockSpec((tm, gr), lambda i, j: (i, C_GL // GATE_RANK)),
                  y_spec, y_spec, y_spec, y_spec,
                  pl.BlockSpec((4 * bw, tn), lambda i, j: (0, j)),
                  pl.BlockSpec((None, 4, gr, tn), lambda i, j: (layer, 0, 0, j)),
                  pl.BlockSpec((None, 4, tn), lambda i, j: (layer, 0, j))],
        out_specs=pl.BlockSpec((tm, tn), lambda i, j: (i, j)),
        out_shape=jax.ShapeDtypeStruct((rows, n), BF16),
        compiler_params=_cparams(("parallel", "arbitrary"), vmem),
        name="merge",
    )(p, *ys, wb, gu, gb)


def _softmax_parts(s_list):
    m = None
    for s in s_list:
        sm = jnp.max(s, axis=-1, keepdims=True)
        m = sm if m is None else jnp.maximum(m, sm)
    e_list = [jnp.exp2(s - m) for s in s_list]
    l = None
    for e in e_list:
        es = jnp.sum(e, axis=-1, keepdims=True)
        l = es if l is None else l + es
    return e_list, l


def _attend(q, ks, vs):
    e_list, l = _softmax_parts([_dot_nt(q, k) for k in ks])
    o = None
    for e, v in zip(e_list, vs):
        t = jnp.dot(e.astype(BF16), v, preferred_element_type=F32)
        o = t if o is None else o + t
    return o / l


def _attend_many(qs, ks, vs, sub):
    parts = [q[r:r + sub] for q in qs for r in range(0, q.shape[0], sub)]
    n = len(parts)
    scores, soft, outs = {}, {}, []
    for t in range(n + 2):
        if t < n:
            scores[t] = [_dot_nt(parts[t], k) for k in ks]
        if 0 <= t - 1 < n:
            soft[t - 1] = _softmax_parts(scores.pop(t - 1))
        if 0 <= t - 2 < n:
            e_list, l = soft.pop(t - 2)
            o = None
            for e, v in zip(e_list, vs):
                pv = jnp.dot(e.astype(BF16), v, preferred_element_type=F32)
                o = pv if o is None else o + pv
            outs.append(o / l)
    per_q = len(parts) // len(qs)
    return [jnp.concatenate(outs[i * per_q:(i + 1) * per_q], axis=0) for i in range(len(qs))]


def _mla_kernel(cos_ref, sin_ref, q_ref, knx_ref, knc_ref, krx_ref, krc_ref, vx_ref, vc_ref, y_init, o_ref,
                kx_s, kc_s, *, tq):
    del y_init
    i = pl.program_id(2)

    @pl.when(i == 0)
    def _():
        kx_s[:, :MLA_NOPE] = knx_ref[...]
        kx_s[:, MLA_NOPE:] = _rope128(krx_ref[...], cos_ref[...], sin_ref[...]).astype(BF16)
        kc_s[:, :MLA_NOPE] = knc_ref[...]
        kc_s[:, MLA_NOPE:] = krc_ref[...].astype(BF16)

    row0 = pl.multiple_of(i * tq, tq)
    q = q_ref[...]
    q_rope = _rope128(q[:, MLA_NOPE:].astype(F32), cos_ref[pl.ds(row0, tq), :], sin_ref[pl.ds(row0, tq), :])
    q = jnp.concatenate([q[:, :MLA_NOPE], q_rope.astype(BF16)], axis=1)
    (o,) = _attend_many((q,), (kc_s[...], kx_s[...]), (vc_ref[...], vx_ref[...]), MLA_SUB)
    o_ref[...] = o.astype(o_ref.dtype)


def _mla_ctx_kernel(q_ref, knc_ref, krc_ref, vc_ref, y_hbm, o_ref):
    del y_hbm
    kc = jnp.concatenate([knc_ref[...], krc_ref[...].astype(BF16)], axis=1)
    o = _attend(q_ref[...], (kc,), (vc_ref[...],))
    o_ref[...] = o.astype(o_ref.dtype)


def _diff_finish(o1, o2, lam_ref, g_ref, out_scale):
    o = o1 - lam_ref[...] * o2
    y = o * lax.rsqrt(jnp.mean(o * o, axis=-1, keepdims=True) + NORM_EPS) * g_ref[...]
    return y * out_scale


def _diff_halves(q):
    lane = lax.broadcasted_iota(jnp.int32, q.shape, 1)
    return jnp.where(lane < DIFF_QK, q, 0.0).astype(BF16), jnp.where(lane < DIFF_QK, 0.0, q).astype(BF16)


def _diff_kernel(cos_ref, sin_ref, lam_ref, g_ref, q_ref, kx_ref, kc_ref, vx_ref, vc_ref, y_init, o_ref,
                 kx_s, kc_s, vx_s, vc_s, *, tq, scale, out_scale):
    del y_init
    i = pl.program_id(2)

    @pl.when(i == 0)
    def _():
        kx_s[...] = _rope128(kx_ref[...], cos_ref[...], sin_ref[...]).astype(BF16)
        kc_s[...] = kc_ref[...].astype(BF16)
        vx_s[...] = vx_ref[...].astype(BF16)
        vc_s[...] = vc_ref[...].astype(BF16)

    row0 = pl.multiple_of(i * tq, tq)
    q = q_ref[...] * (scale * LOG2_E)
    q1, q2 = _diff_halves(_rope128(q, cos_ref[pl.ds(row0, tq), :], sin_ref[pl.ds(row0, tq), :]))
    ks, vs = (kc_s[...], kx_s[...]), (vc_s[...], vx_s[...])
    o1, o2 = _attend_many((q1, q2), ks, vs, DIFF_SUB)
    y = _diff_finish(o1, o2, lam_ref, g_ref, out_scale)
    o_ref[...] = y.astype(o_ref.dtype)


def _diff_ctx_kernel(lam_ref, g_ref, q_ref, kc_ref, vc_ref, y_hbm, o_ref, *, scale, out_scale):
    del y_hbm
    q1, q2 = _diff_halves(q_ref[...] * (scale * LOG2_E))
    ks, vs = (kc_ref[...].astype(BF16),), (vc_ref[...].astype(BF16),)
    y = _diff_finish(_attend(q1, ks, vs), _attend(q2, ks, vs), lam_ref, g_ref, out_scale)
    o_ref[...] = y.astype(o_ref.dtype)


def _mla_attention(q, kv, p, cos_t, sin_t, *, n_batch, t_len, c_len, tq, with_ctx):
    n_x = n_batch * t_len
    nq = t_len // tq
    out_rows = n_x + (n_batch * c_len if with_ctx else 0)
    cblk0 = n_x // c_len
    qmap = lambda b, h, i: (b * nq + i, h)
    full = lambda b, h, i: (0, 0)
    vmem = (4 * t_len * LANES * 4 + 2 * tq * 256 * 2 + 2 * (t_len + c_len) * LANES * (2 + 4 + 2)
            + (t_len + c_len) * 256 * 2 + 2 * tq * LANES * 2 + 6 * tq * (t_len + c_len) * 4)
    y = pl.pallas_call(
        functools.partial(_mla_kernel, tq=tq),
        grid=(n_batch, MLA_HEADS, nq),
        in_specs=[pl.BlockSpec((t_len, LANES), full), pl.BlockSpec((t_len, LANES), full),
                  pl.BlockSpec((tq, MLA_QK_PAD), qmap),
                  pl.BlockSpec((t_len, MLA_NOPE), lambda b, h, i: (b, h)),
                  pl.BlockSpec((c_len, MLA_NOPE), lambda b, h, i: (cblk0 + b, h)),
                  pl.BlockSpec((t_len, LANES), lambda b, h, i: (b, C_KROPE // LANES)),
                  pl.BlockSpec((c_len, LANES), lambda b, h, i: (cblk0 + b, C_KROPE // LANES)),
                  pl.BlockSpec((t_len, MLA_V), lambda b, h, i: (b, MLA_HEADS + h)),
                  pl.BlockSpec((c_len, MLA_V), lambda b, h, i: (cblk0 + b, MLA_HEADS + h)),
                  pl.BlockSpec(memory_space=pl.ANY)],
        out_specs=pl.BlockSpec((tq, MLA_V), qmap),
        out_shape=jax.ShapeDtypeStruct((out_rows, MLA_HEADS * MLA_V), BF16),
        input_output_aliases={9: 0},
        scratch_shapes=[pltpu.VMEM((t_len, MLA_QK_PAD), BF16), pltpu.VMEM((c_len, MLA_QK_PAD), BF16)],
        compiler_params=_cparams(("parallel", "parallel", "arbitrary"), vmem),
        name="mla_attention",
    )(cos_t, sin_t, q, kv, kv, p, p, kv, kv, jnp.zeros((out_rows, MLA_HEADS * MLA_V), BF16))
    if not with_ctx:
        return y
    cmap = lambda b, h: (cblk0 + b, h)
    return pl.pallas_call(
        _mla_ctx_kernel,
        grid=(n_batch, MLA_HEADS),
        in_specs=[pl.BlockSpec((c_len, MLA_QK_PAD), cmap),
                  pl.BlockSpec((c_len, MLA_NOPE), cmap),
                  pl.BlockSpec((c_len, LANES), lambda b, h: (cblk0 + b, C_KROPE // LANES)),
                  pl.BlockSpec((c_len, MLA_V), lambda b, h: (cblk0 + b, MLA_HEADS + h)),
                  pl.BlockSpec(memory_space=pl.ANY)],
        out_specs=pl.BlockSpec((c_len, MLA_V), cmap),
        out_shape=jax.ShapeDtypeStruct(y.shape, y.dtype),
        input_output_aliases={4: 0},
        compiler_params=_cparams(("parallel", "parallel"), 16 * c_len * c_len * 4 + 8 * c_len * 256 * 4),
        name="mla_attention_ctx",
    )(q, kv, p, kv, y)


def _diff_attention(p, cos_t, sin_t, lam_row, g_row, *, n_batch, t_len, c_len, tq, with_ctx, out_scale):
    n_x = n_batch * t_len
    nq = t_len // tq
    out_rows = n_x + (n_batch * c_len if with_ctx else 0)
    cblk0 = n_x // c_len
    scale = DIFF_QK ** -0.5
    full = lambda b, h, i: (0, 0)
    qblk, kblk, vblk = C_DQ // LANES, C_DK // LANES, C_DV // LANES
    vmem = (4 * t_len * LANES * 4 + 2 * tq * LANES * 4 + 4 * (t_len + c_len) * LANES * 4
            + 2 * (t_len + c_len) * LANES * 2 + 2 * tq * LANES * 2 + 8 * tq * (t_len + c_len) * 4)
    y = pl.pallas_call(
        functools.partial(_diff_kernel, tq=tq, scale=scale, out_scale=out_scale),
        grid=(n_batch, DIFF_HEADS, nq),
        in_specs=[pl.BlockSpec((t_len, LANES), full), pl.BlockSpec((t_len, LANES), full),
                  pl.BlockSpec((1, DIFF_V), full), pl.BlockSpec((1, DIFF_V), full),
                  pl.BlockSpec((tq, LANES), lambda b, h, i: (b * nq + i, qblk + h)),
                  pl.BlockSpec((t_len, LANES), lambda b, h, i: (b, kblk + h)),
                  pl.BlockSpec((c_len, LANES), lambda b, h, i: (cblk0 + b, kblk + h)),
                  pl.BlockSpec((t_len, LANES), lambda b, h, i: (b, vblk + h)),
                  pl.BlockSpec((c_len, LANES), lambda b, h, i: (cblk0 + b, vblk + h)),
                  pl.BlockSpec(memory_space=pl.ANY)],
        out_specs=pl.BlockSpec((tq, DIFF_V), lambda b, h, i: (b * nq + i, h)),
        out_shape=jax.ShapeDtypeStruct((out_rows, DIFF_HEADS * DIFF_V), BF16),
        input_output_aliases={9: 0},
        scratch_shapes=[pltpu.VMEM((t_len, LANES), BF16), pltpu.VMEM((c_len, LANES), BF16),
                        pltpu.VMEM((t_len, LANES), BF16), pltpu.VMEM((c_len, LANES), BF16)],
        compiler_params=_cparams(("parallel", "parallel", "arbitrary"), vmem),
        name="diff_attention",
    )(cos_t, sin_t, lam_row, g_row, p, p, p, p, p, jnp.zeros((out_rows, DIFF_HEADS * DIFF_V), BF16))
    if not with_ctx:
        return y
    one = lambda b, h: (0, 0)
    return pl.pallas_call(
        functools.partial(_diff_ctx_kernel, scale=scale, out_scale=out_scale),
        grid=(n_batch, DIFF_HEADS),
        in_specs=[pl.BlockSpec((1, DIFF_V), one), pl.BlockSpec((1, DIFF_V), one),
                  pl.BlockSpec((c_len, LANES), lambda b, h: (cblk0 + b, qblk + h)),
                  pl.BlockSpec((c_len, LANES), lambda b, h: (cblk0 + b, kblk + h)),
                  pl.BlockSpec((c_len, LANES), lambda b, h: (cblk0 + b, vblk + h)),
                  pl.BlockSpec(memory_space=pl.ANY)],
        out_specs=pl.BlockSpec((c_len, DIFF_V), lambda b, h: (cblk0 + b, h)),
        out_shape=jax.ShapeDtypeStruct(y.shape, y.dtype),
        input_output_aliases={5: 0},
        compiler_params=_cparams(("parallel", "parallel"), 24 * c_len * c_len * 4 + 8 * c_len * LANES * 4),
        name="diff_attention_ctx",
    )(lam_row, g_row, p, p, p, y)


def _prep_kernel(r_ref, k_ref, v_ref, cb_ref, cc_ref, cu_ref, lo_ref, hp_ref, hn_ref,
                 mu_ref, mul_ref, kk_w_ref, ka_w_ref, rk_w_ref, w0_ref, a0_ref, w2_ref, a2_ref, g2_ref,
                 cw_ref,
                 r_o, kk_o, v_o, lw_o, kd_o, ka_o, bonus_o, g_o, conv_o, *, tr, n_x, t_len, c_len):
    i = pl.program_id(0)
    g0 = i * tr
    seq = jnp.where(g0 < n_x, t_len, c_len)
    has_prev = ((g0 % seq) != 0).astype(F32)
    has_next = (((g0 + tr) % seq) != 0).astype(F32)
    row = lax.broadcasted_iota(jnp.int32, (tr, 1), 0)

    def neighbours(x, col0):
        width = x.shape[1]
        before = hp_ref[HALO - 1:HALO, col0:col0 + width] * has_prev
        after = hn_ref[0:1, col0:col0 + width] * has_next
        prev = jnp.where(row == 0, before, pltpu.roll(x, 1, 0))
        nxt = jnp.where(row == tr - 1, after, pltpu.roll(x, tr - 1, 0))
        return prev, nxt

    def shifted(x, col0, mu0, mu1):
        prev, nxt = neighbours(x, col0)
        return x + mu0 * (prev - x) + mu1 * (nxt - x)

    r = shifted(r_ref[...], C_R, mu_ref[0:1, :BRANCH_W], mu_ref[1:2, :BRANCH_W])
    k = shifted(k_ref[...], C_K, mu_ref[0:1, BRANCH_W:2 * BRANCH_W], mu_ref[1:2, BRANCH_W:2 * BRANCH_W])
    v = shifted(v_ref[...], C_V, mu_ref[0:1, 2 * BRANCH_W:], mu_ref[1:2, 2 * BRANCH_W:])
    lo = shifted(lo_ref[...], C_LORA, mul_ref[0:1, :], mul_ref[1:2, :])
    wd = jnp.tanh(lo[:, :2 * RWKV_LORA])
    ad = lo[:, 2 * RWKV_LORA:4 * RWKV_LORA]
    gd = jax.nn.sigmoid(lo[:, 4 * RWKV_LORA:])
    w_pre = _dot(wd, w2_ref[...]) + w0_ref[...]
    a_sig = jax.nn.sigmoid(_dot(ad, a2_ref[...]) + a0_ref[...])
    g_o[...] = _dot(gd, g2_ref[...])
    w_log = -(jnp.maximum(-w_pre, 0.0) + jnp.log(1.0 + jnp.exp(-jnp.abs(w_pre)))) - 0.5
    lw = -jnp.exp(w_log)
    kkf = k * kk_w_ref[...]
    kk = kkf * lax.rsqrt(_head_sums(kkf * kkf) + 1e-12)
    r_o[...] = r
    kk_o[...] = kk
    v_o[...] = v
    k_sum = None
    for d in range(2):
        a_d = a_sig[:, d * BRANCH_W:(d + 1) * BRANCH_W]
        k_d = k * (1.0 + (a_d - 1.0) * ka_w_ref[...])
        lw_o[d] = lw[:, d * BRANCH_W:(d + 1) * BRANCH_W]
        kd_o[d] = k_d
        ka_o[d] = kk * a_d
        k_sum = k_d if k_sum is None else k_sum + k_d
    bonus_o[...] = _head_sums(r * k_sum * rk_w_ref[...]) * v

    z = cc_ref[...] * cu_ref[...]
    z_before = hp_ref[HALO - 1:HALO, C_CC:C_CC + BRANCH_W] * hp_ref[HALO - 1:HALO, C_CU:C_CU + BRANCH_W] * has_prev
    z_after = hn_ref[0:1, C_CC:C_CC + BRANCH_W] * hn_ref[0:1, C_CU:C_CU + BRANCH_W] * has_next
    z_prev = jnp.where(row == 0, z_before, pltpu.roll(z, 1, 0))
    z_next = jnp.where(row == tr - 1, z_after, pltpu.roll(z, tr - 1, 0))
    y = cb_ref[...] * (cw_ref[0:1, :] * z_prev + cw_ref[1:2, :] * z + cw_ref[2:3, :] * z_next)
    conv_o[...] = y.astype(conv_o.dtype)


def _mixer_prep(p, mu_rkv, mu_lora, kk_w, ka_w, rk_w, w0, a0, w2bd, a2bd, g2p, conv_w, *, tr, n_x, t_len,
                c_len):
    n_rows = p.shape[0]
    bw = BRANCH_W
    last_halo = n_rows // HALO - 1
    col = lambda c: (lambda i: (i, c))
    const = lambda i: (0, 0)
    main = [pl.BlockSpec((tr, bw), col(C_R // bw)), pl.BlockSpec((tr, bw), col(C_K // bw)),
            pl.BlockSpec((tr, bw), col(C_V // bw)), pl.BlockSpec((tr, bw), col(C_CB // bw)),
            pl.BlockSpec((tr, bw), col(C_CC // bw)), pl.BlockSpec((tr, bw), col(C_CU // bw)),
            pl.BlockSpec((tr, RWKV_LORA_PAD), col(C_LORA // RWKV_LORA_PAD)),
            pl.BlockSpec((HALO, P_COLS), lambda i: (jnp.maximum(i * (tr // HALO) - 1, 0), 0)),
            pl.BlockSpec((HALO, P_COLS), lambda i: (jnp.minimum((i + 1) * (tr // HALO), last_halo), 0))]
    params = [mu_rkv, mu_lora, kk_w, ka_w, rk_w, w0, a0, w2bd, a2bd, g2p, conv_w]
    param_specs = [pl.BlockSpec(a.shape, const) for a in params]
    row_spec = pl.BlockSpec((tr, bw), lambda i: (i, 0))
    dir_spec = pl.BlockSpec((2, tr, bw), lambda i: (0, i, 0))
    f32_rows = jax.ShapeDtypeStruct((n_rows, bw), F32)
    f32_dirs = jax.ShapeDtypeStruct((2, n_rows, bw), F32)
    vmem = 2 * (7 * tr * bw * 4 + 2 * HALO * P_COLS * 4 + 12 * tr * bw * 4) + 30 * tr * bw * 4
    return pl.pallas_call(
        functools.partial(_prep_kernel, tr=tr, n_x=n_x, t_len=t_len, c_len=c_len),
        grid=(n_rows // tr,),
        in_specs=main + param_specs,
        out_specs=[row_spec, row_spec, row_spec, dir_spec, dir_spec, dir_spec, row_spec, row_spec, row_spec],
        out_shape=[f32_rows, f32_rows, f32_rows, f32_dirs, f32_dirs, f32_dirs, f32_rows, f32_rows,
                   jax.ShapeDtypeStruct((n_rows, bw), BF16)],
        compiler_params=_cparams(("parallel",), vmem),
        name="mixer_prep",
    )(p, p, p, p, p, p, p, p, p, *params)


def _rwkv_kernel(r0_ref, kk0_ref, v0_ref, r1_ref, kk1_ref, v1_ref, lw0_ref, kd0_ref, ka0_ref,
                 lw1_ref, kd1_ref, ka1_ref, o0_ref, o1_ref, s_ref):
    C = RWKV_CHUNK
    W = 2 * RWKV_HEAD
    n_pairs = RWKV_HEADS // 2

    @pl.when(pl.program_id(1) == 0)
    def _():
        s_ref[...] = jnp.zeros_like(s_ref)

    mm, mm_nt = _dot, _dot_nt
    lane = lax.broadcasted_iota(jnp.int32, (C, W), 1)
    row = lax.broadcasted_iota(jnp.int32, (C, W), 0)
    first = lane < RWKV_HEAD
    rc = lax.broadcasted_iota(jnp.int32, (C, C), 0)
    cc = lax.broadcasted_iota(jnp.int32, (C, C), 1)

    def bdiag(x):
        return jnp.concatenate([jnp.where(first, x, jnp.zeros_like(x)), jnp.where(first, jnp.zeros_like(x), x)],
                               axis=0)

    def direction(sign, r_ref, kk_ref, v_ref, lw_ref, kd_ref, ka_ref):
        m_incl = jnp.where((rc - cc) * sign >= 0, 1.0, 0.0).astype(BF16)
        lw = lw_ref[0]
        cum = _dot_exact_lhs(m_incl, lw)
        tot = jnp.sum(lw, axis=0, keepdims=True)
        e_ninc = jnp.exp(-cum)
        e_rem = jnp.exp(tot - cum)
        kk, ka, kd = kk_ref[...], ka_ref[0], kd_ref[0]
        order = (row - (lane % RWKV_HEAD)) * sign
        return dict(a_t=kk * jnp.exp(cum - lw), b_t=-ka * e_ninc, k_t=kd * e_ninc, r_t=r_ref[...] * jnp.exp(cum),
                    b_h=-ka * e_rem, k_h=kd * e_rem, v=v_ref[...], e_tot=jnp.exp(tot), strict=order > 0,
                    incl=order >= 0, eye=jnp.where(order == 0, 1.0, 0.0).astype(F32))

    dirs = (direction(1, r0_ref, kk0_ref, v0_ref, lw0_ref, kd0_ref, ka0_ref),
            direction(-1, r1_ref, kk1_ref, v1_ref, lw1_ref, kd1_ref, ka1_ref))
    o_refs = (o0_ref, o1_ref)

    chains = [(d, p) for d in range(2) for p in range(n_pairs)]
    n = range(len(chains))
    sl = [slice(W * p, W * (p + 1)) for (_, p) in chains]
    dd = [dirs[d] for (d, _) in chains]
    ar = [jnp.concatenate([dd[c]["a_t"][:, sl[c]], dd[c]["r_t"][:, sl[c]]], axis=0).astype(BF16) for c in n]
    a_bd = [bdiag(dd[c]["a_t"][:, sl[c]].astype(BF16)) for c in n]
    bk_bd = [jnp.concatenate([bdiag(dd[c]["b_t"][:, sl[c]].astype(BF16)), bdiag(dd[c]["k_t"][:, sl[c]].astype(BF16))],
                             axis=0) for c in n]
    v_bd = [bdiag(dd[c]["v"][:, sl[c]].astype(BF16)) for c in n]
    g = [mm_nt(ar[c], bk_bd[c]) for c in n]
    l_pow = [jnp.where(dd[c]["strict"], g[c][:C, :W], 0.0) for c in n]
    m_ak = [jnp.where(dd[c]["strict"], g[c][:C, W:], 0.0).astype(BF16) for c in n]
    a_r = [jnp.where(jnp.concatenate([dd[c]["incl"], dd[c]["incl"]], axis=1), g[c][C:], 0.0).astype(BF16) for c in n]
    mv = [mm(m_ak[c], v_bd[c]) for c in n]
    l_bd = [bdiag(l_pow[c].astype(BF16)) for c in n]
    t_inv = [dd[c]["eye"] + l_pow[c] for c in n]
    l_pow = [mm(l_pow[c], l_bd[c]) for c in n]
    for _ in range(int(math.log2(C)) - 2):
        l_bd = [bdiag(l_pow[c].astype(BF16)) for c in n]
        lt = [mm(jnp.concatenate([l_pow[c], t_inv[c]], axis=0), l_bd[c]) for c in n]
        l_pow = [lt[c][:C] for c in n]
        t_inv = [t_inv[c] + lt[c][C:] for c in n]
    t_inv = [t_inv[c] + mm(t_inv[c], bdiag(l_pow[c].astype(BF16))) for c in n]
    wu = [mm(t_inv[c], jnp.concatenate([a_bd[c], bdiag(mv[c].astype(BF16))], axis=1)) for c in n]
    s_bd = [s_ref[d, p] for (d, p) in chains]
    wr = [mm(jnp.concatenate([wu[c][:, :W], dd[c]["r_t"][:, sl[c]]], axis=0), s_bd[c]) for c in n]
    zv = [jnp.concatenate([bdiag((wr[c][:C] + wu[c][:, W:]).astype(BF16)), v_bd[c]], axis=0) for c in n]
    for c, (d, p) in enumerate(chains):
        o_refs[d][:, sl[c]] = wr[c][C:] + mm(a_r[c], zv[c])
    for c, (d, p) in enumerate(chains):
        bk_h = jnp.concatenate([bdiag(dd[c]["b_h"][:, sl[c]]), bdiag(dd[c]["k_h"][:, sl[c]])], axis=0)
        tot_col = jnp.broadcast_to(dd[c]["e_tot"][:, sl[c]], (W, W)).T
        s_ref[d, p] = tot_col * s_bd[c] + mm(bk_h.T, zv[c])


def _rwkv_scan(r, kk, v, lw, kd, ka, *, n_batch, t_len, c_len):
    C = RWKV_CHUNK
    rows, width = r.shape
    nct, nxt = c_len // C, t_len // C
    ctx_blk0 = n_batch * t_len // C

    def blk(b, d, s):
        j_c = s if d == 0 else nct - 1 - s
        j_x = s - nct if d == 0 else nxt - 1 - (s - nct)
        return jnp.where(s < nct, ctx_blk0 + b * nct + j_c, b * nxt + j_x)

    def shared(d):
        return pl.BlockSpec((C, width), lambda b, s: (blk(b, d, s), 0))

    def per_dir(d):
        return pl.BlockSpec((1, C, width), lambda b, s: (d, blk(b, d, s), 0))

    vmem = 2 * 14 * C * width * 4 + RWKV_HEADS * 128 * 128 * 4 + 128 * C * width * 4
    out = jax.ShapeDtypeStruct((rows, width), F32)
    return pl.pallas_call(
        _rwkv_kernel,
        grid=(n_batch, nct + nxt),
        in_specs=[shared(0), shared(0), shared(0), shared(1), shared(1), shared(1),
                  per_dir(0), per_dir(0), per_dir(0), per_dir(1), per_dir(1), per_dir(1)],
        out_specs=[shared(0), shared(1)],
        out_shape=[out, out],
        scratch_shapes=[pltpu.VMEM((2, RWKV_HEADS // 2, 2 * RWKV_HEAD, 2 * RWKV_HEAD), F32)],
        compiler_params=_cparams(("parallel", "arbitrary"), vmem),
        name="rwkv_scan",
    )(r, kk, v, r, kk, v, lw, kd, ka, lw, kd, ka)


def _rwkv_readout_kernel(o0_ref, o1_ref, bonus_ref, g_ref, lng_ref, lnb_ref, y_ref):
    o = o0_ref[...] + o1_ref[...]
    mean = _head_sums(o) * (1.0 / RWKV_HEAD)
    cen = o - mean
    var = _head_sums(cen * cen) * (1.0 / RWKV_HEAD)
    o_n = cen * lax.rsqrt(var + RWKV_GN_EPS) * lng_ref[...] + lnb_ref[...]
    y_ref[...] = ((o_n + bonus_ref[...]) * g_ref[...]).astype(y_ref.dtype)


def _rwkv_readout(o_dirs, bonus, g_out, ln_g, ln_b, *, tr):
    n_rows, bw = bonus.shape
    row_spec = pl.BlockSpec((tr, bw), lambda i: (i, 0))
    const = pl.BlockSpec((1, bw), lambda i: (0, 0))
    return pl.pallas_call(
        _rwkv_readout_kernel,
        grid=(n_rows // tr,),
        in_specs=[row_spec, row_spec, row_spec, row_spec, const, const],
        out_specs=row_spec,
        out_shape=jax.ShapeDtypeStruct((n_rows, bw), BF16),
        compiler_params=_cparams(("parallel",), 2 * 5 * tr * bw * 4 + 16 * tr * bw * 4),
        name="rwkv_readout",
    )(o_dirs[0], o_dirs[1], bonus, g_out, ln_g.reshape(1, bw), ln_b.reshape(1, bw))


def _rmsnorm_kernel(x_ref, g_ref, o_ref):
    x = x_ref[...]
    o_ref[...] = x * lax.rsqrt(jnp.mean(x * x, axis=-1, keepdims=True) + NORM_EPS) * g_ref[...]


def _final_norm(xs, g, *, rows, tm):
    dm = xs.shape[1]
    return pl.pallas_call(
        _rmsnorm_kernel,
        grid=(rows // tm,),
        in_specs=[pl.BlockSpec((tm, dm), lambda i: (i, 0)), pl.BlockSpec((1, dm), lambda i: (0, 0))],
        out_specs=pl.BlockSpec((tm, dm), lambda i: (i, 0)),
        out_shape=jax.ShapeDtypeStruct((rows, dm), F32),
        compiler_params=_cparams(("parallel",), 4 * tm * dm * 4),
        name="final_norm",
    )(xs, g.reshape(1, dm))


def _rope_tables128(n_tokens):
    rows = n_tokens // GRID_W
    row = jnp.repeat(jnp.arange(rows, dtype=F32), GRID_W)
    col = jnp.tile(jnp.arange(GRID_W, dtype=F32), rows)
    n_freq = 64 // 4
    inv = ROPE_BASE ** (-jnp.arange(n_freq, dtype=F32) / n_freq)
    ang = jnp.concatenate([row[:, None] * inv, col[:, None] * inv], axis=-1)
    cos, sin = jnp.cos(ang), jnp.sin(ang)
    return jnp.concatenate([cos, cos, cos, cos], axis=-1), jnp.concatenate([-sin, sin, -sin, sin], axis=-1)


def _block_diag2(w2):
    z = jnp.zeros_like(w2[0])
    return jnp.concatenate([jnp.concatenate([w2[0], z], axis=1), jnp.concatenate([z, w2[1]], axis=1)], axis=0)


def kernel(x, c, ctx, c_ctx, norm1_g, norm2_g, mod_down, mod_up, mod_b, w_in, mla_q_norm_g, mla_w_uq,
           mla_kv_norm_g, mla_w_ukv, rwkv_mu, rwkv_w0, rwkv_w2, rwkv_a0, rwkv_a2, rwkv_g2, rwkv_k_k,
           rwkv_k_a, rwkv_r_k, rwkv_ln_g, rwkv_ln_b, conv_w, diff_lambda, diff_norm_g, w_branch, gate_down,
           gate_up, gate_b, w_out, mlp_w1, mlp_w2, final_norm_g):
    n_batch, t_len, dm = x.shape
    c_len = ctx.shape[1]
    depth = w_in.shape[0]
    bw = BRANCH_W
    n_x = n_batch * t_len
    n_c = n_batch * c_len
    n_rows = n_x + n_c
    tm = 512 if (t_len % 512 == 0 and n_c % 512 == 0) else 256
    tq = 2048 if t_len % 2048 == 0 else min(256, c_len)
    tr = min(256, c_len)
    assert dm == D_MODEL and t_len % tm == 0 and n_c % tm == 0 and n_x % c_len == 0
    assert t_len % tq == 0 and c_len % RWKV_CHUNK == 0 and t_len % c_len == 0
    tm_big = 1024 if (t_len % 1024 == 0 and n_c % 1024 == 0) else tm
    mod_index = functools.partial(_mod_index, tm=tm, n_x_rows=n_x, t_len=t_len, n_batch=n_batch)
    mod_index_big = functools.partial(_mod_index, tm=tm_big, n_x_rows=n_x, t_len=t_len, n_batch=n_batch)

    zeros = lambda *s: jnp.zeros(s, F32)
    lora_w = 4 * RWKV_LORA + RWKV_GATE_LORA
    rw0 = MLA_Q_LORA + MLA_KV_LORA + MLA_ROPE
    cv0 = rw0 + 3 * bw + lora_w
    w_in_t = jnp.swapaxes(w_in, 1, 2)
    w_in_p = jnp.concatenate(
        [w_in_t[:, :C_R], w_in_t[:, rw0:rw0 + 3 * bw], w_in_t[:, cv0:],
         w_in_t[:, rw0 + 3 * bw:cv0], zeros(depth, RWKV_LORA_PAD - lora_w, dm), jnp.swapaxes(gate_down, 1, 2),
         w_in_t[:, C_R:rw0], zeros(depth, P_COLS - C_KROPE - MLA_ROPE, dm)], axis=1).astype(BF16)
    w_uq_p = jnp.pad(mla_w_uq.reshape(depth, MLA_Q_LORA, MLA_HEADS, MLA_NOPE + MLA_ROPE),
                     ((0, 0), (0, 0), (0, 0), (0, MLA_QK_PAD - MLA_NOPE - MLA_ROPE))
                     ).reshape(depth, MLA_Q_LORA, MLA_HEADS * MLA_QK_PAD).astype(BF16)
    w_ukv_r = mla_w_ukv.reshape(depth, MLA_KV_LORA, MLA_HEADS, MLA_NOPE + MLA_V)
    w_ukv_p = jnp.concatenate([w_ukv_r[..., :MLA_NOPE].reshape(depth, MLA_KV_LORA, -1),
                               w_ukv_r[..., MLA_NOPE:].reshape(depth, MLA_KV_LORA, -1)], axis=-1).astype(BF16)
    w_branch_2d = w_branch.reshape(depth, 4 * bw, dm)
    gate_up_b = jnp.moveaxis(gate_up, 2, 1).astype(BF16)
    g2_p = jnp.pad(rwkv_g2, ((0, 0), (0, RWKV_LORA_PAD - 4 * RWKV_LORA - RWKV_GATE_LORA), (0, 0))).astype(BF16)
    mu_lora = jnp.pad(rwkv_mu[:, :, 3 * bw:], ((0, 0), (0, 0), (0, RWKV_LORA_PAD - lora_w)))

    cond = jnp.concatenate([c, c_ctx[None, :], zeros(16 - n_batch - 1, dm)], axis=0)
    cond = jax.nn.silu(cond)
    mods = []
    for l in range(depth):
        low = _matmul(cond, mod_down[l], tm=16, tn=MOD_RANK, tk=dm, name="mod_down")
        up = _matmul(low, mod_up[l], tm=16, tn=2048, tk=MOD_RANK, bias=mod_b[l], name="mod_up")
        mods.append(up.reshape(16, N_MOD, dm))

    cos_t, sin_t = _rope_tables128(t_len)
    xs = jnp.concatenate([x.reshape(n_x, dm), ctx.reshape(n_c, dm)], axis=0)

    for l in range(depth):
        need_ctx = l < depth - 1
        mod = mods[l]
        lam_init = 0.8 - 0.6 * math.exp(-0.3 * l)
        lq1, lk1, lq2, lk2 = diff_lambda[l]
        lam = jnp.exp(jnp.sum(lq1 * lk1)) - jnp.exp(jnp.sum(lq2 * lk2)) + lam_init
        lam_row = jnp.full((1, DIFF_V), 1.0, F32) * lam

        h1 = _norm_mod(xs, norm1_g[l], mod, shift_row=0, scale_row=1, mod_index=mod_index, tm=tm,
                       rows=n_rows)
        p, w1_b, w_out_b, w_branch_b = _fullk_matmul(h1, w_in_p, layer=l, tm=tm_big, tn=512, out_dtype=F32,
                                                     name="in_proj", cast_srcs=(mlp_w1, w_out, w_branch_2d),
                                                     w_rows_are_outputs=True)

        q = _nm_matmul(p, mla_q_norm_g[l], w_uq_p[l], tm=tm, x_col_block=C_CQ // MLA_Q_LORA, out_dtype=BF16,
                       name="mla_q",
                       out_scale=(MLA_NOPE + MLA_ROPE) ** -0.5 * LOG2_E)
        kv = _nm_matmul(p, mla_kv_norm_g[l], w_ukv_p[l], tm=tm, x_col_block=C_CKV // MLA_KV_LORA, out_dtype=BF16,
                        name="mla_kv")
        y_mla = _mla_attention(q, kv, p, cos_t, sin_t, n_batch=n_batch, t_len=t_len, c_len=c_len, tq=tq,
                               with_ctx=need_ctx)

        y_diff = _diff_attention(p, cos_t, sin_t, lam_row, diff_norm_g[l].reshape(1, DIFF_V),
                                 n_batch=n_batch, t_len=t_len, c_len=c_len, tq=tq, with_ctx=need_ctx,
                                 out_scale=1.0 - lam_init)

        r_, kk, v_, lw, k_dir, kka, bonus, g_out, y_conv = _mixer_prep(
            p, rwkv_mu[l, :, :3 * bw], mu_lora[l], rwkv_k_k[l].reshape(1, bw), rwkv_k_a[l].reshape(1, bw),
            rwkv_r_k[l].reshape(1, bw), rwkv_w0[l].reshape(1, 2 * bw), rwkv_a0[l].reshape(1, 2 * bw),
            _block_diag2(rwkv_w2[l]).astype(BF16), _block_diag2(rwkv_a2[l]).astype(BF16), g2_p[l],
            conv_w[l], tr=min(128, tr), n_x=n_x, t_len=t_len, c_len=c_len)
        o_dirs = _rwkv_scan(r_, kk, v_, lw, k_dir, kka, n_batch=n_batch, t_len=t_len, c_len=c_len)
        y_rwkv = _rwkv_readout(o_dirs, bonus, g_out, rwkv_ln_g[l], rwkv_ln_b[l], tr=tr)

        rows = n_rows if need_ctx else n_x
        acc = _merge(p, (y_mla, y_rwkv, y_conv, y_diff), w_branch_b, gate_up_b, gate_b, layer=l,
                     tm=tm_big, tn=512, rows=rows)
        (xs_new,) = _fullk_matmul(acc, w_out_b[None], layer=0, tm=tm_big, tn=512, out_dtype=F32, res=xs, mod=mod,
                                  gate_row=2, mod_index=mod_index_big, rows=rows, name="out_proj")

        h2 = _norm_mod(xs_new, norm2_g[l], mod, shift_row=3, scale_row=4, mod_index=mod_index, tm=tm,
                       rows=rows)
        hid, w2_b = _fullk_matmul(h2, w1_b[None], layer=0, tm=tm_big, tn=TN_UP, out_dtype=BF16, act="relu2",
                                  name="mlp_up", cast_srcs=(mlp_w2,), cast_layer=l)
        xs = _matmul(hid, w2_b, tm=tm_big, tn=1024, tk=2048, res=xs_new, mod=mod, gate_row=5,
                     mod_index=mod_index_big, rows=rows, name="mlp_down")

    out = _final_norm(xs, final_norm_g, rows=n_x, tm=tm)
    return out.reshape(n_batch, t_len, dm)
```
